```python
import math
import jax, jax.numpy as jnp
from jax import lax
import numpy as np

D_MODEL = 1024
BATCH = 1
SEQ = 16384
DEPTH = 4

GRID_W = 64
CTX_LEN = 256
RWKV_WIDTH = D_MODEL // 2
HEAD_DIM = 64
RWKV_HEADS = RWKV_WIDTH // HEAD_DIM
LORA_W = 64
LORA_A = 64
LORA_G = 128
DECAY_SCALE = 0.606531
GN_EPS = 6.4e-4
RWKV_COLS = 3 * RWKV_WIDTH + 2 * LORA_W + 2 * LORA_A + LORA_G
RWKV_SPLITS = (RWKV_WIDTH, 2 * RWKV_WIDTH, 3 * RWKV_WIDTH, 3 * RWKV_WIDTH + LORA_W, 3 * RWKV_WIDTH + 2 * LORA_W,
               3 * RWKV_WIDTH + 2 * LORA_W + LORA_A, 3 * RWKV_WIDTH + 2 * LORA_W + 2 * LORA_A)
HYENA_WIDTH = D_MODEL // 2
HYENA_COLS = 3 * HYENA_WIDTH
FILTER_BANDS = 16
FILTER_EMB = 1 + 2 * FILTER_BANDS
FILTER_HIDDEN = 64
HYENA_MIN_DECAY = math.log(1e-2) / 1.5
HYENA_MAX_DECAY = math.log(1e-2) / 0.3
GATE_COLS = 2 * D_MODEL
PROJ_COLS = RWKV_COLS + HYENA_COLS + GATE_COLS
N_EXPERTS = 16
N_GROUPS = 4
EXPERTS_PER_GROUP = N_EXPERTS // N_GROUPS
TOP_K = 2
D_EXPERT = D_MODEL
MOE_BLOCK = 128
ALPHA = (2 * DEPTH) ** 0.25
BETA = (8 * DEPTH) ** -0.25
LN_EPS = 1e-5

kernel_name = 'hybrid_rwkv7_hyena_moe_diffusion'


def _shift_seq(t, offset):
    if offset > 0:
        return jnp.pad(t[:, :-1], ((0, 0), (1, 0), (0, 0)))
    return jnp.pad(t[:, 1:], ((0, 0), (0, 1), (0, 0)))


def token_shift_latent(p):
    b, n, ch = p.shape
    rows = n // GRID_W
    q = ch // 4
    g = p.reshape(b, rows, GRID_W, ch)
    left = jnp.pad(g[:, :, :-1, :q], ((0, 0), (0, 0), (1, 0), (0, 0)))
    right = jnp.pad(g[:, :, 1:, q:2 * q], ((0, 0), (0, 0), (0, 1), (0, 0)))
    up = jnp.pad(g[:, :-1, :, 2 * q:3 * q], ((0, 0), (1, 0), (0, 0), (0, 0)))
    down = jnp.pad(g[:, 1:, :, 3 * q:], ((0, 0), (0, 1), (0, 0), (0, 0)))
    return jnp.concatenate([left, right, up, down], axis=-1).reshape(b, n, ch)


def token_shift_context(p):
    half = p.shape[-1] // 2
    return jnp.concatenate([_shift_seq(p[..., :half], 1), _shift_seq(p[..., half:], -1)], axis=-1)


def layer_norm(x, g, b):
    xf = x.astype(jnp.float32)
    mu = xf.mean(-1, keepdims=True)
    var = jnp.square(xf - mu).mean(-1, keepdims=True)
    return ((xf - mu) * lax.rsqrt(var + LN_EPS) * g + b).astype(x.dtype)


def post_norm(x, s, g, b):
    return layer_norm(ALPHA * x + s, g, b)


def modulate(x, shift, scale):
    return x * (1 + scale) + shift


def _heads(t):
    return t.reshape(t.shape[:-1] + (RWKV_HEADS, HEAD_DIM))


def l2_normalize_heads(t):
    th = _heads(t).astype(jnp.float32)
    nrm = jnp.sqrt(jnp.sum(th * th, axis=-1, keepdims=True))
    return (th / jnp.maximum(nrm, 1e-12)).reshape(t.shape).astype(t.dtype)


def group_norm_heads(y, g, b):
    yh = _heads(y).astype(jnp.float32)
    mu = yh.mean(-1, keepdims=True)
    var = jnp.square(yh - mu).mean(-1, keepdims=True)
    out = ((yh - mu) * lax.rsqrt(var + GN_EPS)).reshape(y.shape)
    return (out * g + b).astype(y.dtype)


def rwkv_prepare(p, shift_fn, mu, w0, w2, a0, a2, k_k, k_a):
    pm = p + (shift_fn(p) - p) * mu
    r, k, v, lw_f, lw_b, la_f, la_b, lg = jnp.split(pm, RWKV_SPLITS, axis=-1)
    lw = jnp.stack([lw_f, lw_b])
    la = jnp.stack([la_f, la_b])
    w = jnp.exp(-DECAY_SCALE * jax.nn.sigmoid(w0[:, None, None, :] + jnp.einsum('dblr,drc->dblc', jnp.tanh(lw), w2)))
    a = jax.nn.sigmoid(a0[:, None, None, :] + jnp.einsum('dblr,drc->dblc', la, a2))
    kk = l2_normalize_heads(k * k_k)
    k_dir = k[None] * (1 + (a - 1) * k_a)
    return r, k_dir, v, w, a, kk, lg


def delta_scan(r, w, k, v, kk, a, s0, reverse, emit):
    b, n, ch = r.shape

    def steps(t):
        return t.astype(jnp.float32).reshape(b, n, RWKV_HEADS, HEAD_DIM).transpose(1, 0, 2, 3)

    xs = (steps(r), steps(w), steps(k), steps(v), steps(-kk), steps(kk * a))

    def step(state, inp):
        r_t, w_t, k_t, v_t, ka_t, kb_t = inp
        sa = jnp.einsum('bhvk,bhk->bhv', state, ka_t)
        state = state * w_t[:, :, None, :] + sa[..., None] * kb_t[:, :, None, :] + v_t[..., None] * k_t[:, :, None, :]
        y = jnp.einsum('bhvk,bhk->bhv', state, r_t) if emit else None
        return state, y

    s_fin, ys = lax.scan(step, s0, xs, reverse=reverse)
    if not emit:
        return None, s_fin
    return ys.transpose(1, 0, 2, 3).reshape(b, n, ch).astype(r.dtype), s_fin


def rwkv_finish(r, k_dir, v, lg, y_f, y_b, r_k, lnx_g, lnx_b, g2):
    y = group_norm_heads(y_f + y_b, lnx_g, lnx_b)
    bonus = jnp.sum(_heads(r)[None] * _heads(k_dir) * r_k.reshape(RWKV_HEADS, HEAD_DIM), axis=(0, -1))
    y = y + (bonus[..., None] * _heads(v)).reshape(v.shape)
    return y * (jax.nn.sigmoid(lg) @ g2)


def hyena_filter(n, w1, b1, w2, b2, w3, b3, w_out, freq):
    pos = jnp.arange(n, dtype=jnp.float32)
    t = pos / max(n - 1, 1)
    bands = jnp.linspace(1e-4, FILTER_BANDS - 1, FILTER_BANDS, dtype=jnp.float32)
    ang = (2.0 * math.pi / n) * pos[:, None] * bands[None, :]
    z = jnp.concatenate([t[:, None], jnp.cos(ang), -jnp.sin(ang)], axis=-1)
    h = jnp.sin(freq * (z @ w1 + b1))
    h = jnp.sin(freq * (h @ w2 + b2))
    h = jnp.sin(freq * (h @ w3 + b3))
    filt = (h @ w_out).astype(jnp.float32)
    dist = jnp.abs(pos - n // 2) * (2.0 / n)
    deltas = jnp.abs(jnp.linspace(HYENA_MIN_DECAY, HYENA_MAX_DECAY, HYENA_WIDTH, dtype=jnp.float32))
    filt = filt * jnp.exp(-dist[:, None] * deltas[None, :])
    return filt / jnp.sum(jnp.abs(filt), axis=0, keepdims=True)


def long_conv(z, filt):
    n = z.shape[1]
    zf = jnp.fft.rfft(z.astype(jnp.float32), n=2 * n, axis=1)
    hf = jnp.fft.rfft(filt, n=2 * n, axis=0)
    y = jnp.fft.irfft(zf * hf[None], n=2 * n, axis=1)[:, n // 2:n // 2 + n]
    return y.astype(z.dtype)


def hyena_branch(p, conv_w, conv_b, filt, bias):
    u = _shift_seq(p, 1) * conv_w[0] + p * conv_w[1] + _shift_seq(p, -1) * conv_w[2] + conv_b
    x0, x1, v = jnp.split(u, 3, axis=-1)
    z = v * x1
    z = long_conv(z, filt) + z * bias
    return z * x0


def merge_branches(p_gate, y_rwkv, y_hyena, w_branch, w_out):
    g_r, g_h = jnp.split(jax.nn.sigmoid(p_gate), 2, axis=-1)
    return (g_r * (y_rwkv @ w_branch[0]) + g_h * (y_hyena @ w_branch[1])) @ w_out


def moe_ffn(h, w_router, router_bias, w_gate, w_up, w_down):
    n_tok, d = h.shape
    scores = jax.nn.softmax((h @ w_router).astype(jnp.float32), axis=-1)
    sel = scores + router_bias
    group_score = lax.top_k(sel.reshape(n_tok, N_GROUPS, EXPERTS_PER_GROUP), TOP_K)[0].sum(-1)
    best_group = jnp.argmax(group_score, axis=-1)
    in_group = (jnp.arange(N_EXPERTS) // EXPERTS_PER_GROUP)[None, :] == best_group[:, None]
    _, idx = lax.top_k(jnp.where(in_group, sel, -jnp.inf), TOP_K)
    wts = jnp.take_along_axis(scores, idx, axis=1)
    wts = wts / wts.sum(-1, keepdims=True)

    n_slots = n_tok * TOP_K
    n_blocks = -(-n_slots // MOE_BLOCK) + N_EXPERTS
    flat_e = idx.reshape(-1).astype(jnp.int32)
    flat_tok = jnp.repeat(jnp.arange(n_tok, dtype=jnp.int32), TOP_K)
    order = jnp.argsort(flat_e, stable=True)
    sorted_e = flat_e[order]
    counts = jnp.bincount(flat_e, length=N_EXPERTS)
    padded = ((counts + MOE_BLOCK - 1) // MOE_BLOCK) * MOE_BLOCK
    start = jnp.cumsum(counts) - counts
    pend = jnp.cumsum(padded)
    pstart = pend - padded
    dest = (pstart[sorted_e] + jnp.arange(n_slots) - start[sorted_e]).astype(jnp.int32)
    buf_tok = jnp.full((n_blocks * MOE_BLOCK,), n_tok, jnp.int32).at[dest].set(flat_tok[order])
    block_e = jnp.clip(jnp.searchsorted(pend, jnp.arange(n_blocks) * MOE_BLOCK, side='right'), 0, N_EXPERTS - 1)
    h_pad = jnp.concatenate([h, jnp.zeros((1, d), h.dtype)], axis=0)
    xb = h_pad[buf_tok].reshape(n_blocks, MOE_BLOCK, d)

    def expert_block(args):
        xblk, e = args
        return (jax.nn.silu(xblk @ w_gate[e]) * (xblk @ w_up[e])) @ w_down[e]

    yb = lax.map(expert_block, (xb, block_e)).reshape(n_blocks * MOE_BLOCK, d)
    y_slots = jnp.zeros((n_slots, d), yb.dtype).at[order].set(yb[dest])
    return (y_slots.reshape(n_tok, TOP_K, d) * wts[..., None].astype(yb.dtype)).sum(1)


def setup_inputs(seed: int = 0) -> dict:
    key = jax.random.key(seed)
    ks = iter(jax.random.split(key, 48))
    d = D_MODEL

    def nrm(shape, s):
        return jax.random.normal(next(ks), shape, jnp.float32) * s

    return {
        'x': nrm((BATCH, SEQ, d), 1.0),
        'c': nrm((BATCH, d), 1.0),
        'ctx': nrm((BATCH, CTX_LEN, d), 1.0),
        'c_ctx': nrm((d,), 1.0),
        'w_mod': nrm((DEPTH, d, 6 * d), 0.5 * d ** -0.5),
        'b_mod': nrm((DEPTH, 6 * d), 0.02),
        'w_in': nrm((DEPTH, d, PROJ_COLS), d ** -0.5),
        'rwkv_mu': jax.random.uniform(next(ks), (DEPTH, RWKV_COLS), jnp.float32),
        'rwkv_w0': nrm((DEPTH, 2, RWKV_WIDTH), 0.5),
        'rwkv_w2': nrm((DEPTH, 2, LORA_W, RWKV_WIDTH), 0.5 * LORA_W ** -0.5),
        'rwkv_a0': nrm((DEPTH, 2, RWKV_WIDTH), 0.5),
        'rwkv_a2': nrm((DEPTH, 2, LORA_A, RWKV_WIDTH), 0.5 * LORA_A ** -0.5),
        'rwkv_g2': nrm((DEPTH, LORA_G, RWKV_WIDTH), LORA_G ** -0.5),
        'rwkv_k_k': 0.85 + nrm((DEPTH, RWKV_WIDTH), 0.05),
        'rwkv_k_a': 1.0 + nrm((DEPTH, RWKV_WIDTH), 0.05),
        'rwkv_r_k': nrm((DEPTH, RWKV_WIDTH), 0.1),
        'rwkv_lnx_g': 1.0 + nrm((DEPTH, RWKV_WIDTH), 0.05),
        'rwkv_lnx_b': nrm((DEPTH, RWKV_WIDTH), 0.01),
        'hy_conv_w': nrm((DEPTH, 3, HYENA_COLS), 3 ** -0.5),
        'hy_conv_b': nrm((DEPTH, HYENA_COLS), 0.01),
        'hy_f_w1': nrm((DEPTH, FILTER_EMB, FILTER_HIDDEN), FILTER_EMB ** -0.5),
        'hy_f_b1': nrm((DEPTH, FILTER_HIDDEN), 0.1),
        'hy_f_w2': nrm((DEPTH, FILTER_HIDDEN, FILTER_HIDDEN), FILTER_HIDDEN ** -0.5),
        'hy_f_b2': nrm((DEPTH, FILTER_HIDDEN), 0.1),
        'hy_f_w3': nrm((DEPTH, FILTER_HIDDEN, FILTER_HIDDEN), FILTER_HIDDEN ** -0.5),
        'hy_f_b3': nrm((DEPTH, FILTER_HIDDEN), 0.1),
        'hy_f_wout': nrm((DEPTH, FILTER_HIDDEN, HYENA_WIDTH), FILTER_HIDDEN ** -0.5),
        'hy_freq': 1.0 + nrm((DEPTH, FILTER_HIDDEN), 0.05),
        'hy_bias': nrm((DEPTH, HYENA_WIDTH), 1.0),
        'w_branch': nrm((DEPTH, 2, RWKV_WIDTH, d), RWKV_WIDTH ** -0.5),
        'w_out': nrm((DEPTH, d, d), BETA * d ** -0.5),
        'ln_g': 1.0 + nrm((DEPTH, 2, d), 0.05),
        'ln_b': nrm((DEPTH, 2, d), 0.01),
        'w_router': nrm((d, N_EXPERTS), d ** -0.5),
        'router_bias': nrm((N_EXPERTS,), 0.01),
        'w_gate': nrm((DEPTH, N_EXPERTS, d, D_EXPERT), d ** -0.5),
        'w_up': nrm((DEPTH, N_EXPERTS, d, D_EXPERT), d ** -0.5),
        'w_down': nrm((DEPTH, N_EXPERTS, D_EXPERT, d), BETA * D_EXPERT ** -0.5),
    }


def reference(x, c, ctx, c_ctx, w_mod, b_mod, w_in, rwkv_mu, rwkv_w0, rwkv_w2, rwkv_a0, rwkv_a2, rwkv_g2,
              rwkv_k_k, rwkv_k_a, rwkv_r_k, rwkv_lnx_g, rwkv_lnx_b, hy_conv_w, hy_conv_b, hy_f_w1, hy_f_b1,
              hy_f_w2, hy_f_b2, hy_f_w3, hy_f_b3, hy_f_wout, hy_freq, hy_bias, w_branch, w_out, ln_g, ln_b,
              w_router, router_bias, w_gate, w_up, w_down):
    b, n_lat, d = x.shape
    n_ctx = ctx.shape[1]
    xl, xc = x, ctx
    hy_end = RWKV_COLS + HYENA_COLS
    for l in range(DEPTH):
        last = l == DEPTH - 1
        mod_l = jnp.split((jax.nn.silu(c) @ w_mod[l] + b_mod[l])[:, None, :], 6, axis=-1)
        mod_c = jnp.split(jax.nn.silu(c_ctx) @ w_mod[l] + b_mod[l], 6, axis=-1)
        rw_args = (rwkv_mu[l], rwkv_w0[l], rwkv_w2[l], rwkv_a0[l], rwkv_a2[l], rwkv_k_k[l], rwkv_k_a[l])
        fin_args = (rwkv_r_k[l], rwkv_lnx_g[l], rwkv_lnx_b[l], rwkv_g2[l])
        filt_args = (hy_f_w1[l], hy_f_b1[l], hy_f_w2[l], hy_f_b2[l], hy_f_w3[l], hy_f_b3[l], hy_f_wout[l], hy_freq[l])

        pc = modulate(xc, mod_c[0], mod_c[1]) @ w_in[l]
        pl = modulate(xl, mod_l[0], mod_l[1]) @ w_in[l]
        r_c, kd_c, v_c, w_c, a_c, kk_c, lg_c = rwkv_prepare(pc[..., :RWKV_COLS], token_shift_context, *rw_args)
        r_l, kd_l, v_l, w_l, a_l, kk_l, lg_l = rwkv_prepare(pl[..., :RWKV_COLS], token_shift_latent, *rw_args)
        s0 = jnp.zeros((b, RWKV_HEADS, HEAD_DIM, HEAD_DIM), jnp.float32)
        yc_f, sc_f = delta_scan(r_c, w_c[0], kd_c[0], v_c, kk_c, a_c[0], s0, False, not last)
        yc_b, sc_b = delta_scan(r_c, w_c[1], kd_c[1], v_c, kk_c, a_c[1], s0, True, not last)
        yl_f, _ = delta_scan(r_l, w_l[0], kd_l[0], v_l, kk_l, a_l[0], sc_f, False, True)
        yl_b, _ = delta_scan(r_l, w_l[1], kd_l[1], v_l, kk_l, a_l[1], sc_b, True, True)
        ro_l = rwkv_finish(r_l, kd_l, v_l, lg_l, yl_f, yl_b, *fin_args)
        ho_l = hyena_branch(pl[..., RWKV_COLS:hy_end], hy_conv_w[l], hy_conv_b[l],
                            hyena_filter(n_lat, *filt_args), hy_bias[l])
        mix_l = merge_branches(pl[..., hy_end:], ro_l, ho_l, w_branch[l], w_out[l])
        xl = post_norm(xl, mod_l[2] * mix_l, ln_g[l, 0], ln_b[l, 0])

        if last:
            hf = modulate(xl, mod_l[3], mod_l[4]).reshape(b * n_lat, d)
            yf = moe_ffn(hf, w_router, router_bias, w_gate[l], w_up[l], w_down[l]).reshape(b, n_lat, d)
            xl = post_norm(xl, mod_l[5] * yf, ln_g[l, 1], ln_b[l, 1])
        else:
            ro_c = rwkv_finish(r_c, kd_c, v_c, lg_c, yc_f, yc_b, *fin_args)
            ho_c = hyena_branch(pc[..., RWKV_COLS:hy_end], hy_conv_w[l], hy_conv_b[l],
                                hyena_filter(n_ctx, *filt_args), hy_bias[l])
            mix_c = merge_branches(pc[..., hy_end:], ro_c, ho_c, w_branch[l], w_out[l])
            xc = post_norm(xc, mod_c[2] * mix_c, ln_g[l, 0], ln_b[l, 0])
            hf = jnp.concatenate([modulate(xc, mod_c[3], mod_c[4]), modulate(xl, mod_l[3], mod_l[4])], axis=1)
            yf = moe_ffn(hf.reshape(b * (n_ctx + n_lat), d), w_router, router_bias,
                         w_gate[l], w_up[l], w_down[l]).reshape(b, n_ctx + n_lat, d)
            xc = post_norm(xc, mod_c[5] * yf[:, :n_ctx], ln_g[l, 1], ln_b[l, 1])
            xl = post_norm(xl, mod_l[5] * yf[:, n_ctx:], ln_g[l, 1], ln_b[l, 1])
    return xl
```

```python
import functools
import math

import numpy as np
import jax
import jax.numpy as jnp
from jax import lax
from jax.experimental import pallas as pl
from jax.experimental.pallas import tpu as pltpu

F32 = jnp.float32
BF = jnp.bfloat16

D_MODEL = 1024
DEPTH = 4
GRID_W = 64
RWKV_WIDTH = 512
HEAD_DIM = 64
RWKV_HEADS = 8
LORA_W = 64
LORA_A = 64
LORA_G = 128
DECAY_SCALE = 0.606531
GN_EPS = 6.4e-4
RWKV_COLS = 3 * RWKV_WIDTH + 2 * LORA_W + 2 * LORA_A + LORA_G
HYENA_WIDTH = 512
HYENA_COLS = 3 * HYENA_WIDTH
FILTER_BANDS = 16
FILTER_HIDDEN = 64
HYENA_MIN_DECAY = math.log(1e-2) / 1.5
HYENA_MAX_DECAY = math.log(1e-2) / 0.3
GATE_COLS = 2 * D_MODEL
PROJ_COLS = RWKV_COLS + HYENA_COLS + GATE_COLS
N_EXPERTS = 16
N_GROUPS = 4
EXPERTS_PER_GROUP = 4
ALPHA = (2 * DEPTH) ** 0.25
LN_EPS = 1e-5

SCAN_CHUNK = 64
LANES = 128
VMEM_LIMIT = 56 * 1024 * 1024

_NN = (((1,), (0,)), ((), ()))
_NT = (((1,), (1,)), ((), ()))
_TN = (((0,), (0,)), ((), ()))


def _sig(x):
    return 1.0 / (1.0 + jnp.exp(-x))


def _parts(a, n):
    out = []
    rem = a
    for i in range(n):
        hi = rem.astype(BF)
        out.append(hi)
        if i + 1 < n:
            rem = rem - hi.astype(F32)
    return out


def _mm(a, b, dn=_NN, passes=1):
    n = {1: 1, 3: 2, 6: 3}[passes]
    pa = _parts(a, n)
    pb = _parts(b, n)
    acc = None
    for i in range(n):
        for j in range(n - i):
            t = lax.dot_general(pa[i], pb[j], dn, preferred_element_type=F32)
            acc = t if acc is None else acc + t
    return acc


def _mm_rx(a, b_exact, n=3, dn=_NN):
    acc = None
    for p in _parts(a, n):
        t = lax.dot_general(p, b_exact, dn, preferred_element_type=F32)
        acc = t if acc is None else acc + t
    return acc


def _mm_lx(a_exact, b, n=3, dn=_NN):
    acc = None
    for p in _parts(b, n):
        t = lax.dot_general(a_exact, p, dn, preferred_element_type=F32)
        acc = t if acc is None else acc + t
    return acc


def _cparams(sem, vmem=VMEM_LIMIT):
    return pltpu.CompilerParams(dimension_semantics=sem, vmem_limit_bytes=vmem)


def _full(shape):
    nd = len(shape)
    return pl.BlockSpec(shape, lambda *_: (0,) * nd)


def _mod_kernel(c_ref, w_ref, b_ref, o_ref):
    cc = c_ref[...]
    s = cc * _sig(cc)
    o_ref[...] = jnp.dot(s.astype(BF), w_ref[...].astype(BF), preferred_element_type=F32) + b_ref[...]


def _mod_call(cc, w_mod, b_mod):
    depth, d, n6 = w_mod.shape
    tn = 1536
    return pl.pallas_call(
        _mod_kernel,
        out_shape=jax.ShapeDtypeStruct((depth, 8, n6), F32),
        grid=(depth, n6 // tn),
        in_specs=[
            pl.BlockSpec((8, d), lambda l, j: (0, 0)),
            pl.BlockSpec((None, d, tn), lambda l, j: (l, 0, j)),
            pl.BlockSpec((None, 1, tn), lambda l, j: (l, 0, j)),
        ],
        out_specs=pl.BlockSpec((None, 8, tn), lambda l, j: (l, 0, j)),
        compiler_params=_cparams(("arbitrary", "arbitrary")),
        name="mod",
    )(cc, w_mod, b_mod.reshape(depth, 1, n6))


def _inproj_kernel(x_ref, sh_ref, sc_ref, w_ref, pr_ref, ph_ref, pg_ref):
    xm = (x_ref[...] * (1.0 + sc_ref[...]) + sh_ref[...]).astype(BF)
    pr_ref[...] = jnp.dot(xm, w_ref[:, :RWKV_COLS], preferred_element_type=F32)
    ph_ref[...] = jnp.dot(xm, w_ref[:, RWKV_COLS:RWKV_COLS + HYENA_COLS], preferred_element_type=F32)
    pg_ref[...] = jnp.dot(xm, w_ref[:, RWKV_COLS + HYENA_COLS:], preferred_element_type=F32)


def _inproj_call(x, sh, sc, w_bf):
    L, d = x.shape
    tm = min(L, 256)
    row = lambda i: (i, 0)
    return pl.pallas_call(
        _inproj_kernel,
        out_shape=(jax.ShapeDtypeStruct((L, RWKV_COLS), F32),
                   jax.ShapeDtypeStruct((L, HYENA_COLS), F32),
                   jax.ShapeDtypeStruct((L, GATE_COLS), F32)),
        grid=(L // tm,),
        in_specs=[pl.BlockSpec((tm, d), row), _full((1, d)), _full((1, d)), _full((d, PROJ_COLS))],
        out_specs=(pl.BlockSpec((tm, RWKV_COLS), row), pl.BlockSpec((tm, HYENA_COLS), row),
                   pl.BlockSpec((tm, GATE_COLS), row)),
        compiler_params=_cparams(("arbitrary",)),
        name="inproj",
    )(x, sh, sc, w_bf)


def _prepare_kernel(*refs, latent, nblk):
    if latent:
        p_ref, up_ref, dn_ref = refs[:3]
        refs = refs[3:]
    else:
        p_ref = refs[0]
        refs = refs[1:]
    (mu_ref, w0_ref, w2_ref, a0_ref, a2_ref, kk_ref, ka_ref, rk_ref, g_ref,
     r_out, v_out, ka_out, lw_out, k_out, kb_out, lg_out, bv_out) = refs
    i = pl.program_id(0)
    p = p_ref[...]
    tb, w = p.shape
    row = lax.broadcasted_iota(jnp.int32, (tb, 1), 0)
    col = lax.broadcasted_iota(jnp.int32, (1, w), 1)
    if latent:
        gw = jnp.bitwise_and(row, GRID_W - 1)
        left = jnp.where(gw == 0, 0.0, pltpu.roll(p, 1, 0))
        right = jnp.where(gw == GRID_W - 1, 0.0, pltpu.roll(p, tb - 1, 0))
        upv = jnp.where(i == 0, 0.0, up_ref[...])
        dnv = jnp.where(i == nblk - 1, 0.0, dn_ref[...])
        if tb > GRID_W:
            up = jnp.concatenate([upv, p[:tb - GRID_W]], axis=0)
            down = jnp.concatenate([p[GRID_W:], dnv], axis=0)
        else:
            up, down = upv, dnv
        q = w // 4
        sh = jnp.where(col < q, left, jnp.where(col < 2 * q, right, jnp.where(col < 3 * q, up, down)))
    else:
        prev = jnp.where(row == 0, 0.0, pltpu.roll(p, 1, 0))
        nxt = jnp.where(row == tb - 1, 0.0, pltpu.roll(p, tb - 1, 0))
        sh = jnp.where(col < w // 2, prev, nxt)
    pm = p + (sh - p) * mu_ref[...]
    hw = RWKV_WIDTH
    r = pm[:, :hw]
    k = pm[:, hw:2 * hw]
    v = pm[:, 2 * hw:3 * hw]
    lw_in = jnp.tanh(pm[:, 3 * hw:3 * hw + 2 * LORA_W])
    la_in = pm[:, 3 * hw + 2 * LORA_W:3 * hw + 2 * LORA_W + 2 * LORA_A]
    lg = pm[:, 3 * hw + 2 * LORA_W + 2 * LORA_A:]
    lw = -DECAY_SCALE * _sig(w0_ref[...] + _mm(lw_in, w2_ref[...], passes=3))
    a = _sig(a0_ref[...] + _mm(la_in, a2_ref[...], passes=3))
    g = g_ref[...]
    kkr = k * kk_ref[...]
    nrm = jnp.sqrt(_mm_rx(kkr * kkr, g, 3))
    kk = kkr / jnp.maximum(nrm, 1e-12)
    k_a = ka_ref[...]
    k_f = k * (1.0 + (a[:, :hw] - 1.0) * k_a)
    k_b = k * (1.0 + (a[:, hw:] - 1.0) * k_a)
    bonus = _mm_rx(r * (k_f + k_b) * rk_ref[...], g, 3)
    r_out[...] = r
    v_out[...] = v
    ka_out[...] = -kk
    lw_out[0] = lw[:, :hw]
    lw_out[1] = lw[:, hw:]
    k_out[0] = k_f
    k_out[1] = k_b
    kb_out[0] = kk * a[:, :hw]
    kb_out[1] = kk * a[:, hw:]
    lg_out[...] = lg
    bv_out[...] = bonus * v


def _prepare_call(p_r, prm, latent):
    L, w = p_r.shape
    hw = RWKV_WIDTH
    tb = 256 if latent else L
    nblk = L // tb
    row = lambda i: (i, 0)
    row3 = lambda i: (0, i, 0)
    in_specs = [pl.BlockSpec((tb, w), row)]
    args = [p_r]
    if latent:
        per = tb // GRID_W
        nrow = L // GRID_W
        in_specs += [
            pl.BlockSpec((GRID_W, w), lambda i: (jnp.maximum(i * per - 1, 0), 0)),
            pl.BlockSpec((GRID_W, w), lambda i: (jnp.minimum((i + 1) * per, nrow - 1), 0)),
        ]
        args += [p_r, p_r]
    names = ("mu", "w0", "w2", "a0", "a2", "k_k", "k_a", "r_k", "G")
    for nm in names:
        in_specs.append(_full(prm[nm].shape))
        args.append(prm[nm])
    sd = jax.ShapeDtypeStruct
    out_shape = (sd((L, hw), F32), sd((L, hw), F32), sd((L, hw), F32),
                 sd((2, L, hw), F32), sd((2, L, hw), F32), sd((2, L, hw), F32),
                 sd((L, LORA_G), F32), sd((L, hw), F32))
    out_specs = (pl.BlockSpec((tb, hw), row), pl.BlockSpec((tb, hw), row), pl.BlockSpec((tb, hw), row),
                 pl.BlockSpec((2, tb, hw), row3), pl.BlockSpec((2, tb, hw), row3), pl.BlockSpec((2, tb, hw), row3),
                 pl.BlockSpec((tb, LORA_G), row), pl.BlockSpec((tb, hw), row))
    return pl.pallas_call(
        functools.partial(_prepare_kernel, latent=latent, nblk=nblk),
        out_shape=out_shape, grid=(nblk,), in_specs=in_specs, out_specs=out_specs,
        compiler_params=_cparams(("arbitrary",)),
        name="rwkv_prepare_lat" if latent else "rwkv_prepare_ctx",
    )(*args)


P_SCORE = 3
P_TINV = 3
P_APPLY = 3
P_STATE = 3


def _scan_kernel(r_ref, v_ref, ka_ref, lw_ref, k_ref, kb_ref, s0_ref, y_ref, sfin_ref, s_scr, *, nchunk):
    d = pl.program_id(0)
    c = pl.program_id(1)
    C = SCAN_CHUNK
    hd = HEAD_DIM

    @pl.when(c == 0)
    def _():
        s_scr[...] = s0_ref[...]

    r = r_ref[...]
    v = v_ref[...]
    ka = ka_ref[...]
    lw = lw_ref[...]
    k = k_ref[...]
    kb = kb_ref[...]

    row = lax.broadcasted_iota(jnp.int32, (C, C), 0)
    col = lax.broadcasted_iota(jnp.int32, (C, C), 1)
    diff = (row - col) * (1 - 2 * d)
    inc = diff >= 0
    strict = diff > 0

    lc = _mm_lx(jnp.where(inc, 1.0, 0.0).astype(BF), lw, 3)
    tot = jnp.sum(lw, axis=0, keepdims=True)
    e_ex = jnp.exp(lc - lw)
    e_in = jnp.exp(lc)
    e_neg = jnp.exp(-lc)
    e_tot = jnp.exp(tot)
    aq = ka * e_ex
    rq = r * e_in
    kbn = kb * e_neg
    kkn = k * e_neg
    kbp = kbn * e_tot
    kkp = kkn * e_tot

    eye = jnp.where(row == col, 1.0, 0.0)
    lvl_masks = []
    b = 1
    while b < C:
        sh = int(math.log2(b))
        same2 = jnp.right_shift(row, sh + 1) == jnp.right_shift(col, sh + 1)
        same1 = jnp.right_shift(row, sh) == jnp.right_shift(col, sh)
        lvl_masks.append(jnp.logical_and(same2, jnp.logical_not(same1)))
        b *= 2

    for h in range(RWKV_HEADS):
        sl = slice(h * hd, (h + 1) * hd)
        aq_h, rq_h, v_h = aq[:, sl], rq[:, sl], v[:, sl]
        sc = _mm(jnp.concatenate([aq_h, rq_h], axis=0),
                 jnp.concatenate([kbn[:, sl], kkn[:, sl]], axis=0), _NT, P_SCORE)
        a_ab = jnp.where(strict, sc[:C, :C], 0.0)
        a_ak = jnp.where(strict, sc[:C, C:], 0.0)
        b_rb = jnp.where(inc, sc[C:, :C], 0.0)
        b_rk = jnp.where(inc, sc[C:, C:], 0.0)
        t = eye + jnp.where(lvl_masks[0], a_ab, 0.0)
        for m in lvl_masks[1:]:
            a_off = jnp.where(m, a_ab, 0.0)
            t = t + _mm(t, _mm(a_off, t, _NN, P_TINV), _NN, P_TINV)
        x = _mm(jnp.concatenate([a_ak, b_rk], axis=0), v_h, _NN, P_APPLY)
        z = _mm(t, jnp.concatenate([aq_h, x[:C]], axis=1), _NN, P_APPLY)
        s_h = s_scr[:, sl]
        gm = _mm(jnp.concatenate([z[:, :hd], rq_h], axis=0), s_h, _NT, P_STATE)
        u = gm[:C] + z[:, hd:]
        y_ref[:, sl] = gm[C:] + _mm(b_rb, u, _NN, P_APPLY) + x[C:]
        s_scr[:, sl] = s_h * e_tot[:, sl] + _mm(jnp.concatenate([u, v_h], axis=0),
                                               jnp.concatenate([kbp[:, sl], kkp[:, sl]], axis=0), _TN, P_STATE)

    @pl.when(c == nchunk - 1)
    def _():
        sfin_ref[...] = s_scr[...]


def _scan_call(r, v, ka, lw, k, kb, s0):
    L, hw = r.shape
    C = SCAN_CHUNK
    nchunk = L // C
    cidx = lambda d, c: c + d * (nchunk - 1 - 2 * c)
    shared = pl.BlockSpec((C, hw), lambda d, c: (cidx(d, c), 0))
    per_dir = pl.BlockSpec((None, C, hw), lambda d, c: (d, cidx(d, c), 0))
    state = pl.BlockSpec((None, HEAD_DIM, hw), lambda d, c: (d, 0, 0))
    return pl.pallas_call(
        functools.partial(_scan_kernel, nchunk=nchunk),
        out_shape=(jax.ShapeDtypeStruct((2, L, hw), F32), jax.ShapeDtypeStruct((2, HEAD_DIM, hw), F32)),
        grid=(2, nchunk),
        in_specs=[shared, shared, shared, per_dir, per_dir, per_dir, state],
        out_specs=(per_dir, state),
        scratch_shapes=[pltpu.VMEM((HEAD_DIM, hw), F32)],
        compiler_params=_cparams(("arbitrary", "arbitrary")),
        name="delta_scan",
    )(r, v, ka, lw, k, kb, s0)


def _hfilt_kernel(bands_ref, w1t_ref, w1c_ref, w1s_ref, b1_ref, w2_ref, b2_ref, w3_ref, b3_ref, wo_ref,
                  fr_ref, dl_ref, f_ref, asum_ref, *, n, rb):
    i = pl.program_id(0)
    pos = (i * rb + lax.broadcasted_iota(jnp.int32, (rb, 1), 0)).astype(F32)
    t = pos / float(max(n - 1, 1))
    ang = ((2.0 * math.pi / n) * pos) * bands_ref[...]
    fr = fr_ref[...]
    h = t * w1t_ref[...] + _mm(jnp.cos(ang), w1c_ref[...], passes=3) + _mm(-jnp.sin(ang), w1s_ref[...], passes=3)
    h = jnp.sin(fr * (h + b1_ref[...]))
    h = jnp.sin(fr * (_mm(h, w2_ref[...], passes=3) + b2_ref[...]))
    h = jnp.sin(fr * (_mm(h, w3_ref[...], passes=3) + b3_ref[...]))
    filt = _mm(h, wo_ref[...], passes=3)
    dist = jnp.abs(pos - float(n // 2)) * (2.0 / n)
    filt = filt * jnp.exp(-dist * dl_ref[...])
    f_ref[...] = filt

    @pl.when(i == 0)
    def _():
        asum_ref[...] = jnp.zeros_like(asum_ref)

    asum_ref[...] += jnp.sum(jnp.abs(filt), axis=0, keepdims=True)


def _hfilt_call(n, fp):
    rb = min(n, 512)
    names = ("bands", "w1t", "w1c", "w1s", "b1", "w2", "b2", "w3", "b3", "wout", "freq", "deltas")
    args = [fp[nm] for nm in names]
    return pl.pallas_call(
        functools.partial(_hfilt_kernel, n=n, rb=rb),
        out_shape=(jax.ShapeDtypeStruct((n, HYENA_WIDTH), F32), jax.ShapeDtypeStruct((1, HYENA_WIDTH), F32)),
        grid=(n // rb,),
        in_specs=[_full(a.shape) for a in args],
        out_specs=(pl.BlockSpec((rb, HYENA_WIDTH), lambda i: (i, 0)), _full((1, HYENA_WIDTH))),
        compiler_params=_cparams(("arbitrary",)),
        name="hyena_filter",
    )(*args)


def _hconv3_kernel(p_ref, pv_ref, nx_ref, cw_ref, cb_ref, z_ref, x0_ref, *, nblk):
    i = pl.program_id(0)
    p = p_ref[...]
    tb = p.shape[0]
    row = lax.broadcasted_iota(jnp.int32, (tb, 1), 0)
    prev_row = jnp.where(i == 0, 0.0, pv_ref[7:8, :])
    next_row = jnp.where(i == nblk - 1, 0.0, nx_ref[0:1, :])
    sp = jnp.where(row == 0, prev_row, pltpu.roll(p, 1, 0))
    sn = jnp.where(row == tb - 1, next_row, pltpu.roll(p, tb - 1, 0))
    u = sp * cw_ref[0:1, :] + p * cw_ref[1:2, :] + sn * cw_ref[2:3, :] + cb_ref[...]
    hw = HYENA_WIDTH
    z_ref[...] = u[:, 2 * hw:] * u[:, hw:2 * hw]
    x0_ref[...] = u[:, :hw]


def _hconv3_call(p_h, cw, cb):
    L, w = p_h.shape
    tb = min(L, 256)
    nblk = L // tb
    per = tb // 8
    row = lambda i: (i, 0)
    return pl.pallas_call(
        functools.partial(_hconv3_kernel, nblk=nblk),
        out_shape=(jax.ShapeDtypeStruct((L, HYENA_WIDTH), F32), jax.ShapeDtypeStruct((L, HYENA_WIDTH), F32)),
        grid=(nblk,),
        in_specs=[pl.BlockSpec((tb, w), row),
                  pl.BlockSpec((8, w), lambda i: (jnp.maximum(i * per - 1, 0), 0)),
                  pl.BlockSpec((8, w), lambda i: (jnp.minimum((i + 1) * per, L // 8 - 1), 0)),
                  _full(cw.shape), _full(cb.shape)],
        out_specs=(pl.BlockSpec((tb, HYENA_WIDTH), row), pl.BlockSpec((tb, HYENA_WIDTH), row)),
        compiler_params=_cparams(("arbitrary",)),
        name="hyena_conv3",
    )(p_h, p_h, p_h, cw, cb)


P_DFT = 3


def _dft1_kernel(m_ref, x_ref, o_ref):
    o_ref[...] = _mm(m_ref[...], x_ref[...], _NN, P_DFT)


def _dft1_call(m1, x2d):
    rows, kdim = m1.shape
    _, cols = x2d.shape
    cb = min(cols, 2048)
    return pl.pallas_call(
        _dft1_kernel,
        out_shape=jax.ShapeDtypeStruct((rows, cols), F32),
        grid=(cols // cb,),
        in_specs=[_full(m1.shape), pl.BlockSpec((kdim, cb), lambda j: (0, j))],
        out_specs=pl.BlockSpec((rows, cb), lambda j: (0, j)),
        compiler_params=_cparams(("arbitrary",)),
        name="dft_stage1",
    )(m1, x2d)


def _dft2_matrix(fc_ref, fs_ref, tc_ref, ts_ref):
    fc, fs = fc_ref[...], fs_ref[...]
    tc, ts = tc_ref[...], ts_ref[...]
    fre = fc * tc - fs * ts
    fim = -(fc * ts + fs * tc)
    return jnp.concatenate([jnp.concatenate([fre, -fim], axis=1),
                            jnp.concatenate([fim, fre], axis=1)], axis=0)


def _dft2_spec_kernel(fc_ref, fs_ref, tc_ref, ts_ref, a_ref, h_ref):
    big = _dft2_matrix(fc_ref, fs_ref, tc_ref, ts_ref)
    n2 = a_ref.shape[1]
    x = _mm(big, jnp.concatenate([a_ref[0], a_ref[1]], axis=0), _NN, P_DFT)
    h_ref[0] = x[:n2]
    h_ref[1] = x[n2:]


def _dft2_conv_kernel(fc_ref, fs_ref, tc_ref, ts_ref, a_ref, h_ref, b_ref):
    big = _dft2_matrix(fc_ref, fs_ref, tc_ref, ts_ref)
    n2 = a_ref.shape[1]
    x = _mm(big, jnp.concatenate([a_ref[0], a_ref[1]], axis=0), _NN, P_DFT)
    xr, xi = x[:n2], x[n2:]
    hr, hi = h_ref[0], h_ref[1]
    y = jnp.concatenate([xr * hr - xi * hi, xr * hi + xi * hr], axis=0)
    bb = _mm(big, y, _TN, P_DFT)
    b_ref[0] = bb[:n2]
    b_ref[1] = bb[n2:]


def _dft2_call(consts, a4, h4=None):
    _, n1, n2, c = a4.shape
    blk = pl.BlockSpec((2, None, n2, c), lambda q: (0, q, 0, 0))
    tw = pl.BlockSpec((None, 1, n2), lambda q: (q, 0, 0))
    in_specs = [_full((n2, n2)), _full((n2, n2)), tw, tw, blk]
    args = [consts["fc"], consts["fs"], consts["twc"], consts["tws"], a4]
    kern = _dft2_spec_kernel
    if h4 is not None:
        in_specs.append(blk)
        args.append(h4)
        kern = _dft2_conv_kernel
    return pl.pallas_call(
        kern,
        out_shape=jax.ShapeDtypeStruct(a4.shape, F32),
        grid=(n1,),
        in_specs=in_specs,
        out_specs=blk,
        compiler_params=_cparams(("arbitrary",)),
        name="dft_stage2_spec" if h4 is None else "dft_stage2_conv",
    )(*args)


def _dft3_kernel(m_ref, b_ref, z_ref, x0_ref, bias_ref, asum_ref, o_ref):
    y = _mm(m_ref[...], b_ref[...], _NN, P_DFT)
    z = z_ref[...]
    o_ref[...] = (y / asum_ref[...] + z * bias_ref[...]) * x0_ref[...]


def _dft3_call(m3, b2d, z2d, x02d, bias_t, asum_t):
    rows, kdim = m3.shape
    _, cols = b2d.shape
    cb = bias_t.shape[1]
    colb = lambda j: (0, j)
    return pl.pallas_call(
        _dft3_kernel,
        out_shape=jax.ShapeDtypeStruct((rows, cols), F32),
        grid=(cols // cb,),
        in_specs=[_full(m3.shape), pl.BlockSpec((kdim, cb), colb), pl.BlockSpec((rows, cb), colb),
                  pl.BlockSpec((rows, cb), colb), _full((1, cb)), _full((1, cb))],
        out_specs=pl.BlockSpec((rows, cb), colb),
        compiler_params=_cparams(("arbitrary",)),
        name="dft_stage3",
    )(m3, b2d, z2d, x02d, bias_t, asum_t)


def _conv_direct_kernel(m1_ref, m3_ref, z_ref, f_ref, x0_ref, bias_ref, asum_ref, o_ref):
    m1 = m1_ref[...]
    z = z_ref[...]
    a = _mm(m1, z, _NN, P_DFT)
    h = _mm(m1, f_ref[...], _NN, P_DFT)
    nn = a.shape[0] // 2
    ar, ai, hr, hi = a[:nn], a[nn:], h[:nn], h[nn:]
    y = jnp.concatenate([ar * hr - ai * hi, ar * hi + ai * hr], axis=0)
    out = _mm(m3_ref[...], y, _NN, P_DFT)
    o_ref[...] = (out / asum_ref[...] + z * bias_ref[...]) * x0_ref[...]


def _conv_direct_call(m1, m3, z, filt, x0, bias, asum):
    args = (m1, m3, z, filt, x0, bias, asum)
    return pl.pallas_call(
        _conv_direct_kernel,
        out_shape=jax.ShapeDtypeStruct(z.shape, F32),
        grid=(1,),
        in_specs=[_full(a.shape) for a in args],
        out_specs=_full(z.shape),
        compiler_params=_cparams(("arbitrary",)),
        name="long_conv_direct",
    )(*args)


DIRECT_CONV_MAX = 256


@functools.lru_cache(maxsize=None)
def _dft_consts_np(n):
    big_n = 2 * n
    if n <= DIRECT_CONV_MAX:
        n1 = big_n
    else:
        n1 = 256 if n >= 8192 else 64
    n2 = big_n // n1
    k1 = np.arange(n1)[:, None].astype(np.float64)
    j1 = np.arange(n1 // 2)[None, :].astype(np.float64)
    ang1 = 2.0 * np.pi * k1 * j1 / n1
    m1 = np.concatenate([np.cos(ang1), -np.sin(ang1)], axis=0)
    o1 = (n1 // 4 + np.arange(n1 // 2))[:, None].astype(np.float64)
    q1 = np.arange(n1)[None, :].astype(np.float64)
    ang3 = 2.0 * np.pi * o1 * q1 / n1
    m3 = np.concatenate([np.cos(ang3), -np.sin(ang3)], axis=1) / big_n
    k2 = np.arange(n2)[:, None].astype(np.float64)
    j2 = np.arange(n2)[None, :].astype(np.float64)
    ang2 = 2.0 * np.pi * k2 * j2 / n2
    angt = 2.0 * np.pi * np.arange(n1)[:, None].astype(np.float64) * j2 / big_n
    f = lambda a: np.asarray(a, np.float32)
    return dict(n1=n1, n2=n2, m1=f(m1), m3=f(m3), fc=f(np.cos(ang2)), fs=f(np.sin(ang2)),
                twc=f(np.cos(angt))[:, None, :], tws=f(np.sin(angt))[:, None, :])


def _long_conv_call(z, x0, filt, asum, bias):
    n, c = z.shape
    cn = _dft_consts_np(n)
    n1, n2 = cn["n1"], cn["n2"]
    consts = {kk: jnp.asarray(vv) for kk, vv in cn.items() if kk not in ("n1", "n2")}
    if n2 == 1:
        return _conv_direct_call(consts["m1"], consts["m3"], z, filt, x0, bias, asum)
    cols = n2 * c
    cb = min(cols, 2048)
    a_f = _dft1_call(consts["m1"], filt.reshape(n1 // 2, cols)).reshape(2, n1, n2, c)
    h4 = _dft2_call(consts, a_f)
    a_z = _dft1_call(consts["m1"], z.reshape(n1 // 2, cols)).reshape(2, n1, n2, c)
    b4 = _dft2_call(consts, a_z, h4)
    out = _dft3_call(consts["m3"], b4.reshape(2 * n1, cols), z.reshape(n1 // 2, cols), x0.reshape(n1 // 2, cols),
                     jnp.tile(bias, (1, cb // c)), jnp.tile(asum, (1, cb // c)))
    return out.reshape(n, c)


def _layer_norm(x, g, b):
    mu = jnp.mean(x, axis=-1, keepdims=True)
    xc = x - mu
    var = jnp.mean(xc * xc, axis=-1, keepdims=True)
    return xc * lax.rsqrt(var + LN_EPS) * g + b


def _merge_kernel(y_ref, bv_ref, lg_ref, ho_ref, pg_ref, x_ref, gx_ref, bx_ref, g2_ref, g_ref,
                  wb_ref, wo_ref, gate_ref, lng_ref, lnb_ref, o_ref):
    g = g_ref[...]
    ys = y_ref[0] + y_ref[1]
    inv_hd = 1.0 / HEAD_DIM
    mu = _mm_rx(ys, g, 3) * inv_hd
    dd = ys - mu
    var = _mm_rx(dd * dd, g, 3) * inv_hd
    yn = dd * lax.rsqrt(var + GN_EPS) * gx_ref[...] + bx_ref[...]
    gate_r = jnp.dot(_sig(lg_ref[...]).astype(BF), g2_ref[...], preferred_element_type=F32)
    ro = (yn + bv_ref[...]) * gate_r
    br = jnp.dot(ro.astype(BF), wb_ref[0], preferred_element_type=F32)
    bh = jnp.dot(ho_ref[...].astype(BF), wb_ref[1], preferred_element_type=F32)
    sg = _sig(pg_ref[...])
    m = sg[:, :D_MODEL] * br + sg[:, D_MODEL:] * bh
    mix = jnp.dot(m.astype(BF), wo_ref[...], preferred_element_type=F32)
    o_ref[...] = _layer_norm(ALPHA * x_ref[...] + gate_ref[...] * mix, lng_ref[...], lnb_ref[...])


def _merge_call(y, bv, lg, ho, pg, x, mp):
    L, d = x.shape
    hw = RWKV_WIDTH
    tb = min(L, 256)
    row = lambda i: (i, 0)
    names = ("lnx_g", "lnx_b", "g2", "G", "w_branch", "w_out", "gate", "ln_g", "ln_b")
    pargs = [mp[nm] for nm in names]
    return pl.pallas_call(
        _merge_kernel,
        out_shape=jax.ShapeDtypeStruct((L, d), F32),
        grid=(L // tb,),
        in_specs=[pl.BlockSpec((2, tb, hw), lambda i: (0, i, 0)), pl.BlockSpec((tb, hw), row),
                  pl.BlockSpec((tb, LORA_G), row), pl.BlockSpec((tb, hw), row),
                  pl.BlockSpec((tb, GATE_COLS), row), pl.BlockSpec((tb, d), row)]
                 + [_full(a.shape) for a in pargs],
        out_specs=pl.BlockSpec((tb, d), row),
        compiler_params=_cparams(("arbitrary",)),
        name="merge_postnorm",
    )(y, bv, lg, ho, pg, x, *pargs)


def _router_kernel(x_ref, sh_ref, sc_ref, wr_ref, rb_ref, cw_ref):
    hf = x_ref[...] * (1.0 + sc_ref[...]) + sh_ref[...]
    logits = _mm(hf, wr_ref[...], _NN, 6)
    lane = lax.broadcasted_iota(jnp.int32, (1, LANES), 1)
    valid = lane < N_EXPERTS
    lg = jnp.where(valid, logits, -jnp.inf)
    mx = jnp.max(lg, axis=1, keepdims=True)
    ex = jnp.where(valid, jnp.exp(lg - mx), 0.0)
    scores = ex / jnp.sum(ex, axis=1, keepdims=True)
    sel = scores + rb_ref[...]
    s = [sel[:, e:e + 1] for e in range(N_EXPERTS)]
    p = [scores[:, e:e + 1] for e in range(N_EXPERTS)]
    gs = []
    for gi in range(N_GROUPS):
        mem = s[gi * EXPERTS_PER_GROUP:(gi + 1) * EXPERTS_PER_GROUP]
        best = None
        for a in range(EXPERTS_PER_GROUP):
            for b in range(a + 1, EXPERTS_PER_GROUP):
                pair = mem[a] + mem[b]
                best = pair if best is None else jnp.maximum(best, pair)
        gs.append(best)
    bg = jnp.where((gs[0] >= gs[1]) & (gs[0] >= gs[2]) & (gs[0] >= gs[3]), 0,
                   jnp.where((gs[1] >= gs[2]) & (gs[1] >= gs[3]), 1, jnp.where(gs[2] >= gs[3], 2, 3)))
    chosen = []
    for e in range(N_EXPERTS):
        gi = e // EXPERTS_PER_GROUP
        beats = None
        for j in range(gi * EXPERTS_PER_GROUP, (gi + 1) * EXPERTS_PER_GROUP):
            if j == e:
                continue
            cond = (s[j] >= s[e]) if j < e else (s[j] > s[e])
            cnt = jnp.where(cond, 1.0, 0.0)
            beats = cnt if beats is None else beats + cnt
        chosen.append((bg == gi) & (beats < 1.5))
    den = None
    for e in range(N_EXPERTS):
        t = jnp.where(chosen[e], p[e], 0.0)
        den = t if den is None else den + t
    out = jnp.zeros(cw_ref.shape, F32)
    for e in range(N_EXPERTS):
        out = out + jnp.where((lane == e) & chosen[e], p[e] / den, 0.0)
    cw_ref[...] = out


def _router_call(x, sh, sc, wr_pad, rb_pad):
    L, d = x.shape
    tb = min(L, 256)
    row = lambda i: (i, 0)
    return pl.pallas_call(
        _router_kernel,
        out_shape=jax.ShapeDtypeStruct((L, LANES), F32),
        grid=(L // tb,),
        in_specs=[pl.BlockSpec((tb, d), row), _full((1, d)), _full((1, d)), _full(wr_pad.shape), _full(rb_pad.shape)],
        out_specs=pl.BlockSpec((tb, LANES), row),
        compiler_params=_cparams(("arbitrary",)),
        name="router",
    )(x, sh, sc, wr_pad, rb_pad)


def _moe_kernel(x_ref, cw_ref, sh_ref, sc_ref, gate_ref, lng_ref, lnb_ref, wg_ref, wu_ref, wd_ref, o_ref,
                hf_scr, acc_scr):
    e = pl.program_id(1)

    @pl.when(e == 0)
    def _():
        hf_scr[...] = (x_ref[...] * (1.0 + sc_ref[...]) + sh_ref[...]).astype(BF)
        acc_scr[...] = jnp.zeros_like(acc_scr)

    hf = hf_scr[...]
    hg = jnp.dot(hf, wg_ref[...], preferred_element_type=F32)
    hu = jnp.dot(hf, wu_ref[...], preferred_element_type=F32)
    act = (hg * _sig(hg) * hu).astype(BF)
    ye = jnp.dot(act, wd_ref[...], preferred_element_type=F32)
    lane = lax.broadcasted_iota(jnp.int32, (1, LANES), 1)
    we = jnp.sum(jnp.where(lane == e, cw_ref[...], 0.0), axis=1, keepdims=True)
    acc_scr[...] += we * ye

    @pl.when(e == N_EXPERTS - 1)
    def _():
        o_ref[...] = _layer_norm(ALPHA * x_ref[...] + gate_ref[...] * acc_scr[...], lng_ref[...], lnb_ref[...])


def _moe_call(x, cw, sh, sc, gate, lng, lnb, wg, wu, wd):
    L, d = x.shape
    tm = min(L, 512)
    row = lambda i, e: (i, 0)
    wspec = pl.BlockSpec((None, d, d), lambda i, e: (e, 0, 0))
    vec = pl.BlockSpec((1, d), lambda i, e: (0, 0))
    return pl.pallas_call(
        _moe_kernel,
        out_shape=jax.ShapeDtypeStruct((L, d), F32),
        grid=(L // tm, N_EXPERTS),
        in_specs=[pl.BlockSpec((tm, d), row), pl.BlockSpec((tm, LANES), row), vec, vec, vec, vec, vec,
                  wspec, wspec, wspec],
        out_specs=pl.BlockSpec((tm, d), row),
        scratch_shapes=[pltpu.VMEM((tm, d), BF), pltpu.VMEM((tm, d), F32)],
        compiler_params=_cparams(("arbitrary", "arbitrary")),
        name="moe_ffn",
    )(x, cw, sh, sc, gate, lng, lnb, wg, wu, wd)


def _blockdiag2(m):
    z = jnp.zeros_like(m[0])
    return jnp.concatenate([jnp.concatenate([m[0], z], axis=1), jnp.concatenate([z, m[1]], axis=1)], axis=0)


def _mixer(x, mod_row, l, w_in_bf, prm, fp, hy, mp, s0, latent):
    L = x.shape[0]
    sh, sc, gate = mod_row[0], mod_row[1], mod_row[2]
    p_r, p_h, p_g = _inproj_call(x, sh, sc, w_in_bf)
    r, v, ka, lw, k, kb, lg, bv = _prepare_call(p_r, prm, latent)
    y, sfin = _scan_call(r, v, ka, lw, k, kb, s0)
    filt, asum = _hfilt_call(L, fp)
    z, x0 = _hconv3_call(p_h, hy["conv_w"], hy["conv_b"])
    ho = _long_conv_call(z, x0, filt, asum, hy["bias"])
    mpl = dict(mp)
    mpl["gate"] = gate
    xn = _merge_call(y, bv, lg, ho, p_g, x, mpl)
    return xn, sfin


def kernel(x, c, ctx, c_ctx, w_mod, b_mod, w_in, rwkv_mu, rwkv_w0, rwkv_w2, rwkv_a0, rwkv_a2, rwkv_g2,
           rwkv_k_k, rwkv_k_a, rwkv_r_k, rwkv_lnx_g, rwkv_lnx_b, hy_conv_w, hy_conv_b, hy_f_w1, hy_f_b1,
           hy_f_w2, hy_f_b2, hy_f_w3, hy_f_b3, hy_f_wout, hy_freq, hy_bias, w_branch, w_out, ln_g, ln_b,
           w_router, router_bias, w_gate, w_up, w_down):
    b, n_lat, d = x.shape
    assert b == 1 and d == D_MODEL
    n_ctx = ctx.shape[1]
    depth = w_mod.shape[0]
    hw = RWKV_WIDTH
    xl = x[0]
    xc = ctx[0]

    cc = jnp.concatenate([c[:1], c_ctx[None, :], jnp.zeros((6, d), F32)], axis=0)
    mod = _mod_call(cc, w_mod, b_mod)

    head_of = np.arange(hw) // HEAD_DIM
    G = jnp.asarray((head_of[:, None] == head_of[None, :]).astype(np.float32), dtype=BF)
    bands = jnp.linspace(1e-4, FILTER_BANDS - 1, FILTER_BANDS, dtype=F32)[None, :]
    deltas = jnp.abs(jnp.linspace(HYENA_MIN_DECAY, HYENA_MAX_DECAY, HYENA_WIDTH, dtype=F32))[None, :]
    wr_pad = jnp.pad(w_router, ((0, 0), (0, LANES - N_EXPERTS)))
    rb_pad = jnp.pad(router_bias, (0, LANES - N_EXPERTS))[None, :]
    w_in_bf = w_in.astype(BF)
    w_branch_bf = w_branch.astype(BF)
    w_out_bf = w_out.astype(BF)
    g2_bf = rwkv_g2.astype(BF)
    wg_bf, wu_bf, wd_bf = w_gate.astype(BF), w_up.astype(BF), w_down.astype(BF)

    for l in range(depth):
        last = l == depth - 1
        ml = [mod[l, 0:1, j * d:(j + 1) * d] for j in range(6)]
        mc = [mod[l, 1:2, j * d:(j + 1) * d] for j in range(6)]
        prm = dict(mu=rwkv_mu[l][None, :],
                   w0=rwkv_w0[l].reshape(1, 2 * hw), w2=_blockdiag2(rwkv_w2[l]),
                   a0=rwkv_a0[l].reshape(1, 2 * hw), a2=_blockdiag2(rwkv_a2[l]),
                   k_k=rwkv_k_k[l][None, :], k_a=rwkv_k_a[l][None, :], r_k=rwkv_r_k[l][None, :], G=G)
        w1 = hy_f_w1[l]
        fp = dict(bands=bands, w1t=w1[0:1], w1c=w1[1:1 + FILTER_BANDS], w1s=w1[1 + FILTER_BANDS:],
                  b1=hy_f_b1[l][None, :], w2=hy_f_w2[l], b2=hy_f_b2[l][None, :], w3=hy_f_w3[l],
                  b3=hy_f_b3[l][None, :], wout=hy_f_wout[l], freq=hy_freq[l][None, :], deltas=deltas)
        hy = dict(conv_w=hy_conv_w[l], conv_b=hy_conv_b[l][None, :], bias=hy_bias[l][None, :])
        mp = dict(lnx_g=rwkv_lnx_g[l][None, :], lnx_b=rwkv_lnx_b[l][None, :], g2=g2_bf[l], G=G,
                  w_branch=w_branch_bf[l], w_out=w_out_bf[l], ln_g=ln_g[l, 0][None, :], ln_b=ln_b[l, 0][None, :])
        s0 = jnp.zeros((2, HEAD_DIM, hw), F32)
        xc_new, s_ctx = _mixer(xc, mc, l, w_in_bf[l], prm, fp, hy, mp, s0, latent=False)
        xl, _ = _mixer(xl, ml, l, w_in_bf[l], prm, fp, hy, mp, s_ctx, latent=True)
        lng, lnb = ln_g[l, 1][None, :], ln_b[l, 1][None, :]
        if not last:
            xc = xc_new
            cw_c = _router_call(xc, mc[3], mc[4], wr_pad, rb_pad)
            xc = _moe_call(xc, cw_c, mc[3], mc[4], mc[5], lng, lnb, wg_bf[l], wu_bf[l], wd_bf[l])
        cw_l = _router_call(xl, ml[3], ml[4], wr_pad, rb_pad)
        xl = _moe_call(xl, cw_l, ml[3], ml[4], ml[5], lng, lnb, wg_bf[l], wu_bf[l], wd_bf[l])
    return xl[None]
```

```python
import functools
import math

import numpy as np
import jax
import jax.numpy as jnp
from jax import lax
from jax.experimental import pallas as pl
from jax.experimental.pallas import tpu as pltpu

F32 = jnp.float32
BF = jnp.bfloat16

D_MODEL = 1024
DEPTH = 4
GRID_W = 64
RWKV_WIDTH = 512
HEAD_DIM = 64
RWKV_HEADS = 8
LORA_W = 64
LORA_A = 64
LORA_G = 128
DECAY_SCALE = 0.606531
GN_EPS = 6.4e-4
RWKV_COLS = 3 * RWKV_WIDTH + 2 * LORA_W + 2 * LORA_A + LORA_G
HYENA_WIDTH = 512
HYENA_COLS = 3 * HYENA_WIDTH
FILTER_BANDS = 16
FILTER_HIDDEN = 64
HYENA_MIN_DECAY = math.log(1e-2) / 1.5
HYENA_MAX_DECAY = math.log(1e-2) / 0.3
GATE_COLS = 2 * D_MODEL
PROJ_COLS = RWKV_COLS + HYENA_COLS + GATE_COLS
N_EXPERTS = 16
N_GROUPS = 4
EXPERTS_PER_GROUP = 4
ALPHA = (2 * DEPTH) ** 0.25
LN_EPS = 1e-5

SCAN_CHUNK = 64
LANES = 128
VMEM_LIMIT = 56 * 1024 * 1024

_NN = (((1,), (0,)), ((), ()))
_NT = (((1,), (1,)), ((), ()))
_TN = (((0,), (0,)), ((), ()))


def _sig(x):
    return 1.0 / (1.0 + jnp.exp(-x))


def _parts(a, n):
    out = []
    rem = a
    for i in range(n):
        hi = rem.astype(BF)
        out.append(hi)
        if i + 1 < n:
            rem = rem - hi.astype(F32)
    return out


def _mm(a, b, dn=_NN, passes=1):
    n = {1: 1, 3: 2, 6: 3}[passes]
    pa = _parts(a, n)
    pb = _parts(b, n)
    acc = None
    for i in range(n):
        for j in range(n - i):
            t = lax.dot_general(pa[i], pb[j], dn, preferred_element_type=F32)
            acc = t if acc is None else acc + t
    return acc


def _mm_rx(a, b_exact, n=3, dn=_NN):
    acc = None
    for p in _parts(a, n):
        t = lax.dot_general(p, b_exact, dn, preferred_element_type=F32)
        acc = t if acc is None else acc + t
    return acc


def _mm_lx(a_exact, b, n=3, dn=_NN):
    acc = None
    for p in _parts(b, n):
        t = lax.dot_general(a_exact, p, dn, preferred_element_type=F32)
        acc = t if acc is None else acc + t
    return acc


def _cparams(sem, vmem=VMEM_LIMIT):
    return pltpu.CompilerParams(dimension_semantics=sem, vmem_limit_bytes=vmem)


def _full(shape):
    nd = len(shape)
    return pl.BlockSpec(shape, lambda *_: (0,) * nd)


def _mod_kernel(c_ref, w_ref, b_ref, o_ref):
    cc = c_ref[...]
    s = cc * _sig(cc)
    o_ref[...] = jnp.dot(s.astype(BF), w_ref[...].astype(BF), preferred_element_type=F32) + b_ref[...]


def _mod_call(cc, w_mod, b_mod):
    depth, d, n6 = w_mod.shape
    tn = 1536
    return pl.pallas_call(
        _mod_kernel,
        out_shape=jax.ShapeDtypeStruct((depth, 8, n6), F32),
        grid=(depth, n6 // tn),
        in_specs=[
            pl.BlockSpec((8, d), lambda l, j: (0, 0)),
            pl.BlockSpec((None, d, tn), lambda l, j: (l, 0, j)),
            pl.BlockSpec((None, 1, tn), lambda l, j: (l, 0, j)),
        ],
        out_specs=pl.BlockSpec((None, 8, tn), lambda l, j: (l, 0, j)),
        compiler_params=_cparams(("arbitrary", "arbitrary")),
        name="mod",
    )(cc, w_mod, b_mod.reshape(depth, 1, n6))


def _inproj_kernel(x_ref, sh_ref, sc_ref, w_ref, pr_ref, ph_ref, pg_ref):
    xm = (x_ref[...] * (1.0 + sc_ref[...]) + sh_ref[...]).astype(BF)
    pr_ref[...] = jnp.dot(xm, w_ref[:, :RWKV_COLS], preferred_element_type=F32)
    ph_ref[...] = jnp.dot(xm, w_ref[:, RWKV_COLS:RWKV_COLS + HYENA_COLS], preferred_element_type=F32)
    pg_ref[...] = jnp.dot(xm, w_ref[:, RWKV_COLS + HYENA_COLS:], preferred_element_type=F32)


def _inproj_call(x, sh, sc, w_bf):
    L, d = x.shape
    tm = min(L, 256)
    row = lambda i: (i, 0)
    return pl.pallas_call(
        _inproj_kernel,
        out_shape=(jax.ShapeDtypeStruct((L, RWKV_COLS), F32),
                   jax.ShapeDtypeStruct((L, HYENA_COLS), F32),
                   jax.ShapeDtypeStruct((L, GATE_COLS), F32)),
        grid=(L // tm,),
        in_specs=[pl.BlockSpec((tm, d), row), _full((1, d)), _full((1, d)), _full((d, PROJ_COLS))],
        out_specs=(pl.BlockSpec((tm, RWKV_COLS), row), pl.BlockSpec((tm, HYENA_COLS), row),
                   pl.BlockSpec((tm, GATE_COLS), row)),
        compiler_params=_cparams(("arbitrary",)),
        name="inproj",
    )(x, sh, sc, w_bf)


def _prepare_kernel(*refs, latent, nblk):
    if latent:
        p_ref, up_ref, dn_ref = refs[:3]
        refs = refs[3:]
    else:
        p_ref = refs[0]
        refs = refs[1:]
    (mu_ref, w0_ref, w2_ref, a0_ref, a2_ref, kk_ref, ka_ref, rk_ref, g_ref,
     r_out, v_out, ka_out, lw_out, k_out, kb_out, lg_out, bv_out) = refs
    i = pl.program_id(0)
    p = p_ref[...]
    tb, w = p.shape
    row = lax.broadcasted_iota(jnp.int32, (tb, 1), 0)
    col = lax.broadcasted_iota(jnp.int32, (1, w), 1)
    if latent:
        gw = jnp.bitwise_and(row, GRID_W - 1)
        left = jnp.where(gw == 0, 0.0, pltpu.roll(p, 1, 0))
        right = jnp.where(gw == GRID_W - 1, 0.0, pltpu.roll(p, tb - 1, 0))
        upv = jnp.where(i == 0, 0.0, up_ref[...])
        dnv = jnp.where(i == nblk - 1, 0.0, dn_ref[...])
        if tb > GRID_W:
            up = jnp.concatenate([upv, p[:tb - GRID_W]], axis=0)
            down = jnp.concatenate([p[GRID_W:], dnv], axis=0)
        else:
            up, down = upv, dnv
        q = w // 4
        sh = jnp.where(col < q, left, jnp.where(col < 2 * q, right, jnp.where(col < 3 * q, up, down)))
    else:
        prev = jnp.where(row == 0, 0.0, pltpu.roll(p, 1, 0))
        nxt = jnp.where(row == tb - 1, 0.0, pltpu.roll(p, tb - 1, 0))
        sh = jnp.where(col < w // 2, prev, nxt)
    pm = p + (sh - p) * mu_ref[...]
    hw = RWKV_WIDTH
    r = pm[:, :hw]
    k = pm[:, hw:2 * hw]
    v = pm[:, 2 * hw:3 * hw]
    lw_in = jnp.tanh(pm[:, 3 * hw:3 * hw + 2 * LORA_W])
    la_in = pm[:, 3 * hw + 2 * LORA_W:3 * hw + 2 * LORA_W + 2 * LORA_A]
    lg = pm[:, 3 * hw + 2 * LORA_W + 2 * LORA_A:]
    lw = -DECAY_SCALE * _sig(w0_ref[...] + _mm(lw_in, w2_ref[...], passes=3))
    a = _sig(a0_ref[...] + _mm(la_in, a2_ref[...], passes=3))
    g = g_ref[...]
    kkr = k * kk_ref[...]
    nrm = jnp.sqrt(_mm_rx(kkr * kkr, g, 3))
    kk = kkr / jnp.maximum(nrm, 1e-12)
    k_a = ka_ref[...]
    k_f = k * (1.0 + (a[:, :hw] - 1.0) * k_a)
    k_b = k * (1.0 + (a[:, hw:] - 1.0) * k_a)
    bonus = _mm_rx(r * (k_f + k_b) * rk_ref[...], g, 3)
    r_out[...] = r
    v_out[...] = v
    ka_out[...] = -kk
    lw_out[0] = lw[:, :hw]
    lw_out[1] = lw[:, hw:]
    k_out[0] = k_f
    k_out[1] = k_b
    kb_out[0] = kk * a[:, :hw]
    kb_out[1] = kk * a[:, hw:]
    lg_out[...] = lg
    bv_out[...] = bonus * v


def _prepare_call(p_r, prm, latent):
    L, w = p_r.shape
    hw = RWKV_WIDTH
    tb = 256 if latent else L
    nblk = L // tb
    row = lambda i: (i, 0)
    row3 = lambda i: (0, i, 0)
    in_specs = [pl.BlockSpec((tb, w), row)]
    args = [p_r]
    if latent:
        per = tb // GRID_W
        nrow = L // GRID_W
        in_specs += [
            pl.BlockSpec((GRID_W, w), lambda i: (jnp.maximum(i * per - 1, 0), 0)),
            pl.BlockSpec((GRID_W, w), lambda i: (jnp.minimum((i + 1) * per, nrow - 1), 0)),
        ]
        args += [p_r, p_r]
    names = ("mu", "w0", "w2", "a0", "a2", "k_k", "k_a", "r_k", "G")
    for nm in names:
        in_specs.append(_full(prm[nm].shape))
        args.append(prm[nm])
    sd = jax.ShapeDtypeStruct
    out_shape = (sd((L, hw), F32), sd((L, hw), F32), sd((L, hw), F32),
                 sd((2, L, hw), F32), sd((2, L, hw), F32), sd((2, L, hw), F32),
                 sd((L, LORA_G), F32), sd((L, hw), F32))
    out_specs = (pl.BlockSpec((tb, hw), row), pl.BlockSpec((tb, hw), row), pl.BlockSpec((tb, hw), row),
                 pl.BlockSpec((2, tb, hw), row3), pl.BlockSpec((2, tb, hw), row3), pl.BlockSpec((2, tb, hw), row3),
                 pl.BlockSpec((tb, LORA_G), row), pl.BlockSpec((tb, hw), row))
    return pl.pallas_call(
        functools.partial(_prepare_kernel, latent=latent, nblk=nblk),
        out_shape=out_shape, grid=(nblk,), in_specs=in_specs, out_specs=out_specs,
        compiler_params=_cparams(("arbitrary",)),
        name="rwkv_prepare_lat" if latent else "rwkv_prepare_ctx",
    )(*args)


P_SCORE = 3
P_TINV = 3
P_APPLY = 3
P_STATE = 3


def _scan_kernel(rf_ref, vf_ref, kaf_ref, rb_ref, vb_ref, kab_ref, lwf_ref, kf_ref, kbf_ref,
                 lwb_ref, kbk_ref, kbb_ref, s0_ref, yf_ref, yb_ref, sfin_ref, s_scr, *, nchunk):
    c = pl.program_id(0)
    C = SCAN_CHUNK
    hd = HEAD_DIM
    nh = RWKV_HEADS

    @pl.when(c == 0)
    def _():
        s_scr[...] = s0_ref[...]

    row = lax.broadcasted_iota(jnp.int32, (C, C), 0)
    col = lax.broadcasted_iota(jnp.int32, (C, C), 1)
    eye = jnp.where(row == col, 1.0, 0.0)
    lvl_masks = []
    for sh in range(int(math.log2(C))):
        same2 = jnp.right_shift(row, sh + 1) == jnp.right_shift(col, sh + 1)
        same1 = jnp.right_shift(row, sh) == jnp.right_shift(col, sh)
        lvl_masks.append(jnp.logical_and(same2, jnp.logical_not(same1)))

    dirs = []
    for d, (r_ref, v_ref, ka_ref, lw_ref, k_ref, kb_ref) in enumerate((
            (rf_ref, vf_ref, kaf_ref, lwf_ref, kf_ref, kbf_ref),
            (rb_ref, vb_ref, kab_ref, lwb_ref, kbk_ref, kbb_ref))):
        inc = (row >= col) if d == 0 else (row <= col)
        strict = (row > col) if d == 0 else (row < col)
        lw = lw_ref[...]
        lc = _mm_lx(jnp.where(inc, 1.0, 0.0).astype(BF), lw, 3)
        e_neg = jnp.exp(-lc)
        e_tot = jnp.exp(jnp.sum(lw, axis=0, keepdims=True))
        kbn = kb_ref[...] * e_neg
        kkn = k_ref[...] * e_neg
        dirs.append(dict(inc=inc, strict=strict, v=v_ref[...], aq=ka_ref[...] * jnp.exp(lc - lw),
                         rq=r_ref[...] * jnp.exp(lc), kbn=kbn, kkn=kkn, kbp=kbn * e_tot, kkp=kkn * e_tot,
                         e_tot=e_tot, s=s_scr[d]))

    units = [(d, h) for h in range(nh) for d in range(2)]
    hs = lambda arr, h: arr[:, h * hd:(h + 1) * hd]
    sc = {}
    for (d, h) in units:
        D = dirs[d]
        sc[d, h] = _mm(jnp.concatenate([hs(D["aq"], h), hs(D["rq"], h)], axis=0),
                       jnp.concatenate([hs(D["kbn"], h), hs(D["kkn"], h)], axis=0), _NT, P_SCORE)
    a_ab, t, x = {}, {}, {}
    for u in units:
        a_ab[u] = jnp.where(dirs[u[0]]["strict"], sc[u][:C, :C], 0.0)
        t[u] = eye + jnp.where(lvl_masks[0], a_ab[u], 0.0)
    for u in units:
        D = dirs[u[0]]
        lhs = jnp.concatenate([jnp.where(D["strict"], sc[u][:C, C:], 0.0),
                               jnp.where(D["inc"], sc[u][C:, C:], 0.0)], axis=0)
        x[u] = _mm(lhs, hs(D["v"], u[1]), _NN, P_APPLY)
    for m in lvl_masks[1:]:
        tmp = {u: _mm(jnp.where(m, a_ab[u], 0.0), t[u], _NN, P_TINV) for u in units}
        for u in units:
            t[u] = t[u] + _mm(t[u], tmp[u], _NN, P_TINV)
    z, gm, uu, yy, sn = {}, {}, {}, {}, {}
    for u in units:
        z[u] = _mm(t[u], jnp.concatenate([hs(dirs[u[0]]["aq"], u[1]), x[u][:C]], axis=1), _NN, P_APPLY)
    for u in units:
        D = dirs[u[0]]
        gm[u] = _mm(jnp.concatenate([z[u][:, :hd], hs(D["rq"], u[1])], axis=0), hs(D["s"], u[1]), _NT, P_STATE)
        uu[u] = gm[u][:C] + z[u][:, hd:]
    for u in units:
        D = dirs[u[0]]
        b_rb = jnp.where(D["inc"], sc[u][C:, :C], 0.0)
        yy[u] = gm[u][C:] + _mm(b_rb, uu[u], _NN, P_APPLY) + x[u][C:]
        sn[u] = hs(D["s"], u[1]) * hs(D["e_tot"], u[1]) + _mm(
            jnp.concatenate([uu[u], hs(D["v"], u[1])], axis=0),
            jnp.concatenate([hs(D["kbp"], u[1]), hs(D["kkp"], u[1])], axis=0), _TN, P_STATE)
    yf_ref[...] = jnp.concatenate([yy[0, h] for h in range(nh)], axis=1)
    yb_ref[...] = jnp.concatenate([yy[1, h] for h in range(nh)], axis=1)
    s_scr[0] = jnp.concatenate([sn[0, h] for h in range(nh)], axis=1)
    s_scr[1] = jnp.concatenate([sn[1, h] for h in range(nh)], axis=1)

    @pl.when(c == nchunk - 1)
    def _():
        sfin_ref[...] = s_scr[...]


def _scan_call(r, v, ka, lw, k, kb, s0):
    L, hw = r.shape
    C = SCAN_CHUNK
    nchunk = L // C
    sh_f = pl.BlockSpec((C, hw), lambda c: (c, 0))
    sh_b = pl.BlockSpec((C, hw), lambda c: (nchunk - 1 - c, 0))
    pd_f = pl.BlockSpec((None, C, hw), lambda c: (0, c, 0))
    pd_b = pl.BlockSpec((None, C, hw), lambda c: (1, nchunk - 1 - c, 0))
    state = _full((2, HEAD_DIM, hw))
    return pl.pallas_call(
        functools.partial(_scan_kernel, nchunk=nchunk),
        out_shape=(jax.ShapeDtypeStruct((L, hw), F32), jax.ShapeDtypeStruct((L, hw), F32),
                   jax.ShapeDtypeStruct((2, HEAD_DIM, hw), F32)),
        grid=(nchunk,),
        in_specs=[sh_f, sh_f, sh_f, sh_b, sh_b, sh_b, pd_f, pd_f, pd_f, pd_b, pd_b, pd_b, state],
        out_specs=(sh_f, sh_b, state),
        scratch_shapes=[pltpu.VMEM((2, HEAD_DIM, hw), F32)],
        compiler_params=_cparams(("arbitrary",)),
        name="delta_scan",
    )(r, v, ka, r, v, ka, lw, k, kb, lw, k, kb, s0)


def _hfilt_kernel(bands_ref, w1t_ref, w1c_ref, w1s_ref, b1_ref, w2_ref, b2_ref, w3_ref, b3_ref, wo_ref,
                  fr_ref, dl_ref, f_ref, asum_ref, *, n, rb):
    i = pl.program_id(0)
    pos = (i * rb + lax.broadcasted_iota(jnp.int32, (rb, 1), 0)).astype(F32)
    t = pos / float(max(n - 1, 1))
    ang = ((2.0 * math.pi / n) * pos) * bands_ref[...]
    fr = fr_ref[...]
    h = t * w1t_ref[...] + _mm(jnp.cos(ang), w1c_ref[...], passes=3) + _mm(-jnp.sin(ang), w1s_ref[...], passes=3)
    h = jnp.sin(fr * (h + b1_ref[...]))
    h = jnp.sin(fr * (_mm(h, w2_ref[...], passes=3) + b2_ref[...]))
    h = jnp.sin(fr * (_mm(h, w3_ref[...], passes=3) + b3_ref[...]))
    filt = _mm(h, wo_ref[...], passes=3)
    dist = jnp.abs(pos - float(n // 2)) * (2.0 / n)
    filt = filt * jnp.exp(-dist * dl_ref[...])
    f_ref[...] = filt

    @pl.when(i == 0)
    def _():
        asum_ref[...] = jnp.zeros_like(asum_ref)

    asum_ref[...] += jnp.sum(jnp.abs(filt), axis=0, keepdims=True)


def _hfilt_call(n, fp):
    rb = min(n, 512)
    names = ("bands", "w1t", "w1c", "w1s", "b1", "w2", "b2", "w3", "b3", "wout", "freq", "deltas")
    args = [fp[nm] for nm in names]
    return pl.pallas_call(
        functools.partial(_hfilt_kernel, n=n, rb=rb),
        out_shape=(jax.ShapeDtypeStruct((n, HYENA_WIDTH), F32), jax.ShapeDtypeStruct((1, HYENA_WIDTH), F32)),
        grid=(n // rb,),
        in_specs=[_full(a.shape) for a in args],
        out_specs=(pl.BlockSpec((rb, HYENA_WIDTH), lambda i: (i, 0)), _full((1, HYENA_WIDTH))),
        compiler_params=_cparams(("arbitrary",)),
        name="hyena_filter",
    )(*args)


def _hconv3_kernel(p_ref, pv_ref, nx_ref, cw_ref, cb_ref, z_ref, x0_ref, *, nblk):
    i = pl.program_id(0)
    p = p_ref[...]
    tb = p.shape[0]
    row = lax.broadcasted_iota(jnp.int32, (tb, 1), 0)
    prev_row = jnp.where(i == 0, 0.0, pv_ref[7:8, :])
    next_row = jnp.where(i == nblk - 1, 0.0, nx_ref[0:1, :])
    sp = jnp.where(row == 0, prev_row, pltpu.roll(p, 1, 0))
    sn = jnp.where(row == tb - 1, next_row, pltpu.roll(p, tb - 1, 0))
    u = sp * cw_ref[0:1, :] + p * cw_ref[1:2, :] + sn * cw_ref[2:3, :] + cb_ref[...]
    hw = HYENA_WIDTH
    z_ref[...] = u[:, 2 * hw:] * u[:, hw:2 * hw]
    x0_ref[...] = u[:, :hw]


def _hconv3_call(p_h, cw, cb):
    L, w = p_h.shape
    tb = min(L, 256)
    nblk = L // tb
    per = tb // 8
    row = lambda i: (i, 0)
    return pl.pallas_call(
        functools.partial(_hconv3_kernel, nblk=nblk),
        out_shape=(jax.ShapeDtypeStruct((L, HYENA_WIDTH), F32), jax.ShapeDtypeStruct((L, HYENA_WIDTH), F32)),
        grid=(nblk,),
        in_specs=[pl.BlockSpec((tb, w), row),
                  pl.BlockSpec((8, w), lambda i: (jnp.maximum(i * per - 1, 0), 0)),
                  pl.BlockSpec((8, w), lambda i: (jnp.minimum((i + 1) * per, L // 8 - 1), 0)),
                  _full(cw.shape), _full(cb.shape)],
        out_specs=(pl.BlockSpec((tb, HYENA_WIDTH), row), pl.BlockSpec((tb, HYENA_WIDTH), row)),
        compiler_params=_cparams(("arbitrary",)),
        name="hyena_conv3",
    )(p_h, p_h, p_h, cw, cb)


P_DFT = 3


def _dft1_kernel(m_ref, x_ref, o_ref):
    o_ref[...] = _mm(m_ref[...], x_ref[...], _NN, P_DFT)


def _dft1_call(m1, x2d):
    rows, kdim = m1.shape
    _, cols = x2d.shape
    cb = min(cols, 2048)
    return pl.pallas_call(
        _dft1_kernel,
        out_shape=jax.ShapeDtypeStruct((rows, cols), F32),
        grid=(cols // cb,),
        in_specs=[_full(m1.shape), pl.BlockSpec((kdim, cb), lambda j: (0, j))],
        out_specs=pl.BlockSpec((rows, cb), lambda j: (0, j)),
        compiler_params=_cparams(("arbitrary",)),
        name="dft_stage1",
    )(m1, x2d)


def _dft2_matrix(fc_ref, fs_ref, tc_ref, ts_ref):
    fc, fs = fc_ref[...], fs_ref[...]
    tc, ts = tc_ref[...], ts_ref[...]
    fre = fc * tc - fs * ts
    fim = -(fc * ts + fs * tc)
    return jnp.concatenate([jnp.concatenate([fre, -fim], axis=1),
                            jnp.concatenate([fim, fre], axis=1)], axis=0)


def _dft2_spec_kernel(fc_ref, fs_ref, tc_ref, ts_ref, a_ref, h_ref):
    big = _dft2_matrix(fc_ref, fs_ref, tc_ref, ts_ref)
    n2 = a_ref.shape[1]
    x = _mm(big, jnp.concatenate([a_ref[0], a_ref[1]], axis=0), _NN, P_DFT)
    h_ref[0] = x[:n2]
    h_ref[1] = x[n2:]


def _dft2_conv_kernel(fc_ref, fs_ref, tc_ref, ts_ref, a_ref, h_ref, b_ref):
    big = _dft2_matrix(fc_ref, fs_ref, tc_ref, ts_ref)
    n2 = a_ref.shape[1]
    x = _mm(big, jnp.concatenate([a_ref[0], a_ref[1]], axis=0), _NN, P_DFT)
    xr, xi = x[:n2], x[n2:]
    hr, hi = h_ref[0], h_ref[1]
    y = jnp.concatenate([xr * hr - xi * hi, xr * hi + xi * hr], axis=0)
    bb = _mm(big, y, _TN, P_DFT)
    b_ref[0] = bb[:n2]
    b_ref[1] = bb[n2:]


def _dft2_call(consts, a4, h4=None):
    _, n1, n2, c = a4.shape
    blk = pl.BlockSpec((2, None, n2, c), lambda q: (0, q, 0, 0))
    tw = pl.BlockSpec((None, 1, n2), lambda q: (q, 0, 0))
    in_specs = [_full((n2, n2)), _full((n2, n2)), tw, tw, blk]
    args = [consts["fc"], consts["fs"], consts["twc"], consts["tws"], a4]
    kern = _dft2_spec_kernel
    if h4 is not None:
        in_specs.append(blk)
        args.append(h4)
        kern = _dft2_conv_kernel
    return pl.pallas_call(
        kern,
        out_shape=jax.ShapeDtypeStruct(a4.shape, F32),
        grid=(n1,),
        in_specs=in_specs,
        out_specs=blk,
        compiler_params=_cparams(("arbitrary",)),
        name="dft_stage2_spec" if h4 is None else "dft_stage2_conv",
    )(*args)


def _dft3_kernel(m_ref, b_ref, z_ref, x0_ref, bias_ref, asum_ref, o_ref):
    y = _mm(m_ref[...], b_ref[...], _NN, P_DFT)
    z = z_ref[...]
    o_ref[...] = (y / asum_ref[...] + z * bias_ref[...]) * x0_ref[...]


def _dft3_call(m3, b2d, z2d, x02d, bias_t, asum_t):
    rows, kdim = m3.shape
    _, cols = b2d.shape
    cb = bias_t.shape[1]
    colb = lambda j: (0, j)
    return pl.pallas_call(
        _dft3_kernel,
        out_shape=jax.ShapeDtypeStruct((rows, cols), F32),
        grid=(cols // cb,),
        in_specs=[_full(m3.shape), pl.BlockSpec((kdim, cb), colb), pl.BlockSpec((rows, cb), colb),
                  pl.BlockSpec((rows, cb), colb), _full((1, cb)), _full((1, cb))],
        out_specs=pl.BlockSpec((rows, cb), colb),
        compiler_params=_cparams(("arbitrary",)),
        name="dft_stage3",
    )(m3, b2d, z2d, x02d, bias_t, asum_t)


def _conv_direct_kernel(m1_ref, m3_ref, z_ref, f_ref, x0_ref, bias_ref, asum_ref, o_ref):
    m1 = m1_ref[...]
    z = z_ref[...]
    a = _mm(m1, z, _NN, P_DFT)
    h = _mm(m1, f_ref[...], _NN, P_DFT)
    nn = a.shape[0] // 2
    ar, ai, hr, hi = a[:nn], a[nn:], h[:nn], h[nn:]
    y = jnp.concatenate([ar * hr - ai * hi, ar * hi + ai * hr], axis=0)
    out = _mm(m3_ref[...], y, _NN, P_DFT)
    o_ref[...] = (out / asum_ref[...] + z * bias_ref[...]) * x0_ref[...]


def _conv_direct_call(m1, m3, z, filt, x0, bias, asum):
    args = (m1, m3, z, filt, x0, bias, asum)
    return pl.pallas_call(
        _conv_direct_kernel,
        out_shape=jax.ShapeDtypeStruct(z.shape, F32),
        grid=(1,),
        in_specs=[_full(a.shape) for a in args],
        out_specs=_full(z.shape),
        compiler_params=_cparams(("arbitrary",)),
        name="long_conv_direct",
    )(*args)


DIRECT_CONV_MAX = 256


@functools.lru_cache(maxsize=None)
def _dft_consts_np(n):
    big_n = 2 * n
    if n <= DIRECT_CONV_MAX:
        n1 = big_n
    else:
        n1 = 256 if n >= 8192 else 64
    n2 = big_n // n1
    k1 = np.arange(n1)[:, None].astype(np.float64)
    j1 = np.arange(n1 // 2)[None, :].astype(np.float64)
    ang1 = 2.0 * np.pi * k1 * j1 / n1
    m1 = np.concatenate([np.cos(ang1), -np.sin(ang1)], axis=0)
    o1 = (n1 // 4 + np.arange(n1 // 2))[:, None].astype(np.float64)
    q1 = np.arange(n1)[None, :].astype(np.float64)
    ang3 = 2.0 * np.pi * o1 * q1 / n1
    m3 = np.concatenate([np.cos(ang3), -np.sin(ang3)], axis=1) / big_n
    k2 = np.arange(n2)[:, None].astype(np.float64)
    j2 = np.arange(n2)[None, :].astype(np.float64)
    ang2 = 2.0 * np.pi * k2 * j2 / n2
    angt = 2.0 * np.pi * np.arange(n1)[:, None].astype(np.float64) * j2 / big_n
    f = lambda a: np.asarray(a, np.float32)
    return dict(n1=n1, n2=n2, m1=f(m1), m3=f(m3), fc=f(np.cos(ang2)), fs=f(np.sin(ang2)),
                twc=f(np.cos(angt))[:, None, :], tws=f(np.sin(angt))[:, None, :])


def _long_conv_call(z, x0, filt, asum, bias):
    n, c = z.shape
    cn = _dft_consts_np(n)
    n1, n2 = cn["n1"], cn["n2"]
    consts = {kk: jnp.asarray(vv) for kk, vv in cn.items() if kk not in ("n1", "n2")}
    if n2 == 1:
        return _conv_direct_call(consts["m1"], consts["m3"], z, filt, x0, bias, asum)
    cols = n2 * c
    cb = min(cols, 2048)
    a_f = _dft1_call(consts["m1"], filt.reshape(n1 // 2, cols)).reshape(2, n1, n2, c)
    h4 = _dft2_call(consts, a_f)
    a_z = _dft1_call(consts["m1"], z.reshape(n1 // 2, cols)).reshape(2, n1, n2, c)
    b4 = _dft2_call(consts, a_z, h4)
    out = _dft3_call(consts["m3"], b4.reshape(2 * n1, cols), z.reshape(n1 // 2, cols), x0.reshape(n1 // 2, cols),
                     jnp.tile(bias, (1, cb // c)), jnp.tile(asum, (1, cb // c)))
    return out.reshape(n, c)


def _layer_norm(x, g, b):
    mu = jnp.mean(x, axis=-1, keepdims=True)
    xc = x - mu
    var = jnp.mean(xc * xc, axis=-1, keepdims=True)
    return xc * lax.rsqrt(var + LN_EPS) * g + b


def _merge_kernel(yf_ref, yb_ref, bv_ref, lg_ref, ho_ref, pg_ref, x_ref, gx_ref, bx_ref, g2_ref, g_ref,
                  wb_ref, wo_ref, gate_ref, lng_ref, lnb_ref, o_ref):
    g = g_ref[...]
    ys = yf_ref[...] + yb_ref[...]
    inv_hd = 1.0 / HEAD_DIM
    mu = _mm_rx(ys, g, 3) * inv_hd
    dd = ys - mu
    var = _mm_rx(dd * dd, g, 3) * inv_hd
    yn = dd * lax.rsqrt(var + GN_EPS) * gx_ref[...] + bx_ref[...]
    gate_r = jnp.dot(_sig(lg_ref[...]).astype(BF), g2_ref[...], preferred_element_type=F32)
    ro = (yn + bv_ref[...]) * gate_r
    br = jnp.dot(ro.astype(BF), wb_ref[0], preferred_element_type=F32)
    bh = jnp.dot(ho_ref[...].astype(BF), wb_ref[1], preferred_element_type=F32)
    sg = _sig(pg_ref[...])
    m = sg[:, :D_MODEL] * br + sg[:, D_MODEL:] * bh
    mix = jnp.dot(m.astype(BF), wo_ref[...], preferred_element_type=F32)
    o_ref[...] = _layer_norm(ALPHA * x_ref[...] + gate_ref[...] * mix, lng_ref[...], lnb_ref[...])


def _merge_call(yf, yb, bv, lg, ho, pg, x, mp):
    L, d = x.shape
    hw = RWKV_WIDTH
    tb = min(L, 256)
    row = lambda i: (i, 0)
    names = ("lnx_g", "lnx_b", "g2", "G", "w_branch", "w_out", "gate", "ln_g", "ln_b")
    pargs = [mp[nm] for nm in names]
    return pl.pallas_call(
        _merge_kernel,
        out_shape=jax.ShapeDtypeStruct((L, d), F32),
        grid=(L // tb,),
        in_specs=[pl.BlockSpec((tb, hw), row), pl.BlockSpec((tb, hw), row), pl.BlockSpec((tb, hw), row),
                  pl.BlockSpec((tb, LORA_G), row), pl.BlockSpec((tb, hw), row),
                  pl.BlockSpec((tb, GATE_COLS), row), pl.BlockSpec((tb, d), row)]
                 + [_full(a.shape) for a in pargs],
        out_specs=pl.BlockSpec((tb, d), row),
        compiler_params=_cparams(("arbitrary",)),
        name="merge_postnorm",
    )(yf, yb, bv, lg, ho, pg, x, *pargs)


def _router_kernel(x_ref, sh_ref, sc_ref, wrt_ref, rb_ref, cw_ref):
    hf = x_ref[...] * (1.0 + sc_ref[...]) + sh_ref[...]
    lt = _mm(wrt_ref[...], hf, _NT, 6)
    rid = lax.broadcasted_iota(jnp.int32, (LANES, 1), 0)
    valid = rid < N_EXPERTS
    lg = jnp.where(valid, lt, -jnp.inf)
    mx = jnp.max(lg, axis=0, keepdims=True)
    ex = jnp.where(valid, jnp.exp(lg - mx), 0.0)
    scores = ex / jnp.sum(ex, axis=0, keepdims=True)
    sel = scores + rb_ref[...]
    s = [sel[e:e + 1, :] for e in range(N_EXPERTS)]
    p = [scores[e:e + 1, :] for e in range(N_EXPERTS)]
    gs = []
    for gi in range(N_GROUPS):
        mem = s[gi * EXPERTS_PER_GROUP:(gi + 1) * EXPERTS_PER_GROUP]
        best = None
        for a in range(EXPERTS_PER_GROUP):
            for b in range(a + 1, EXPERTS_PER_GROUP):
                pair = mem[a] + mem[b]
                best = pair if best is None else jnp.maximum(best, pair)
        gs.append(best)
    bg = jnp.where((gs[0] >= gs[1]) & (gs[0] >= gs[2]) & (gs[0] >= gs[3]), 0,
                   jnp.where((gs[1] >= gs[2]) & (gs[1] >= gs[3]), 1, jnp.where(gs[2] >= gs[3], 2, 3)))
    chosen = []
    for e in range(N_EXPERTS):
        gi = e // EXPERTS_PER_GROUP
        beats = None
        for j in range(gi * EXPERTS_PER_GROUP, (gi + 1) * EXPERTS_PER_GROUP):
            if j == e:
                continue
            cond = (s[j] >= s[e]) if j < e else (s[j] > s[e])
            cnt = jnp.where(cond, 1.0, 0.0)
            beats = cnt if beats is None else beats + cnt
        chosen.append((bg == gi) & (beats < 1.5))
    den = None
    for e in range(N_EXPERTS):
        t = jnp.where(chosen[e], p[e], 0.0)
        den = t if den is None else den + t
    out = jnp.zeros(lt.shape, F32)
    for e in range(N_EXPERTS):
        out = jnp.where((rid == e) & chosen[e], p[e] / den, out)
    cw_ref[...] = out.T


def _router_call(x, sh, sc, wrt_pad, rb_pad):
    L, d = x.shape
    tb = min(L, 256)
    row = lambda i: (i, 0)
    return pl.pallas_call(
        _router_kernel,
        out_shape=jax.ShapeDtypeStruct((L, LANES), F32),
        grid=(L // tb,),
        in_specs=[pl.BlockSpec((tb, d), row), _full((1, d)), _full((1, d)), _full(wrt_pad.shape), _full(rb_pad.shape)],
        out_specs=pl.BlockSpec((tb, LANES), row),
        compiler_params=_cparams(("arbitrary",)),
        name="router",
    )(x, sh, sc, wrt_pad, rb_pad)


def _moe_kernel(x_ref, cw_ref, sh_ref, sc_ref, gate_ref, lng_ref, lnb_ref, wg_ref, wu_ref, wd_ref, o_ref,
                hf_scr, acc_scr):
    e = pl.program_id(1)

    @pl.when(e == 0)
    def _():
        hf_scr[...] = (x_ref[...] * (1.0 + sc_ref[...]) + sh_ref[...]).astype(BF)
        acc_scr[...] = jnp.zeros_like(acc_scr)

    hf = hf_scr[...]
    hg = jnp.dot(hf, wg_ref[...], preferred_element_type=F32)
    hu = jnp.dot(hf, wu_ref[...], preferred_element_type=F32)
    act = (hg * _sig(hg) * hu).astype(BF)
    ye = jnp.dot(act, wd_ref[...], preferred_element_type=F32)
    lane = lax.broadcasted_iota(jnp.int32, (1, LANES), 1)
    we = jnp.sum(jnp.where(lane == e, cw_ref[...], 0.0), axis=1, keepdims=True)
    acc_scr[...] += we * ye

    @pl.when(e == N_EXPERTS - 1)
    def _():
        o_ref[...] = _layer_norm(ALPHA * x_ref[...] + gate_ref[...] * acc_scr[...], lng_ref[...], lnb_ref[...])


def _moe_call(x, cw, sh, sc, gate, lng, lnb, wg, wu, wd):
    L, d = x.shape
    tm = min(L, 512)
    row = lambda i, e: (i, 0)
    wspec = pl.BlockSpec((None, d, d), lambda i, e: (e, 0, 0))
    vec = pl.BlockSpec((1, d), lambda i, e: (0, 0))
    return pl.pallas_call(
        _moe_kernel,
        out_shape=jax.ShapeDtypeStruct((L, d), F32),
        grid=(L // tm, N_EXPERTS),
        in_specs=[pl.BlockSpec((tm, d), row), pl.BlockSpec((tm, LANES), row), vec, vec, vec, vec, vec,
                  wspec, wspec, wspec],
        out_specs=pl.BlockSpec((tm, d), row),
        scratch_shapes=[pltpu.VMEM((tm, d), BF), pltpu.VMEM((tm, d), F32)],
        compiler_params=_cparams(("arbitrary", "arbitrary")),
        name="moe_ffn",
    )(x, cw, sh, sc, gate, lng, lnb, wg, wu, wd)


def _blockdiag2(m):
    z = jnp.zeros_like(m[0])
    return jnp.concatenate([jnp.concatenate([m[0], z], axis=1), jnp.concatenate([z, m[1]], axis=1)], axis=0)


def _mixer(x, mod_row, l, w_in_bf, prm, fp, hy, mp, s0, latent):
    L = x.shape[0]
    sh, sc, gate = mod_row[0], mod_row[1], mod_row[2]
    p_r, p_h, p_g = _inproj_call(x, sh, sc, w_in_bf)
    r, v, ka, lw, k, kb, lg, bv = _prepare_call(p_r, prm, latent)
    yf, yb, sfin = _scan_call(r, v, ka, lw, k, kb, s0)
    filt, asum = _hfilt_call(L, fp)
    z, x0 = _hconv3_call(p_h, hy["conv_w"], hy["conv_b"])
    ho = _long_conv_call(z, x0, filt, asum, hy["bias"])
    mpl = dict(mp)
    mpl["gate"] = gate
    xn = _merge_call(yf, yb, bv, lg, ho, p_g, x, mpl)
    return xn, sfin


def kernel(x, c, ctx, c_ctx, w_mod, b_mod, w_in, rwkv_mu, rwkv_w0, rwkv_w2, rwkv_a0, rwkv_a2, rwkv_g2,
           rwkv_k_k, rwkv_k_a, rwkv_r_k, rwkv_lnx_g, rwkv_lnx_b, hy_conv_w, hy_conv_b, hy_f_w1, hy_f_b1,
           hy_f_w2, hy_f_b2, hy_f_w3, hy_f_b3, hy_f_wout, hy_freq, hy_bias, w_branch, w_out, ln_g, ln_b,
           w_router, router_bias, w_gate, w_up, w_down):
    b, n_lat, d = x.shape
    assert b == 1 and d == D_MODEL
    n_ctx = ctx.shape[1]
    depth = w_mod.shape[0]
    hw = RWKV_WIDTH
    xl = x[0]
    xc = ctx[0]

    cc = jnp.concatenate([c[:1], c_ctx[None, :], jnp.zeros((6, d), F32)], axis=0)
    mod = _mod_call(cc, w_mod, b_mod)

    head_of = np.arange(hw) // HEAD_DIM
    G = jnp.asarray((head_of[:, None] == head_of[None, :]).astype(np.float32), dtype=BF)
    bands = jnp.linspace(1e-4, FILTER_BANDS - 1, FILTER_BANDS, dtype=F32)[None, :]
    deltas = jnp.abs(jnp.linspace(HYENA_MIN_DECAY, HYENA_MAX_DECAY, HYENA_WIDTH, dtype=F32))[None, :]
    wr_pad = jnp.pad(w_router.T, ((0, LANES - N_EXPERTS), (0, 0)))
    rb_pad = jnp.pad(router_bias, (0, LANES - N_EXPERTS))[:, None]
    w_in_bf = w_in.astype(BF)
    w_branch_bf = w_branch.astype(BF)
    w_out_bf = w_out.astype(BF)
    g2_bf = rwkv_g2.astype(BF)
    wg_bf, wu_bf, wd_bf = w_gate.astype(BF), w_up.astype(BF), w_down.astype(BF)

    for l in range(depth):
        last = l == depth - 1
        ml = [mod[l, 0:1, j * d:(j + 1) * d] for j in range(6)]
        mc = [mod[l, 1:2, j * d:(j + 1) * d] for j in range(6)]
        prm = dict(mu=rwkv_mu[l][None, :],
                   w0=rwkv_w0[l].reshape(1, 2 * hw), w2=_blockdiag2(rwkv_w2[l]),
                   a0=rwkv_a0[l].reshape(1, 2 * hw), a2=_blockdiag2(rwkv_a2[l]),
                   k_k=rwkv_k_k[l][None, :], k_a=rwkv_k_a[l][None, :], r_k=rwkv_r_k[l][None, :], G=G)
        w1 = hy_f_w1[l]
        fp = dict(bands=bands, w1t=w1[0:1], w1c=w1[1:1 + FILTER_BANDS], w1s=w1[1 + FILTER_BANDS:],
                  b1=hy_f_b1[l][None, :], w2=hy_f_w2[l], b2=hy_f_b2[l][None, :], w3=hy_f_w3[l],
                  b3=hy_f_b3[l][None, :], wout=hy_f_wout[l], freq=hy_freq[l][None, :], deltas=deltas)
        hy = dict(conv_w=hy_conv_w[l], conv_b=hy_conv_b[l][None, :], bias=hy_bias[l][None, :])
        mp = dict(lnx_g=rwkv_lnx_g[l][None, :], lnx_b=rwkv_lnx_b[l][None, :], g2=g2_bf[l], G=G,
                  w_branch=w_branch_bf[l], w_out=w_out_bf[l], ln_g=ln_g[l, 0][None, :], ln_b=ln_b[l, 0][None, :])
        s0 = jnp.zeros((2, HEAD_DIM, hw), F32)
        xc_new, s_ctx = _mixer(xc, mc, l, w_in_bf[l], prm, fp, hy, mp, s0, latent=False)
        xl, _ = _mixer(xl, ml, l, w_in_bf[l], prm, fp, hy, mp, s_ctx, latent=True)
        lng, lnb = ln_g[l, 1][None, :], ln_b[l, 1][None, :]
        if not last:
            xc = xc_new
            cw_c = _router_call(xc, mc[3], mc[4], wr_pad, rb_pad)
            xc = _moe_call(xc, cw_c, mc[3], mc[4], mc[5], lng, lnb, wg_bf[l], wu_bf[l], wd_bf[l])
        cw_l = _router_call(xl, ml[3], ml[4], wr_pad, rb_pad)
        xl = _moe_call(xl, cw_l, ml[3], ml[4], ml[5], lng, lnb, wg_bf[l], wu_bf[l], wd_bf[l])
    return xl[None]
```

```python
import functools
import math

import numpy as np
import jax
import jax.numpy as jnp
from jax import lax
from jax.experimental import pallas as pl
from jax.experimental.pallas import tpu as pltpu

F32 = jnp.float32
BF = jnp.bfloat16

D_MODEL = 1024
DEPTH = 4
GRID_W = 64
RWKV_WIDTH = 512
HEAD_DIM = 64
RWKV_HEADS = 8
LORA_W = 64
LORA_A = 64
LORA_G = 128
DECAY_SCALE = 0.606531
GN_EPS = 6.4e-4
RWKV_COLS = 3 * RWKV_WIDTH + 2 * LORA_W + 2 * LORA_A + LORA_G
HYENA_WIDTH = 512
HYENA_COLS = 3 * HYENA_WIDTH
FILTER_BANDS = 16
FILTER_HIDDEN = 64
HYENA_MIN_DECAY = math.log(1e-2) / 1.5
HYENA_MAX_DECAY = math.log(1e-2) / 0.3
GATE_COLS = 2 * D_MODEL
PROJ_COLS = RWKV_COLS + HYENA_COLS + GATE_COLS
N_EXPERTS = 16
N_GROUPS = 4
EXPERTS_PER_GROUP = 4
ALPHA = (2 * DEPTH) ** 0.25
LN_EPS = 1e-5

SCAN_CHUNK = 64
LANES = 128
VMEM_LIMIT = 56 * 1024 * 1024

_NN = (((1,), (0,)), ((), ()))
_NT = (((1,), (1,)), ((), ()))
_TN = (((0,), (0,)), ((), ()))


def _sig(x):
    return 1.0 / (1.0 + jnp.exp(-x))


def _parts(a, n):
    out = []
    rem = a
    for i in range(n):
        hi = rem.astype(BF)
        out.append(hi)
        if i + 1 < n:
            rem = rem - hi.astype(F32)
    return out


def _mm(a, b, dn=_NN, passes=1):
    n = {1: 1, 3: 2, 6: 3}[passes]
    pa = _parts(a, n)
    pb = _parts(b, n)
    acc = None
    for i in range(n):
        for j in range(n - i):
            t = lax.dot_general(pa[i], pb[j], dn, preferred_element_type=F32)
            acc = t if acc is None else acc + t
    return acc


def _mm_rx(a, b_exact, n=3, dn=_NN):
    acc = None
    for p in _parts(a, n):
        t = lax.dot_general(p, b_exact, dn, preferred_element_type=F32)
        acc = t if acc is None else acc + t
    return acc


def _mm_lx(a_exact, b, n=3, dn=_NN):
    acc = None
    for p in _parts(b, n):
        t = lax.dot_general(a_exact, p, dn, preferred_element_type=F32)
        acc = t if acc is None else acc + t
    return acc


def _cparams(sem, vmem=VMEM_LIMIT):
    return pltpu.CompilerParams(dimension_semantics=sem, vmem_limit_bytes=vmem)


def _full(shape):
    nd = len(shape)
    return pl.BlockSpec(shape, lambda *_: (0,) * nd)


def _mod_kernel(c_ref, w_ref, b_ref, o_ref):
    cc = c_ref[...]
    s = cc * _sig(cc)
    o_ref[...] = jnp.dot(s.astype(BF), w_ref[...].astype(BF), preferred_element_type=F32) + b_ref[...]


def _mod_call(cc, w_mod, b_mod):
    depth, d, n6 = w_mod.shape
    tn = 1536
    return pl.pallas_call(
        _mod_kernel,
        out_shape=jax.ShapeDtypeStruct((depth, 8, n6), F32),
        grid=(depth, n6 // tn),
        in_specs=[
            pl.BlockSpec((8, d), lambda l, j: (0, 0)),
            pl.BlockSpec((None, d, tn), lambda l, j: (l, 0, j)),
            pl.BlockSpec((None, 1, tn), lambda l, j: (l, 0, j)),
        ],
        out_specs=pl.BlockSpec((None, 8, tn), lambda l, j: (l, 0, j)),
        compiler_params=_cparams(("arbitrary", "arbitrary")),
        name="mod",
    )(cc, w_mod, b_mod.reshape(depth, 1, n6))


def _inproj_kernel(x_ref, sh_ref, sc_ref, w_ref, pr_ref, ph_ref, pg_ref):
    xm = (x_ref[...] * (1.0 + sc_ref[...]) + sh_ref[...]).astype(BF)
    pr_ref[...] = jnp.dot(xm, w_ref[:, :RWKV_COLS], preferred_element_type=F32)
    ph_ref[...] = jnp.dot(xm, w_ref[:, RWKV_COLS:RWKV_COLS + HYENA_COLS], preferred_element_type=F32)
    pg_ref[...] = jnp.dot(xm, w_ref[:, RWKV_COLS + HYENA_COLS:], preferred_element_type=F32)


def _inproj_call(x, sh, sc, w_bf):
    L, d = x.shape
    tm = min(L, 256)
    row = lambda i: (i, 0)
    return pl.pallas_call(
        _inproj_kernel,
        out_shape=(jax.ShapeDtypeStruct((L, RWKV_COLS), F32),
                   jax.ShapeDtypeStruct((L, HYENA_COLS), F32),
                   jax.ShapeDtypeStruct((L, GATE_COLS), F32)),
        grid=(L // tm,),
        in_specs=[pl.BlockSpec((tm, d), row), _full((1, d)), _full((1, d)), _full((d, PROJ_COLS))],
        out_specs=(pl.BlockSpec((tm, RWKV_COLS), row), pl.BlockSpec((tm, HYENA_COLS), row),
                   pl.BlockSpec((tm, GATE_COLS), row)),
        compiler_params=_cparams(("arbitrary",)),
        name="inproj",
    )(x, sh, sc, w_bf)


def _prepare_kernel(*refs, latent, nblk):
    if latent:
        p_ref, up_ref, dn_ref = refs[:3]
        refs = refs[3:]
    else:
        p_ref = refs[0]
        refs = refs[1:]
    (mu_ref, w0_ref, w2_ref, a0_ref, a2_ref, kk_ref, ka_ref, rk_ref, g_ref,
     r_out, v_out, ka_out, lw_out, k_out, kb_out, lg_out, bv_out) = refs
    i = pl.program_id(0)
    p = p_ref[...]
    tb, w = p.shape
    row = lax.broadcasted_iota(jnp.int32, (tb, 1), 0)
    col = lax.broadcasted_iota(jnp.int32, (1, w), 1)
    if latent:
        gw = jnp.bitwise_and(row, GRID_W - 1)
        left = jnp.where(gw == 0, 0.0, pltpu.roll(p, 1, 0))
        right = jnp.where(gw == GRID_W - 1, 0.0, pltpu.roll(p, tb - 1, 0))
        upv = jnp.where(i == 0, 0.0, up_ref[...])
        dnv = jnp.where(i == nblk - 1, 0.0, dn_ref[...])
        if tb > GRID_W:
            up = jnp.concatenate([upv, p[:tb - GRID_W]], axis=0)
            down = jnp.concatenate([p[GRID_W:], dnv], axis=0)
        else:
            up, down = upv, dnv
        q = w // 4
        sh = jnp.where(col < q, left, jnp.where(col < 2 * q, right, jnp.where(col < 3 * q, up, down)))
    else:
        prev = jnp.where(row == 0, 0.0, pltpu.roll(p, 1, 0))
        nxt = jnp.where(row == tb - 1, 0.0, pltpu.roll(p, tb - 1, 0))
        sh = jnp.where(col < w // 2, prev, nxt)
    pm = p + (sh - p) * mu_ref[...]
    hw = RWKV_WIDTH
    r = pm[:, :hw]
    k = pm[:, hw:2 * hw]
    v = pm[:, 2 * hw:3 * hw]
    lw_in = jnp.tanh(pm[:, 3 * hw:3 * hw + 2 * LORA_W])
    la_in = pm[:, 3 * hw + 2 * LORA_W:3 * hw + 2 * LORA_W + 2 * LORA_A]
    lg = pm[:, 3 * hw + 2 * LORA_W + 2 * LORA_A:]
    lw = -DECAY_SCALE * _sig(w0_ref[...] + _mm(lw_in, w2_ref[...], passes=3))
    a = _sig(a0_ref[...] + _mm(la_in, a2_ref[...], passes=3))
    g = g_ref[...]
    kkr = k * kk_ref[...]
    nrm = jnp.sqrt(_mm_rx(kkr * kkr, g, 3))
    kk = kkr / jnp.maximum(nrm, 1e-12)
    k_a = ka_ref[...]
    k_f = k * (1.0 + (a[:, :hw] - 1.0) * k_a)
    k_b = k * (1.0 + (a[:, hw:] - 1.0) * k_a)
    bonus = _mm_rx(r * (k_f + k_b) * rk_ref[...], g, 3)
    r_out[...] = r
    v_out[...] = v
    ka_out[...] = -kk
    lw_out[0] = lw[:, :hw]
    lw_out[1] = lw[:, hw:]
    k_out[0] = k_f
    k_out[1] = k_b
    kb_out[0] = kk * a[:, :hw]
    kb_out[1] = kk * a[:, hw:]
    lg_out[...] = lg
    bv_out[...] = bonus * v


def _prepare_call(p_r, prm, latent):
    L, w = p_r.shape
    hw = RWKV_WIDTH
    tb = 256 if latent else L
    nblk = L // tb
    row = lambda i: (i, 0)
    row3 = lambda i: (0, i, 0)
    in_specs = [pl.BlockSpec((tb, w), row)]
    args = [p_r]
    if latent:
        per = tb // GRID_W
        nrow = L // GRID_W
        in_specs += [
            pl.BlockSpec((GRID_W, w), lambda i: (jnp.maximum(i * per - 1, 0), 0)),
            pl.BlockSpec((GRID_W, w), lambda i: (jnp.minimum((i + 1) * per, nrow - 1), 0)),
        ]
        args += [p_r, p_r]
    names = ("mu", "w0", "w2", "a0", "a2", "k_k", "k_a", "r_k", "G")
    for nm in names:
        in_specs.append(_full(prm[nm].shape))
        args.append(prm[nm])
    sd = jax.ShapeDtypeStruct
    out_shape = (sd((L, hw), F32), sd((L, hw), F32), sd((L, hw), F32),
                 sd((2, L, hw), F32), sd((2, L, hw), F32), sd((2, L, hw), F32),
                 sd((L, LORA_G), F32), sd((L, hw), F32))
    out_specs = (pl.BlockSpec((tb, hw), row), pl.BlockSpec((tb, hw), row), pl.BlockSpec((tb, hw), row),
                 pl.BlockSpec((2, tb, hw), row3), pl.BlockSpec((2, tb, hw), row3), pl.BlockSpec((2, tb, hw), row3),
                 pl.BlockSpec((tb, LORA_G), row), pl.BlockSpec((tb, hw), row))
    return pl.pallas_call(
        functools.partial(_prepare_kernel, latent=latent, nblk=nblk),
        out_shape=out_shape, grid=(nblk,), in_specs=in_specs, out_specs=out_specs,
        compiler_params=_cparams(("arbitrary",)),
        name="rwkv_prepare_lat" if latent else "rwkv_prepare_ctx",
    )(*args)


P_SCORE = 3
P_TINV = 3
P_APPLY = 1
P_STATE = 1


def _scan_kernel(rf_ref, vf_ref, kaf_ref, rb_ref, vb_ref, kab_ref, lwf_ref, kf_ref, kbf_ref,
                 lwb_ref, kbk_ref, kbb_ref, s0_ref, yf_ref, yb_ref, sfin_ref, s_scr, *, nchunk):
    c = pl.program_id(0)
    C = SCAN_CHUNK
    hd = HEAD_DIM
    nh = RWKV_HEADS

    @pl.when(c == 0)
    def _():
        s_scr[...] = s0_ref[...]

    row = lax.broadcasted_iota(jnp.int32, (C, C), 0)
    col = lax.broadcasted_iota(jnp.int32, (C, C), 1)
    eye = jnp.where(row == col, 1.0, 0.0)
    lvl_masks = []
    for sh in range(int(math.log2(C))):
        same2 = jnp.right_shift(row, sh + 1) == jnp.right_shift(col, sh + 1)
        same1 = jnp.right_shift(row, sh) == jnp.right_shift(col, sh)
        lvl_masks.append(jnp.logical_and(same2, jnp.logical_not(same1)))

    dirs = []
    for d, (r_ref, v_ref, ka_ref, lw_ref, k_ref, kb_ref) in enumerate((
            (rf_ref, vf_ref, kaf_ref, lwf_ref, kf_ref, kbf_ref),
            (rb_ref, vb_ref, kab_ref, lwb_ref, kbk_ref, kbb_ref))):
        inc = (row >= col) if d == 0 else (row <= col)
        strict = (row > col) if d == 0 else (row < col)
        lw = lw_ref[...]
        lc = _mm_lx(jnp.where(inc, 1.0, 0.0).astype(BF), lw, 3)
        e_neg = jnp.exp(-lc)
        e_tot = jnp.exp(jnp.sum(lw, axis=0, keepdims=True))
        kbn = kb_ref[...] * e_neg
        kkn = k_ref[...] * e_neg
        dirs.append(dict(inc=inc, strict=strict, v=v_ref[...], aq=ka_ref[...] * jnp.exp(lc - lw),
                         rq=r_ref[...] * jnp.exp(lc), kbn=kbn, kkn=kkn, kbp=kbn * e_tot, kkp=kkn * e_tot,
                         e_tot=e_tot, s=s_scr[d]))

    units = [(d, h) for h in range(nh) for d in range(2)]
    hs = lambda arr, h: arr[:, h * hd:(h + 1) * hd]
    sc = {}
    for (d, h) in units:
        D = dirs[d]
        sc[d, h] = _mm(jnp.concatenate([hs(D["aq"], h), hs(D["rq"], h)], axis=0),
                       jnp.concatenate([hs(D["kbn"], h), hs(D["kkn"], h)], axis=0), _NT, P_SCORE)
    a_ab, t, x = {}, {}, {}
    for u in units:
        a_ab[u] = jnp.where(dirs[u[0]]["strict"], sc[u][:C, :C], 0.0)
        t[u] = eye + jnp.where(lvl_masks[0], a_ab[u], 0.0)
    for u in units:
        D = dirs[u[0]]
        lhs = jnp.concatenate([jnp.where(D["strict"], sc[u][:C, C:], 0.0),
                               jnp.where(D["inc"], sc[u][C:, C:], 0.0)], axis=0)
        x[u] = _mm(lhs, hs(D["v"], u[1]), _NN, P_APPLY)
    for m in lvl_masks[1:]:
        tmp = {u: _mm(jnp.where(m, a_ab[u], 0.0), t[u], _NN, P_TINV) for u in units}
        for u in units:
            t[u] = t[u] + _mm(t[u], tmp[u], _NN, P_TINV)
    z, gm, uu, yy, sn = {}, {}, {}, {}, {}
    for u in units:
        z[u] = _mm(t[u], jnp.concatenate([hs(dirs[u[0]]["aq"], u[1]), x[u][:C]], axis=1), _NN, P_APPLY)
    for u in units:
        D = dirs[u[0]]
        gm[u] = _mm(jnp.concatenate([z[u][:, :hd], hs(D["rq"], u[1])], axis=0), hs(D["s"], u[1]), _NT, P_STATE)
        uu[u] = gm[u][:C] + z[u][:, hd:]
    for u in units:
        D = dirs[u[0]]
        b_rb = jnp.where(D["inc"], sc[u][C:, :C], 0.0)
        yy[u] = gm[u][C:] + _mm(b_rb, uu[u], _NN, P_APPLY) + x[u][C:]
        sn[u] = hs(D["s"], u[1]) * hs(D["e_tot"], u[1]) + _mm(
            jnp.concatenate([uu[u], hs(D["v"], u[1])], axis=0),
            jnp.concatenate([hs(D["kbp"], u[1]), hs(D["kkp"], u[1])], axis=0), _TN, P_STATE)
    yf_ref[...] = jnp.concatenate([yy[0, h] for h in range(nh)], axis=1)
    yb_ref[...] = jnp.concatenate([yy[1, h] for h in range(nh)], axis=1)
    s_scr[0] = jnp.concatenate([sn[0, h] for h in range(nh)], axis=1)
    s_scr[1] = jnp.concatenate([sn[1, h] for h in range(nh)], axis=1)

    @pl.when(c == nchunk - 1)
    def _():
        sfin_ref[...] = s_scr[...]


def _scan_call(r, v, ka, lw, k, kb, s0):
    L, hw = r.shape
    C = SCAN_CHUNK
    nchunk = L // C
    sh_f = pl.BlockSpec((C, hw), lambda c: (c, 0))
    sh_b = pl.BlockSpec((C, hw), lambda c: (nchunk - 1 - c, 0))
    pd_f = pl.BlockSpec((None, C, hw), lambda c: (0, c, 0))
    pd_b = pl.BlockSpec((None, C, hw), lambda c: (1, nchunk - 1 - c, 0))
    state = _full((2, HEAD_DIM, hw))
    return pl.pallas_call(
        functools.partial(_scan_kernel, nchunk=nchunk),
        out_shape=(jax.ShapeDtypeStruct((L, hw), F32), jax.ShapeDtypeStruct((L, hw), F32),
                   jax.ShapeDtypeStruct((2, HEAD_DIM, hw), F32)),
        grid=(nchunk,),
        in_specs=[sh_f, sh_f, sh_f, sh_b, sh_b, sh_b, pd_f, pd_f, pd_f, pd_b, pd_b, pd_b, state],
        out_specs=(sh_f, sh_b, state),
        scratch_shapes=[pltpu.VMEM((2, HEAD_DIM, hw), F32)],
        compiler_params=_cparams(("arbitrary",)),
        name="delta_scan",
    )(r, v, ka, r, v, ka, lw, k, kb, lw, k, kb, s0)


def _hfilt_kernel(bands_ref, w1t_ref, w1c_ref, w1s_ref, b1_ref, w2_ref, b2_ref, w3_ref, b3_ref, wo_ref,
                  fr_ref, dl_ref, f_ref, asum_ref, *, n, rb):
    i = pl.program_id(0)
    pos = (i * rb + lax.broadcasted_iota(jnp.int32, (rb, 1), 0)).astype(F32)
    t = pos / float(max(n - 1, 1))
    ang = ((2.0 * math.pi / n) * pos) * bands_ref[...]
    fr = fr_ref[...]
    h = t * w1t_ref[...] + _mm(jnp.cos(ang), w1c_ref[...], passes=3) + _mm(-jnp.sin(ang), w1s_ref[...], passes=3)
    h = jnp.sin(fr * (h + b1_ref[...]))
    h = jnp.sin(fr * (_mm(h, w2_ref[...], passes=3) + b2_ref[...]))
    h = jnp.sin(fr * (_mm(h, w3_ref[...], passes=3) + b3_ref[...]))
    filt = _mm(h, wo_ref[...], passes=3)
    dist = jnp.abs(pos - float(n // 2)) * (2.0 / n)
    filt = filt * jnp.exp(-dist * dl_ref[...])
    f_ref[...] = filt

    @pl.when(i == 0)
    def _():
        asum_ref[...] = jnp.zeros_like(asum_ref)

    asum_ref[...] += jnp.sum(jnp.abs(filt), axis=0, keepdims=True)


def _hfilt_call(n, fp):
    rb = min(n, 512)
    names = ("bands", "w1t", "w1c", "w1s", "b1", "w2", "b2", "w3", "b3", "wout", "freq", "deltas")
    args = [fp[nm] for nm in names]
    return pl.pallas_call(
        functools.partial(_hfilt_kernel, n=n, rb=rb),
        out_shape=(jax.ShapeDtypeStruct((n, HYENA_WIDTH), F32), jax.ShapeDtypeStruct((1, HYENA_WIDTH), F32)),
        grid=(n // rb,),
        in_specs=[_full(a.shape) for a in args],
        out_specs=(pl.BlockSpec((rb, HYENA_WIDTH), lambda i: (i, 0)), _full((1, HYENA_WIDTH))),
        compiler_params=_cparams(("arbitrary",)),
        name="hyena_filter",
    )(*args)


def _hconv3_kernel(p_ref, pv_ref, nx_ref, cw_ref, cb_ref, z_ref, x0_ref, *, nblk):
    i = pl.program_id(0)
    p = p_ref[...]
    tb = p.shape[0]
    row = lax.broadcasted_iota(jnp.int32, (tb, 1), 0)
    prev_row = jnp.where(i == 0, 0.0, pv_ref[7:8, :])
    next_row = jnp.where(i == nblk - 1, 0.0, nx_ref[0:1, :])
    sp = jnp.where(row == 0, prev_row, pltpu.roll(p, 1, 0))
    sn = jnp.where(row == tb - 1, next_row, pltpu.roll(p, tb - 1, 0))
    u = sp * cw_ref[0:1, :] + p * cw_ref[1:2, :] + sn * cw_ref[2:3, :] + cb_ref[...]
    hw = HYENA_WIDTH
    z_ref[...] = u[:, 2 * hw:] * u[:, hw:2 * hw]
    x0_ref[...] = u[:, :hw]


def _hconv3_call(p_h, cw, cb):
    L, w = p_h.shape
    tb = min(L, 256)
    nblk = L // tb
    per = tb // 8
    row = lambda i: (i, 0)
    return pl.pallas_call(
        functools.partial(_hconv3_kernel, nblk=nblk),
        out_shape=(jax.ShapeDtypeStruct((L, HYENA_WIDTH), F32), jax.ShapeDtypeStruct((L, HYENA_WIDTH), F32)),
        grid=(nblk,),
        in_specs=[pl.BlockSpec((tb, w), row),
                  pl.BlockSpec((8, w), lambda i: (jnp.maximum(i * per - 1, 0), 0)),
                  pl.BlockSpec((8, w), lambda i: (jnp.minimum((i + 1) * per, L // 8 - 1), 0)),
                  _full(cw.shape), _full(cb.shape)],
        out_specs=(pl.BlockSpec((tb, HYENA_WIDTH), row), pl.BlockSpec((tb, HYENA_WIDTH), row)),
        compiler_params=_cparams(("arbitrary",)),
        name="hyena_conv3",
    )(p_h, p_h, p_h, cw, cb)


P_DFT = 3


def _dft1_kernel(m_ref, x_ref, o_ref):
    o_ref[...] = _mm(m_ref[...], x_ref[...], _NN, P_DFT)


def _dft1_call(m1, x2d):
    rows, kdim = m1.shape
    _, cols = x2d.shape
    cb = min(cols, 2048)
    return pl.pallas_call(
        _dft1_kernel,
        out_shape=jax.ShapeDtypeStruct((rows, cols), F32),
        grid=(cols // cb,),
        in_specs=[_full(m1.shape), pl.BlockSpec((kdim, cb), lambda j: (0, j))],
        out_specs=pl.BlockSpec((rows, cb), lambda j: (0, j)),
        compiler_params=_cparams(("arbitrary",)),
        name="dft_stage1",
    )(m1, x2d)


def _dft2_matrix(fc_ref, fs_ref, tc_ref, ts_ref):
    fc, fs = fc_ref[...], fs_ref[...]
    tc, ts = tc_ref[...], ts_ref[...]
    fre = fc * tc - fs * ts
    fim = -(fc * ts + fs * tc)
    return jnp.concatenate([jnp.concatenate([fre, -fim], axis=1),
                            jnp.concatenate([fim, fre], axis=1)], axis=0)


def _dft2_spec_kernel(fc_ref, fs_ref, tc_ref, ts_ref, a_ref, h_ref):
    big = _dft2_matrix(fc_ref, fs_ref, tc_ref, ts_ref)
    n2 = a_ref.shape[1]
    x = _mm(big, jnp.concatenate([a_ref[0], a_ref[1]], axis=0), _NN, P_DFT)
    h_ref[0] = x[:n2]
    h_ref[1] = x[n2:]


def _dft2_conv_kernel(fc_ref, fs_ref, tc_ref, ts_ref, a_ref, h_ref, b_ref):
    big = _dft2_matrix(fc_ref, fs_ref, tc_ref, ts_ref)
    n2 = a_ref.shape[1]
    x = _mm(big, jnp.concatenate([a_ref[0], a_ref[1]], axis=0), _NN, P_DFT)
    xr, xi = x[:n2], x[n2:]
    hr, hi = h_ref[0], h_ref[1]
    y = jnp.concatenate([xr * hr - xi * hi, xr * hi + xi * hr], axis=0)
    bb = _mm(big, y, _TN, P_DFT)
    b_ref[0] = bb[:n2]
    b_ref[1] = bb[n2:]


def _dft2_call(consts, a4, h4=None):
    _, n1, n2, c = a4.shape
    blk = pl.BlockSpec((2, None, n2, c), lambda q: (0, q, 0, 0))
    tw = pl.BlockSpec((None, 1, n2), lambda q: (q, 0, 0))
    in_specs = [_full((n2, n2)), _full((n2, n2)), tw, tw, blk]
    args = [consts["fc"], consts["fs"], consts["twc"], consts["tws"], a4]
    kern = _dft2_spec_kernel
    if h4 is not None:
        in_specs.append(blk)
        args.append(h4)
        kern = _dft2_conv_kernel
    return pl.pallas_call(
        kern,
        out_shape=jax.ShapeDtypeStruct(a4.shape, F32),
        grid=(n1,),
        in_specs=in_specs,
        out_specs=blk,
        compiler_params=_cparams(("arbitrary",)),
        name="dft_stage2_spec" if h4 is None else "dft_stage2_conv",
    )(*args)


def _dft3_kernel(m_ref, b_ref, z_ref, x0_ref, bias_ref, asum_ref, o_ref):
    y = _mm(m_ref[...], b_ref[...], _NN, P_DFT)
    z = z_ref[...]
    o_ref[...] = (y / asum_ref[...] + z * bias_ref[...]) * x0_ref[...]


def _dft3_call(m3, b2d, z2d, x02d, bias_t, asum_t):
    rows, kdim = m3.shape
    _, cols = b2d.shape
    cb = bias_t.shape[1]
    colb = lambda j: (0, j)
    return pl.pallas_call(
        _dft3_kernel,
        out_shape=jax.ShapeDtypeStruct((rows, cols), F32),
        grid=(cols // cb,),
        in_specs=[_full(m3.shape), pl.BlockSpec((kdim, cb), colb), pl.BlockSpec((rows, cb), colb),
                  pl.BlockSpec((rows, cb), colb), _full((1, cb)), _full((1, cb))],
        out_specs=pl.BlockSpec((rows, cb), colb),
        compiler_params=_cparams(("arbitrary",)),
        name="dft_stage3",
    )(m3, b2d, z2d, x02d, bias_t, asum_t)


def _conv_direct_kernel(m1_ref, m3_ref, z_ref, f_ref, x0_ref, bias_ref, asum_ref, o_ref):
    m1 = m1_ref[...]
    z = z_ref[...]
    a = _mm(m1, z, _NN, P_DFT)
    h = _mm(m1, f_ref[...], _NN, P_DFT)
    nn = a.shape[0] // 2
    ar, ai, hr, hi = a[:nn], a[nn:], h[:nn], h[nn:]
    y = jnp.concatenate([ar * hr - ai * hi, ar * hi + ai * hr], axis=0)
    out = _mm(m3_ref[...], y, _NN, P_DFT)
    o_ref[...] = (out / asum_ref[...] + z * bias_ref[...]) * x0_ref[...]


def _conv_direct_call(m1, m3, z, filt, x0, bias, asum):
    args = (m1, m3, z, filt, x0, bias, asum)
    return pl.pallas_call(
        _conv_direct_kernel,
        out_shape=jax.ShapeDtypeStruct(z.shape, F32),
        grid=(1,),
        in_specs=[_full(a.shape) for a in args],
        out_specs=_full(z.shape),
        compiler_params=_cparams(("arbitrary",)),
        name="long_conv_direct",
    )(*args)


DIRECT_CONV_MAX = 256


@functools.lru_cache(maxsize=None)
def _dft_consts_np(n):
    big_n = 2 * n
    if n <= DIRECT_CONV_MAX:
        n1 = big_n
    else:
        n1 = 256 if n >= 8192 else 64
    n2 = big_n // n1
    k1 = np.arange(n1)[:, None].astype(np.float64)
    j1 = np.arange(n1 // 2)[None, :].astype(np.float64)
    ang1 = 2.0 * np.pi * k1 * j1 / n1
    m1 = np.concatenate([np.cos(ang1), -np.sin(ang1)], axis=0)
    o1 = (n1 // 4 + np.arange(n1 // 2))[:, None].astype(np.float64)
    q1 = np.arange(n1)[None, :].astype(np.float64)
    ang3 = 2.0 * np.pi * o1 * q1 / n1
    m3 = np.concatenate([np.cos(ang3), -np.sin(ang3)], axis=1) / big_n
    k2 = np.arange(n2)[:, None].astype(np.float64)
    j2 = np.arange(n2)[None, :].astype(np.float64)
    ang2 = 2.0 * np.pi * k2 * j2 / n2
    angt = 2.0 * np.pi * np.arange(n1)[:, None].astype(np.float64) * j2 / big_n
    f = lambda a: np.asarray(a, np.float32)
    return dict(n1=n1, n2=n2, m1=f(m1), m3=f(m3), fc=f(np.cos(ang2)), fs=f(np.sin(ang2)),
                twc=f(np.cos(angt))[:, None, :], tws=f(np.sin(angt))[:, None, :])


def _long_conv_call(z, x0, filt, asum, bias):
    n, c = z.shape
    cn = _dft_consts_np(n)
    n1, n2 = cn["n1"], cn["n2"]
    consts = {kk: jnp.asarray(vv) for kk, vv in cn.items() if kk not in ("n1", "n2")}
    if n2 == 1:
        return _conv_direct_call(consts["m1"], consts["m3"], z, filt, x0, bias, asum)
    cols = n2 * c
    cb = min(cols, 2048)
    a_f = _dft1_call(consts["m1"], filt.reshape(n1 // 2, cols)).reshape(2, n1, n2, c)
    h4 = _dft2_call(consts, a_f)
    a_z = _dft1_call(consts["m1"], z.reshape(n1 // 2, cols)).reshape(2, n1, n2, c)
    b4 = _dft2_call(consts, a_z, h4)
    out = _dft3_call(consts["m3"], b4.reshape(2 * n1, cols), z.reshape(n1 // 2, cols), x0.reshape(n1 // 2, cols),
                     jnp.tile(bias, (1, cb // c)), jnp.tile(asum, (1, cb // c)))
    return out.reshape(n, c)


def _layer_norm(x, g, b):
    mu = jnp.mean(x, axis=-1, keepdims=True)
    xc = x - mu
    var = jnp.mean(xc * xc, axis=-1, keepdims=True)
    return xc * lax.rsqrt(var + LN_EPS) * g + b


def _merge_kernel(yf_ref, yb_ref, bv_ref, lg_ref, ho_ref, pg_ref, x_ref, gx_ref, bx_ref, g2_ref, g_ref,
                  wb_ref, wo_ref, gate_ref, lng_ref, lnb_ref, sh2_ref, sc2_ref, o_ref, hf_ref):
    g = g_ref[...]
    ys = yf_ref[...] + yb_ref[...]
    inv_hd = 1.0 / HEAD_DIM
    mu = _mm_rx(ys, g, 3) * inv_hd
    dd = ys - mu
    var = _mm_rx(dd * dd, g, 3) * inv_hd
    yn = dd * lax.rsqrt(var + GN_EPS) * gx_ref[...] + bx_ref[...]
    gate_r = jnp.dot(_sig(lg_ref[...]).astype(BF), g2_ref[...], preferred_element_type=F32)
    ro = (yn + bv_ref[...]) * gate_r
    br = jnp.dot(ro.astype(BF), wb_ref[0], preferred_element_type=F32)
    bh = jnp.dot(ho_ref[...].astype(BF), wb_ref[1], preferred_element_type=F32)
    sg = _sig(pg_ref[...])
    m = sg[:, :D_MODEL] * br + sg[:, D_MODEL:] * bh
    mix = jnp.dot(m.astype(BF), wo_ref[...], preferred_element_type=F32)
    xn = _layer_norm(ALPHA * x_ref[...] + gate_ref[...] * mix, lng_ref[...], lnb_ref[...])
    o_ref[...] = xn
    hf_ref[...] = xn * (1.0 + sc2_ref[...]) + sh2_ref[...]


def _merge_call(yf, yb, bv, lg, ho, pg, x, mp):
    L, d = x.shape
    hw = RWKV_WIDTH
    tb = min(L, 256)
    row = lambda i: (i, 0)
    names = ("lnx_g", "lnx_b", "g2", "G", "w_branch", "w_out", "gate", "ln_g", "ln_b", "sh2", "sc2")
    pargs = [mp[nm] for nm in names]
    return pl.pallas_call(
        _merge_kernel,
        out_shape=(jax.ShapeDtypeStruct((L, d), F32), jax.ShapeDtypeStruct((L, d), F32)),
        grid=(L // tb,),
        in_specs=[pl.BlockSpec((tb, hw), row), pl.BlockSpec((tb, hw), row), pl.BlockSpec((tb, hw), row),
                  pl.BlockSpec((tb, LORA_G), row), pl.BlockSpec((tb, hw), row),
                  pl.BlockSpec((tb, GATE_COLS), row), pl.BlockSpec((tb, d), row)]
                 + [_full(a.shape) for a in pargs],
        out_specs=(pl.BlockSpec((tb, d), row), pl.BlockSpec((tb, d), row)),
        compiler_params=_cparams(("arbitrary",)),
        name="merge_postnorm",
    )(yf, yb, bv, lg, ho, pg, x, *pargs)


def _router_kernel(hf_ref, wrt_ref, rb_ref, cw_ref, sel_ref):
    lt = _mm(wrt_ref[...], hf_ref[...], _NT, 6)
    rid = lax.broadcasted_iota(jnp.int32, (LANES, 1), 0)
    valid = rid < N_EXPERTS
    lg = jnp.where(valid, lt, -jnp.inf)
    mx = jnp.max(lg, axis=0, keepdims=True)
    ex = jnp.where(valid, jnp.exp(lg - mx), 0.0)
    scores = ex / jnp.sum(ex, axis=0, keepdims=True)
    sel = scores + rb_ref[...]
    s = [sel[e:e + 1, :] for e in range(N_EXPERTS)]
    p = [scores[e:e + 1, :] for e in range(N_EXPERTS)]
    gs = []
    for gi in range(N_GROUPS):
        mem = s[gi * EXPERTS_PER_GROUP:(gi + 1) * EXPERTS_PER_GROUP]
        best = None
        for a in range(EXPERTS_PER_GROUP):
            for b in range(a + 1, EXPERTS_PER_GROUP):
                pair = mem[a] + mem[b]
                best = pair if best is None else jnp.maximum(best, pair)
        gs.append(best)
    bg = jnp.where((gs[0] >= gs[1]) & (gs[0] >= gs[2]) & (gs[0] >= gs[3]), 0,
                   jnp.where((gs[1] >= gs[2]) & (gs[1] >= gs[3]), 1, jnp.where(gs[2] >= gs[3], 2, 3)))
    chosen = []
    for e in range(N_EXPERTS):
        gi = e // EXPERTS_PER_GROUP
        beats = None
        for j in range(gi * EXPERTS_PER_GROUP, (gi + 1) * EXPERTS_PER_GROUP):
            if j == e:
                continue
            cond = (s[j] >= s[e]) if j < e else (s[j] > s[e])
            cnt = jnp.where(cond, 1.0, 0.0)
            beats = cnt if beats is None else beats + cnt
        chosen.append((bg == gi) & (beats < 1.5))
    den = None
    for e in range(N_EXPERTS):
        t = jnp.where(chosen[e], p[e], 0.0)
        den = t if den is None else den + t
    out = jnp.zeros(lt.shape, F32)
    msk = jnp.zeros(lt.shape, F32)
    for e in range(N_EXPERTS):
        hit = (rid == e) & chosen[e]
        out = jnp.where(hit, p[e] / den, out)
        msk = jnp.where(hit, 1.0, msk)
    cw_ref[...] = out.T
    sel_ref[...] = msk.T


def _router_call(hf, wrt_pad, rb_pad):
    L, d = hf.shape
    tb = min(L, 256)
    row = lambda i: (i, 0)
    return pl.pallas_call(
        _router_kernel,
        out_shape=(jax.ShapeDtypeStruct((L, LANES), F32), jax.ShapeDtypeStruct((L, LANES), F32)),
        grid=(L // tb,),
        in_specs=[pl.BlockSpec((tb, d), row), _full(wrt_pad.shape), _full(rb_pad.shape)],
        out_specs=(pl.BlockSpec((tb, LANES), row), pl.BlockSpec((tb, LANES), row)),
        compiler_params=_cparams(("arbitrary",)),
        name="router",
    )(hf, wrt_pad, rb_pad)


MOE_BLK = 256


def _moe_ffn_kernel(tok_ref, slot_ref, be_ref, hf_hbm, w_ref, wg_ref, wu_ref, wd_ref, y_hbm,
                    xbuf, ybuf, gsem, ssem, *, nb):
    b = pl.program_id(0)
    nrow = MOE_BLK
    s = lax.rem(b, 2)
    o = 1 - s

    def row_in(idx, sl, r):
        return pltpu.make_async_copy(hf_hbm.at[pl.ds(idx, 1)], xbuf.at[sl, pl.ds(r, 1)], gsem.at[sl])

    def row_out(idx, sl, r):
        return pltpu.make_async_copy(ybuf.at[sl, pl.ds(r, 1)], y_hbm.at[pl.ds(idx, 1)], ssem.at[sl])

    @pl.when(b == 0)
    def _():
        for r in range(nrow):
            row_in(tok_ref[r], 0, r).start()

    for r in range(nrow):
        row_in(0, s, r).wait()

    @pl.when(b >= 2)
    def _():
        for r in range(nrow):
            row_out(0, s, r).wait()

    nxt = (b + 1) * nrow
    for r in range(nrow):
        row_in(tok_ref[nxt + r], o, r).start()

    x = xbuf[s].astype(BF)
    hg = jnp.dot(x, wg_ref[...], preferred_element_type=F32)
    hu = jnp.dot(x, wu_ref[...], preferred_element_type=F32)
    act = (hg * _sig(hg) * hu).astype(BF)
    ybuf[s] = jnp.dot(act, wd_ref[...], preferred_element_type=F32) * w_ref[...]
    cur = b * nrow
    for r in range(nrow):
        row_out(slot_ref[cur + r], s, r).start()

    @pl.when(b == nb - 1)
    def _():
        for r in range(nrow):
            row_in(0, o, r).wait()
        for r in range(nrow):
            row_out(0, s, r).wait()
        for r in range(nrow):
            row_out(0, o, r).wait()


def _moe_ffn_call(row_tok, row_slot, block_e, hf_all, row_w, wg, wu, wd):
    nr = row_slot.shape[0]
    d = hf_all.shape[1]
    nb = nr // MOE_BLK
    assert nb >= 2
    wspec = pl.BlockSpec((None, d, d), lambda b, tok, slot, be: (be[b], 0, 0))
    return pl.pallas_call(
        functools.partial(_moe_ffn_kernel, nb=nb),
        out_shape=jax.ShapeDtypeStruct((nr, d), F32),
        grid_spec=pltpu.PrefetchScalarGridSpec(
            num_scalar_prefetch=3,
            grid=(nb,),
            in_specs=[pl.BlockSpec(memory_space=pl.ANY),
                      pl.BlockSpec((MOE_BLK, 1), lambda b, tok, slot, be: (b, 0)),
                      wspec, wspec, wspec],
            out_specs=pl.BlockSpec(memory_space=pl.ANY),
            scratch_shapes=[pltpu.VMEM((2, MOE_BLK, d), F32), pltpu.VMEM((2, MOE_BLK, d), F32),
                            pltpu.SemaphoreType.DMA((2,)), pltpu.SemaphoreType.DMA((2,))]),
        compiler_params=_cparams(("arbitrary",)),
        name="moe_ffn",
    )(row_tok, row_slot, block_e, hf_all, row_w, wg, wu, wd)


def _moe_dispatch(sel, cw):
    t_all = sel.shape[0]
    blk = MOE_BLK
    nb = (2 * t_all) // blk + N_EXPERTS
    nr = nb * blk
    mi = (sel > 0.5).astype(jnp.int32)
    rank = jnp.cumsum(mi, axis=0) - mi
    counts = jnp.sum(mi, axis=0)
    padded = ((counts + blk - 1) // blk) * blk
    pend = jnp.cumsum(padded)
    pstart = pend - padded
    dest = jnp.where(mi > 0, pstart[None, :] + rank, nr).reshape(-1)
    kk = jnp.cumsum(mi, axis=1) - mi
    tok = lax.broadcasted_iota(jnp.int32, mi.shape, 0)
    slot = (kk * t_all + tok).reshape(-1)
    row_tok = jnp.zeros((nr + blk,), jnp.int32).at[dest].set(tok.reshape(-1), mode="drop")
    row_slot = jnp.full((nr,), -1, jnp.int32).at[dest].set(slot, mode="drop")
    row_w = jnp.zeros((nr,), F32).at[dest].set(cw.reshape(-1), mode="drop")
    is_pad = row_slot < 0
    row_slot = jnp.where(is_pad, 2 * t_all + jnp.cumsum(is_pad.astype(jnp.int32)) - 1, row_slot)
    block_e = jnp.clip(jnp.searchsorted(pend, jnp.arange(nb, dtype=jnp.int32) * blk, side="right"),
                       0, N_EXPERTS - 1).astype(jnp.int32)
    return row_tok, row_slot, block_e, row_w[:, None]


def _combine_kernel(x_ref, y0_ref, y1_ref, gate_ref, lng_ref, lnb_ref, o_ref):
    y = y0_ref[...] + y1_ref[...]
    o_ref[...] = _layer_norm(ALPHA * x_ref[...] + gate_ref[...] * y, lng_ref[...], lnb_ref[...])


def _combine_call(x, y_slots, row_off, t_all, gate, lng, lnb, tm):
    L, d = x.shape
    off0 = row_off // tm
    off1 = (t_all + row_off) // tm
    vec = _full((1, d))
    return pl.pallas_call(
        _combine_kernel,
        out_shape=jax.ShapeDtypeStruct((L, d), F32),
        grid=(L // tm,),
        in_specs=[pl.BlockSpec((tm, d), lambda i: (i, 0)),
                  pl.BlockSpec((tm, d), lambda i: (off0 + i, 0)),
                  pl.BlockSpec((tm, d), lambda i: (off1 + i, 0)), vec, vec, vec],
        out_specs=pl.BlockSpec((tm, d), lambda i: (i, 0)),
        compiler_params=_cparams(("arbitrary",)),
        name="moe_combine",
    )(x, y_slots, y_slots, gate, lng, lnb)


def _moe_layer(segs, wrt_pad, rb_pad, lng, lnb, wg, wu, wd):
    routed = [_router_call(hf, wrt_pad, rb_pad) for (_, hf, _) in segs]
    cw = jnp.concatenate([r[0][:, :N_EXPERTS] for r in routed], axis=0)
    sel = jnp.concatenate([r[1][:, :N_EXPERTS] for r in routed], axis=0)
    hf_all = jnp.concatenate([hf for (_, hf, _) in segs], axis=0) if len(segs) > 1 else segs[0][1]
    t_all = hf_all.shape[0]
    row_tok, row_slot, block_e, row_w = _moe_dispatch(sel, cw)
    y_slots = _moe_ffn_call(row_tok, row_slot, block_e, hf_all, row_w, wg, wu, wd)
    tm = min(256, min(x.shape[0] for (x, _, _) in segs))
    outs = []
    off = 0
    for (x, _, gate) in segs:
        assert off % tm == 0 and t_all % tm == 0 and x.shape[0] % tm == 0
        outs.append(_combine_call(x, y_slots, off, t_all, gate, lng, lnb, tm))
        off += x.shape[0]
    return outs


def _blockdiag2(m):
    z = jnp.zeros_like(m[0])
    return jnp.concatenate([jnp.concatenate([m[0], z], axis=1), jnp.concatenate([z, m[1]], axis=1)], axis=0)


def _mixer(x, mod_row, l, w_in_bf, prm, fp, hy, mp, s0, latent):
    L = x.shape[0]
    sh, sc, gate = mod_row[0], mod_row[1], mod_row[2]
    p_r, p_h, p_g = _inproj_call(x, sh, sc, w_in_bf)
    r, v, ka, lw, k, kb, lg, bv = _prepare_call(p_r, prm, latent)
    yf, yb, sfin = _scan_call(r, v, ka, lw, k, kb, s0)
    filt, asum = _hfilt_call(L, fp)
    z, x0 = _hconv3_call(p_h, hy["conv_w"], hy["conv_b"])
    ho = _long_conv_call(z, x0, filt, asum, hy["bias"])
    mpl = dict(mp)
    mpl["gate"], mpl["sh2"], mpl["sc2"] = gate, mod_row[3], mod_row[4]
    xn, hf = _merge_call(yf, yb, bv, lg, ho, p_g, x, mpl)
    return xn, hf, sfin


def kernel(x, c, ctx, c_ctx, w_mod, b_mod, w_in, rwkv_mu, rwkv_w0, rwkv_w2, rwkv_a0, rwkv_a2, rwkv_g2,
           rwkv_k_k, rwkv_k_a, rwkv_r_k, rwkv_lnx_g, rwkv_lnx_b, hy_conv_w, hy_conv_b, hy_f_w1, hy_f_b1,
           hy_f_w2, hy_f_b2, hy_f_w3, hy_f_b3, hy_f_wout, hy_freq, hy_bias, w_branch, w_out, ln_g, ln_b,
           w_router, router_bias, w_gate, w_up, w_down):
    b, n_lat, d = x.shape
    assert b == 1 and d == D_MODEL
    n_ctx = ctx.shape[1]
    depth = w_mod.shape[0]
    hw = RWKV_WIDTH
    xl = x[0]
    xc = ctx[0]

    cc = jnp.concatenate([c[:1], c_ctx[None, :], jnp.zeros((6, d), F32)], axis=0)
    mod = _mod_call(cc, w_mod, b_mod)

    head_of = np.arange(hw) // HEAD_DIM
    G = jnp.asarray((head_of[:, None] == head_of[None, :]).astype(np.float32), dtype=BF)
    bands = jnp.linspace(1e-4, FILTER_BANDS - 1, FILTER_BANDS, dtype=F32)[None, :]
    deltas = jnp.abs(jnp.linspace(HYENA_MIN_DECAY, HYENA_MAX_DECAY, HYENA_WIDTH, dtype=F32))[None, :]
    wr_pad = jnp.pad(w_router.T, ((0, LANES - N_EXPERTS), (0, 0)))
    rb_pad = jnp.pad(router_bias, (0, LANES - N_EXPERTS))[:, None]
    w_in_bf = w_in.astype(BF)
    w_branch_bf = w_branch.astype(BF)
    w_out_bf = w_out.astype(BF)
    g2_bf = rwkv_g2.astype(BF)
    wg_bf, wu_bf, wd_bf = w_gate.astype(BF), w_up.astype(BF), w_down.astype(BF)

    for l in range(depth):
        last = l == depth - 1
        ml = [mod[l, 0:1, j * d:(j + 1) * d] for j in range(6)]
        mc = [mod[l, 1:2, j * d:(j + 1) * d] for j in range(6)]
        prm = dict(mu=rwkv_mu[l][None, :],
                   w0=rwkv_w0[l].reshape(1, 2 * hw), w2=_blockdiag2(rwkv_w2[l]),
                   a0=rwkv_a0[l].reshape(1, 2 * hw), a2=_blockdiag2(rwkv_a2[l]),
                   k_k=rwkv_k_k[l][None, :], k_a=rwkv_k_a[l][None, :], r_k=rwkv_r_k[l][None, :], G=G)
        w1 = hy_f_w1[l]
        fp = dict(bands=bands, w1t=w1[0:1], w1c=w1[1:1 + FILTER_BANDS], w1s=w1[1 + FILTER_BANDS:],
                  b1=hy_f_b1[l][None, :], w2=hy_f_w2[l], b2=hy_f_b2[l][None, :], w3=hy_f_w3[l],
                  b3=hy_f_b3[l][None, :], wout=hy_f_wout[l], freq=hy_freq[l][None, :], deltas=deltas)
        hy = dict(conv_w=hy_conv_w[l], conv_b=hy_conv_b[l][None, :], bias=hy_bias[l][None, :])
        mp = dict(lnx_g=rwkv_lnx_g[l][None, :], lnx_b=rwkv_lnx_b[l][None, :], g2=g2_bf[l], G=G,
                  w_branch=w_branch_bf[l], w_out=w_out_bf[l], ln_g=ln_g[l, 0][None, :], ln_b=ln_b[l, 0][None, :])
        s0 = jnp.zeros((2, HEAD_DIM, hw), F32)
        xc_new, hf_c, s_ctx = _mixer(xc, mc, l, w_in_bf[l], prm, fp, hy, mp, s0, latent=False)
        xl, hf_l, _ = _mixer(xl, ml, l, w_in_bf[l], prm, fp, hy, mp, s_ctx, latent=True)
        lng, lnb = ln_g[l, 1][None, :], ln_b[l, 1][None, :]
        if last:
            (xl,) = _moe_layer([(xl, hf_l, ml[5])], wr_pad, rb_pad, lng, lnb, wg_bf[l], wu_bf[l], wd_bf[l])
        else:
            xc, xl = _moe_layer([(xc_new, hf_c, mc[5]), (xl, hf_l, ml[5])], wr_pad, rb_pad, lng, lnb,
                                wg_bf[l], wu_bf[l], wd_bf[l])
    return xl[None]
```

```python
import functools
import math

import numpy as np
import jax
import jax.numpy as jnp
from jax import lax
from jax.experimental import pallas as pl
from jax.experimental.pallas import tpu as pltpu

F32 = jnp.float32
BF = jnp.bfloat16

D_MODEL = 1024
DEPTH = 4
GRID_W = 64
RWKV_WIDTH = 512
HEAD_DIM = 64
RWKV_HEADS = 8
LORA_W = 64
LORA_A = 64
LORA_G = 128
DECAY_SCALE = 0.606531
GN_EPS = 6.4e-4
RWKV_COLS = 3 * RWKV_WIDTH + 2 * LORA_W + 2 * LORA_A + LORA_G
HYENA_WIDTH = 512
HYENA_COLS = 3 * HYENA_WIDTH
FILTER_BANDS = 16
FILTER_HIDDEN = 64
HYENA_MIN_DECAY = math.log(1e-2) / 1.5
HYENA_MAX_DECAY = math.log(1e-2) / 0.3
GATE_COLS = 2 * D_MODEL
PROJ_COLS = RWKV_COLS + HYENA_COLS + GATE_COLS
N_EXPERTS = 16
N_GROUPS = 4
EXPERTS_PER_GROUP = 4
ALPHA = (2 * DEPTH) ** 0.25
LN_EPS = 1e-5

SCAN_CHUNK = 64
LANES = 128
VMEM_LIMIT = 56 * 1024 * 1024

_NN = (((1,), (0,)), ((), ()))
_NT = (((1,), (1,)), ((), ()))
_TN = (((0,), (0,)), ((), ()))


def _sig(x):
    return 1.0 / (1.0 + jnp.exp(-x))


def _parts(a, n):
    out = []
    rem = a
    for i in range(n):
        hi = rem.astype(BF)
        out.append(hi)
        if i + 1 < n:
            rem = rem - hi.astype(F32)
    return out


def _mm(a, b, dn=_NN, passes=1):
    n = {1: 1, 3: 2, 6: 3}[passes]
    pa = _parts(a, n)
    pb = _parts(b, n)
    acc = None
    for i in range(n):
        for j in range(n - i):
            t = lax.dot_general(pa[i], pb[j], dn, preferred_element_type=F32)
            acc = t if acc is None else acc + t
    return acc


def _mm_rx(a, b_exact, n=3, dn=_NN):
    acc = None
    for p in _parts(a, n):
        t = lax.dot_general(p, b_exact, dn, preferred_element_type=F32)
        acc = t if acc is None else acc + t
    return acc


def _mm_lx(a_exact, b, n=3, dn=_NN):
    acc = None
    for p in _parts(b, n):
        t = lax.dot_general(a_exact, p, dn, preferred_element_type=F32)
        acc = t if acc is None else acc + t
    return acc


def _cparams(sem, vmem=VMEM_LIMIT):
    return pltpu.CompilerParams(dimension_semantics=sem, vmem_limit_bytes=vmem)


def _full(shape):
    nd = len(shape)
    return pl.BlockSpec(shape, lambda *_: (0,) * nd)


def _mod_kernel(c_ref, w_ref, b_ref, o_ref):
    cc = c_ref[...]
    s = cc * _sig(cc)
    o_ref[...] = jnp.dot(s.astype(BF), w_ref[...].astype(BF), preferred_element_type=F32) + b_ref[...]


def _mod_call(cc, w_mod, b_mod):
    depth, d, n6 = w_mod.shape
    tn = 1536
    return pl.pallas_call(
        _mod_kernel,
        out_shape=jax.ShapeDtypeStruct((depth, 8, n6), F32),
        grid=(depth, n6 // tn),
        in_specs=[
            pl.BlockSpec((8, d), lambda l, j: (0, 0)),
            pl.BlockSpec((None, d, tn), lambda l, j: (l, 0, j)),
            pl.BlockSpec((None, 1, tn), lambda l, j: (l, 0, j)),
        ],
        out_specs=pl.BlockSpec((None, 8, tn), lambda l, j: (l, 0, j)),
        compiler_params=_cparams(("arbitrary", "arbitrary")),
        name="mod",
    )(cc, w_mod, b_mod.reshape(depth, 1, n6))


def _inproj_kernel(x_ref, sh_ref, sc_ref, w_ref, pr_ref, ph_ref, pg_ref):
    xm = (x_ref[...] * (1.0 + sc_ref[...]) + sh_ref[...]).astype(BF)
    pr_ref[...] = jnp.dot(xm, w_ref[:, :RWKV_COLS], preferred_element_type=F32)
    ph_ref[...] = jnp.dot(xm, w_ref[:, RWKV_COLS:RWKV_COLS + HYENA_COLS], preferred_element_type=F32)
    pg_ref[...] = jnp.dot(xm, w_ref[:, RWKV_COLS + HYENA_COLS:], preferred_element_type=F32)


def _inproj_call(x, sh, sc, w_bf):
    L, d = x.shape
    tm = min(L, 256)
    row = lambda i: (i, 0)
    return pl.pallas_call(
        _inproj_kernel,
        out_shape=(jax.ShapeDtypeStruct((L, RWKV_COLS), F32),
                   jax.ShapeDtypeStruct((L, HYENA_COLS), F32),
                   jax.ShapeDtypeStruct((L, GATE_COLS), F32)),
        grid=(L // tm,),
        in_specs=[pl.BlockSpec((tm, d), row), _full((1, d)), _full((1, d)), _full((d, PROJ_COLS))],
        out_specs=(pl.BlockSpec((tm, RWKV_COLS), row), pl.BlockSpec((tm, HYENA_COLS), row),
                   pl.BlockSpec((tm, GATE_COLS), row)),
        compiler_params=_cparams(("arbitrary",)),
        name="inproj",
    )(x, sh, sc, w_bf)


def _prepare_kernel(*refs, latent, nblk):
    if latent:
        p_ref, up_ref, dn_ref = refs[:3]
        refs = refs[3:]
    else:
        p_ref = refs[0]
        refs = refs[1:]
    (mu_ref, w0_ref, w2_ref, a0_ref, a2_ref, kk_ref, ka_ref, rk_ref, g_ref,
     r_out, v_out, ka_out, lw_out, k_out, kb_out, lg_out, bv_out) = refs
    i = pl.program_id(0)
    p = p_ref[...]
    tb, w = p.shape
    row = lax.broadcasted_iota(jnp.int32, (tb, 1), 0)
    col = lax.broadcasted_iota(jnp.int32, (1, w), 1)
    if latent:
        gw = jnp.bitwise_and(row, GRID_W - 1)
        left = jnp.where(gw == 0, 0.0, pltpu.roll(p, 1, 0))
        right = jnp.where(gw == GRID_W - 1, 0.0, pltpu.roll(p, tb - 1, 0))
        upv = jnp.where(i == 0, 0.0, up_ref[...])
        dnv = jnp.where(i == nblk - 1, 0.0, dn_ref[...])
        if tb > GRID_W:
            up = jnp.concatenate([upv, p[:tb - GRID_W]], axis=0)
            down = jnp.concatenate([p[GRID_W:], dnv], axis=0)
        else:
            up, down = upv, dnv
        q = w // 4
        sh = jnp.where(col < q, left, jnp.where(col < 2 * q, right, jnp.where(col < 3 * q, up, down)))
    else:
        prev = jnp.where(row == 0, 0.0, pltpu.roll(p, 1, 0))
        nxt = jnp.where(row == tb - 1, 0.0, pltpu.roll(p, tb - 1, 0))
        sh = jnp.where(col < w // 2, prev, nxt)
    pm = p + (sh - p) * mu_ref[...]
    hw = RWKV_WIDTH
    r = pm[:, :hw]
    k = pm[:, hw:2 * hw]
    v = pm[:, 2 * hw:3 * hw]
    lw_in = jnp.tanh(pm[:, 3 * hw:3 * hw + 2 * LORA_W])
    la_in = pm[:, 3 * hw + 2 * LORA_W:3 * hw + 2 * LORA_W + 2 * LORA_A]
    lg = pm[:, 3 * hw + 2 * LORA_W + 2 * LORA_A:]
    lw = -DECAY_SCALE * _sig(w0_ref[...] + _mm(lw_in, w2_ref[...], passes=3))
    a = _sig(a0_ref[...] + _mm(la_in, a2_ref[...], passes=3))
    g = g_ref[...]
    kkr = k * kk_ref[...]
    nrm = jnp.sqrt(_mm_rx(kkr * kkr, g, 3))
    kk = kkr / jnp.maximum(nrm, 1e-12)
    k_a = ka_ref[...]
    k_f = k * (1.0 + (a[:, :hw] - 1.0) * k_a)
    k_b = k * (1.0 + (a[:, hw:] - 1.0) * k_a)
    bonus = _mm_rx(r * (k_f + k_b) * rk_ref[...], g, 3)
    r_out[...] = r
    v_out[...] = v
    ka_out[...] = -kk
    lw_out[0] = lw[:, :hw]
    lw_out[1] = lw[:, hw:]
    k_out[0] = k_f
    k_out[1] = k_b
    kb_out[0] = kk * a[:, :hw]
    kb_out[1] = kk * a[:, hw:]
    lg_out[...] = lg
    bv_out[...] = bonus * v


def _prepare_call(p_r, prm, latent):
    L, w = p_r.shape
    hw = RWKV_WIDTH
    tb = 256 if latent else L
    nblk = L // tb
    row = lambda i: (i, 0)
    row3 = lambda i: (0, i, 0)
    in_specs = [pl.BlockSpec((tb, w), row)]
    args = [p_r]
    if latent:
        per = tb // GRID_W
        nrow = L // GRID_W
        in_specs += [
            pl.BlockSpec((GRID_W, w), lambda i: (jnp.maximum(i * per - 1, 0), 0)),
            pl.BlockSpec((GRID_W, w), lambda i: (jnp.minimum((i + 1) * per, nrow - 1), 0)),
        ]
        args += [p_r, p_r]
    names = ("mu", "w0", "w2", "a0", "a2", "k_k", "k_a", "r_k", "G")
    for nm in names:
        in_specs.append(_full(prm[nm].shape))
        args.append(prm[nm])
    sd = jax.ShapeDtypeStruct
    out_shape = (sd((L, hw), F32), sd((L, hw), F32), sd((L, hw), F32),
                 sd((2, L, hw), F32), sd((2, L, hw), F32), sd((2, L, hw), F32),
                 sd((L, LORA_G), F32), sd((L, hw), F32))
    out_specs = (pl.BlockSpec((tb, hw), row), pl.BlockSpec((tb, hw), row), pl.BlockSpec((tb, hw), row),
                 pl.BlockSpec((2, tb, hw), row3), pl.BlockSpec((2, tb, hw), row3), pl.BlockSpec((2, tb, hw), row3),
                 pl.BlockSpec((tb, LORA_G), row), pl.BlockSpec((tb, hw), row))
    return pl.pallas_call(
        functools.partial(_prepare_kernel, latent=latent, nblk=nblk),
        out_shape=out_shape, grid=(nblk,), in_specs=in_specs, out_specs=out_specs,
        compiler_params=_cparams(("arbitrary",)),
        name="rwkv_prepare_lat" if latent else "rwkv_prepare_ctx",
    )(*args)


P_SCORE = 3
P_TINV = 3
P_APPLY = 1
P_STATE = 1


def _scan_kernel(rf_ref, vf_ref, kaf_ref, rb_ref, vb_ref, kab_ref, lwf_ref, kf_ref, kbf_ref,
                 lwb_ref, kbk_ref, kbb_ref, s0_ref, yf_ref, yb_ref, sfin_ref, s_scr, *, nchunk):
    c = pl.program_id(0)
    C = SCAN_CHUNK
    hd = HEAD_DIM
    nh = RWKV_HEADS

    @pl.when(c == 0)
    def _():
        s_scr[...] = s0_ref[...]

    row = lax.broadcasted_iota(jnp.int32, (C, C), 0)
    col = lax.broadcasted_iota(jnp.int32, (C, C), 1)
    eye = jnp.where(row == col, 1.0, 0.0)
    lvl_masks = []
    for sh in range(int(math.log2(C))):
        same2 = jnp.right_shift(row, sh + 1) == jnp.right_shift(col, sh + 1)
        same1 = jnp.right_shift(row, sh) == jnp.right_shift(col, sh)
        lvl_masks.append(jnp.logical_and(same2, jnp.logical_not(same1)))

    dirs = []
    for d, (r_ref, v_ref, ka_ref, lw_ref, k_ref, kb_ref) in enumerate((
            (rf_ref, vf_ref, kaf_ref, lwf_ref, kf_ref, kbf_ref),
            (rb_ref, vb_ref, kab_ref, lwb_ref, kbk_ref, kbb_ref))):
        inc = (row >= col) if d == 0 else (row <= col)
        strict = (row > col) if d == 0 else (row < col)
        lw = lw_ref[...]
        lc = _mm_lx(jnp.where(inc, 1.0, 0.0).astype(BF), lw, 3)
        e_neg = jnp.exp(-lc)
        e_tot = jnp.exp(jnp.sum(lw, axis=0, keepdims=True))
        kbn = kb_ref[...] * e_neg
        kkn = k_ref[...] * e_neg
        dirs.append(dict(inc=inc, strict=strict, v=v_ref[...], aq=ka_ref[...] * jnp.exp(lc - lw),
                         rq=r_ref[...] * jnp.exp(lc), kbn=kbn, kkn=kkn, kbp=kbn * e_tot, kkp=kkn * e_tot,
                         e_tot=e_tot, s=s_scr[d]))

    units = [(d, h) for h in range(nh) for d in range(2)]
    hs = lambda arr, h: arr[:, h * hd:(h + 1) * hd]
    sc = {}
    for (d, h) in units:
        D = dirs[d]
        sc[d, h] = _mm(jnp.concatenate([hs(D["aq"], h), hs(D["rq"], h)], axis=0),
                       jnp.concatenate([hs(D["kbn"], h), hs(D["kkn"], h)], axis=0), _NT, P_SCORE)
    a_ab, t, x = {}, {}, {}
    for u in units:
        a_ab[u] = jnp.where(dirs[u[0]]["strict"], sc[u][:C, :C], 0.0)
        t[u] = eye + jnp.where(lvl_masks[0], a_ab[u], 0.0)
    for u in units:
        D = dirs[u[0]]
        lhs = jnp.concatenate([jnp.where(D["strict"], sc[u][:C, C:], 0.0),
                               jnp.where(D["inc"], sc[u][C:, C:], 0.0)], axis=0)
        x[u] = _mm(lhs, hs(D["v"], u[1]), _NN, P_APPLY)
    for m in lvl_masks[1:]:
        tmp = {u: _mm(jnp.where(m, a_ab[u], 0.0), t[u], _NN, P_TINV) for u in units}
        for u in units:
            t[u] = t[u] + _mm(t[u], tmp[u], _NN, P_TINV)
    z, gm, uu, yy, sn = {}, {}, {}, {}, {}
    for u in units:
        z[u] = _mm(t[u], jnp.concatenate([hs(dirs[u[0]]["aq"], u[1]), x[u][:C]], axis=1), _NN, P_APPLY)
    for u in units:
        D = dirs[u[0]]
        gm[u] = _mm(jnp.concatenate([z[u][:, :hd], hs(D["rq"], u[1])], axis=0), hs(D["s"], u[1]), _NT, P_STATE)
        uu[u] = gm[u][:C] + z[u][:, hd:]
    for u in units:
        D = dirs[u[0]]
        b_rb = jnp.where(D["inc"], sc[u][C:, :C], 0.0)
        yy[u] = gm[u][C:] + _mm(b_rb, uu[u], _NN, P_APPLY) + x[u][C:]
        sn[u] = hs(D["s"], u[1]) * hs(D["e_tot"], u[1]) + _mm(
            jnp.concatenate([uu[u], hs(D["v"], u[1])], axis=0),
            jnp.concatenate([hs(D["kbp"], u[1]), hs(D["kkp"], u[1])], axis=0), _TN, P_STATE)
    yf_ref[...] = jnp.concatenate([yy[0, h] for h in range(nh)], axis=1)
    yb_ref[...] = jnp.concatenate([yy[1, h] for h in range(nh)], axis=1)
    s_scr[0] = jnp.concatenate([sn[0, h] for h in range(nh)], axis=1)
    s_scr[1] = jnp.concatenate([sn[1, h] for h in range(nh)], axis=1)

    @pl.when(c == nchunk - 1)
    def _():
        sfin_ref[...] = s_scr[...]


def _scan_call(r, v, ka, lw, k, kb, s0):
    L, hw = r.shape
    C = SCAN_CHUNK
    nchunk = L // C
    sh_f = pl.BlockSpec((C, hw), lambda c: (c, 0))
    sh_b = pl.BlockSpec((C, hw), lambda c: (nchunk - 1 - c, 0))
    pd_f = pl.BlockSpec((None, C, hw), lambda c: (0, c, 0))
    pd_b = pl.BlockSpec((None, C, hw), lambda c: (1, nchunk - 1 - c, 0))
    state = _full((2, HEAD_DIM, hw))
    return pl.pallas_call(
        functools.partial(_scan_kernel, nchunk=nchunk),
        out_shape=(jax.ShapeDtypeStruct((L, hw), F32), jax.ShapeDtypeStruct((L, hw), F32),
                   jax.ShapeDtypeStruct((2, HEAD_DIM, hw), F32)),
        grid=(nchunk,),
        in_specs=[sh_f, sh_f, sh_f, sh_b, sh_b, sh_b, pd_f, pd_f, pd_f, pd_b, pd_b, pd_b, state],
        out_specs=(sh_f, sh_b, state),
        scratch_shapes=[pltpu.VMEM((2, HEAD_DIM, hw), F32)],
        compiler_params=_cparams(("arbitrary",)),
        name="delta_scan",
    )(r, v, ka, r, v, ka, lw, k, kb, lw, k, kb, s0)


def _hfilt_kernel(bands_ref, w1t_ref, w1c_ref, w1s_ref, b1_ref, w2_ref, b2_ref, w3_ref, b3_ref, wo_ref,
                  fr_ref, dl_ref, f_ref, asum_ref, *, n, rb):
    i = pl.program_id(0)
    pos = (i * rb + lax.broadcasted_iota(jnp.int32, (rb, 1), 0)).astype(F32)
    t = pos / float(max(n - 1, 1))
    ang = ((2.0 * math.pi / n) * pos) * bands_ref[...]
    fr = fr_ref[...]
    h = t * w1t_ref[...] + _mm(jnp.cos(ang), w1c_ref[...], passes=3) + _mm(-jnp.sin(ang), w1s_ref[...], passes=3)
    h = jnp.sin(fr * (h + b1_ref[...]))
    h = jnp.sin(fr * (_mm(h, w2_ref[...], passes=3) + b2_ref[...]))
    h = jnp.sin(fr * (_mm(h, w3_ref[...], passes=3) + b3_ref[...]))
    filt = _mm(h, wo_ref[...], passes=3)
    dist = jnp.abs(pos - float(n // 2)) * (2.0 / n)
    filt = filt * jnp.exp(-dist * dl_ref[...])
    f_ref[...] = filt

    @pl.when(i == 0)
    def _():
        asum_ref[...] = jnp.zeros_like(asum_ref)

    asum_ref[...] += jnp.sum(jnp.abs(filt), axis=0, keepdims=True)


def _hfilt_call(n, fp):
    rb = min(n, 512)
    names = ("bands", "w1t", "w1c", "w1s", "b1", "w2", "b2", "w3", "b3", "wout", "freq", "deltas")
    args = [fp[nm] for nm in names]
    return pl.pallas_call(
        functools.partial(_hfilt_kernel, n=n, rb=rb),
        out_shape=(jax.ShapeDtypeStruct((n, HYENA_WIDTH), F32), jax.ShapeDtypeStruct((1, HYENA_WIDTH), F32)),
        grid=(n // rb,),
        in_specs=[_full(a.shape) for a in args],
        out_specs=(pl.BlockSpec((rb, HYENA_WIDTH), lambda i: (i, 0)), _full((1, HYENA_WIDTH))),
        compiler_params=_cparams(("arbitrary",)),
        name="hyena_filter",
    )(*args)


def _hconv3_kernel(p_ref, pv_ref, nx_ref, cw_ref, cb_ref, z_ref, x0_ref, *, nblk):
    i = pl.program_id(0)
    p = p_ref[...]
    tb = p.shape[0]
    row = lax.broadcasted_iota(jnp.int32, (tb, 1), 0)
    prev_row = jnp.where(i == 0, 0.0, pv_ref[7:8, :])
    next_row = jnp.where(i == nblk - 1, 0.0, nx_ref[0:1, :])
    sp = jnp.where(row == 0, prev_row, pltpu.roll(p, 1, 0))
    sn = jnp.where(row == tb - 1, next_row, pltpu.roll(p, tb - 1, 0))
    u = sp * cw_ref[0:1, :] + p * cw_ref[1:2, :] + sn * cw_ref[2:3, :] + cb_ref[...]
    hw = HYENA_WIDTH
    z_ref[...] = u[:, 2 * hw:] * u[:, hw:2 * hw]
    x0_ref[...] = u[:, :hw]


def _hconv3_call(p_h, cw, cb):
    L, w = p_h.shape
    tb = min(L, 256)
    nblk = L // tb
    per = tb // 8
    row = lambda i: (i, 0)
    return pl.pallas_call(
        functools.partial(_hconv3_kernel, nblk=nblk),
        out_shape=(jax.ShapeDtypeStruct((L, HYENA_WIDTH), F32), jax.ShapeDtypeStruct((L, HYENA_WIDTH), F32)),
        grid=(nblk,),
        in_specs=[pl.BlockSpec((tb, w), row),
                  pl.BlockSpec((8, w), lambda i: (jnp.maximum(i * per - 1, 0), 0)),
                  pl.BlockSpec((8, w), lambda i: (jnp.minimum((i + 1) * per, L // 8 - 1), 0)),
                  _full(cw.shape), _full(cb.shape)],
        out_specs=(pl.BlockSpec((tb, HYENA_WIDTH), row), pl.BlockSpec((tb, HYENA_WIDTH), row)),
        compiler_params=_cparams(("arbitrary",)),
        name="hyena_conv3",
    )(p_h, p_h, p_h, cw, cb)


P_DFT = 3


def _dft1_kernel(m_ref, x_ref, o_ref):
    o_ref[...] = _mm(m_ref[...], x_ref[...], _NN, P_DFT)


def _dft1_call(m1, x2d):
    rows, kdim = m1.shape
    _, cols = x2d.shape
    cb = min(cols, 2048)
    return pl.pallas_call(
        _dft1_kernel,
        out_shape=jax.ShapeDtypeStruct((rows, cols), F32),
        grid=(cols // cb,),
        in_specs=[_full(m1.shape), pl.BlockSpec((kdim, cb), lambda j: (0, j))],
        out_specs=pl.BlockSpec((rows, cb), lambda j: (0, j)),
        compiler_params=_cparams(("arbitrary",)),
        name="dft_stage1",
    )(m1, x2d)


def _dft2_matrix(fc_ref, fs_ref, tc_ref, ts_ref):
    fc, fs = fc_ref[...], fs_ref[...]
    tc, ts = tc_ref[...], ts_ref[...]
    fre = fc * tc - fs * ts
    fim = -(fc * ts + fs * tc)
    return jnp.concatenate([jnp.concatenate([fre, -fim], axis=1),
                            jnp.concatenate([fim, fre], axis=1)], axis=0)


def _dft2_spec_kernel(fc_ref, fs_ref, tc_ref, ts_ref, a_ref, h_ref):
    big = _dft2_matrix(fc_ref, fs_ref, tc_ref, ts_ref)
    n2 = a_ref.shape[1]
    x = _mm(big, jnp.concatenate([a_ref[0], a_ref[1]], axis=0), _NN, P_DFT)
    h_ref[0] = x[:n2]
    h_ref[1] = x[n2:]


def _dft2_conv_kernel(fc_ref, fs_ref, tc_ref, ts_ref, a_ref, h_ref, b_ref):
    big = _dft2_matrix(fc_ref, fs_ref, tc_ref, ts_ref)
    n2 = a_ref.shape[1]
    x = _mm(big, jnp.concatenate([a_ref[0], a_ref[1]], axis=0), _NN, P_DFT)
    xr, xi = x[:n2], x[n2:]
    hr, hi = h_ref[0], h_ref[1]
    y = jnp.concatenate([xr * hr - xi * hi, xr * hi + xi * hr], axis=0)
    bb = _mm(big, y, _TN, P_DFT)
    b_ref[0] = bb[:n2]
    b_ref[1] = bb[n2:]


def _dft2_call(consts, a4, h4=None):
    _, n1, n2, c = a4.shape
    blk = pl.BlockSpec((2, None, n2, c), lambda q: (0, q, 0, 0))
    tw = pl.BlockSpec((None, 1, n2), lambda q: (q, 0, 0))
    in_specs = [_full((n2, n2)), _full((n2, n2)), tw, tw, blk]
    args = [consts["fc"], consts["fs"], consts["twc"], consts["tws"], a4]
    kern = _dft2_spec_kernel
    if h4 is not None:
        in_specs.append(blk)
        args.append(h4)
        kern = _dft2_conv_kernel
    return pl.pallas_call(
        kern,
        out_shape=jax.ShapeDtypeStruct(a4.shape, F32),
        grid=(n1,),
        in_specs=in_specs,
        out_specs=blk,
        compiler_params=_cparams(("arbitrary",)),
        name="dft_stage2_spec" if h4 is None else "dft_stage2_conv",
    )(*args)


def _dft3_kernel(m_ref, b_ref, z_ref, x0_ref, bias_ref, asum_ref, o_ref):
    y = _mm(m_ref[...], b_ref[...], _NN, P_DFT)
    z = z_ref[...]
    o_ref[...] = (y / asum_ref[...] + z * bias_ref[...]) * x0_ref[...]


def _dft3_call(m3, b2d, z2d, x02d, bias_t, asum_t):
    rows, kdim = m3.shape
    _, cols = b2d.shape
    cb = bias_t.shape[1]
    colb = lambda j: (0, j)
    return pl.pallas_call(
        _dft3_kernel,
        out_shape=jax.ShapeDtypeStruct((rows, cols), F32),
        grid=(cols // cb,),
        in_specs=[_full(m3.shape), pl.BlockSpec((kdim, cb), colb), pl.BlockSpec((rows, cb), colb),
                  pl.BlockSpec((rows, cb), colb), _full((1, cb)), _full((1, cb))],
        out_specs=pl.BlockSpec((rows, cb), colb),
        compiler_params=_cparams(("arbitrary",)),
        name="dft_stage3",
    )(m3, b2d, z2d, x02d, bias_t, asum_t)


def _conv_direct_kernel(m1_ref, m3_ref, z_ref, f_ref, x0_ref, bias_ref, asum_ref, o_ref):
    m1 = m1_ref[...]
    z = z_ref[...]
    a = _mm(m1, z, _NN, P_DFT)
    h = _mm(m1, f_ref[...], _NN, P_DFT)
    nn = a.shape[0] // 2
    ar, ai, hr, hi = a[:nn], a[nn:], h[:nn], h[nn:]
    y = jnp.concatenate([ar * hr - ai * hi, ar * hi + ai * hr], axis=0)
    out = _mm(m3_ref[...], y, _NN, P_DFT)
    o_ref[...] = (out / asum_ref[...] + z * bias_ref[...]) * x0_ref[...]


def _conv_direct_call(m1, m3, z, filt, x0, bias, asum):
    args = (m1, m3, z, filt, x0, bias, asum)
    return pl.pallas_call(
        _conv_direct_kernel,
        out_shape=jax.ShapeDtypeStruct(z.shape, F32),
        grid=(1,),
        in_specs=[_full(a.shape) for a in args],
        out_specs=_full(z.shape),
        compiler_params=_cparams(("arbitrary",)),
        name="long_conv_direct",
    )(*args)


DIRECT_CONV_MAX = 256


@functools.lru_cache(maxsize=None)
def _dft_consts_np(n):
    big_n = 2 * n
    if n <= DIRECT_CONV_MAX:
        n1 = big_n
    else:
        n1 = 256 if n >= 8192 else 64
    n2 = big_n // n1
    k1 = np.arange(n1)[:, None].astype(np.float64)
    j1 = np.arange(n1 // 2)[None, :].astype(np.float64)
    ang1 = 2.0 * np.pi * k1 * j1 / n1
    m1 = np.concatenate([np.cos(ang1), -np.sin(ang1)], axis=0)
    o1 = (n1 // 4 + np.arange(n1 // 2))[:, None].astype(np.float64)
    q1 = np.arange(n1)[None, :].astype(np.float64)
    ang3 = 2.0 * np.pi * o1 * q1 / n1
    m3 = np.concatenate([np.cos(ang3), -np.sin(ang3)], axis=1) / big_n
    k2 = np.arange(n2)[:, None].astype(np.float64)
    j2 = np.arange(n2)[None, :].astype(np.float64)
    ang2 = 2.0 * np.pi * k2 * j2 / n2
    angt = 2.0 * np.pi * np.arange(n1)[:, None].astype(np.float64) * j2 / big_n
    f = lambda a: np.asarray(a, np.float32)
    return dict(n1=n1, n2=n2, m1=f(m1), m3=f(m3), fc=f(np.cos(ang2)), fs=f(np.sin(ang2)),
                twc=f(np.cos(angt))[:, None, :], tws=f(np.sin(angt))[:, None, :])


def _long_conv_call(z, x0, filt, asum, bias):
    n, c = z.shape
    cn = _dft_consts_np(n)
    n1, n2 = cn["n1"], cn["n2"]
    consts = {kk: jnp.asarray(vv) for kk, vv in cn.items() if kk not in ("n1", "n2")}
    if n2 == 1:
        return _conv_direct_call(consts["m1"], consts["m3"], z, filt, x0, bias, asum)
    cols = n2 * c
    cb = min(cols, 2048)
    a_f = _dft1_call(consts["m1"], filt.reshape(n1 // 2, cols)).reshape(2, n1, n2, c)
    h4 = _dft2_call(consts, a_f)
    a_z = _dft1_call(consts["m1"], z.reshape(n1 // 2, cols)).reshape(2, n1, n2, c)
    b4 = _dft2_call(consts, a_z, h4)
    out = _dft3_call(consts["m3"], b4.reshape(2 * n1, cols), z.reshape(n1 // 2, cols), x0.reshape(n1 // 2, cols),
                     jnp.tile(bias, (1, cb // c)), jnp.tile(asum, (1, cb // c)))
    return out.reshape(n, c)


def _layer_norm(x, g, b):
    mu = jnp.mean(x, axis=-1, keepdims=True)
    xc = x - mu
    var = jnp.mean(xc * xc, axis=-1, keepdims=True)
    return xc * lax.rsqrt(var + LN_EPS) * g + b


def _merge_kernel(yf_ref, yb_ref, bv_ref, lg_ref, ho_ref, pg_ref, x_ref, gx_ref, bx_ref, g2_ref, g_ref,
                  wb_ref, wo_ref, gate_ref, lng_ref, lnb_ref, sh2_ref, sc2_ref, o_ref, hf_ref):
    g = g_ref[...]
    ys = yf_ref[...] + yb_ref[...]
    inv_hd = 1.0 / HEAD_DIM
    mu = _mm_rx(ys, g, 3) * inv_hd
    dd = ys - mu
    var = _mm_rx(dd * dd, g, 3) * inv_hd
    yn = dd * lax.rsqrt(var + GN_EPS) * gx_ref[...] + bx_ref[...]
    gate_r = jnp.dot(_sig(lg_ref[...]).astype(BF), g2_ref[...], preferred_element_type=F32)
    ro = (yn + bv_ref[...]) * gate_r
    br = jnp.dot(ro.astype(BF), wb_ref[0], preferred_element_type=F32)
    bh = jnp.dot(ho_ref[...].astype(BF), wb_ref[1], preferred_element_type=F32)
    sg = _sig(pg_ref[...])
    m = sg[:, :D_MODEL] * br + sg[:, D_MODEL:] * bh
    mix = jnp.dot(m.astype(BF), wo_ref[...], preferred_element_type=F32)
    xn = _layer_norm(ALPHA * x_ref[...] + gate_ref[...] * mix, lng_ref[...], lnb_ref[...])
    o_ref[...] = xn
    hf_ref[...] = xn * (1.0 + sc2_ref[...]) + sh2_ref[...]


def _merge_call(yf, yb, bv, lg, ho, pg, x, mp):
    L, d = x.shape
    hw = RWKV_WIDTH
    tb = min(L, 256)
    row = lambda i: (i, 0)
    names = ("lnx_g", "lnx_b", "g2", "G", "w_branch", "w_out", "gate", "ln_g", "ln_b", "sh2", "sc2")
    pargs = [mp[nm] for nm in names]
    return pl.pallas_call(
        _merge_kernel,
        out_shape=(jax.ShapeDtypeStruct((L, d), F32), jax.ShapeDtypeStruct((L, d), F32)),
        grid=(L // tb,),
        in_specs=[pl.BlockSpec((tb, hw), row), pl.BlockSpec((tb, hw), row), pl.BlockSpec((tb, hw), row),
                  pl.BlockSpec((tb, LORA_G), row), pl.BlockSpec((tb, hw), row),
                  pl.BlockSpec((tb, GATE_COLS), row), pl.BlockSpec((tb, d), row)]
                 + [_full(a.shape) for a in pargs],
        out_specs=(pl.BlockSpec((tb, d), row), pl.BlockSpec((tb, d), row)),
        compiler_params=_cparams(("arbitrary",)),
        name="merge_postnorm",
    )(yf, yb, bv, lg, ho, pg, x, *pargs)


def _router_kernel(hf_ref, wrt_ref, rb_ref, cw_ref, sel_ref):
    lt = _mm(wrt_ref[...], hf_ref[...], _NT, 6)
    rid = lax.broadcasted_iota(jnp.int32, (LANES, 1), 0)
    valid = rid < N_EXPERTS
    lg = jnp.where(valid, lt, -jnp.inf)
    mx = jnp.max(lg, axis=0, keepdims=True)
    ex = jnp.where(valid, jnp.exp(lg - mx), 0.0)
    scores = ex / jnp.sum(ex, axis=0, keepdims=True)
    sel = scores + rb_ref[...]
    s = [sel[e:e + 1, :] for e in range(N_EXPERTS)]
    p = [scores[e:e + 1, :] for e in range(N_EXPERTS)]
    gs = []
    for gi in range(N_GROUPS):
        mem = s[gi * EXPERTS_PER_GROUP:(gi + 1) * EXPERTS_PER_GROUP]
        best = None
        for a in range(EXPERTS_PER_GROUP):
            for b in range(a + 1, EXPERTS_PER_GROUP):
                pair = mem[a] + mem[b]
                best = pair if best is None else jnp.maximum(best, pair)
        gs.append(best)
    bg = jnp.where((gs[0] >= gs[1]) & (gs[0] >= gs[2]) & (gs[0] >= gs[3]), 0,
                   jnp.where((gs[1] >= gs[2]) & (gs[1] >= gs[3]), 1, jnp.where(gs[2] >= gs[3], 2, 3)))
    chosen = []
    for e in range(N_EXPERTS):
        gi = e // EXPERTS_PER_GROUP
        beats = None
        for j in range(gi * EXPERTS_PER_GROUP, (gi + 1) * EXPERTS_PER_GROUP):
            if j == e:
                continue
            cond = (s[j] >= s[e]) if j < e else (s[j] > s[e])
            cnt = jnp.where(cond, 1.0, 0.0)
            beats = cnt if beats is None else beats + cnt
        chosen.append((bg == gi) & (beats < 1.5))
    den = None
    for e in range(N_EXPERTS):
        t = jnp.where(chosen[e], p[e], 0.0)
        den = t if den is None else den + t
    out = jnp.zeros(lt.shape, F32)
    msk = jnp.zeros(lt.shape, F32)
    for e in range(N_EXPERTS):
        hit = (rid == e) & chosen[e]
        out = jnp.where(hit, p[e] / den, out)
        msk = jnp.where(hit, 1.0, msk)
    cw_ref[...] = out.T
    sel_ref[...] = msk.T


def _router_call(hf, wrt_pad, rb_pad):
    L, d = hf.shape
    tb = min(L, 256)
    row = lambda i: (i, 0)
    return pl.pallas_call(
        _router_kernel,
        out_shape=(jax.ShapeDtypeStruct((L, LANES), F32), jax.ShapeDtypeStruct((L, LANES), F32)),
        grid=(L // tb,),
        in_specs=[pl.BlockSpec((tb, d), row), _full(wrt_pad.shape), _full(rb_pad.shape)],
        out_specs=(pl.BlockSpec((tb, LANES), row), pl.BlockSpec((tb, LANES), row)),
        compiler_params=_cparams(("arbitrary",)),
        name="router",
    )(hf, wrt_pad, rb_pad)


MOE_BLK = 256


def _moe_ffn_kernel(tok_ref, slot_ref, be_ref, hf_hbm, wg_ref, wu_ref, wd_ref, y_hbm,
                    xbuf, ybuf, gsem, ssem, *, nb):
    b = pl.program_id(0)
    nrow = MOE_BLK
    s = lax.rem(b, 2)
    o = 1 - s

    def row_in(idx, sl, r):
        return pltpu.make_async_copy(hf_hbm.at[pl.ds(idx, 1)], xbuf.at[sl, pl.ds(r, 1)], gsem.at[sl])

    def row_out(idx, sl, r):
        return pltpu.make_async_copy(ybuf.at[sl, pl.ds(r, 1)], y_hbm.at[pl.ds(idx, 1)], ssem.at[sl])

    @pl.when(b == 0)
    def _():
        for r in range(nrow):
            row_in(tok_ref[r], 0, r).start()

    for r in range(nrow):
        row_in(0, s, r).wait()

    @pl.when(b >= 2)
    def _():
        for r in range(nrow):
            row_out(0, s, r).wait()

    nxt = (b + 1) * nrow
    for r in range(nrow):
        row_in(tok_ref[nxt + r], o, r).start()

    x = xbuf[s].astype(BF)
    hg = jnp.dot(x, wg_ref[...], preferred_element_type=F32)
    hu = jnp.dot(x, wu_ref[...], preferred_element_type=F32)
    act = (hg * _sig(hg) * hu).astype(BF)
    ybuf[s] = jnp.dot(act, wd_ref[...], preferred_element_type=F32)
    cur = b * nrow
    for r in range(nrow):
        row_out(slot_ref[cur + r], s, r).start()

    @pl.when(b == nb - 1)
    def _():
        for r in range(nrow):
            row_in(0, o, r).wait()
        for r in range(nrow):
            row_out(0, s, r).wait()
        for r in range(nrow):
            row_out(0, o, r).wait()


def _moe_ffn_call(row_tok, row_slot, block_e, hf_all, wg, wu, wd):
    nr = row_slot.shape[0]
    d = hf_all.shape[1]
    nb = nr // MOE_BLK
    assert nb >= 2
    wspec = pl.BlockSpec((None, d, d), lambda b, tok, slot, be: (be[b], 0, 0))
    return pl.pallas_call(
        functools.partial(_moe_ffn_kernel, nb=nb),
        out_shape=jax.ShapeDtypeStruct((nr, d), F32),
        grid_spec=pltpu.PrefetchScalarGridSpec(
            num_scalar_prefetch=3,
            grid=(nb,),
            in_specs=[pl.BlockSpec(memory_space=pl.ANY), wspec, wspec, wspec],
            out_specs=pl.BlockSpec(memory_space=pl.ANY),
            scratch_shapes=[pltpu.VMEM((2, MOE_BLK, d), F32), pltpu.VMEM((2, MOE_BLK, d), F32),
                            pltpu.SemaphoreType.DMA((2,)), pltpu.SemaphoreType.DMA((2,))]),
        compiler_params=_cparams(("arbitrary",)),
        name="moe_ffn",
    )(row_tok, row_slot, block_e, hf_all, wg, wu, wd)


def _moe_dispatch(sel, cw):
    t_all = sel.shape[0]
    blk = MOE_BLK
    nb = (2 * t_all) // blk + N_EXPERTS
    nr = nb * blk
    mi = (sel > 0.5).astype(jnp.int32)
    rank = jnp.cumsum(mi, axis=0) - mi
    counts = jnp.sum(mi, axis=0)
    padded = ((counts + blk - 1) // blk) * blk
    pend = jnp.cumsum(padded)
    dest = pend - padded + rank
    kk = jnp.cumsum(mi, axis=1) - mi
    first = (mi > 0) & (kk == 0)
    second = (mi > 0) & (kk == 1)
    d0 = jnp.sum(jnp.where(first, dest, 0), axis=1)
    d1 = jnp.sum(jnp.where(second, dest, 0), axis=1)
    w0 = jnp.sum(jnp.where(first, cw, 0.0), axis=1, keepdims=True)
    w1 = jnp.sum(jnp.where(second, cw, 0.0), axis=1, keepdims=True)
    tok2 = 2 * jnp.arange(t_all, dtype=jnp.int32)
    packed = jnp.full((nr,), -1, jnp.int32).at[jnp.concatenate([d0, d1])].set(
        jnp.concatenate([tok2, tok2 + 1]), unique_indices=True)
    is_pad = packed < 0
    row_tok = jnp.where(is_pad, 0, jnp.right_shift(packed, 1))
    block_e = jnp.clip(jnp.searchsorted(pend, jnp.arange(nb, dtype=jnp.int32) * blk, side="right"),
                       0, N_EXPERTS - 1).astype(jnp.int32)
    real_before = jnp.repeat(jnp.cumsum(counts)[block_e], blk)
    pad_slot = 2 * t_all + jnp.arange(nr, dtype=jnp.int32) - real_before
    row_slot = jnp.where(is_pad, pad_slot, jnp.bitwise_and(packed, 1) * t_all + row_tok)
    row_tok = jnp.concatenate([row_tok, jnp.zeros((blk,), jnp.int32)])
    return row_tok, row_slot, block_e, w0, w1


def _combine_kernel(x_ref, y0_ref, y1_ref, w0_ref, w1_ref, gate_ref, lng_ref, lnb_ref, o_ref):
    y = y0_ref[...] * w0_ref[...] + y1_ref[...] * w1_ref[...]
    o_ref[...] = _layer_norm(ALPHA * x_ref[...] + gate_ref[...] * y, lng_ref[...], lnb_ref[...])


def _combine_call(x, y_slots, w0, w1, row_off, t_all, gate, lng, lnb, tm):
    L, d = x.shape
    off0 = row_off // tm
    off1 = (t_all + row_off) // tm
    vec = _full((1, d))
    return pl.pallas_call(
        _combine_kernel,
        out_shape=jax.ShapeDtypeStruct((L, d), F32),
        grid=(L // tm,),
        in_specs=[pl.BlockSpec((tm, d), lambda i: (i, 0)),
                  pl.BlockSpec((tm, d), lambda i: (off0 + i, 0)),
                  pl.BlockSpec((tm, d), lambda i: (off1 + i, 0)),
                  pl.BlockSpec((tm, 1), lambda i: (off0 + i, 0)),
                  pl.BlockSpec((tm, 1), lambda i: (off0 + i, 0)), vec, vec, vec],
        out_specs=pl.BlockSpec((tm, d), lambda i: (i, 0)),
        compiler_params=_cparams(("arbitrary",)),
        name="moe_combine",
    )(x, y_slots, y_slots, w0, w1, gate, lng, lnb)


def _moe_layer(segs, wrt_pad, rb_pad, lng, lnb, wg, wu, wd):
    routed = [_router_call(hf, wrt_pad, rb_pad) for (_, hf, _) in segs]
    cw = jnp.concatenate([r[0][:, :N_EXPERTS] for r in routed], axis=0)
    sel = jnp.concatenate([r[1][:, :N_EXPERTS] for r in routed], axis=0)
    hf_all = jnp.concatenate([hf for (_, hf, _) in segs], axis=0) if len(segs) > 1 else segs[0][1]
    t_all = hf_all.shape[0]
    row_tok, row_slot, block_e, w0, w1 = _moe_dispatch(sel, cw)
    y_slots = _moe_ffn_call(row_tok, row_slot, block_e, hf_all, wg, wu, wd)
    tm = min(256, min(x.shape[0] for (x, _, _) in segs))
    outs = []
    off = 0
    for (x, _, gate) in segs:
        assert off % tm == 0 and t_all % tm == 0 and x.shape[0] % tm == 0
        outs.append(_combine_call(x, y_slots, w0, w1, off, t_all, gate, lng, lnb, tm))
        off += x.shape[0]
    return outs


def _blockdiag2(m):
    z = jnp.zeros_like(m[0])
    return jnp.concatenate([jnp.concatenate([m[0], z], axis=1), jnp.concatenate([z, m[1]], axis=1)], axis=0)


def _mixer(x, mod_row, l, w_in_bf, prm, fp, hy, mp, s0, latent):
    L = x.shape[0]
    sh, sc, gate = mod_row[0], mod_row[1], mod_row[2]
    p_r, p_h, p_g = _inproj_call(x, sh, sc, w_in_bf)
    r, v, ka, lw, k, kb, lg, bv = _prepare_call(p_r, prm, latent)
    yf, yb, sfin = _scan_call(r, v, ka, lw, k, kb, s0)
    filt, asum = _hfilt_call(L, fp)
    z, x0 = _hconv3_call(p_h, hy["conv_w"], hy["conv_b"])
    ho = _long_conv_call(z, x0, filt, asum, hy["bias"])
    mpl = dict(mp)
    mpl["gate"], mpl["sh2"], mpl["sc2"] = gate, mod_row[3], mod_row[4]
    xn, hf = _merge_call(yf, yb, bv, lg, ho, p_g, x, mpl)
    return xn, hf, sfin


def kernel(x, c, ctx, c_ctx, w_mod, b_mod, w_in, rwkv_mu, rwkv_w0, rwkv_w2, rwkv_a0, rwkv_a2, rwkv_g2,
           rwkv_k_k, rwkv_k_a, rwkv_r_k, rwkv_lnx_g, rwkv_lnx_b, hy_conv_w, hy_conv_b, hy_f_w1, hy_f_b1,
           hy_f_w2, hy_f_b2, hy_f_w3, hy_f_b3, hy_f_wout, hy_freq, hy_bias, w_branch, w_out, ln_g, ln_b,
           w_router, router_bias, w_gate, w_up, w_down):
    b, n_lat, d = x.shape
    assert b == 1 and d == D_MODEL
    n_ctx = ctx.shape[1]
    depth = w_mod.shape[0]
    hw = RWKV_WIDTH
    xl = x[0]
    xc = ctx[0]

    cc = jnp.concatenate([c[:1], c_ctx[None, :], jnp.zeros((6, d), F32)], axis=0)
    mod = _mod_call(cc, w_mod, b_mod)

    head_of = np.arange(hw) // HEAD_DIM
    G = jnp.asarray((head_of[:, None] == head_of[None, :]).astype(np.float32), dtype=BF)
    bands = jnp.linspace(1e-4, FILTER_BANDS - 1, FILTER_BANDS, dtype=F32)[None, :]
    deltas = jnp.abs(jnp.linspace(HYENA_MIN_DECAY, HYENA_MAX_DECAY, HYENA_WIDTH, dtype=F32))[None, :]
    wr_pad = jnp.pad(w_router.T, ((0, LANES - N_EXPERTS), (0, 0)))
    rb_pad = jnp.pad(router_bias, (0, LANES - N_EXPERTS))[:, None]
    w_in_bf = w_in.astype(BF)
    w_branch_bf = w_branch.astype(BF)
    w_out_bf = w_out.astype(BF)
    g2_bf = rwkv_g2.astype(BF)
    wg_bf, wu_bf, wd_bf = w_gate.astype(BF), w_up.astype(BF), w_down.astype(BF)

    for l in range(depth):
        last = l == depth - 1
        ml = [mod[l, 0:1, j * d:(j + 1) * d] for j in range(6)]
        mc = [mod[l, 1:2, j * d:(j + 1) * d] for j in range(6)]
        prm = dict(mu=rwkv_mu[l][None, :],
                   w0=rwkv_w0[l].reshape(1, 2 * hw), w2=_blockdiag2(rwkv_w2[l]),
                   a0=rwkv_a0[l].reshape(1, 2 * hw), a2=_blockdiag2(rwkv_a2[l]),
                   k_k=rwkv_k_k[l][None, :], k_a=rwkv_k_a[l][None, :], r_k=rwkv_r_k[l][None, :], G=G)
        w1 = hy_f_w1[l]
        fp = dict(bands=bands, w1t=w1[0:1], w1c=w1[1:1 + FILTER_BANDS], w1s=w1[1 + FILTER_BANDS:],
                  b1=hy_f_b1[l][None, :], w2=hy_f_w2[l], b2=hy_f_b2[l][None, :], w3=hy_f_w3[l],
                  b3=hy_f_b3[l][None, :], wout=hy_f_wout[l], freq=hy_freq[l][None, :], deltas=deltas)
        hy = dict(conv_w=hy_conv_w[l], conv_b=hy_conv_b[l][None, :], bias=hy_bias[l][None, :])
        mp = dict(lnx_g=rwkv_lnx_g[l][None, :], lnx_b=rwkv_lnx_b[l][None, :], g2=g2_bf[l], G=G,
                  w_branch=w_branch_bf[l], w_out=w_out_bf[l], ln_g=ln_g[l, 0][None, :], ln_b=ln_b[l, 0][None, :])
        s0 = jnp.zeros((2, HEAD_DIM, hw), F32)
        xc_new, hf_c, s_ctx = _mixer(xc, mc, l, w_in_bf[l], prm, fp, hy, mp, s0, latent=False)
        xl, hf_l, _ = _mixer(xl, ml, l, w_in_bf[l], prm, fp, hy, mp, s_ctx, latent=True)
        lng, lnb = ln_g[l, 1][None, :], ln_b[l, 1][None, :]
        if last:
            (xl,) = _moe_layer([(xl, hf_l, ml[5])], wr_pad, rb_pad, lng, lnb, wg_bf[l], wu_bf[l], wd_bf[l])
        else:
            xc, xl = _moe_layer([(xc_new, hf_c, mc[5]), (xl, hf_l, ml[5])], wr_pad, rb_pad, lng, lnb,
                                wg_bf[l], wu_bf[l], wd_bf[l])
    return xl[None]
```

```python
import functools
import math

import numpy as np
import jax
import jax.numpy as jnp
from jax import lax
from jax.experimental import pallas as pl
from jax.experimental.pallas import tpu as pltpu

F32 = jnp.float32
BF = jnp.bfloat16

D_MODEL = 1024
DEPTH = 4
GRID_W = 64
RWKV_WIDTH = 512
HEAD_DIM = 64
RWKV_HEADS = 8
LORA_W = 64
LORA_A = 64
LORA_G = 128
DECAY_SCALE = 0.606531
GN_EPS = 6.4e-4
RWKV_COLS = 3 * RWKV_WIDTH + 2 * LORA_W + 2 * LORA_A + LORA_G
HYENA_WIDTH = 512
HYENA_COLS = 3 * HYENA_WIDTH
FILTER_BANDS = 16
FILTER_HIDDEN = 64
HYENA_MIN_DECAY = math.log(1e-2) / 1.5
HYENA_MAX_DECAY = math.log(1e-2) / 0.3
GATE_COLS = 2 * D_MODEL
PROJ_COLS = RWKV_COLS + HYENA_COLS + GATE_COLS
N_EXPERTS = 16
N_GROUPS = 4
EXPERTS_PER_GROUP = 4
ALPHA = (2 * DEPTH) ** 0.25
LN_EPS = 1e-5

SCAN_CHUNK = 64
LANES = 128
VMEM_LIMIT = 56 * 1024 * 1024

_NN = (((1,), (0,)), ((), ()))
_NT = (((1,), (1,)), ((), ()))
_TN = (((0,), (0,)), ((), ()))


def _sig(x):
    return 1.0 / (1.0 + jnp.exp(-x))


def _parts(a, n):
    out = []
    rem = a
    for i in range(n):
        hi = rem.astype(BF)
        out.append(hi)
        if i + 1 < n:
            rem = rem - hi.astype(F32)
    return out


def _mm(a, b, dn=_NN, passes=1):
    n = {1: 1, 3: 2, 6: 3}[passes]
    pa = _parts(a, n)
    pb = _parts(b, n)
    acc = None
    for i in range(n):
        for j in range(n - i):
            t = lax.dot_general(pa[i], pb[j], dn, preferred_element_type=F32)
            acc = t if acc is None else acc + t
    return acc


def _mm_rx(a, b_exact, n=3, dn=_NN):
    acc = None
    for p in _parts(a, n):
        t = lax.dot_general(p, b_exact, dn, preferred_element_type=F32)
        acc = t if acc is None else acc + t
    return acc


def _mm_lx(a_exact, b, n=3, dn=_NN):
    acc = None
    for p in _parts(b, n):
        t = lax.dot_general(a_exact, p, dn, preferred_element_type=F32)
        acc = t if acc is None else acc + t
    return acc


def _cparams(sem, vmem=VMEM_LIMIT):
    return pltpu.CompilerParams(dimension_semantics=sem, vmem_limit_bytes=vmem)


def _full(shape):
    nd = len(shape)
    return pl.BlockSpec(shape, lambda *_: (0,) * nd)


def _mod_kernel(c_ref, w_ref, b_ref, o_ref):
    cc = c_ref[...]
    s = cc * _sig(cc)
    o_ref[...] = jnp.dot(s.astype(BF), w_ref[...].astype(BF), preferred_element_type=F32) + b_ref[...]


def _mod_call(cc, w_mod, b_mod):
    depth, d, n6 = w_mod.shape
    tn = 1536
    return pl.pallas_call(
        _mod_kernel,
        out_shape=jax.ShapeDtypeStruct((depth, 8, n6), F32),
        grid=(depth, n6 // tn),
        in_specs=[
            pl.BlockSpec((8, d), lambda l, j: (0, 0)),
            pl.BlockSpec((None, d, tn), lambda l, j: (l, 0, j)),
            pl.BlockSpec((None, 1, tn), lambda l, j: (l, 0, j)),
        ],
        out_specs=pl.BlockSpec((None, 8, tn), lambda l, j: (l, 0, j)),
        compiler_params=_cparams(("arbitrary", "arbitrary")),
        name="mod",
    )(cc, w_mod, b_mod.reshape(depth, 1, n6))


def _inproj_kernel(x_ref, sh_ref, sc_ref, w_ref, pr_ref, ph_ref, pg_ref):
    xm = (x_ref[...] * (1.0 + sc_ref[...]) + sh_ref[...]).astype(BF)
    pr_ref[...] = jnp.dot(xm, w_ref[:, :RWKV_COLS], preferred_element_type=F32)
    ph_ref[...] = jnp.dot(xm, w_ref[:, RWKV_COLS:RWKV_COLS + HYENA_COLS], preferred_element_type=F32)
    pg_ref[...] = jnp.dot(xm, w_ref[:, RWKV_COLS + HYENA_COLS:], preferred_element_type=F32)


def _inproj_call(x, sh, sc, w_bf):
    L, d = x.shape
    tm = min(L, 256)
    row = lambda i: (i, 0)
    return pl.pallas_call(
        _inproj_kernel,
        out_shape=(jax.ShapeDtypeStruct((L, RWKV_COLS), F32),
                   jax.ShapeDtypeStruct((L, HYENA_COLS), F32),
                   jax.ShapeDtypeStruct((L, GATE_COLS), F32)),
        grid=(L // tm,),
        in_specs=[pl.BlockSpec((tm, d), row), _full((1, d)), _full((1, d)), _full((d, PROJ_COLS))],
        out_specs=(pl.BlockSpec((tm, RWKV_COLS), row), pl.BlockSpec((tm, HYENA_COLS), row),
                   pl.BlockSpec((tm, GATE_COLS), row)),
        compiler_params=_cparams(("arbitrary",)),
        name="inproj",
    )(x, sh, sc, w_bf)


def _prepare_kernel(*refs, latent, nblk):
    if latent:
        p_ref, up_ref, dn_ref = refs[:3]
        refs = refs[3:]
    else:
        p_ref = refs[0]
        refs = refs[1:]
    (mu_ref, w0_ref, w2_ref, a0_ref, a2_ref, kk_ref, ka_ref, rk_ref, g_ref,
     r_out, v_out, ka_out, lw_out, k_out, kb_out, lg_out, bv_out) = refs
    i = pl.program_id(0)
    p = p_ref[...]
    tb, w = p.shape
    row = lax.broadcasted_iota(jnp.int32, (tb, 1), 0)
    col = lax.broadcasted_iota(jnp.int32, (1, w), 1)
    if latent:
        gw = jnp.bitwise_and(row, GRID_W - 1)
        left = jnp.where(gw == 0, 0.0, pltpu.roll(p, 1, 0))
        right = jnp.where(gw == GRID_W - 1, 0.0, pltpu.roll(p, tb - 1, 0))
        upv = jnp.where(i == 0, 0.0, up_ref[...])
        dnv = jnp.where(i == nblk - 1, 0.0, dn_ref[...])
        if tb > GRID_W:
            up = jnp.concatenate([upv, p[:tb - GRID_W]], axis=0)
            down = jnp.concatenate([p[GRID_W:], dnv], axis=0)
        else:
            up, down = upv, dnv
        q = w // 4
        sh = jnp.where(col < q, left, jnp.where(col < 2 * q, right, jnp.where(col < 3 * q, up, down)))
    else:
        prev = jnp.where(row == 0, 0.0, pltpu.roll(p, 1, 0))
        nxt = jnp.where(row == tb - 1, 0.0, pltpu.roll(p, tb - 1, 0))
        sh = jnp.where(col < w // 2, prev, nxt)
    pm = p + (sh - p) * mu_ref[...]
    hw = RWKV_WIDTH
    r = pm[:, :hw]
    k = pm[:, hw:2 * hw]
    v = pm[:, 2 * hw:3 * hw]
    lw_in = jnp.tanh(pm[:, 3 * hw:3 * hw + 2 * LORA_W])
    la_in = pm[:, 3 * hw + 2 * LORA_W:3 * hw + 2 * LORA_W + 2 * LORA_A]
    lg = pm[:, 3 * hw + 2 * LORA_W + 2 * LORA_A:]
    lw = -DECAY_SCALE * _sig(w0_ref[...] + _mm(lw_in, w2_ref[...], passes=3))
    a = _sig(a0_ref[...] + _mm(la_in, a2_ref[...], passes=3))
    g = g_ref[...]
    kkr = k * kk_ref[...]
    nrm = jnp.sqrt(_mm_rx(kkr * kkr, g, 3))
    kk = kkr / jnp.maximum(nrm, 1e-12)
    k_a = ka_ref[...]
    k_f = k * (1.0 + (a[:, :hw] - 1.0) * k_a)
    k_b = k * (1.0 + (a[:, hw:] - 1.0) * k_a)
    bonus = _mm_rx(r * (k_f + k_b) * rk_ref[...], g, 3)
    r_out[...] = r
    v_out[...] = v
    ka_out[...] = -kk
    lw_out[0] = lw[:, :hw]
    lw_out[1] = lw[:, hw:]
    k_out[0] = k_f
    k_out[1] = k_b
    kb_out[0] = kk * a[:, :hw]
    kb_out[1] = kk * a[:, hw:]
    lg_out[...] = lg
    bv_out[...] = bonus * v


def _prepare_call(p_r, prm, latent):
    L, w = p_r.shape
    hw = RWKV_WIDTH
    tb = 256 if latent else L
    nblk = L // tb
    row = lambda i: (i, 0)
    row3 = lambda i: (0, i, 0)
    in_specs = [pl.BlockSpec((tb, w), row)]
    args = [p_r]
    if latent:
        per = tb // GRID_W
        nrow = L // GRID_W
        in_specs += [
            pl.BlockSpec((GRID_W, w), lambda i: (jnp.maximum(i * per - 1, 0), 0)),
            pl.BlockSpec((GRID_W, w), lambda i: (jnp.minimum((i + 1) * per, nrow - 1), 0)),
        ]
        args += [p_r, p_r]
    names = ("mu", "w0", "w2", "a0", "a2", "k_k", "k_a", "r_k", "G")
    for nm in names:
        in_specs.append(_full(prm[nm].shape))
        args.append(prm[nm])
    sd = jax.ShapeDtypeStruct
    out_shape = (sd((L, hw), F32), sd((L, hw), F32), sd((L, hw), F32),
                 sd((2, L, hw), F32), sd((2, L, hw), F32), sd((2, L, hw), F32),
                 sd((L, LORA_G), F32), sd((L, hw), F32))
    out_specs = (pl.BlockSpec((tb, hw), row), pl.BlockSpec((tb, hw), row), pl.BlockSpec((tb, hw), row),
                 pl.BlockSpec((2, tb, hw), row3), pl.BlockSpec((2, tb, hw), row3), pl.BlockSpec((2, tb, hw), row3),
                 pl.BlockSpec((tb, LORA_G), row), pl.BlockSpec((tb, hw), row))
    return pl.pallas_call(
        functools.partial(_prepare_kernel, latent=latent, nblk=nblk),
        out_shape=out_shape, grid=(nblk,), in_specs=in_specs, out_specs=out_specs,
        compiler_params=_cparams(("arbitrary",)),
        name="rwkv_prepare_lat" if latent else "rwkv_prepare_ctx",
    )(*args)


P_SCORE = 3
P_TINV = 3
P_APPLY = 1
P_STATE = 1


def _scan_kernel(rf_ref, vf_ref, kaf_ref, rb_ref, vb_ref, kab_ref, lwf_ref, kf_ref, kbf_ref,
                 lwb_ref, kbk_ref, kbb_ref, s0_ref, yf_ref, yb_ref, sfin_ref, s_scr, *, nchunk):
    c = pl.program_id(0)
    C = SCAN_CHUNK
    hd = HEAD_DIM
    nh = RWKV_HEADS

    @pl.when(c == 0)
    def _():
        s_scr[...] = s0_ref[...]

    row = lax.broadcasted_iota(jnp.int32, (C, C), 0)
    col = lax.broadcasted_iota(jnp.int32, (C, C), 1)
    eye = jnp.where(row == col, 1.0, 0.0)
    lvl_masks = []
    for sh in range(int(math.log2(C))):
        same2 = jnp.right_shift(row, sh + 1) == jnp.right_shift(col, sh + 1)
        same1 = jnp.right_shift(row, sh) == jnp.right_shift(col, sh)
        lvl_masks.append(jnp.logical_and(same2, jnp.logical_not(same1)))

    dirs = []
    for d, (r_ref, v_ref, ka_ref, lw_ref, k_ref, kb_ref) in enumerate((
            (rf_ref, vf_ref, kaf_ref, lwf_ref, kf_ref, kbf_ref),
            (rb_ref, vb_ref, kab_ref, lwb_ref, kbk_ref, kbb_ref))):
        inc = (row >= col) if d == 0 else (row <= col)
        strict = (row > col) if d == 0 else (row < col)
        lw = lw_ref[...]
        lc = _mm_lx(jnp.where(inc, 1.0, 0.0).astype(BF), lw, 3)
        e_neg = jnp.exp(-lc)
        e_tot = jnp.exp(jnp.sum(lw, axis=0, keepdims=True))
        kbn = kb_ref[...] * e_neg
        kkn = k_ref[...] * e_neg
        dirs.append(dict(inc=inc, strict=strict, v=v_ref[...], aq=ka_ref[...] * jnp.exp(lc - lw),
                         rq=r_ref[...] * jnp.exp(lc), kbn=kbn, kkn=kkn, kbp=kbn * e_tot, kkp=kkn * e_tot,
                         e_tot=e_tot, s=s_scr[d]))

    units = [(d, h) for h in range(nh) for d in range(2)]
    hs = lambda arr, h: arr[:, h * hd:(h + 1) * hd]
    sc = {}
    for (d, h) in units:
        D = dirs[d]
        sc[d, h] = _mm(jnp.concatenate([hs(D["aq"], h), hs(D["rq"], h)], axis=0),
                       jnp.concatenate([hs(D["kbn"], h), hs(D["kkn"], h)], axis=0), _NT, P_SCORE)
    a_ab, t, x = {}, {}, {}
    for u in units:
        a_ab[u] = jnp.where(dirs[u[0]]["strict"], sc[u][:C, :C], 0.0)
        t[u] = eye + jnp.where(lvl_masks[0], a_ab[u], 0.0)
    for u in units:
        D = dirs[u[0]]
        lhs = jnp.concatenate([jnp.where(D["strict"], sc[u][:C, C:], 0.0),
                               jnp.where(D["inc"], sc[u][C:, C:], 0.0)], axis=0)
        x[u] = _mm(lhs, hs(D["v"], u[1]), _NN, P_APPLY)
    for m in lvl_masks[1:]:
        tmp = {u: _mm(jnp.where(m, a_ab[u], 0.0), t[u], _NN, 1) for u in units}
        for u in units:
            t[u] = t[u] + _mm(t[u], tmp[u], _NN, 1)
    res = {u: eye - t[u] + _mm(a_ab[u], t[u], _NN, P_TINV) for u in units}
    for u in units:
        t[u] = t[u] + _mm(t[u], res[u], _NN, 1)
    z, gm, uu, yy, sn = {}, {}, {}, {}, {}
    for u in units:
        z[u] = _mm(t[u], jnp.concatenate([hs(dirs[u[0]]["aq"], u[1]), x[u][:C]], axis=1), _NN, P_APPLY)
    for u in units:
        D = dirs[u[0]]
        gm[u] = _mm(jnp.concatenate([z[u][:, :hd], hs(D["rq"], u[1])], axis=0), hs(D["s"], u[1]), _NT, P_STATE)
        uu[u] = gm[u][:C] + z[u][:, hd:]
    for u in units:
        D = dirs[u[0]]
        b_rb = jnp.where(D["inc"], sc[u][C:, :C], 0.0)
        yy[u] = gm[u][C:] + _mm(b_rb, uu[u], _NN, P_APPLY) + x[u][C:]
        sn[u] = hs(D["s"], u[1]) * hs(D["e_tot"], u[1]) + _mm(
            jnp.concatenate([uu[u], hs(D["v"], u[1])], axis=0),
            jnp.concatenate([hs(D["kbp"], u[1]), hs(D["kkp"], u[1])], axis=0), _TN, P_STATE)
    yf_ref[...] = jnp.concatenate([yy[0, h] for h in range(nh)], axis=1)
    yb_ref[...] = jnp.concatenate([yy[1, h] for h in range(nh)], axis=1)
    s_scr[0] = jnp.concatenate([sn[0, h] for h in range(nh)], axis=1)
    s_scr[1] = jnp.concatenate([sn[1, h] for h in range(nh)], axis=1)

    @pl.when(c == nchunk - 1)
    def _():
        sfin_ref[...] = s_scr[...]


def _scan_call(r, v, ka, lw, k, kb, s0):
    L, hw = r.shape
    C = SCAN_CHUNK
    nchunk = L // C
    sh_f = pl.BlockSpec((C, hw), lambda c: (c, 0))
    sh_b = pl.BlockSpec((C, hw), lambda c: (nchunk - 1 - c, 0))
    pd_f = pl.BlockSpec((None, C, hw), lambda c: (0, c, 0))
    pd_b = pl.BlockSpec((None, C, hw), lambda c: (1, nchunk - 1 - c, 0))
    state = _full((2, HEAD_DIM, hw))
    return pl.pallas_call(
        functools.partial(_scan_kernel, nchunk=nchunk),
        out_shape=(jax.ShapeDtypeStruct((L, hw), F32), jax.ShapeDtypeStruct((L, hw), F32),
                   jax.ShapeDtypeStruct((2, HEAD_DIM, hw), F32)),
        grid=(nchunk,),
        in_specs=[sh_f, sh_f, sh_f, sh_b, sh_b, sh_b, pd_f, pd_f, pd_f, pd_b, pd_b, pd_b, state],
        out_specs=(sh_f, sh_b, state),
        scratch_shapes=[pltpu.VMEM((2, HEAD_DIM, hw), F32)],
        compiler_params=_cparams(("arbitrary",)),
        name="delta_scan",
    )(r, v, ka, r, v, ka, lw, k, kb, lw, k, kb, s0)


def _hfilt_kernel(bands_ref, w1t_ref, w1c_ref, w1s_ref, b1_ref, w2_ref, b2_ref, w3_ref, b3_ref, wo_ref,
                  fr_ref, dl_ref, f_ref, asum_ref, *, n, rb):
    i = pl.program_id(0)
    pos = (i * rb + lax.broadcasted_iota(jnp.int32, (rb, 1), 0)).astype(F32)
    t = pos / float(max(n - 1, 1))
    ang = ((2.0 * math.pi / n) * pos) * bands_ref[...]
    fr = fr_ref[...]
    h = t * w1t_ref[...] + _mm(jnp.cos(ang), w1c_ref[...], passes=3) + _mm(-jnp.sin(ang), w1s_ref[...], passes=3)
    h = jnp.sin(fr * (h + b1_ref[...]))
    h = jnp.sin(fr * (_mm(h, w2_ref[...], passes=3) + b2_ref[...]))
    h = jnp.sin(fr * (_mm(h, w3_ref[...], passes=3) + b3_ref[...]))
    filt = _mm(h, wo_ref[...], passes=3)
    dist = jnp.abs(pos - float(n // 2)) * (2.0 / n)
    filt = filt * jnp.exp(-dist * dl_ref[...])
    f_ref[...] = filt

    @pl.when(i == 0)
    def _():
        asum_ref[...] = jnp.zeros_like(asum_ref)

    asum_ref[...] += jnp.sum(jnp.abs(filt), axis=0, keepdims=True)


def _hfilt_call(n, fp):
    rb = min(n, 512)
    names = ("bands", "w1t", "w1c", "w1s", "b1", "w2", "b2", "w3", "b3", "wout", "freq", "deltas")
    args = [fp[nm] for nm in names]
    return pl.pallas_call(
        functools.partial(_hfilt_kernel, n=n, rb=rb),
        out_shape=(jax.ShapeDtypeStruct((n, HYENA_WIDTH), F32), jax.ShapeDtypeStruct((1, HYENA_WIDTH), F32)),
        grid=(n // rb,),
        in_specs=[_full(a.shape) for a in args],
        out_specs=(pl.BlockSpec((rb, HYENA_WIDTH), lambda i: (i, 0)), _full((1, HYENA_WIDTH))),
        compiler_params=_cparams(("arbitrary",)),
        name="hyena_filter",
    )(*args)


def _hconv3_kernel(p_ref, pv_ref, nx_ref, cw_ref, cb_ref, z_ref, x0_ref, *, nblk):
    i = pl.program_id(0)
    p = p_ref[...]
    tb = p.shape[0]
    row = lax.broadcasted_iota(jnp.int32, (tb, 1), 0)
    prev_row = jnp.where(i == 0, 0.0, pv_ref[7:8, :])
    next_row = jnp.where(i == nblk - 1, 0.0, nx_ref[0:1, :])
    sp = jnp.where(row == 0, prev_row, pltpu.roll(p, 1, 0))
    sn = jnp.where(row == tb - 1, next_row, pltpu.roll(p, tb - 1, 0))
    u = sp * cw_ref[0:1, :] + p * cw_ref[1:2, :] + sn * cw_ref[2:3, :] + cb_ref[...]
    hw = HYENA_WIDTH
    z_ref[...] = u[:, 2 * hw:] * u[:, hw:2 * hw]
    x0_ref[...] = u[:, :hw]


def _hconv3_call(p_h, cw, cb):
    L, w = p_h.shape
    tb = min(L, 256)
    nblk = L // tb
    per = tb // 8
    row = lambda i: (i, 0)
    return pl.pallas_call(
        functools.partial(_hconv3_kernel, nblk=nblk),
        out_shape=(jax.ShapeDtypeStruct((L, HYENA_WIDTH), F32), jax.ShapeDtypeStruct((L, HYENA_WIDTH), F32)),
        grid=(nblk,),
        in_specs=[pl.BlockSpec((tb, w), row),
                  pl.BlockSpec((8, w), lambda i: (jnp.maximum(i * per - 1, 0), 0)),
                  pl.BlockSpec((8, w), lambda i: (jnp.minimum((i + 1) * per, L // 8 - 1), 0)),
                  _full(cw.shape), _full(cb.shape)],
        out_specs=(pl.BlockSpec((tb, HYENA_WIDTH), row), pl.BlockSpec((tb, HYENA_WIDTH), row)),
        compiler_params=_cparams(("arbitrary",)),
        name="hyena_conv3",
    )(p_h, p_h, p_h, cw, cb)


P_DFT = 1


def _dft1_kernel(m_ref, x_ref, o_ref):
    o_ref[...] = _mm(m_ref[...], x_ref[...], _NN, P_DFT)


def _dft1_call(m1, x2d):
    rows, kdim = m1.shape
    _, cols = x2d.shape
    cb = min(cols, 2048)
    return pl.pallas_call(
        _dft1_kernel,
        out_shape=jax.ShapeDtypeStruct((rows, cols), F32),
        grid=(cols // cb,),
        in_specs=[_full(m1.shape), pl.BlockSpec((kdim, cb), lambda j: (0, j))],
        out_specs=pl.BlockSpec((rows, cb), lambda j: (0, j)),
        compiler_params=_cparams(("arbitrary",)),
        name="dft_stage1",
    )(m1, x2d)


def _dft2_matrix(fc_ref, fs_ref, tc_ref, ts_ref):
    fc, fs = fc_ref[...], fs_ref[...]
    tc, ts = tc_ref[...], ts_ref[...]
    fre = fc * tc - fs * ts
    fim = -(fc * ts + fs * tc)
    return jnp.concatenate([jnp.concatenate([fre, -fim], axis=1),
                            jnp.concatenate([fim, fre], axis=1)], axis=0)


def _dft2_spec_kernel(fc_ref, fs_ref, tc_ref, ts_ref, a_ref, h_ref):
    big = _dft2_matrix(fc_ref, fs_ref, tc_ref, ts_ref)
    n2 = a_ref.shape[1]
    x = _mm(big, jnp.concatenate([a_ref[0], a_ref[1]], axis=0), _NN, P_DFT)
    h_ref[0] = x[:n2]
    h_ref[1] = x[n2:]


def _dft2_conv_kernel(fc_ref, fs_ref, tc_ref, ts_ref, a_ref, h_ref, b_ref):
    big = _dft2_matrix(fc_ref, fs_ref, tc_ref, ts_ref)
    n2 = a_ref.shape[1]
    x = _mm(big, jnp.concatenate([a_ref[0], a_ref[1]], axis=0), _NN, P_DFT)
    xr, xi = x[:n2], x[n2:]
    hr, hi = h_ref[0], h_ref[1]
    y = jnp.concatenate([xr * hr - xi * hi, xr * hi + xi * hr], axis=0)
    bb = _mm(big, y, _TN, P_DFT)
    b_ref[0] = bb[:n2]
    b_ref[1] = bb[n2:]


def _dft2_call(consts, a4, h4=None):
    _, n1, n2, c = a4.shape
    blk = pl.BlockSpec((2, None, n2, c), lambda q: (0, q, 0, 0))
    tw = pl.BlockSpec((None, 1, n2), lambda q: (q, 0, 0))
    in_specs = [_full((n2, n2)), _full((n2, n2)), tw, tw, blk]
    args = [consts["fc"], consts["fs"], consts["twc"], consts["tws"], a4]
    kern = _dft2_spec_kernel
    if h4 is not None:
        in_specs.append(blk)
        args.append(h4)
        kern = _dft2_conv_kernel
    return pl.pallas_call(
        kern,
        out_shape=jax.ShapeDtypeStruct(a4.shape, F32),
        grid=(n1,),
        in_specs=in_specs,
        out_specs=blk,
        compiler_params=_cparams(("arbitrary",)),
        name="dft_stage2_spec" if h4 is None else "dft_stage2_conv",
    )(*args)


def _dft3_kernel(m_ref, b_ref, z_ref, x0_ref, bias_ref, asum_ref, o_ref):
    y = _mm(m_ref[...], b_ref[...], _NN, P_DFT)
    z = z_ref[...]
    o_ref[...] = (y / asum_ref[...] + z * bias_ref[...]) * x0_ref[...]


def _dft3_call(m3, b2d, z2d, x02d, bias_t, asum_t):
    rows, kdim = m3.shape
    _, cols = b2d.shape
    cb = bias_t.shape[1]
    colb = lambda j: (0, j)
    return pl.pallas_call(
        _dft3_kernel,
        out_shape=jax.ShapeDtypeStruct((rows, cols), F32),
        grid=(cols // cb,),
        in_specs=[_full(m3.shape), pl.BlockSpec((kdim, cb), colb), pl.BlockSpec((rows, cb), colb),
                  pl.BlockSpec((rows, cb), colb), _full((1, cb)), _full((1, cb))],
        out_specs=pl.BlockSpec((rows, cb), colb),
        compiler_params=_cparams(("arbitrary",)),
        name="dft_stage3",
    )(m3, b2d, z2d, x02d, bias_t, asum_t)


def _conv_direct_kernel(m1_ref, m3_ref, z_ref, f_ref, x0_ref, bias_ref, asum_ref, o_ref):
    m1 = m1_ref[...]
    z = z_ref[...]
    a = _mm(m1, z, _NN, P_DFT)
    h = _mm(m1, f_ref[...], _NN, P_DFT)
    nn = a.shape[0] // 2
    ar, ai, hr, hi = a[:nn], a[nn:], h[:nn], h[nn:]
    y = jnp.concatenate([ar * hr - ai * hi, ar * hi + ai * hr], axis=0)
    out = _mm(m3_ref[...], y, _NN, P_DFT)
    o_ref[...] = (out / asum_ref[...] + z * bias_ref[...]) * x0_ref[...]


def _conv_direct_call(m1, m3, z, filt, x0, bias, asum):
    args = (m1, m3, z, filt, x0, bias, asum)
    return pl.pallas_call(
        _conv_direct_kernel,
        out_shape=jax.ShapeDtypeStruct(z.shape, F32),
        grid=(1,),
        in_specs=[_full(a.shape) for a in args],
        out_specs=_full(z.shape),
        compiler_params=_cparams(("arbitrary",)),
        name="long_conv_direct",
    )(*args)


DIRECT_CONV_MAX = 256


@functools.lru_cache(maxsize=None)
def _dft_consts_np(n):
    big_n = 2 * n
    if n <= DIRECT_CONV_MAX:
        n1 = big_n
    else:
        n1 = 256 if n >= 8192 else 64
    n2 = big_n // n1
    k1 = np.arange(n1)[:, None].astype(np.float64)
    j1 = np.arange(n1 // 2)[None, :].astype(np.float64)
    ang1 = 2.0 * np.pi * k1 * j1 / n1
    m1 = np.concatenate([np.cos(ang1), -np.sin(ang1)], axis=0)
    o1 = (n1 // 4 + np.arange(n1 // 2))[:, None].astype(np.float64)
    q1 = np.arange(n1)[None, :].astype(np.float64)
    ang3 = 2.0 * np.pi * o1 * q1 / n1
    m3 = np.concatenate([np.cos(ang3), -np.sin(ang3)], axis=1) / big_n
    k2 = np.arange(n2)[:, None].astype(np.float64)
    j2 = np.arange(n2)[None, :].astype(np.float64)
    ang2 = 2.0 * np.pi * k2 * j2 / n2
    angt = 2.0 * np.pi * np.arange(n1)[:, None].astype(np.float64) * j2 / big_n
    f = lambda a: np.asarray(a, np.float32)
    return dict(n1=n1, n2=n2, m1=f(m1), m3=f(m3), fc=f(np.cos(ang2)), fs=f(np.sin(ang2)),
                twc=f(np.cos(angt))[:, None, :], tws=f(np.sin(angt))[:, None, :])


def _long_conv_call(z, x0, filt, asum, bias):
    n, c = z.shape
    cn = _dft_consts_np(n)
    n1, n2 = cn["n1"], cn["n2"]
    consts = {kk: jnp.asarray(vv) for kk, vv in cn.items() if kk not in ("n1", "n2")}
    if n2 == 1:
        return _conv_direct_call(consts["m1"], consts["m3"], z, filt, x0, bias, asum)
    cols = n2 * c
    cb = min(cols, 2048)
    a_f = _dft1_call(consts["m1"], filt.reshape(n1 // 2, cols)).reshape(2, n1, n2, c)
    h4 = _dft2_call(consts, a_f)
    a_z = _dft1_call(consts["m1"], z.reshape(n1 // 2, cols)).reshape(2, n1, n2, c)
    b4 = _dft2_call(consts, a_z, h4)
    out = _dft3_call(consts["m3"], b4.reshape(2 * n1, cols), z.reshape(n1 // 2, cols), x0.reshape(n1 // 2, cols),
                     jnp.tile(bias, (1, cb // c)), jnp.tile(asum, (1, cb // c)))
    return out.reshape(n, c)


def _layer_norm(x, g, b):
    mu = jnp.mean(x, axis=-1, keepdims=True)
    xc = x - mu
    var = jnp.mean(xc * xc, axis=-1, keepdims=True)
    return xc * lax.rsqrt(var + LN_EPS) * g + b


def _merge_kernel(yf_ref, yb_ref, bv_ref, lg_ref, ho_ref, pg_ref, x_ref, gx_ref, bx_ref, g2_ref, g_ref,
                  wb_ref, wo_ref, gate_ref, lng_ref, lnb_ref, sh2_ref, sc2_ref, o_ref, hf_ref):
    g = g_ref[...]
    ys = yf_ref[...] + yb_ref[...]
    inv_hd = 1.0 / HEAD_DIM
    mu = _mm_rx(ys, g, 3) * inv_hd
    dd = ys - mu
    var = _mm_rx(dd * dd, g, 3) * inv_hd
    yn = dd * lax.rsqrt(var + GN_EPS) * gx_ref[...] + bx_ref[...]
    gate_r = jnp.dot(_sig(lg_ref[...]).astype(BF), g2_ref[...], preferred_element_type=F32)
    ro = (yn + bv_ref[...]) * gate_r
    br = jnp.dot(ro.astype(BF), wb_ref[0], preferred_element_type=F32)
    bh = jnp.dot(ho_ref[...].astype(BF), wb_ref[1], preferred_element_type=F32)
    sg = _sig(pg_ref[...])
    m = sg[:, :D_MODEL] * br + sg[:, D_MODEL:] * bh
    mix = jnp.dot(m.astype(BF), wo_ref[...], preferred_element_type=F32)
    xn = _layer_norm(ALPHA * x_ref[...] + gate_ref[...] * mix, lng_ref[...], lnb_ref[...])
    o_ref[...] = xn
    hf_ref[...] = xn * (1.0 + sc2_ref[...]) + sh2_ref[...]


def _merge_call(yf, yb, bv, lg, ho, pg, x, mp):
    L, d = x.shape
    hw = RWKV_WIDTH
    tb = min(L, 256)
    row = lambda i: (i, 0)
    names = ("lnx_g", "lnx_b", "g2", "G", "w_branch", "w_out", "gate", "ln_g", "ln_b", "sh2", "sc2")
    pargs = [mp[nm] for nm in names]
    return pl.pallas_call(
        _merge_kernel,
        out_shape=(jax.ShapeDtypeStruct((L, d), F32), jax.ShapeDtypeStruct((L, d), F32)),
        grid=(L // tb,),
        in_specs=[pl.BlockSpec((tb, hw), row), pl.BlockSpec((tb, hw), row), pl.BlockSpec((tb, hw), row),
                  pl.BlockSpec((tb, LORA_G), row), pl.BlockSpec((tb, hw), row),
                  pl.BlockSpec((tb, GATE_COLS), row), pl.BlockSpec((tb, d), row)]
                 + [_full(a.shape) for a in pargs],
        out_specs=(pl.BlockSpec((tb, d), row), pl.BlockSpec((tb, d), row)),
        compiler_params=_cparams(("arbitrary",)),
        name="merge_postnorm",
    )(yf, yb, bv, lg, ho, pg, x, *pargs)


def _router_kernel(hf_ref, wrt_ref, rb_ref, cw_ref, sel_ref):
    lt = _mm(wrt_ref[...], hf_ref[...], _NT, 6)
    rid = lax.broadcasted_iota(jnp.int32, (LANES, 1), 0)
    valid = rid < N_EXPERTS
    lg = jnp.where(valid, lt, -jnp.inf)
    mx = jnp.max(lg, axis=0, keepdims=True)
    ex = jnp.where(valid, jnp.exp(lg - mx), 0.0)
    scores = ex / jnp.sum(ex, axis=0, keepdims=True)
    sel = scores + rb_ref[...]
    s = [sel[e:e + 1, :] for e in range(N_EXPERTS)]
    p = [scores[e:e + 1, :] for e in range(N_EXPERTS)]
    gs = []
    for gi in range(N_GROUPS):
        mem = s[gi * EXPERTS_PER_GROUP:(gi + 1) * EXPERTS_PER_GROUP]
        best = None
        for a in range(EXPERTS_PER_GROUP):
            for b in range(a + 1, EXPERTS_PER_GROUP):
                pair = mem[a] + mem[b]
                best = pair if best is None else jnp.maximum(best, pair)
        gs.append(best)
    bg = jnp.where((gs[0] >= gs[1]) & (gs[0] >= gs[2]) & (gs[0] >= gs[3]), 0,
                   jnp.where((gs[1] >= gs[2]) & (gs[1] >= gs[3]), 1, jnp.where(gs[2] >= gs[3], 2, 3)))
    chosen = []
    for e in range(N_EXPERTS):
        gi = e // EXPERTS_PER_GROUP
        beats = None
        for j in range(gi * EXPERTS_PER_GROUP, (gi + 1) * EXPERTS_PER_GROUP):
            if j == e:
                continue
            cond = (s[j] >= s[e]) if j < e else (s[j] > s[e])
            cnt = jnp.where(cond, 1.0, 0.0)
            beats = cnt if beats is None else beats + cnt
        chosen.append((bg == gi) & (beats < 1.5))
    den = None
    for e in range(N_EXPERTS):
        t = jnp.where(chosen[e], p[e], 0.0)
        den = t if den is None else den + t
    out = jnp.zeros(lt.shape, F32)
    msk = jnp.zeros(lt.shape, F32)
    for e in range(N_EXPERTS):
        hit = (rid == e) & chosen[e]
        out = jnp.where(hit, p[e] / den, out)
        msk = jnp.where(hit, 1.0, msk)
    cw_ref[...] = out.T
    sel_ref[...] = msk.T


def _router_call(hf, wrt_pad, rb_pad):
    L, d = hf.shape
    tb = min(L, 256)
    row = lambda i: (i, 0)
    return pl.pallas_call(
        _router_kernel,
        out_shape=(jax.ShapeDtypeStruct((L, LANES), F32), jax.ShapeDtypeStruct((L, LANES), F32)),
        grid=(L // tb,),
        in_specs=[pl.BlockSpec((tb, d), row), _full(wrt_pad.shape), _full(rb_pad.shape)],
        out_specs=(pl.BlockSpec((tb, LANES), row), pl.BlockSpec((tb, LANES), row)),
        compiler_params=_cparams(("arbitrary",)),
        name="router",
    )(hf, wrt_pad, rb_pad)


MOE_BLK = 256


def _moe_ffn_kernel(tok_ref, slot_ref, be_ref, hf_hbm, wg_ref, wu_ref, wd_ref, y_hbm,
                    xbuf, ybuf, gsem, ssem, *, nb):
    b = pl.program_id(0)
    nrow = MOE_BLK
    s = lax.rem(b, 2)
    o = 1 - s

    def row_in(idx, sl, r):
        return pltpu.make_async_copy(hf_hbm.at[pl.ds(idx, 1)], xbuf.at[sl, pl.ds(r, 1)], gsem.at[sl])

    def row_out(idx, sl, r):
        return pltpu.make_async_copy(ybuf.at[sl, pl.ds(r, 1)], y_hbm.at[pl.ds(idx, 1)], ssem.at[sl])

    @pl.when(b == 0)
    def _():
        for r in range(nrow):
            row_in(tok_ref[r], 0, r).start()
        ybuf[1] = jnp.zeros(ybuf.shape[1:], F32)

    for r in range(nrow):
        row_in(0, s, r).wait()

    @pl.when(b >= 1)
    def _():
        for r in range(nrow):
            row_out(0, s, r).wait()

    nxt = (b + 1) * nrow
    for r in range(nrow):
        row_in(tok_ref[nxt + r], o, r).start()
    cur = b * nrow
    for r in range(nrow):
        row_out(slot_ref[cur + r], o, r).start()

    x = xbuf[s].astype(BF)
    hg = jnp.dot(x, wg_ref[...], preferred_element_type=F32)
    hu = jnp.dot(x, wu_ref[...], preferred_element_type=F32)
    act = (hg * _sig(hg) * hu).astype(BF)
    ybuf[s] = jnp.dot(act, wd_ref[...], preferred_element_type=F32)

    @pl.when(b == nb)
    def _():
        for r in range(nrow):
            row_in(0, o, r).wait()
        for r in range(nrow):
            row_out(0, o, r).wait()


def _moe_ffn_call(row_tok, row_slot, block_e, hf_all, wg, wu, wd):
    d = hf_all.shape[1]
    nb = block_e.shape[0] - 1
    wspec = pl.BlockSpec((None, d, d), lambda b, tok, slot, be: (be[b], 0, 0))
    return pl.pallas_call(
        functools.partial(_moe_ffn_kernel, nb=nb),
        out_shape=jax.ShapeDtypeStruct(((nb + 1) * MOE_BLK, d), F32),
        grid_spec=pltpu.PrefetchScalarGridSpec(
            num_scalar_prefetch=3,
            grid=(nb + 1,),
            in_specs=[pl.BlockSpec(memory_space=pl.ANY), wspec, wspec, wspec],
            out_specs=pl.BlockSpec(memory_space=pl.ANY),
            scratch_shapes=[pltpu.VMEM((2, MOE_BLK, d), F32), pltpu.VMEM((2, MOE_BLK, d), F32),
                            pltpu.SemaphoreType.DMA((2,)), pltpu.SemaphoreType.DMA((2,))]),
        compiler_params=_cparams(("arbitrary",)),
        name="moe_ffn",
    )(row_tok, row_slot, block_e, hf_all, wg, wu, wd)


def _moe_dispatch(sel, cw):
    t_all = sel.shape[0]
    blk = MOE_BLK
    nb = (2 * t_all) // blk + N_EXPERTS
    nr = nb * blk
    mi = (sel > 0.5).astype(jnp.int32)
    rank = jnp.cumsum(mi, axis=0) - mi
    counts = jnp.sum(mi, axis=0)
    padded = ((counts + blk - 1) // blk) * blk
    pend = jnp.cumsum(padded)
    dest = pend - padded + rank
    kk = jnp.cumsum(mi, axis=1) - mi
    first = (mi > 0) & (kk == 0)
    second = (mi > 0) & (kk == 1)
    d0 = jnp.sum(jnp.where(first, dest, 0), axis=1)
    d1 = jnp.sum(jnp.where(second, dest, 0), axis=1)
    w0 = jnp.sum(jnp.where(first, cw, 0.0), axis=1, keepdims=True)
    w1 = jnp.sum(jnp.where(second, cw, 0.0), axis=1, keepdims=True)
    tok2 = 2 * jnp.arange(t_all, dtype=jnp.int32)
    packed = jnp.full((nr,), -1, jnp.int32).at[jnp.concatenate([d0, d1])].set(
        jnp.concatenate([tok2, tok2 + 1]), unique_indices=True)
    is_pad = packed < 0
    row_tok = jnp.where(is_pad, 0, jnp.right_shift(packed, 1))
    block_e = jnp.clip(jnp.searchsorted(pend, jnp.arange(nb, dtype=jnp.int32) * blk, side="right"),
                       0, N_EXPERTS - 1).astype(jnp.int32)
    real_before = jnp.repeat(jnp.cumsum(counts)[block_e], blk)
    pad_slot = 2 * t_all + jnp.arange(nr, dtype=jnp.int32) - real_before
    row_slot = jnp.where(is_pad, pad_slot, jnp.bitwise_and(packed, 1) * t_all + row_tok)
    row_tok = jnp.concatenate([row_tok, jnp.zeros((2 * blk,), jnp.int32)])
    row_slot = jnp.concatenate([nr + jnp.arange(blk, dtype=jnp.int32), row_slot])
    block_e = jnp.concatenate([block_e, block_e[-1:]])
    return row_tok, row_slot, block_e, w0, w1


def _combine_kernel(x_ref, y0_ref, y1_ref, w0_ref, w1_ref, gate_ref, lng_ref, lnb_ref, o_ref):
    y = y0_ref[...] * w0_ref[...] + y1_ref[...] * w1_ref[...]
    o_ref[...] = _layer_norm(ALPHA * x_ref[...] + gate_ref[...] * y, lng_ref[...], lnb_ref[...])


def _combine_call(x, y_slots, w0, w1, row_off, t_all, gate, lng, lnb, tm):
    L, d = x.shape
    off0 = row_off // tm
    off1 = (t_all + row_off) // tm
    vec = _full((1, d))
    return pl.pallas_call(
        _combine_kernel,
        out_shape=jax.ShapeDtypeStruct((L, d), F32),
        grid=(L // tm,),
        in_specs=[pl.BlockSpec((tm, d), lambda i: (i, 0)),
                  pl.BlockSpec((tm, d), lambda i: (off0 + i, 0)),
                  pl.BlockSpec((tm, d), lambda i: (off1 + i, 0)),
                  pl.BlockSpec((tm, 1), lambda i: (off0 + i, 0)),
                  pl.BlockSpec((tm, 1), lambda i: (off0 + i, 0)), vec, vec, vec],
        out_specs=pl.BlockSpec((tm, d), lambda i: (i, 0)),
        compiler_params=_cparams(("arbitrary",)),
        name="moe_combine",
    )(x, y_slots, y_slots, w0, w1, gate, lng, lnb)


def _moe_layer(segs, wrt_pad, rb_pad, lng, lnb, wg, wu, wd):
    routed = [_router_call(hf, wrt_pad, rb_pad) for (_, hf, _) in segs]
    cw = jnp.concatenate([r[0][:, :N_EXPERTS] for r in routed], axis=0)
    sel = jnp.concatenate([r[1][:, :N_EXPERTS] for r in routed], axis=0)
    hf_all = jnp.concatenate([hf for (_, hf, _) in segs], axis=0) if len(segs) > 1 else segs[0][1]
    t_all = hf_all.shape[0]
    row_tok, row_slot, block_e, w0, w1 = _moe_dispatch(sel, cw)
    y_slots = _moe_ffn_call(row_tok, row_slot, block_e, hf_all, wg, wu, wd)
    tm = min(256, min(x.shape[0] for (x, _, _) in segs))
    outs = []
    off = 0
    for (x, _, gate) in segs:
        assert off % tm == 0 and t_all % tm == 0 and x.shape[0] % tm == 0
        outs.append(_combine_call(x, y_slots, w0, w1, off, t_all, gate, lng, lnb, tm))
        off += x.shape[0]
    return outs


def _blockdiag2(m):
    z = jnp.zeros_like(m[0])
    return jnp.concatenate([jnp.concatenate([m[0], z], axis=1), jnp.concatenate([z, m[1]], axis=1)], axis=0)


def _mixer(x, mod_row, l, w_in_bf, prm, fp, hy, mp, s0, latent):
    L = x.shape[0]
    sh, sc, gate = mod_row[0], mod_row[1], mod_row[2]
    p_r, p_h, p_g = _inproj_call(x, sh, sc, w_in_bf)
    r, v, ka, lw, k, kb, lg, bv = _prepare_call(p_r, prm, latent)
    yf, yb, sfin = _scan_call(r, v, ka, lw, k, kb, s0)
    filt, asum = _hfilt_call(L, fp)
    z, x0 = _hconv3_call(p_h, hy["conv_w"], hy["conv_b"])
    ho = _long_conv_call(z, x0, filt, asum, hy["bias"])
    mpl = dict(mp)
    mpl["gate"], mpl["sh2"], mpl["sc2"] = gate, mod_row[3], mod_row[4]
    xn, hf = _merge_call(yf, yb, bv, lg, ho, p_g, x, mpl)
    return xn, hf, sfin


def kernel(x, c, ctx, c_ctx, w_mod, b_mod, w_in, rwkv_mu, rwkv_w0, rwkv_w2, rwkv_a0, rwkv_a2, rwkv_g2,
           rwkv_k_k, rwkv_k_a, rwkv_r_k, rwkv_lnx_g, rwkv_lnx_b, hy_conv_w, hy_conv_b, hy_f_w1, hy_f_b1,
           hy_f_w2, hy_f_b2, hy_f_w3, hy_f_b3, hy_f_wout, hy_freq, hy_bias, w_branch, w_out, ln_g, ln_b,
           w_router, router_bias, w_gate, w_up, w_down):
    b, n_lat, d = x.shape
    assert b == 1 and d == D_MODEL
    n_ctx = ctx.shape[1]
    depth = w_mod.shape[0]
    hw = RWKV_WIDTH
    xl = x[0]
    xc = ctx[0]

    cc = jnp.concatenate([c[:1], c_ctx[None, :], jnp.zeros((6, d), F32)], axis=0)
    mod = _mod_call(cc, w_mod, b_mod)

    head_of = np.arange(hw) // HEAD_DIM
    G = jnp.asarray((head_of[:, None] == head_of[None, :]).astype(np.float32), dtype=BF)
    bands = jnp.linspace(1e-4, FILTER_BANDS - 1, FILTER_BANDS, dtype=F32)[None, :]
    deltas = jnp.abs(jnp.linspace(HYENA_MIN_DECAY, HYENA_MAX_DECAY, HYENA_WIDTH, dtype=F32))[None, :]
    wr_pad = jnp.pad(w_router.T, ((0, LANES - N_EXPERTS), (0, 0)))
    rb_pad = jnp.pad(router_bias, (0, LANES - N_EXPERTS))[:, None]
    w_in_bf = w_in.astype(BF)
    w_branch_bf = w_branch.astype(BF)
    w_out_bf = w_out.astype(BF)
    g2_bf = rwkv_g2.astype(BF)
    wg_bf, wu_bf, wd_bf = w_gate.astype(BF), w_up.astype(BF), w_down.astype(BF)

    for l in range(depth):
        last = l == depth - 1
        ml = [mod[l, 0:1, j * d:(j + 1) * d] for j in range(6)]
        mc = [mod[l, 1:2, j * d:(j + 1) * d] for j in range(6)]
        prm = dict(mu=rwkv_mu[l][None, :],
                   w0=rwkv_w0[l].reshape(1, 2 * hw), w2=_blockdiag2(rwkv_w2[l]),
                   a0=rwkv_a0[l].reshape(1, 2 * hw), a2=_blockdiag2(rwkv_a2[l]),
                   k_k=rwkv_k_k[l][None, :], k_a=rwkv_k_a[l][None, :], r_k=rwkv_r_k[l][None, :], G=G)
        w1 = hy_f_w1[l]
        fp = dict(bands=bands, w1t=w1[0:1], w1c=w1[1:1 + FILTER_BANDS], w1s=w1[1 + FILTER_BANDS:],
                  b1=hy_f_b1[l][None, :], w2=hy_f_w2[l], b2=hy_f_b2[l][None, :], w3=hy_f_w3[l],
                  b3=hy_f_b3[l][None, :], wout=hy_f_wout[l], freq=hy_freq[l][None, :], deltas=deltas)
        hy = dict(conv_w=hy_conv_w[l], conv_b=hy_conv_b[l][None, :], bias=hy_bias[l][None, :])
        mp = dict(lnx_g=rwkv_lnx_g[l][None, :], lnx_b=rwkv_lnx_b[l][None, :], g2=g2_bf[l], G=G,
                  w_branch=w_branch_bf[l], w_out=w_out_bf[l], ln_g=ln_g[l, 0][None, :], ln_b=ln_b[l, 0][None, :])
        s0 = jnp.zeros((2, HEAD_DIM, hw), F32)
        xc_new, hf_c, s_ctx = _mixer(xc, mc, l, w_in_bf[l], prm, fp, hy, mp, s0, latent=False)
        xl, hf_l, _ = _mixer(xl, ml, l, w_in_bf[l], prm, fp, hy, mp, s_ctx, latent=True)
        lng, lnb = ln_g[l, 1][None, :], ln_b[l, 1][None, :]
        if last:
            (xl,) = _moe_layer([(xl, hf_l, ml[5])], wr_pad, rb_pad, lng, lnb, wg_bf[l], wu_bf[l], wd_bf[l])
        else:
            xc, xl = _moe_layer([(xc_new, hf_c, mc[5]), (xl, hf_l, ml[5])], wr_pad, rb_pad, lng, lnb,
                                wg_bf[l], wu_bf[l], wd_bf[l])
    return xl[None]
```

```python
import functools
import math

import numpy as np
import jax
import jax.numpy as jnp
from jax import lax
from jax.experimental import pallas as pl
from jax.experimental.pallas import tpu as pltpu

F32 = jnp.float32
BF = jnp.bfloat16

D_MODEL = 1024
DEPTH = 4
GRID_W = 64
RWKV_WIDTH = 512
HEAD_DIM = 64
RWKV_HEADS = 8
LORA_W = 64
LORA_A = 64
LORA_G = 128
DECAY_SCALE = 0.606531
GN_EPS = 6.4e-4
RWKV_COLS = 3 * RWKV_WIDTH + 2 * LORA_W + 2 * LORA_A + LORA_G
HYENA_WIDTH = 512
HYENA_COLS = 3 * HYENA_WIDTH
FILTER_BANDS = 16
FILTER_HIDDEN = 64
HYENA_MIN_DECAY = math.log(1e-2) / 1.5
HYENA_MAX_DECAY = math.log(1e-2) / 0.3
GATE_COLS = 2 * D_MODEL
PROJ_COLS = RWKV_COLS + HYENA_COLS + GATE_COLS
N_EXPERTS = 16
N_GROUPS = 4
EXPERTS_PER_GROUP = 4
ALPHA = (2 * DEPTH) ** 0.25
LN_EPS = 1e-5

SCAN_CHUNK = 64
LANES = 128
VMEM_LIMIT = 56 * 1024 * 1024

_NN = (((1,), (0,)), ((), ()))
_NT = (((1,), (1,)), ((), ()))
_TN = (((0,), (0,)), ((), ()))


def _sig(x):
    return 1.0 / (1.0 + jnp.exp(-x))


def _parts(a, n):
    out = []
    rem = a
    for i in range(n):
        hi = rem.astype(BF)
        out.append(hi)
        if i + 1 < n:
            rem = rem - hi.astype(F32)
    return out


def _mm(a, b, dn=_NN, passes=1):
    n = {1: 1, 3: 2, 6: 3}[passes]
    pa = _parts(a, n)
    pb = _parts(b, n)
    acc = None
    for i in range(n):
        for j in range(n - i):
            t = lax.dot_general(pa[i], pb[j], dn, preferred_element_type=F32)
            acc = t if acc is None else acc + t
    return acc


def _mm_rx(a, b_exact, n=3, dn=_NN):
    acc = None
    for p in _parts(a, n):
        t = lax.dot_general(p, b_exact, dn, preferred_element_type=F32)
        acc = t if acc is None else acc + t
    return acc


def _mm_lx(a_exact, b, n=3, dn=_NN):
    acc = None
    for p in _parts(b, n):
        t = lax.dot_general(a_exact, p, dn, preferred_element_type=F32)
        acc = t if acc is None else acc + t
    return acc


def _cparams(sem, vmem=VMEM_LIMIT):
    return pltpu.CompilerParams(dimension_semantics=sem, vmem_limit_bytes=vmem)


def _full(shape):
    nd = len(shape)
    return pl.BlockSpec(shape, lambda *_: (0,) * nd)


def _mod_kernel(c_ref, w_ref, b_ref, o_ref):
    cc = c_ref[...]
    s = cc * _sig(cc)
    o_ref[...] = jnp.dot(s.astype(BF), w_ref[...].astype(BF), preferred_element_type=F32) + b_ref[...]


def _mod_call(cc, w_mod, b_mod):
    depth, d, n6 = w_mod.shape
    tn = 1536
    return pl.pallas_call(
        _mod_kernel,
        out_shape=jax.ShapeDtypeStruct((depth, 8, n6), F32),
        grid=(depth, n6 // tn),
        in_specs=[
            pl.BlockSpec((8, d), lambda l, j: (0, 0)),
            pl.BlockSpec((None, d, tn), lambda l, j: (l, 0, j)),
            pl.BlockSpec((None, 1, tn), lambda l, j: (l, 0, j)),
        ],
        out_specs=pl.BlockSpec((None, 8, tn), lambda l, j: (l, 0, j)),
        compiler_params=_cparams(("arbitrary", "arbitrary")),
        name="mod",
    )(cc, w_mod, b_mod.reshape(depth, 1, n6))


def _inproj_kernel(x_ref, sh_ref, sc_ref, w_ref, pr_ref, ph_ref, pg_ref):
    xm = (x_ref[...] * (1.0 + sc_ref[...]) + sh_ref[...]).astype(BF)
    pr_ref[...] = jnp.dot(xm, w_ref[:, :RWKV_COLS], preferred_element_type=F32)
    ph_ref[...] = jnp.dot(xm, w_ref[:, RWKV_COLS:RWKV_COLS + HYENA_COLS], preferred_element_type=F32)
    pg_ref[...] = jnp.dot(xm, w_ref[:, RWKV_COLS + HYENA_COLS:], preferred_element_type=F32)


def _inproj_call(x, sh, sc, w_bf, l):
    L, d = x.shape
    tm = min(L, 256)
    row = lambda i: (i, 0)
    return pl.pallas_call(
        _inproj_kernel,
        out_shape=(jax.ShapeDtypeStruct((L, RWKV_COLS), F32),
                   jax.ShapeDtypeStruct((L, HYENA_COLS), F32),
                   jax.ShapeDtypeStruct((L, GATE_COLS), F32)),
        grid=(L // tm,),
        in_specs=[pl.BlockSpec((tm, d), row), _full((1, d)), _full((1, d)),
                  pl.BlockSpec((None, d, PROJ_COLS), lambda i: (l, 0, 0))],
        out_specs=(pl.BlockSpec((tm, RWKV_COLS), row), pl.BlockSpec((tm, HYENA_COLS), row),
                   pl.BlockSpec((tm, GATE_COLS), row)),
        compiler_params=_cparams(("arbitrary",)),
        name="inproj",
    )(x, sh, sc, w_bf)


def _prepare_kernel(*refs, latent, nblk):
    if latent:
        p_ref, up_ref, dn_ref = refs[:3]
        refs = refs[3:]
    else:
        p_ref = refs[0]
        refs = refs[1:]
    (mu_ref, w0_ref, w2_ref, a0_ref, a2_ref, kk_ref, ka_ref, rk_ref, g_ref,
     r_out, v_out, ka_out, lw_out, k_out, kb_out, lg_out, bv_out) = refs
    i = pl.program_id(0)
    p = p_ref[...]
    tb, w = p.shape
    row = lax.broadcasted_iota(jnp.int32, (tb, 1), 0)
    col = lax.broadcasted_iota(jnp.int32, (1, w), 1)
    if latent:
        gw = jnp.bitwise_and(row, GRID_W - 1)
        left = jnp.where(gw == 0, 0.0, pltpu.roll(p, 1, 0))
        right = jnp.where(gw == GRID_W - 1, 0.0, pltpu.roll(p, tb - 1, 0))
        upv = jnp.where(i == 0, 0.0, up_ref[...])
        dnv = jnp.where(i == nblk - 1, 0.0, dn_ref[...])
        if tb > GRID_W:
            up = jnp.concatenate([upv, p[:tb - GRID_W]], axis=0)
            down = jnp.concatenate([p[GRID_W:], dnv], axis=0)
        else:
            up, down = upv, dnv
        q = w // 4
        sh = jnp.where(col < q, left, jnp.where(col < 2 * q, right, jnp.where(col < 3 * q, up, down)))
    else:
        prev = jnp.where(row == 0, 0.0, pltpu.roll(p, 1, 0))
        nxt = jnp.where(row == tb - 1, 0.0, pltpu.roll(p, tb - 1, 0))
        sh = jnp.where(col < w // 2, prev, nxt)
    pm = p + (sh - p) * mu_ref[...]
    hw = RWKV_WIDTH
    r = pm[:, :hw]
    k = pm[:, hw:2 * hw]
    v = pm[:, 2 * hw:3 * hw]
    lw_in = jnp.tanh(pm[:, 3 * hw:3 * hw + 2 * LORA_W])
    la_in = pm[:, 3 * hw + 2 * LORA_W:3 * hw + 2 * LORA_W + 2 * LORA_A]
    lg = pm[:, 3 * hw + 2 * LORA_W + 2 * LORA_A:]
    lw = -DECAY_SCALE * _sig(w0_ref[...] + _mm(lw_in, w2_ref[...], passes=3))
    a = _sig(a0_ref[...] + _mm(la_in, a2_ref[...], passes=3))
    g = g_ref[...]
    kkr = k * kk_ref[...]
    nrm = jnp.sqrt(_mm_rx(kkr * kkr, g, 3))
    kk = kkr / jnp.maximum(nrm, 1e-12)
    k_a = ka_ref[...]
    k_f = k * (1.0 + (a[:, :hw] - 1.0) * k_a)
    k_b = k * (1.0 + (a[:, hw:] - 1.0) * k_a)
    bonus = _mm_rx(r * (k_f + k_b) * rk_ref[...], g, 3)
    r_out[...] = r
    v_out[...] = v
    ka_out[...] = -kk
    lw_out[0] = lw[:, :hw]
    lw_out[1] = lw[:, hw:]
    k_out[0] = k_f
    k_out[1] = k_b
    kb_out[0] = kk * a[:, :hw]
    kb_out[1] = kk * a[:, hw:]
    lg_out[...] = lg
    bv_out[...] = bonus * v


def _prepare_call(p_r, prm, latent):
    L, w = p_r.shape
    hw = RWKV_WIDTH
    tb = 256 if latent else L
    nblk = L // tb
    row = lambda i: (i, 0)
    row3 = lambda i: (0, i, 0)
    in_specs = [pl.BlockSpec((tb, w), row)]
    args = [p_r]
    if latent:
        per = tb // GRID_W
        nrow = L // GRID_W
        in_specs += [
            pl.BlockSpec((GRID_W, w), lambda i: (jnp.maximum(i * per - 1, 0), 0)),
            pl.BlockSpec((GRID_W, w), lambda i: (jnp.minimum((i + 1) * per, nrow - 1), 0)),
        ]
        args += [p_r, p_r]
    names = ("mu", "w0", "w2", "a0", "a2", "k_k", "k_a", "r_k", "G")
    for nm in names:
        in_specs.append(_full(prm[nm].shape))
        args.append(prm[nm])
    sd = jax.ShapeDtypeStruct
    out_shape = (sd((L, hw), F32), sd((L, hw), F32), sd((L, hw), F32),
                 sd((2, L, hw), F32), sd((2, L, hw), F32), sd((2, L, hw), F32),
                 sd((L, LORA_G), F32), sd((L, hw), F32))
    out_specs = (pl.BlockSpec((tb, hw), row), pl.BlockSpec((tb, hw), row), pl.BlockSpec((tb, hw), row),
                 pl.BlockSpec((2, tb, hw), row3), pl.BlockSpec((2, tb, hw), row3), pl.BlockSpec((2, tb, hw), row3),
                 pl.BlockSpec((tb, LORA_G), row), pl.BlockSpec((tb, hw), row))
    return pl.pallas_call(
        functools.partial(_prepare_kernel, latent=latent, nblk=nblk),
        out_shape=out_shape, grid=(nblk,), in_specs=in_specs, out_specs=out_specs,
        compiler_params=_cparams(("arbitrary",)),
        name="rwkv_prepare_lat" if latent else "rwkv_prepare_ctx",
    )(*args)


P_SCORE = 3
P_TINV = 3
P_APPLY = 1
P_STATE = 1


def _scan_kernel(rf_ref, vf_ref, kaf_ref, rb_ref, vb_ref, kab_ref, lwf_ref, kf_ref, kbf_ref,
                 lwb_ref, kbk_ref, kbb_ref, s0_ref, yf_ref, yb_ref, sfin_ref, s_scr, *, nchunk):
    c = pl.program_id(0)
    C = SCAN_CHUNK
    hd = HEAD_DIM
    nh = RWKV_HEADS

    @pl.when(c == 0)
    def _():
        s_scr[...] = s0_ref[...]

    row = lax.broadcasted_iota(jnp.int32, (C, C), 0)
    col = lax.broadcasted_iota(jnp.int32, (C, C), 1)
    eye = jnp.where(row == col, 1.0, 0.0)
    lvl_masks = []
    for sh in range(int(math.log2(C))):
        same2 = jnp.right_shift(row, sh + 1) == jnp.right_shift(col, sh + 1)
        same1 = jnp.right_shift(row, sh) == jnp.right_shift(col, sh)
        lvl_masks.append(jnp.logical_and(same2, jnp.logical_not(same1)))

    dirs = []
    for d, (r_ref, v_ref, ka_ref, lw_ref, k_ref, kb_ref) in enumerate((
            (rf_ref, vf_ref, kaf_ref, lwf_ref, kf_ref, kbf_ref),
            (rb_ref, vb_ref, kab_ref, lwb_ref, kbk_ref, kbb_ref))):
        inc = (row >= col) if d == 0 else (row <= col)
        strict = (row > col) if d == 0 else (row < col)
        lw = lw_ref[...]
        lc = _mm_lx(jnp.where(inc, 1.0, 0.0).astype(BF), lw, 3)
        e_neg = jnp.exp(-lc)
        e_tot = jnp.exp(jnp.sum(lw, axis=0, keepdims=True))
        kbn = kb_ref[...] * e_neg
        kkn = k_ref[...] * e_neg
        dirs.append(dict(inc=inc, strict=strict, v=v_ref[...], aq=ka_ref[...] * jnp.exp(lc - lw),
                         rq=r_ref[...] * jnp.exp(lc), kbn=kbn, kkn=kkn, kbp=kbn * e_tot, kkp=kkn * e_tot,
                         e_tot=e_tot, s=s_scr[d]))

    units = [(d, h) for h in range(nh) for d in range(2)]
    hs = lambda arr, h: arr[:, h * hd:(h + 1) * hd]
    sc = {}
    for (d, h) in units:
        D = dirs[d]
        sc[d, h] = _mm(jnp.concatenate([hs(D["aq"], h), hs(D["rq"], h)], axis=0),
                       jnp.concatenate([hs(D["kbn"], h), hs(D["kkn"], h)], axis=0), _NT, P_SCORE)
    a_ab, t, x = {}, {}, {}
    for u in units:
        a_ab[u] = jnp.where(dirs[u[0]]["strict"], sc[u][:C, :C], 0.0)
        t[u] = eye + jnp.where(lvl_masks[0], a_ab[u], 0.0)
    for u in units:
        D = dirs[u[0]]
        lhs = jnp.concatenate([jnp.where(D["strict"], sc[u][:C, C:], 0.0),
                               jnp.where(D["inc"], sc[u][C:, C:], 0.0)], axis=0)
        x[u] = _mm(lhs, hs(D["v"], u[1]), _NN, P_APPLY)
    for m in lvl_masks[1:]:
        tmp = {u: _mm(jnp.where(m, a_ab[u], 0.0), t[u], _NN, 1) for u in units}
        for u in units:
            t[u] = t[u] + _mm(t[u], tmp[u], _NN, 1)
    res = {u: eye - t[u] + _mm(a_ab[u], t[u], _NN, P_TINV) for u in units}
    for u in units:
        t[u] = t[u] + _mm(t[u], res[u], _NN, 1)
    z, gm, uu, yy, sn = {}, {}, {}, {}, {}
    for u in units:
        z[u] = _mm(t[u], jnp.concatenate([hs(dirs[u[0]]["aq"], u[1]), x[u][:C]], axis=1), _NN, P_APPLY)
    for u in units:
        D = dirs[u[0]]
        gm[u] = _mm(jnp.concatenate([z[u][:, :hd], hs(D["rq"], u[1])], axis=0), hs(D["s"], u[1]), _NT, P_STATE)
        uu[u] = gm[u][:C] + z[u][:, hd:]
    for u in units:
        D = dirs[u[0]]
        b_rb = jnp.where(D["inc"], sc[u][C:, :C], 0.0)
        yy[u] = gm[u][C:] + _mm(b_rb, uu[u], _NN, P_APPLY) + x[u][C:]
        sn[u] = hs(D["s"], u[1]) * hs(D["e_tot"], u[1]) + _mm(
            jnp.concatenate([uu[u], hs(D["v"], u[1])], axis=0),
            jnp.concatenate([hs(D["kbp"], u[1]), hs(D["kkp"], u[1])], axis=0), _TN, P_STATE)
    yf_ref[...] = jnp.concatenate([yy[0, h] for h in range(nh)], axis=1)
    yb_ref[...] = jnp.concatenate([yy[1, h] for h in range(nh)], axis=1)
    s_scr[0] = jnp.concatenate([sn[0, h] for h in range(nh)], axis=1)
    s_scr[1] = jnp.concatenate([sn[1, h] for h in range(nh)], axis=1)

    @pl.when(c == nchunk - 1)
    def _():
        sfin_ref[...] = s_scr[...]


def _scan_call(r, v, ka, lw, k, kb, s0):
    L, hw = r.shape
    C = SCAN_CHUNK
    nchunk = L // C
    sh_f = pl.BlockSpec((C, hw), lambda c: (c, 0))
    sh_b = pl.BlockSpec((C, hw), lambda c: (nchunk - 1 - c, 0))
    pd_f = pl.BlockSpec((None, C, hw), lambda c: (0, c, 0))
    pd_b = pl.BlockSpec((None, C, hw), lambda c: (1, nchunk - 1 - c, 0))
    state = _full((2, HEAD_DIM, hw))
    return pl.pallas_call(
        functools.partial(_scan_kernel, nchunk=nchunk),
        out_shape=(jax.ShapeDtypeStruct((L, hw), F32), jax.ShapeDtypeStruct((L, hw), F32),
                   jax.ShapeDtypeStruct((2, HEAD_DIM, hw), F32)),
        grid=(nchunk,),
        in_specs=[sh_f, sh_f, sh_f, sh_b, sh_b, sh_b, pd_f, pd_f, pd_f, pd_b, pd_b, pd_b, state],
        out_specs=(sh_f, sh_b, state),
        scratch_shapes=[pltpu.VMEM((2, HEAD_DIM, hw), F32)],
        compiler_params=_cparams(("arbitrary",)),
        name="delta_scan",
    )(r, v, ka, r, v, ka, lw, k, kb, lw, k, kb, s0)


def _hfilt_kernel(bands_ref, w1t_ref, w1c_ref, w1s_ref, b1_ref, w2_ref, b2_ref, w3_ref, b3_ref, wo_ref,
                  fr_ref, dl_ref, f_ref, asum_ref, *, n, rb):
    i = pl.program_id(0)
    pos = (i * rb + lax.broadcasted_iota(jnp.int32, (rb, 1), 0)).astype(F32)
    t = pos / float(max(n - 1, 1))
    ang = ((2.0 * math.pi / n) * pos) * bands_ref[...]
    fr = fr_ref[...]
    h = t * w1t_ref[...] + _mm(jnp.cos(ang), w1c_ref[...], passes=3) + _mm(-jnp.sin(ang), w1s_ref[...], passes=3)
    h = jnp.sin(fr * (h + b1_ref[...]))
    h = jnp.sin(fr * (_mm(h, w2_ref[...], passes=3) + b2_ref[...]))
    h = jnp.sin(fr * (_mm(h, w3_ref[...], passes=3) + b3_ref[...]))
    filt = _mm(h, wo_ref[...], passes=3)
    dist = jnp.abs(pos - float(n // 2)) * (2.0 / n)
    filt = filt * jnp.exp(-dist * dl_ref[...])
    f_ref[...] = filt

    @pl.when(i == 0)
    def _():
        asum_ref[...] = jnp.zeros_like(asum_ref)

    asum_ref[...] += jnp.sum(jnp.abs(filt), axis=0, keepdims=True)


def _hfilt_call(n, fp):
    rb = min(n, 512)
    names = ("bands", "w1t", "w1c", "w1s", "b1", "w2", "b2", "w3", "b3", "wout", "freq", "deltas")
    args = [fp[nm] for nm in names]
    return pl.pallas_call(
        functools.partial(_hfilt_kernel, n=n, rb=rb),
        out_shape=(jax.ShapeDtypeStruct((n, HYENA_WIDTH), F32), jax.ShapeDtypeStruct((1, HYENA_WIDTH), F32)),
        grid=(n // rb,),
        in_specs=[_full(a.shape) for a in args],
        out_specs=(pl.BlockSpec((rb, HYENA_WIDTH), lambda i: (i, 0)), _full((1, HYENA_WIDTH))),
        compiler_params=_cparams(("arbitrary",)),
        name="hyena_filter",
    )(*args)


def _hconv3_kernel(p_ref, pv_ref, nx_ref, cw_ref, cb_ref, z_ref, x0_ref, *, nblk):
    i = pl.program_id(0)
    p = p_ref[...]
    tb = p.shape[0]
    row = lax.broadcasted_iota(jnp.int32, (tb, 1), 0)
    prev_row = jnp.where(i == 0, 0.0, pv_ref[7:8, :])
    next_row = jnp.where(i == nblk - 1, 0.0, nx_ref[0:1, :])
    sp = jnp.where(row == 0, prev_row, pltpu.roll(p, 1, 0))
    sn = jnp.where(row == tb - 1, next_row, pltpu.roll(p, tb - 1, 0))
    u = sp * cw_ref[0:1, :] + p * cw_ref[1:2, :] + sn * cw_ref[2:3, :] + cb_ref[...]
    hw = HYENA_WIDTH
    z_ref[...] = u[:, 2 * hw:] * u[:, hw:2 * hw]
    x0_ref[...] = u[:, :hw]


def _hconv3_call(p_h, cw, cb):
    L, w = p_h.shape
    tb = min(L, 256)
    nblk = L // tb
    per = tb // 8
    row = lambda i: (i, 0)
    return pl.pallas_call(
        functools.partial(_hconv3_kernel, nblk=nblk),
        out_shape=(jax.ShapeDtypeStruct((L, HYENA_WIDTH), F32), jax.ShapeDtypeStruct((L, HYENA_WIDTH), F32)),
        grid=(nblk,),
        in_specs=[pl.BlockSpec((tb, w), row),
                  pl.BlockSpec((8, w), lambda i: (jnp.maximum(i * per - 1, 0), 0)),
                  pl.BlockSpec((8, w), lambda i: (jnp.minimum((i + 1) * per, L // 8 - 1), 0)),
                  _full(cw.shape), _full(cb.shape)],
        out_specs=(pl.BlockSpec((tb, HYENA_WIDTH), row), pl.BlockSpec((tb, HYENA_WIDTH), row)),
        compiler_params=_cparams(("arbitrary",)),
        name="hyena_conv3",
    )(p_h, p_h, p_h, cw, cb)


P_DFT = 1


def _dft1_kernel(m_ref, x_ref, o_ref):
    o_ref[...] = _mm(m_ref[...], x_ref[...], _NN, P_DFT)


def _dft1_call(m1, x2d):
    rows, kdim = m1.shape
    _, cols = x2d.shape
    cb = min(cols, 2048)
    return pl.pallas_call(
        _dft1_kernel,
        out_shape=jax.ShapeDtypeStruct((rows, cols), F32),
        grid=(cols // cb,),
        in_specs=[_full(m1.shape), pl.BlockSpec((kdim, cb), lambda j: (0, j))],
        out_specs=pl.BlockSpec((rows, cb), lambda j: (0, j)),
        compiler_params=_cparams(("arbitrary",)),
        name="dft_stage1",
    )(m1, x2d)


def _dft2_matrix(fc_ref, fs_ref, tc_ref, ts_ref):
    fc, fs = fc_ref[...], fs_ref[...]
    tc, ts = tc_ref[...], ts_ref[...]
    fre = fc * tc - fs * ts
    fim = -(fc * ts + fs * tc)
    return jnp.concatenate([jnp.concatenate([fre, -fim], axis=1),
                            jnp.concatenate([fim, fre], axis=1)], axis=0)


def _dft2_conv_kernel(fc_ref, fs_ref, tc_ref, ts_ref, a_ref, af_ref, b_ref):
    big = _dft2_matrix(fc_ref, fs_ref, tc_ref, ts_ref)
    n2, c = a_ref.shape[1], a_ref.shape[2]
    rhs = jnp.concatenate([jnp.concatenate([a_ref[0], af_ref[0]], axis=1),
                           jnp.concatenate([a_ref[1], af_ref[1]], axis=1)], axis=0)
    xh = _mm(big, rhs, _NN, P_DFT)
    xr, hr = xh[:n2, :c], xh[:n2, c:]
    xi, hi = xh[n2:, :c], xh[n2:, c:]
    y = jnp.concatenate([xr * hr - xi * hi, xr * hi + xi * hr], axis=0)
    bb = _mm(big, y, _TN, P_DFT)
    b_ref[0] = bb[:n2]
    b_ref[1] = bb[n2:]


def _dft2_call(consts, a4, af4):
    _, n1, n2, c = a4.shape
    blk = pl.BlockSpec((2, None, n2, c), lambda q: (0, q, 0, 0))
    tw = pl.BlockSpec((None, 1, n2), lambda q: (q, 0, 0))
    return pl.pallas_call(
        _dft2_conv_kernel,
        out_shape=jax.ShapeDtypeStruct(a4.shape, F32),
        grid=(n1,),
        in_specs=[_full((n2, n2)), _full((n2, n2)), tw, tw, blk, blk],
        out_specs=blk,
        compiler_params=_cparams(("arbitrary",)),
        name="dft_stage2_conv",
    )(consts["fc"], consts["fs"], consts["twc"], consts["tws"], a4, af4)


def _dft3_kernel(m_ref, b_ref, z_ref, x0_ref, bias_ref, asum_ref, o_ref):
    y = _mm(m_ref[...], b_ref[...], _NN, P_DFT)
    z = z_ref[...]
    o_ref[...] = (y / asum_ref[...] + z * bias_ref[...]) * x0_ref[...]


def _dft3_call(m3, b2d, z2d, x02d, bias_t, asum_t):
    rows, kdim = m3.shape
    _, cols = b2d.shape
    cb = bias_t.shape[1]
    colb = lambda j: (0, j)
    return pl.pallas_call(
        _dft3_kernel,
        out_shape=jax.ShapeDtypeStruct((rows, cols), F32),
        grid=(cols // cb,),
        in_specs=[_full(m3.shape), pl.BlockSpec((kdim, cb), colb), pl.BlockSpec((rows, cb), colb),
                  pl.BlockSpec((rows, cb), colb), _full((1, cb)), _full((1, cb))],
        out_specs=pl.BlockSpec((rows, cb), colb),
        compiler_params=_cparams(("arbitrary",)),
        name="dft_stage3",
    )(m3, b2d, z2d, x02d, bias_t, asum_t)


def _conv_direct_kernel(m1_ref, m3_ref, z_ref, f_ref, x0_ref, bias_ref, asum_ref, o_ref):
    m1 = m1_ref[...]
    z = z_ref[...]
    a = _mm(m1, z, _NN, P_DFT)
    h = _mm(m1, f_ref[...], _NN, P_DFT)
    nn = a.shape[0] // 2
    ar, ai, hr, hi = a[:nn], a[nn:], h[:nn], h[nn:]
    y = jnp.concatenate([ar * hr - ai * hi, ar * hi + ai * hr], axis=0)
    out = _mm(m3_ref[...], y, _NN, P_DFT)
    o_ref[...] = (out / asum_ref[...] + z * bias_ref[...]) * x0_ref[...]


def _conv_direct_call(m1, m3, z, filt, x0, bias, asum):
    args = (m1, m3, z, filt, x0, bias, asum)
    return pl.pallas_call(
        _conv_direct_kernel,
        out_shape=jax.ShapeDtypeStruct(z.shape, F32),
        grid=(1,),
        in_specs=[_full(a.shape) for a in args],
        out_specs=_full(z.shape),
        compiler_params=_cparams(("arbitrary",)),
        name="long_conv_direct",
    )(*args)


DIRECT_CONV_MAX = 256


@functools.lru_cache(maxsize=None)
def _dft_consts_np(n):
    big_n = 2 * n
    if n <= DIRECT_CONV_MAX:
        n1 = big_n
    else:
        n1 = 256 if n >= 8192 else 64
    n2 = big_n // n1
    k1 = np.arange(n1)[:, None].astype(np.float64)
    j1 = np.arange(n1 // 2)[None, :].astype(np.float64)
    ang1 = 2.0 * np.pi * k1 * j1 / n1
    m1 = np.concatenate([np.cos(ang1), -np.sin(ang1)], axis=0)
    o1 = (n1 // 4 + np.arange(n1 // 2))[:, None].astype(np.float64)
    q1 = np.arange(n1)[None, :].astype(np.float64)
    ang3 = 2.0 * np.pi * o1 * q1 / n1
    m3 = np.concatenate([np.cos(ang3), -np.sin(ang3)], axis=1) / big_n
    k2 = np.arange(n2)[:, None].astype(np.float64)
    j2 = np.arange(n2)[None, :].astype(np.float64)
    ang2 = 2.0 * np.pi * k2 * j2 / n2
    angt = 2.0 * np.pi * np.arange(n1)[:, None].astype(np.float64) * j2 / big_n
    f = lambda a: np.asarray(a, np.float32)
    return dict(n1=n1, n2=n2, m1=f(m1), m3=f(m3), fc=f(np.cos(ang2)), fs=f(np.sin(ang2)),
                twc=f(np.cos(angt))[:, None, :], tws=f(np.sin(angt))[:, None, :])


def _long_conv_call(z, x0, filt, asum, bias):
    n, c = z.shape
    cn = _dft_consts_np(n)
    n1, n2 = cn["n1"], cn["n2"]
    consts = {kk: jnp.asarray(vv) for kk, vv in cn.items() if kk not in ("n1", "n2")}
    if n2 == 1:
        return _conv_direct_call(consts["m1"], consts["m3"], z, filt, x0, bias, asum)
    cols = n2 * c
    cb = min(cols, 2048)
    a_f = _dft1_call(consts["m1"], filt.reshape(n1 // 2, cols)).reshape(2, n1, n2, c)
    a_z = _dft1_call(consts["m1"], z.reshape(n1 // 2, cols)).reshape(2, n1, n2, c)
    b4 = _dft2_call(consts, a_z, a_f)
    out = _dft3_call(consts["m3"], b4.reshape(2 * n1, cols), z.reshape(n1 // 2, cols), x0.reshape(n1 // 2, cols),
                     jnp.tile(bias, (1, cb // c)), jnp.tile(asum, (1, cb // c)))
    return out.reshape(n, c)


def _layer_norm(x, g, b):
    mu = jnp.mean(x, axis=-1, keepdims=True)
    xc = x - mu
    var = jnp.mean(xc * xc, axis=-1, keepdims=True)
    return xc * lax.rsqrt(var + LN_EPS) * g + b


def _merge_kernel(yf_ref, yb_ref, bv_ref, lg_ref, ho_ref, pg_ref, x_ref, gx_ref, bx_ref, g2_ref, g_ref,
                  wb_ref, wo_ref, gate_ref, lng_ref, lnb_ref, sh2_ref, sc2_ref, o_ref, hf_ref):
    g = g_ref[...]
    ys = yf_ref[...] + yb_ref[...]
    inv_hd = 1.0 / HEAD_DIM
    mu = _mm_rx(ys, g, 3) * inv_hd
    dd = ys - mu
    var = _mm_rx(dd * dd, g, 3) * inv_hd
    yn = dd * lax.rsqrt(var + GN_EPS) * gx_ref[...] + bx_ref[...]
    gate_r = jnp.dot(_sig(lg_ref[...]).astype(BF), g2_ref[...], preferred_element_type=F32)
    ro = (yn + bv_ref[...]) * gate_r
    br = jnp.dot(ro.astype(BF), wb_ref[0], preferred_element_type=F32)
    bh = jnp.dot(ho_ref[...].astype(BF), wb_ref[1], preferred_element_type=F32)
    sg = _sig(pg_ref[...])
    m = sg[:, :D_MODEL] * br + sg[:, D_MODEL:] * bh
    mix = jnp.dot(m.astype(BF), wo_ref[...], preferred_element_type=F32)
    xn = _layer_norm(ALPHA * x_ref[...] + gate_ref[...] * mix, lng_ref[...], lnb_ref[...])
    o_ref[...] = xn
    hf_ref[...] = xn * (1.0 + sc2_ref[...]) + sh2_ref[...]


def _merge_call(yf, yb, bv, lg, ho, pg, x, mp, l):
    L, d = x.shape
    hw = RWKV_WIDTH
    tb = min(L, 256)
    row = lambda i: (i, 0)
    names = ("lnx_g", "lnx_b", "g2", "G", "w_branch", "w_out", "gate", "ln_g", "ln_b", "sh2", "sc2")
    pargs = [mp[nm] for nm in names]

    def pspec(nm, a):
        if nm in ("g2", "w_branch", "w_out"):
            nd = a.ndim - 1
            return pl.BlockSpec((None,) + a.shape[1:], lambda i: (l,) + (0,) * nd)
        return _full(a.shape)

    return pl.pallas_call(
        _merge_kernel,
        out_shape=(jax.ShapeDtypeStruct((L, d), F32), jax.ShapeDtypeStruct((L, d), F32)),
        grid=(L // tb,),
        in_specs=[pl.BlockSpec((tb, hw), row), pl.BlockSpec((tb, hw), row), pl.BlockSpec((tb, hw), row),
                  pl.BlockSpec((tb, LORA_G), row), pl.BlockSpec((tb, hw), row),
                  pl.BlockSpec((tb, GATE_COLS), row), pl.BlockSpec((tb, d), row)]
                 + [pspec(nm, a) for nm, a in zip(names, pargs)],
        out_specs=(pl.BlockSpec((tb, d), row), pl.BlockSpec((tb, d), row)),
        compiler_params=_cparams(("arbitrary",)),
        name="merge_postnorm",
    )(yf, yb, bv, lg, ho, pg, x, *pargs)


def _router_kernel(hf_ref, wrt_ref, rb_ref, cw_ref, sel_ref):
    lt = _mm(wrt_ref[...], hf_ref[...], _NT, 6)
    rid = lax.broadcasted_iota(jnp.int32, (LANES, 1), 0)
    valid = rid < N_EXPERTS
    lg = jnp.where(valid, lt, -jnp.inf)
    mx = jnp.max(lg, axis=0, keepdims=True)
    ex = jnp.where(valid, jnp.exp(lg - mx), 0.0)
    scores = ex / jnp.sum(ex, axis=0, keepdims=True)
    sel = scores + rb_ref[...]
    s = [sel[e:e + 1, :] for e in range(N_EXPERTS)]
    p = [scores[e:e + 1, :] for e in range(N_EXPERTS)]
    gs = []
    for gi in range(N_GROUPS):
        mem = s[gi * EXPERTS_PER_GROUP:(gi + 1) * EXPERTS_PER_GROUP]
        best = None
        for a in range(EXPERTS_PER_GROUP):
            for b in range(a + 1, EXPERTS_PER_GROUP):
                pair = mem[a] + mem[b]
                best = pair if best is None else jnp.maximum(best, pair)
        gs.append(best)
    bg = jnp.where((gs[0] >= gs[1]) & (gs[0] >= gs[2]) & (gs[0] >= gs[3]), 0,
                   jnp.where((gs[1] >= gs[2]) & (gs[1] >= gs[3]), 1, jnp.where(gs[2] >= gs[3], 2, 3)))
    chosen = []
    for e in range(N_EXPERTS):
        gi = e // EXPERTS_PER_GROUP
        beats = None
        for j in range(gi * EXPERTS_PER_GROUP, (gi + 1) * EXPERTS_PER_GROUP):
            if j == e:
                continue
            cond = (s[j] >= s[e]) if j < e else (s[j] > s[e])
            cnt = jnp.where(cond, 1.0, 0.0)
            beats = cnt if beats is None else beats + cnt
        chosen.append((bg == gi) & (beats < 1.5))
    den = None
    for e in range(N_EXPERTS):
        t = jnp.where(chosen[e], p[e], 0.0)
        den = t if den is None else den + t
    out = jnp.zeros(lt.shape, F32)
    msk = jnp.zeros(lt.shape, F32)
    for e in range(N_EXPERTS):
        hit = (rid == e) & chosen[e]
        out = jnp.where(hit, p[e] / den, out)
        msk = jnp.where(hit, 1.0, msk)
    cw_ref[...] = out.T
    sel_ref[...] = msk.T


def _router_call(hf, wrt_pad, rb_pad):
    L, d = hf.shape
    tb = min(L, 256)
    row = lambda i: (i, 0)
    return pl.pallas_call(
        _router_kernel,
        out_shape=(jax.ShapeDtypeStruct((L, LANES), F32), jax.ShapeDtypeStruct((L, LANES), F32)),
        grid=(L // tb,),
        in_specs=[pl.BlockSpec((tb, d), row), _full(wrt_pad.shape), _full(rb_pad.shape)],
        out_specs=(pl.BlockSpec((tb, LANES), row), pl.BlockSpec((tb, LANES), row)),
        compiler_params=_cparams(("arbitrary",)),
        name="router",
    )(hf, wrt_pad, rb_pad)


MOE_BLK = 256


def _moe_ffn_kernel(tok_ref, slot_ref, be_ref, hf_hbm, wg_ref, wu_ref, wd_ref, y_hbm,
                    xbuf, ybuf, gsem, ssem, *, nb):
    b = pl.program_id(0)
    nrow = MOE_BLK
    s = lax.rem(b, 2)
    o = 1 - s

    def row_in(idx, sl, r):
        return pltpu.make_async_copy(hf_hbm.at[pl.ds(idx, 1)], xbuf.at[sl, pl.ds(r, 1)], gsem.at[sl])

    def row_out(idx, sl, r):
        return pltpu.make_async_copy(ybuf.at[sl, pl.ds(r, 1)], y_hbm.at[pl.ds(idx, 1)], ssem.at[sl])

    @pl.when(b == 0)
    def _():
        for r in range(nrow):
            row_in(tok_ref[r], 0, r).start()
        ybuf[1] = jnp.zeros(ybuf.shape[1:], F32)

    for r in range(nrow):
        row_in(0, s, r).wait()

    @pl.when(b >= 1)
    def _():
        for r in range(nrow):
            row_out(0, s, r).wait()

    nxt = (b + 1) * nrow
    for r in range(nrow):
        row_in(tok_ref[nxt + r], o, r).start()
    cur = b * nrow
    for r in range(nrow):
        row_out(slot_ref[cur + r], o, r).start()

    x = xbuf[s].astype(BF)
    hg = jnp.dot(x, wg_ref[...], preferred_element_type=F32)
    hu = jnp.dot(x, wu_ref[...], preferred_element_type=F32)
    act = (hg * _sig(hg) * hu).astype(BF)
    ybuf[s] = jnp.dot(act, wd_ref[...], preferred_element_type=F32)

    @pl.when(b == nb)
    def _():
        for r in range(nrow):
            row_in(0, o, r).wait()
        for r in range(nrow):
            row_out(0, o, r).wait()


def _moe_ffn_call(row_tok, row_slot, block_e, hf_all, wg, wu, wd, l):
    d = hf_all.shape[1]
    nb = block_e.shape[0] - 1
    wspec = pl.BlockSpec((None, None, d, d), lambda b, tok, slot, be: (l, be[b], 0, 0))
    return pl.pallas_call(
        functools.partial(_moe_ffn_kernel, nb=nb),
        out_shape=jax.ShapeDtypeStruct(((nb + 1) * MOE_BLK, d), F32),
        grid_spec=pltpu.PrefetchScalarGridSpec(
            num_scalar_prefetch=3,
            grid=(nb + 1,),
            in_specs=[pl.BlockSpec(memory_space=pl.ANY), wspec, wspec, wspec],
            out_specs=pl.BlockSpec(memory_space=pl.ANY),
            scratch_shapes=[pltpu.VMEM((2, MOE_BLK, d), F32), pltpu.VMEM((2, MOE_BLK, d), F32),
                            pltpu.SemaphoreType.DMA((2,)), pltpu.SemaphoreType.DMA((2,))]),
        compiler_params=_cparams(("arbitrary",)),
        name="moe_ffn",
    )(row_tok, row_slot, block_e, hf_all, wg, wu, wd)


def _moe_dispatch(sel, cw):
    t_all = sel.shape[0]
    blk = MOE_BLK
    nb = (2 * t_all) // blk + N_EXPERTS
    nr = nb * blk
    mi = (sel > 0.5).astype(jnp.int32)
    rank = jnp.cumsum(mi, axis=0) - mi
    counts = jnp.sum(mi, axis=0)
    padded = ((counts + blk - 1) // blk) * blk
    pend = jnp.cumsum(padded)
    dest = pend - padded + rank
    kk = jnp.cumsum(mi, axis=1) - mi
    first = (mi > 0) & (kk == 0)
    second = (mi > 0) & (kk == 1)
    d0 = jnp.sum(jnp.where(first, dest, 0), axis=1)
    d1 = jnp.sum(jnp.where(second, dest, 0), axis=1)
    w0 = jnp.sum(jnp.where(first, cw, 0.0), axis=1, keepdims=True)
    w1 = jnp.sum(jnp.where(second, cw, 0.0), axis=1, keepdims=True)
    tok2 = 2 * jnp.arange(t_all, dtype=jnp.int32)
    packed = jnp.full((nr,), -1, jnp.int32).at[jnp.concatenate([d0, d1])].set(
        jnp.concatenate([tok2, tok2 + 1]), unique_indices=True)
    is_pad = packed < 0
    row_tok = jnp.where(is_pad, 0, jnp.right_shift(packed, 1))
    blk_start = jnp.arange(nb, dtype=jnp.int32)[:, None] * blk
    block_e = jnp.clip(jnp.sum((blk_start >= pend[None, :]).astype(jnp.int32), axis=1), 0, N_EXPERTS - 1)
    real_before = jnp.repeat(jnp.cumsum(counts)[block_e], blk)
    pad_slot = 2 * t_all + jnp.arange(nr, dtype=jnp.int32) - real_before
    row_slot = jnp.where(is_pad, pad_slot, jnp.bitwise_and(packed, 1) * t_all + row_tok)
    row_tok = jnp.concatenate([row_tok, jnp.zeros((2 * blk,), jnp.int32)])
    row_slot = jnp.concatenate([nr + jnp.arange(blk, dtype=jnp.int32), row_slot])
    block_e = jnp.concatenate([block_e, block_e[-1:]])
    return row_tok, row_slot, block_e, w0, w1


def _combine_kernel(x_ref, y0_ref, y1_ref, w0_ref, w1_ref, gate_ref, lng_ref, lnb_ref, o_ref):
    y = y0_ref[...] * w0_ref[...] + y1_ref[...] * w1_ref[...]
    o_ref[...] = _layer_norm(ALPHA * x_ref[...] + gate_ref[...] * y, lng_ref[...], lnb_ref[...])


def _combine_call(x, y_slots, w0, w1, row_off, t_all, gate, lng, lnb, tm):
    L, d = x.shape
    off0 = row_off // tm
    off1 = (t_all + row_off) // tm
    vec = _full((1, d))
    return pl.pallas_call(
        _combine_kernel,
        out_shape=jax.ShapeDtypeStruct((L, d), F32),
        grid=(L // tm,),
        in_specs=[pl.BlockSpec((tm, d), lambda i: (i, 0)),
                  pl.BlockSpec((tm, d), lambda i: (off0 + i, 0)),
                  pl.BlockSpec((tm, d), lambda i: (off1 + i, 0)),
                  pl.BlockSpec((tm, 1), lambda i: (off0 + i, 0)),
                  pl.BlockSpec((tm, 1), lambda i: (off0 + i, 0)), vec, vec, vec],
        out_specs=pl.BlockSpec((tm, d), lambda i: (i, 0)),
        compiler_params=_cparams(("arbitrary",)),
        name="moe_combine",
    )(x, y_slots, y_slots, w0, w1, gate, lng, lnb)


def _moe_layer(segs, wrt_pad, rb_pad, lng, lnb, wg, wu, wd, l):
    routed = [_router_call(hf, wrt_pad, rb_pad) for (_, hf, _) in segs]
    cw = jnp.concatenate([r[0][:, :N_EXPERTS] for r in routed], axis=0)
    sel = jnp.concatenate([r[1][:, :N_EXPERTS] for r in routed], axis=0)
    hf_all = jnp.concatenate([hf for (_, hf, _) in segs], axis=0) if len(segs) > 1 else segs[0][1]
    t_all = hf_all.shape[0]
    row_tok, row_slot, block_e, w0, w1 = _moe_dispatch(sel, cw)
    y_slots = _moe_ffn_call(row_tok, row_slot, block_e, hf_all, wg, wu, wd, l)
    tm = min(256, min(x.shape[0] for (x, _, _) in segs))
    outs = []
    off = 0
    for (x, _, gate) in segs:
        assert off % tm == 0 and t_all % tm == 0 and x.shape[0] % tm == 0
        outs.append(_combine_call(x, y_slots, w0, w1, off, t_all, gate, lng, lnb, tm))
        off += x.shape[0]
    return outs


def _blockdiag2(m):
    z = jnp.zeros_like(m[0])
    return jnp.concatenate([jnp.concatenate([m[0], z], axis=1), jnp.concatenate([z, m[1]], axis=1)], axis=0)


def _mixer(x, mod_row, l, w_in_bf, prm, fp, hy, mp, s0, latent):
    L = x.shape[0]
    sh, sc, gate = mod_row[0], mod_row[1], mod_row[2]
    p_r, p_h, p_g = _inproj_call(x, sh, sc, w_in_bf, l)
    r, v, ka, lw, k, kb, lg, bv = _prepare_call(p_r, prm, latent)
    yf, yb, sfin = _scan_call(r, v, ka, lw, k, kb, s0)
    filt, asum = _hfilt_call(L, fp)
    z, x0 = _hconv3_call(p_h, hy["conv_w"], hy["conv_b"])
    ho = _long_conv_call(z, x0, filt, asum, hy["bias"])
    mpl = dict(mp)
    mpl["gate"], mpl["sh2"], mpl["sc2"] = gate, mod_row[3], mod_row[4]
    xn, hf = _merge_call(yf, yb, bv, lg, ho, p_g, x, mpl, l)
    return xn, hf, sfin


def kernel(x, c, ctx, c_ctx, w_mod, b_mod, w_in, rwkv_mu, rwkv_w0, rwkv_w2, rwkv_a0, rwkv_a2, rwkv_g2,
           rwkv_k_k, rwkv_k_a, rwkv_r_k, rwkv_lnx_g, rwkv_lnx_b, hy_conv_w, hy_conv_b, hy_f_w1, hy_f_b1,
           hy_f_w2, hy_f_b2, hy_f_w3, hy_f_b3, hy_f_wout, hy_freq, hy_bias, w_branch, w_out, ln_g, ln_b,
           w_router, router_bias, w_gate, w_up, w_down):
    b, n_lat, d = x.shape
    assert b == 1 and d == D_MODEL
    n_ctx = ctx.shape[1]
    depth = w_mod.shape[0]
    hw = RWKV_WIDTH
    xl = x[0]
    xc = ctx[0]

    cc = jnp.concatenate([c[:1], c_ctx[None, :], jnp.zeros((6, d), F32)], axis=0)
    mod = _mod_call(cc, w_mod, b_mod)

    head_of = np.arange(hw) // HEAD_DIM
    G = jnp.asarray((head_of[:, None] == head_of[None, :]).astype(np.float32), dtype=BF)
    bands = jnp.linspace(1e-4, FILTER_BANDS - 1, FILTER_BANDS, dtype=F32)[None, :]
    deltas = jnp.abs(jnp.linspace(HYENA_MIN_DECAY, HYENA_MAX_DECAY, HYENA_WIDTH, dtype=F32))[None, :]
    wr_pad = jnp.pad(w_router.T, ((0, LANES - N_EXPERTS), (0, 0)))
    rb_pad = jnp.pad(router_bias, (0, LANES - N_EXPERTS))[:, None]
    w_in_bf = w_in.astype(BF)
    w_branch_bf = w_branch.astype(BF)
    w_out_bf = w_out.astype(BF)
    g2_bf = rwkv_g2.astype(BF)
    wg_bf, wu_bf, wd_bf = w_gate.astype(BF), w_up.astype(BF), w_down.astype(BF)

    for l in range(depth):
        last = l == depth - 1
        ml = [mod[l, 0:1, j * d:(j + 1) * d] for j in range(6)]
        mc = [mod[l, 1:2, j * d:(j + 1) * d] for j in range(6)]
        prm = dict(mu=rwkv_mu[l][None, :],
                   w0=rwkv_w0[l].reshape(1, 2 * hw), w2=_blockdiag2(rwkv_w2[l]),
                   a0=rwkv_a0[l].reshape(1, 2 * hw), a2=_blockdiag2(rwkv_a2[l]),
                   k_k=rwkv_k_k[l][None, :], k_a=rwkv_k_a[l][None, :], r_k=rwkv_r_k[l][None, :], G=G)
        w1 = hy_f_w1[l]
        fp = dict(bands=bands, w1t=w1[0:1], w1c=w1[1:1 + FILTER_BANDS], w1s=w1[1 + FILTER_BANDS:],
                  b1=hy_f_b1[l][None, :], w2=hy_f_w2[l], b2=hy_f_b2[l][None, :], w3=hy_f_w3[l],
                  b3=hy_f_b3[l][None, :], wout=hy_f_wout[l], freq=hy_freq[l][None, :], deltas=deltas)
        hy = dict(conv_w=hy_conv_w[l], conv_b=hy_conv_b[l][None, :], bias=hy_bias[l][None, :])
        mp = dict(lnx_g=rwkv_lnx_g[l][None, :], lnx_b=rwkv_lnx_b[l][None, :], g2=g2_bf, G=G,
                  w_branch=w_branch_bf, w_out=w_out_bf, ln_g=ln_g[l, 0][None, :], ln_b=ln_b[l, 0][None, :])
        s0 = jnp.zeros((2, HEAD_DIM, hw), F32)
        xc_new, hf_c, s_ctx = _mixer(xc, mc, l, w_in_bf, prm, fp, hy, mp, s0, latent=False)
        xl, hf_l, _ = _mixer(xl, ml, l, w_in_bf, prm, fp, hy, mp, s_ctx, latent=True)
        lng, lnb = ln_g[l, 1][None, :], ln_b[l, 1][None, :]
        if last:
            (xl,) = _moe_layer([(xl, hf_l, ml[5])], wr_pad, rb_pad, lng, lnb, wg_bf, wu_bf, wd_bf, l)
        else:
            xc, xl = _moe_layer([(xc_new, hf_c, mc[5]), (xl, hf_l, ml[5])], wr_pad, rb_pad, lng, lnb,
                                wg_bf, wu_bf, wd_bf, l)
    return xl[None]
```

```python
import functools
import math

import numpy as np
import jax
import jax.numpy as jnp
from jax import lax
from jax.experimental import pallas as pl
from jax.experimental.pallas import tpu as pltpu

F32 = jnp.float32
BF = jnp.bfloat16

D_MODEL = 1024
DEPTH = 4
GRID_W = 64
RWKV_WIDTH = 512
HEAD_DIM = 64
RWKV_HEADS = 8
LORA_W = 64
LORA_A = 64
LORA_G = 128
DECAY_SCALE = 0.606531
GN_EPS = 6.4e-4
RWKV_COLS = 3 * RWKV_WIDTH + 2 * LORA_W + 2 * LORA_A + LORA_G
HYENA_WIDTH = 512
HYENA_COLS = 3 * HYENA_WIDTH
FILTER_BANDS = 16
FILTER_HIDDEN = 64
HYENA_MIN_DECAY = math.log(1e-2) / 1.5
HYENA_MAX_DECAY = math.log(1e-2) / 0.3
GATE_COLS = 2 * D_MODEL
PROJ_COLS = RWKV_COLS + HYENA_COLS + GATE_COLS
N_EXPERTS = 16
N_GROUPS = 4
EXPERTS_PER_GROUP = 4
ALPHA = (2 * DEPTH) ** 0.25
LN_EPS = 1e-5

SCAN_CHUNK = 64
LANES = 128
VMEM_LIMIT = 56 * 1024 * 1024

_NN = (((1,), (0,)), ((), ()))
_NT = (((1,), (1,)), ((), ()))
_TN = (((0,), (0,)), ((), ()))


def _sig(x):
    return 1.0 / (1.0 + jnp.exp(-x))


def _parts(a, n):
    out = []
    rem = a
    for i in range(n):
        hi = rem.astype(BF)
        out.append(hi)
        if i + 1 < n:
            rem = rem - hi.astype(F32)
    return out


def _mm(a, b, dn=_NN, passes=1):
    n = {1: 1, 3: 2, 6: 3}[passes]
    pa = _parts(a, n)
    pb = _parts(b, n)
    acc = None
    for i in range(n):
        for j in range(n - i):
            t = lax.dot_general(pa[i], pb[j], dn, preferred_element_type=F32)
            acc = t if acc is None else acc + t
    return acc


def _mm_rx(a, b_exact, n=3, dn=_NN):
    acc = None
    for p in _parts(a, n):
        t = lax.dot_general(p, b_exact, dn, preferred_element_type=F32)
        acc = t if acc is None else acc + t
    return acc


def _mm_lx(a_exact, b, n=3, dn=_NN):
    acc = None
    for p in _parts(b, n):
        t = lax.dot_general(a_exact, p, dn, preferred_element_type=F32)
        acc = t if acc is None else acc + t
    return acc


def _cparams(sem, vmem=VMEM_LIMIT):
    return pltpu.CompilerParams(dimension_semantics=sem, vmem_limit_bytes=vmem)


def _full(shape):
    nd = len(shape)
    return pl.BlockSpec(shape, lambda *_: (0,) * nd)


def _mod_kernel(c_ref, w_ref, b_ref, o_ref):
    cc = c_ref[...]
    s = cc * _sig(cc)
    o_ref[...] = jnp.dot(s.astype(BF), w_ref[...].astype(BF), preferred_element_type=F32) + b_ref[...]


def _mod_call(cc, w_mod, b_mod):
    depth, d, n6 = w_mod.shape
    tn = 1536
    return pl.pallas_call(
        _mod_kernel,
        out_shape=jax.ShapeDtypeStruct((depth, 8, n6), F32),
        grid=(depth, n6 // tn),
        in_specs=[
            pl.BlockSpec((8, d), lambda l, j: (0, 0)),
            pl.BlockSpec((None, d, tn), lambda l, j: (l, 0, j)),
            pl.BlockSpec((None, 1, tn), lambda l, j: (l, 0, j)),
        ],
        out_specs=pl.BlockSpec((None, 8, tn), lambda l, j: (l, 0, j)),
        compiler_params=_cparams(("arbitrary", "arbitrary")),
        name="mod",
    )(cc, w_mod, b_mod.reshape(depth, 1, n6))


def _inproj_kernel(x_ref, sh_ref, sc_ref, w_ref, pr_ref, ph_ref, pg_ref):
    xm = (x_ref[...] * (1.0 + sc_ref[...]) + sh_ref[...]).astype(BF)
    pr_ref[...] = jnp.dot(xm, w_ref[:, :RWKV_COLS], preferred_element_type=F32)
    ph_ref[...] = jnp.dot(xm, w_ref[:, RWKV_COLS:RWKV_COLS + HYENA_COLS], preferred_element_type=F32)
    pg_ref[...] = jnp.dot(xm, w_ref[:, RWKV_COLS + HYENA_COLS:], preferred_element_type=F32)


def _inproj_call(x, sh, sc, w_bf, l):
    L, d = x.shape
    tm = min(L, 256)
    row = lambda i: (i, 0)
    return pl.pallas_call(
        _inproj_kernel,
        out_shape=(jax.ShapeDtypeStruct((L, RWKV_COLS), F32),
                   jax.ShapeDtypeStruct((L, HYENA_COLS), F32),
                   jax.ShapeDtypeStruct((L, GATE_COLS), F32)),
        grid=(L // tm,),
        in_specs=[pl.BlockSpec((tm, d), row), _full((1, d)), _full((1, d)),
                  pl.BlockSpec((None, d, PROJ_COLS), lambda i: (l, 0, 0))],
        out_specs=(pl.BlockSpec((tm, RWKV_COLS), row), pl.BlockSpec((tm, HYENA_COLS), row),
                   pl.BlockSpec((tm, GATE_COLS), row)),
        compiler_params=_cparams(("arbitrary",)),
        name="inproj",
    )(x, sh, sc, w_bf)


def _prepare_kernel(*refs, latent, nblk):
    if latent:
        p_ref, up_ref, dn_ref = refs[:3]
        refs = refs[3:]
    else:
        p_ref = refs[0]
        refs = refs[1:]
    (mu_ref, w0_ref, w2_ref, a0_ref, a2_ref, kk_ref, ka_ref, rk_ref, g_ref,
     r_out, v_out, ka_out, lw_out, k_out, kb_out, lg_out, bv_out) = refs
    i = pl.program_id(0)
    p = p_ref[...]
    tb, w = p.shape
    row = lax.broadcasted_iota(jnp.int32, (tb, 1), 0)
    col = lax.broadcasted_iota(jnp.int32, (1, w), 1)
    if latent:
        gw = jnp.bitwise_and(row, GRID_W - 1)
        left = jnp.where(gw == 0, 0.0, pltpu.roll(p, 1, 0))
        right = jnp.where(gw == GRID_W - 1, 0.0, pltpu.roll(p, tb - 1, 0))
        upv = jnp.where(i == 0, 0.0, up_ref[...])
        dnv = jnp.where(i == nblk - 1, 0.0, dn_ref[...])
        if tb > GRID_W:
            up = jnp.concatenate([upv, p[:tb - GRID_W]], axis=0)
            down = jnp.concatenate([p[GRID_W:], dnv], axis=0)
        else:
            up, down = upv, dnv
        q = w // 4
        sh = jnp.where(col < q, left, jnp.where(col < 2 * q, right, jnp.where(col < 3 * q, up, down)))
    else:
        prev = jnp.where(row == 0, 0.0, pltpu.roll(p, 1, 0))
        nxt = jnp.where(row == tb - 1, 0.0, pltpu.roll(p, tb - 1, 0))
        sh = jnp.where(col < w // 2, prev, nxt)
    pm = p + (sh - p) * mu_ref[...]
    hw = RWKV_WIDTH
    r = pm[:, :hw]
    k = pm[:, hw:2 * hw]
    v = pm[:, 2 * hw:3 * hw]
    lw_in = jnp.tanh(pm[:, 3 * hw:3 * hw + 2 * LORA_W])
    la_in = pm[:, 3 * hw + 2 * LORA_W:3 * hw + 2 * LORA_W + 2 * LORA_A]
    lg = pm[:, 3 * hw + 2 * LORA_W + 2 * LORA_A:]
    lw = -DECAY_SCALE * _sig(w0_ref[...] + _mm(lw_in, w2_ref[...], passes=3))
    a = _sig(a0_ref[...] + _mm(la_in, a2_ref[...], passes=3))
    g = g_ref[...]
    kkr = k * kk_ref[...]
    nrm = jnp.sqrt(_mm_rx(kkr * kkr, g, 3))
    kk = kkr / jnp.maximum(nrm, 1e-12)
    k_a = ka_ref[...]
    k_f = k * (1.0 + (a[:, :hw] - 1.0) * k_a)
    k_b = k * (1.0 + (a[:, hw:] - 1.0) * k_a)
    bonus = _mm_rx(r * (k_f + k_b) * rk_ref[...], g, 3)
    r_out[...] = r
    v_out[...] = v
    ka_out[...] = -kk
    lw_out[0] = lw[:, :hw]
    lw_out[1] = lw[:, hw:]
    k_out[0] = k_f
    k_out[1] = k_b
    kb_out[0] = kk * a[:, :hw]
    kb_out[1] = kk * a[:, hw:]
    lg_out[...] = lg
    bv_out[...] = bonus * v


def _prepare_call(p_r, prm, latent):
    L, w = p_r.shape
    hw = RWKV_WIDTH
    tb = 256 if latent else L
    nblk = L // tb
    row = lambda i: (i, 0)
    row3 = lambda i: (0, i, 0)
    in_specs = [pl.BlockSpec((tb, w), row)]
    args = [p_r]
    if latent:
        per = tb // GRID_W
        nrow = L // GRID_W
        in_specs += [
            pl.BlockSpec((GRID_W, w), lambda i: (jnp.maximum(i * per - 1, 0), 0)),
            pl.BlockSpec((GRID_W, w), lambda i: (jnp.minimum((i + 1) * per, nrow - 1), 0)),
        ]
        args += [p_r, p_r]
    names = ("mu", "w0", "w2", "a0", "a2", "k_k", "k_a", "r_k", "G")
    for nm in names:
        in_specs.append(_full(prm[nm].shape))
        args.append(prm[nm])
    sd = jax.ShapeDtypeStruct
    out_shape = (sd((L, hw), F32), sd((L, hw), F32), sd((L, hw), F32),
                 sd((2, L, hw), F32), sd((2, L, hw), F32), sd((2, L, hw), F32),
                 sd((L, LORA_G), F32), sd((L, hw), F32))
    out_specs = (pl.BlockSpec((tb, hw), row), pl.BlockSpec((tb, hw), row), pl.BlockSpec((tb, hw), row),
                 pl.BlockSpec((2, tb, hw), row3), pl.BlockSpec((2, tb, hw), row3), pl.BlockSpec((2, tb, hw), row3),
                 pl.BlockSpec((tb, LORA_G), row), pl.BlockSpec((tb, hw), row))
    return pl.pallas_call(
        functools.partial(_prepare_kernel, latent=latent, nblk=nblk),
        out_shape=out_shape, grid=(nblk,), in_specs=in_specs, out_specs=out_specs,
        compiler_params=_cparams(("arbitrary",)),
        name="rwkv_prepare_lat" if latent else "rwkv_prepare_ctx",
    )(*args)


P_SCORE = 3
P_TINV = 3
P_APPLY = 1
P_STATE = 1


SCAN_SUB = 2


def _scan_chunk(fin, bin_, states, masks):
    C = SCAN_CHUNK
    hd = HEAD_DIM
    nh = RWKV_HEADS
    row, col, eye, lvl_masks = masks
    dirs = []
    for d, (r, v, ka, lw, k, kb) in enumerate((fin, bin_)):
        inc = (row >= col) if d == 0 else (row <= col)
        strict = (row > col) if d == 0 else (row < col)
        lc = _mm_lx(jnp.where(inc, 1.0, 0.0).astype(BF), lw, 3)
        e_neg = jnp.exp(-lc)
        e_tot = jnp.exp(jnp.sum(lw, axis=0, keepdims=True))
        kbn = kb * e_neg
        kkn = k * e_neg
        dirs.append(dict(inc=inc, strict=strict, v=v, aq=ka * jnp.exp(lc - lw), rq=r * jnp.exp(lc),
                         kbn=kbn, kkn=kkn, kbp=kbn * e_tot, kkp=kkn * e_tot, e_tot=e_tot, s=states[d]))

    units = [(d, h) for h in range(nh) for d in range(2)]
    hs = lambda arr, h: arr[:, h * hd:(h + 1) * hd]
    sc = {}
    for (d, h) in units:
        D = dirs[d]
        sc[d, h] = _mm(jnp.concatenate([hs(D["aq"], h), hs(D["rq"], h)], axis=0),
                       jnp.concatenate([hs(D["kbn"], h), hs(D["kkn"], h)], axis=0), _NT, P_SCORE)
    a_ab, t, x = {}, {}, {}
    for u in units:
        a_ab[u] = jnp.where(dirs[u[0]]["strict"], sc[u][:C, :C], 0.0)
        t[u] = eye + jnp.where(lvl_masks[0], a_ab[u], 0.0)
    for u in units:
        D = dirs[u[0]]
        lhs = jnp.concatenate([jnp.where(D["strict"], sc[u][:C, C:], 0.0),
                               jnp.where(D["inc"], sc[u][C:, C:], 0.0)], axis=0)
        x[u] = _mm(lhs, hs(D["v"], u[1]), _NN, P_APPLY)
    for m in lvl_masks[1:]:
        tmp = {u: _mm(jnp.where(m, a_ab[u], 0.0), t[u], _NN, 1) for u in units}
        for u in units:
            t[u] = t[u] + _mm(t[u], tmp[u], _NN, 1)
    res = {u: eye - t[u] + _mm(a_ab[u], t[u], _NN, P_TINV) for u in units}
    for u in units:
        t[u] = t[u] + _mm(t[u], res[u], _NN, 1)
    z, gm, uu, yy, sn = {}, {}, {}, {}, {}
    for u in units:
        z[u] = _mm(t[u], jnp.concatenate([hs(dirs[u[0]]["aq"], u[1]), x[u][:C]], axis=1), _NN, P_APPLY)
    for u in units:
        D = dirs[u[0]]
        gm[u] = _mm(jnp.concatenate([z[u][:, :hd], hs(D["rq"], u[1])], axis=0), hs(D["s"], u[1]), _NT, P_STATE)
        uu[u] = gm[u][:C] + z[u][:, hd:]
    for u in units:
        D = dirs[u[0]]
        b_rb = jnp.where(D["inc"], sc[u][C:, :C], 0.0)
        yy[u] = gm[u][C:] + _mm(b_rb, uu[u], _NN, P_APPLY) + x[u][C:]
        sn[u] = hs(D["s"], u[1]) * hs(D["e_tot"], u[1]) + _mm(
            jnp.concatenate([uu[u], hs(D["v"], u[1])], axis=0),
            jnp.concatenate([hs(D["kbp"], u[1]), hs(D["kkp"], u[1])], axis=0), _TN, P_STATE)
    cat = lambda d, src: jnp.concatenate([src[d, h] for h in range(nh)], axis=1)
    return cat(0, yy), cat(1, yy), [cat(0, sn), cat(1, sn)]


def _scan_kernel(rf_ref, vf_ref, kaf_ref, rb_ref, vb_ref, kab_ref, lwf_ref, kf_ref, kbf_ref,
                 lwb_ref, kbk_ref, kbb_ref, s0_ref, yf_ref, yb_ref, sfin_ref, s_scr, *, nstep):
    c = pl.program_id(0)
    C = SCAN_CHUNK

    @pl.when(c == 0)
    def _():
        s_scr[...] = s0_ref[...]

    row = lax.broadcasted_iota(jnp.int32, (C, C), 0)
    col = lax.broadcasted_iota(jnp.int32, (C, C), 1)
    eye = jnp.where(row == col, 1.0, 0.0)
    lvl_masks = []
    for sh in range(int(math.log2(C))):
        same2 = jnp.right_shift(row, sh + 1) == jnp.right_shift(col, sh + 1)
        same1 = jnp.right_shift(row, sh) == jnp.right_shift(col, sh)
        lvl_masks.append(jnp.logical_and(same2, jnp.logical_not(same1)))
    masks = (row, col, eye, lvl_masks)

    f_refs = (rf_ref, vf_ref, kaf_ref, lwf_ref, kf_ref, kbf_ref)
    b_refs = (rb_ref, vb_ref, kab_ref, lwb_ref, kbk_ref, kbb_ref)
    states = [s_scr[0], s_scr[1]]
    for j in range(SCAN_SUB):
        fsl = slice(j * C, (j + 1) * C)
        bsl = slice((SCAN_SUB - 1 - j) * C, (SCAN_SUB - j) * C)
        yf, yb, states = _scan_chunk(tuple(ref[fsl, :] for ref in f_refs), tuple(ref[bsl, :] for ref in b_refs),
                                     states, masks)
        yf_ref[fsl, :] = yf
        yb_ref[bsl, :] = yb
    s_scr[0] = states[0]
    s_scr[1] = states[1]

    @pl.when(c == nstep - 1)
    def _():
        sfin_ref[...] = s_scr[...]


def _scan_call(r, v, ka, lw, k, kb, s0):
    L, hw = r.shape
    C = SCAN_CHUNK * SCAN_SUB
    nchunk = L // C
    assert L % C == 0
    sh_f = pl.BlockSpec((C, hw), lambda c: (c, 0))
    sh_b = pl.BlockSpec((C, hw), lambda c: (nchunk - 1 - c, 0))
    pd_f = pl.BlockSpec((None, C, hw), lambda c: (0, c, 0))
    pd_b = pl.BlockSpec((None, C, hw), lambda c: (1, nchunk - 1 - c, 0))
    state = _full((2, HEAD_DIM, hw))
    return pl.pallas_call(
        functools.partial(_scan_kernel, nstep=nchunk),
        out_shape=(jax.ShapeDtypeStruct((L, hw), F32), jax.ShapeDtypeStruct((L, hw), F32),
                   jax.ShapeDtypeStruct((2, HEAD_DIM, hw), F32)),
        grid=(nchunk,),
        in_specs=[sh_f, sh_f, sh_f, sh_b, sh_b, sh_b, pd_f, pd_f, pd_f, pd_b, pd_b, pd_b, state],
        out_specs=(sh_f, sh_b, state),
        scratch_shapes=[pltpu.VMEM((2, HEAD_DIM, hw), F32)],
        compiler_params=_cparams(("arbitrary",)),
        name="delta_scan",
    )(r, v, ka, r, v, ka, lw, k, kb, lw, k, kb, s0)


def _hfilt_kernel(bands_ref, w1t_ref, w1c_ref, w1s_ref, b1_ref, w2_ref, b2_ref, w3_ref, b3_ref, wo_ref,
                  fr_ref, dl_ref, f_ref, asum_ref, *, n, rb):
    i = pl.program_id(0)
    pos = (i * rb + lax.broadcasted_iota(jnp.int32, (rb, 1), 0)).astype(F32)
    t = pos / float(max(n - 1, 1))
    ang = ((2.0 * math.pi / n) * pos) * bands_ref[...]
    fr = fr_ref[...]
    h = t * w1t_ref[...] + _mm(jnp.cos(ang), w1c_ref[...], passes=3) + _mm(-jnp.sin(ang), w1s_ref[...], passes=3)
    h = jnp.sin(fr * (h + b1_ref[...]))
    h = jnp.sin(fr * (_mm(h, w2_ref[...], passes=3) + b2_ref[...]))
    h = jnp.sin(fr * (_mm(h, w3_ref[...], passes=3) + b3_ref[...]))
    filt = _mm(h, wo_ref[...], passes=3)
    dist = jnp.abs(pos - float(n // 2)) * (2.0 / n)
    filt = filt * jnp.exp(-dist * dl_ref[...])
    f_ref[...] = filt

    @pl.when(i == 0)
    def _():
        asum_ref[...] = jnp.zeros_like(asum_ref)

    asum_ref[...] += jnp.sum(jnp.abs(filt), axis=0, keepdims=True)


def _hfilt_call(n, fp):
    rb = min(n, 512)
    names = ("bands", "w1t", "w1c", "w1s", "b1", "w2", "b2", "w3", "b3", "wout", "freq", "deltas")
    args = [fp[nm] for nm in names]
    return pl.pallas_call(
        functools.partial(_hfilt_kernel, n=n, rb=rb),
        out_shape=(jax.ShapeDtypeStruct((n, HYENA_WIDTH), F32), jax.ShapeDtypeStruct((1, HYENA_WIDTH), F32)),
        grid=(n // rb,),
        in_specs=[_full(a.shape) for a in args],
        out_specs=(pl.BlockSpec((rb, HYENA_WIDTH), lambda i: (i, 0)), _full((1, HYENA_WIDTH))),
        compiler_params=_cparams(("arbitrary",)),
        name="hyena_filter",
    )(*args)


def _hconv3_kernel(p_ref, pv_ref, nx_ref, cw_ref, cb_ref, z_ref, x0_ref, *, nblk):
    i = pl.program_id(0)
    p = p_ref[...]
    tb = p.shape[0]
    row = lax.broadcasted_iota(jnp.int32, (tb, 1), 0)
    prev_row = jnp.where(i == 0, 0.0, pv_ref[7:8, :])
    next_row = jnp.where(i == nblk - 1, 0.0, nx_ref[0:1, :])
    sp = jnp.where(row == 0, prev_row, pltpu.roll(p, 1, 0))
    sn = jnp.where(row == tb - 1, next_row, pltpu.roll(p, tb - 1, 0))
    u = sp * cw_ref[0:1, :] + p * cw_ref[1:2, :] + sn * cw_ref[2:3, :] + cb_ref[...]
    hw = HYENA_WIDTH
    z_ref[...] = u[:, 2 * hw:] * u[:, hw:2 * hw]
    x0_ref[...] = u[:, :hw]


def _hconv3_call(p_h, cw, cb):
    L, w = p_h.shape
    tb = min(L, 256)
    nblk = L // tb
    per = tb // 8
    row = lambda i: (i, 0)
    return pl.pallas_call(
        functools.partial(_hconv3_kernel, nblk=nblk),
        out_shape=(jax.ShapeDtypeStruct((L, HYENA_WIDTH), F32), jax.ShapeDtypeStruct((L, HYENA_WIDTH), F32)),
        grid=(nblk,),
        in_specs=[pl.BlockSpec((tb, w), row),
                  pl.BlockSpec((8, w), lambda i: (jnp.maximum(i * per - 1, 0), 0)),
                  pl.BlockSpec((8, w), lambda i: (jnp.minimum((i + 1) * per, L // 8 - 1), 0)),
                  _full(cw.shape), _full(cb.shape)],
        out_specs=(pl.BlockSpec((tb, HYENA_WIDTH), row), pl.BlockSpec((tb, HYENA_WIDTH), row)),
        compiler_params=_cparams(("arbitrary",)),
        name="hyena_conv3",
    )(p_h, p_h, p_h, cw, cb)


P_DFT = 1


DFT_NB = 8


def _dft1_kernel(m_ref, x_ref, o_ref):
    m = m_ref[...]
    n1 = o_ref.shape[1]
    for j in range(x_ref.shape[1]):
        r = _mm(m, x_ref[:, j, :], _NN, P_DFT)
        o_ref[0, :, j, :] = r[:n1]
        o_ref[1, :, j, :] = r[n1:]


def _dft1_call(m1, x3):
    rows, kdim = m1.shape
    _, n2, c = x3.shape
    n1 = rows // 2
    nbk = min(n2, DFT_NB)
    return pl.pallas_call(
        _dft1_kernel,
        out_shape=jax.ShapeDtypeStruct((2, n1, n2, c), F32),
        grid=(n2 // nbk,),
        in_specs=[_full(m1.shape), pl.BlockSpec((kdim, nbk, c), lambda j: (0, j, 0))],
        out_specs=pl.BlockSpec((2, n1, nbk, c), lambda j: (0, 0, j, 0)),
        compiler_params=_cparams(("arbitrary",)),
        name="dft_stage1",
    )(m1, x3)


def _dft2_matrix(fc_ref, fs_ref, tc_ref, ts_ref):
    fc, fs = fc_ref[...], fs_ref[...]
    tc, ts = tc_ref[...], ts_ref[...]
    fre = fc * tc - fs * ts
    fim = -(fc * ts + fs * tc)
    return jnp.concatenate([jnp.concatenate([fre, -fim], axis=1),
                            jnp.concatenate([fim, fre], axis=1)], axis=0)


def _dft2_conv_kernel(fc_ref, fs_ref, tc_ref, ts_ref, a_ref, af_ref, b_ref):
    big = _dft2_matrix(fc_ref, fs_ref, tc_ref, ts_ref)
    n2, c = a_ref.shape[1], a_ref.shape[2]
    rhs = jnp.concatenate([jnp.concatenate([a_ref[0], af_ref[0]], axis=1),
                           jnp.concatenate([a_ref[1], af_ref[1]], axis=1)], axis=0)
    xh = _mm(big, rhs, _NN, P_DFT)
    xr, hr = xh[:n2, :c], xh[:n2, c:]
    xi, hi = xh[n2:, :c], xh[n2:, c:]
    y = jnp.concatenate([xr * hr - xi * hi, xr * hi + xi * hr], axis=0)
    bb = _mm(big, y, _TN, P_DFT)
    b_ref[0] = bb[:n2]
    b_ref[1] = bb[n2:]


def _dft2_call(consts, a4, af4):
    _, n1, n2, c = a4.shape
    blk = pl.BlockSpec((2, None, n2, c), lambda q: (0, q, 0, 0))
    tw = pl.BlockSpec((None, 1, n2), lambda q: (q, 0, 0))
    return pl.pallas_call(
        _dft2_conv_kernel,
        out_shape=jax.ShapeDtypeStruct(a4.shape, F32),
        grid=(n1,),
        in_specs=[_full((n2, n2)), _full((n2, n2)), tw, tw, blk, blk],
        out_specs=blk,
        compiler_params=_cparams(("arbitrary",)),
        name="dft_stage2_conv",
    )(consts["fc"], consts["fs"], consts["twc"], consts["tws"], a4, af4)


def _dft3_kernel(m_ref, b_ref, z_ref, x0_ref, bias_ref, asum_ref, o_ref):
    m = m_ref[...]
    for j in range(z_ref.shape[1]):
        bj = jnp.concatenate([b_ref[0, :, j, :], b_ref[1, :, j, :]], axis=0)
        y = _mm(m, bj, _NN, P_DFT)
        o_ref[:, j, :] = (y / asum_ref[...] + z_ref[:, j, :] * bias_ref[...]) * x0_ref[:, j, :]


def _dft3_call(m3, b4, z3, x03, bias, asum):
    rows, _ = m3.shape
    _, n1, n2, c = b4.shape
    nbk = min(n2, DFT_NB)
    slab = pl.BlockSpec((rows, nbk, c), lambda j: (0, j, 0))
    return pl.pallas_call(
        _dft3_kernel,
        out_shape=jax.ShapeDtypeStruct((rows, n2, c), F32),
        grid=(n2 // nbk,),
        in_specs=[_full(m3.shape), pl.BlockSpec((2, n1, nbk, c), lambda j: (0, 0, j, 0)), slab, slab,
                  _full((1, c)), _full((1, c))],
        out_specs=slab,
        compiler_params=_cparams(("arbitrary",)),
        name="dft_stage3",
    )(m3, b4, z3, x03, bias, asum)


def _conv_direct_kernel(m1_ref, m3_ref, z_ref, f_ref, x0_ref, bias_ref, asum_ref, o_ref):
    m1 = m1_ref[...]
    z = z_ref[...]
    a = _mm(m1, z, _NN, P_DFT)
    h = _mm(m1, f_ref[...], _NN, P_DFT)
    nn = a.shape[0] // 2
    ar, ai, hr, hi = a[:nn], a[nn:], h[:nn], h[nn:]
    y = jnp.concatenate([ar * hr - ai * hi, ar * hi + ai * hr], axis=0)
    out = _mm(m3_ref[...], y, _NN, P_DFT)
    o_ref[...] = (out / asum_ref[...] + z * bias_ref[...]) * x0_ref[...]


def _conv_direct_call(m1, m3, z, filt, x0, bias, asum):
    args = (m1, m3, z, filt, x0, bias, asum)
    return pl.pallas_call(
        _conv_direct_kernel,
        out_shape=jax.ShapeDtypeStruct(z.shape, F32),
        grid=(1,),
        in_specs=[_full(a.shape) for a in args],
        out_specs=_full(z.shape),
        compiler_params=_cparams(("arbitrary",)),
        name="long_conv_direct",
    )(*args)


DIRECT_CONV_MAX = 256


@functools.lru_cache(maxsize=None)
def _dft_consts_np(n):
    big_n = 2 * n
    if n <= DIRECT_CONV_MAX:
        n1 = big_n
    else:
        n1 = 256 if n >= 8192 else 64
    n2 = big_n // n1
    k1 = np.arange(n1)[:, None].astype(np.float64)
    j1 = np.arange(n1 // 2)[None, :].astype(np.float64)
    ang1 = 2.0 * np.pi * k1 * j1 / n1
    m1 = np.concatenate([np.cos(ang1), -np.sin(ang1)], axis=0)
    o1 = (n1 // 4 + np.arange(n1 // 2))[:, None].astype(np.float64)
    q1 = np.arange(n1)[None, :].astype(np.float64)
    ang3 = 2.0 * np.pi * o1 * q1 / n1
    m3 = np.concatenate([np.cos(ang3), -np.sin(ang3)], axis=1) / big_n
    k2 = np.arange(n2)[:, None].astype(np.float64)
    j2 = np.arange(n2)[None, :].astype(np.float64)
    ang2 = 2.0 * np.pi * k2 * j2 / n2
    angt = 2.0 * np.pi * np.arange(n1)[:, None].astype(np.float64) * j2 / big_n
    f = lambda a: np.asarray(a, np.float32)
    return dict(n1=n1, n2=n2, m1=f(m1), m3=f(m3), fc=f(np.cos(ang2)), fs=f(np.sin(ang2)),
                twc=f(np.cos(angt))[:, None, :], tws=f(np.sin(angt))[:, None, :])


def _long_conv_call(z, x0, filt, asum, bias):
    n, c = z.shape
    cn = _dft_consts_np(n)
    n1, n2 = cn["n1"], cn["n2"]
    consts = {kk: jnp.asarray(vv) for kk, vv in cn.items() if kk not in ("n1", "n2")}
    if n2 == 1:
        return _conv_direct_call(consts["m1"], consts["m3"], z, filt, x0, bias, asum)
    z3 = z.reshape(n1 // 2, n2, c)
    a_f = _dft1_call(consts["m1"], filt.reshape(n1 // 2, n2, c))
    a_z = _dft1_call(consts["m1"], z3)
    b4 = _dft2_call(consts, a_z, a_f)
    out = _dft3_call(consts["m3"], b4, z3, x0.reshape(n1 // 2, n2, c), bias, asum)
    return out.reshape(n, c)


def _layer_norm(x, g, b):
    mu = jnp.mean(x, axis=-1, keepdims=True)
    xc = x - mu
    var = jnp.mean(xc * xc, axis=-1, keepdims=True)
    return xc * lax.rsqrt(var + LN_EPS) * g + b


def _merge_kernel(yf_ref, yb_ref, bv_ref, lg_ref, ho_ref, pg_ref, x_ref, gx_ref, bx_ref, g2_ref, g_ref,
                  wb_ref, wo_ref, gate_ref, lng_ref, lnb_ref, sh2_ref, sc2_ref, o_ref, hf_ref):
    g = g_ref[...]
    ys = yf_ref[...] + yb_ref[...]
    inv_hd = 1.0 / HEAD_DIM
    mu = _mm_rx(ys, g, 3) * inv_hd
    dd = ys - mu
    var = _mm_rx(dd * dd, g, 3) * inv_hd
    yn = dd * lax.rsqrt(var + GN_EPS) * gx_ref[...] + bx_ref[...]
    gate_r = jnp.dot(_sig(lg_ref[...]).astype(BF), g2_ref[...], preferred_element_type=F32)
    ro = (yn + bv_ref[...]) * gate_r
    br = jnp.dot(ro.astype(BF), wb_ref[0], preferred_element_type=F32)
    bh = jnp.dot(ho_ref[...].astype(BF), wb_ref[1], preferred_element_type=F32)
    sg = _sig(pg_ref[...])
    m = sg[:, :D_MODEL] * br + sg[:, D_MODEL:] * bh
    mix = jnp.dot(m.astype(BF), wo_ref[...], preferred_element_type=F32)
    xn = _layer_norm(ALPHA * x_ref[...] + gate_ref[...] * mix, lng_ref[...], lnb_ref[...])
    o_ref[...] = xn
    hf_ref[...] = xn * (1.0 + sc2_ref[...]) + sh2_ref[...]


def _merge_call(yf, yb, bv, lg, ho, pg, x, mp, l):
    L, d = x.shape
    hw = RWKV_WIDTH
    tb = min(L, 256)
    row = lambda i: (i, 0)
    names = ("lnx_g", "lnx_b", "g2", "G", "w_branch", "w_out", "gate", "ln_g", "ln_b", "sh2", "sc2")
    pargs = [mp[nm] for nm in names]

    def pspec(nm, a):
        if nm in ("g2", "w_branch", "w_out"):
            nd = a.ndim - 1
            return pl.BlockSpec((None,) + a.shape[1:], lambda i: (l,) + (0,) * nd)
        return _full(a.shape)

    return pl.pallas_call(
        _merge_kernel,
        out_shape=(jax.ShapeDtypeStruct((L, d), F32), jax.ShapeDtypeStruct((L, d), F32)),
        grid=(L // tb,),
        in_specs=[pl.BlockSpec((tb, hw), row), pl.BlockSpec((tb, hw), row), pl.BlockSpec((tb, hw), row),
                  pl.BlockSpec((tb, LORA_G), row), pl.BlockSpec((tb, hw), row),
                  pl.BlockSpec((tb, GATE_COLS), row), pl.BlockSpec((tb, d), row)]
                 + [pspec(nm, a) for nm, a in zip(names, pargs)],
        out_specs=(pl.BlockSpec((tb, d), row), pl.BlockSpec((tb, d), row)),
        compiler_params=_cparams(("arbitrary",)),
        name="merge_postnorm",
    )(yf, yb, bv, lg, ho, pg, x, *pargs)


def _router_kernel(hf_ref, wrt_ref, rb_ref, cw_ref, sel_ref):
    lt = _mm(wrt_ref[...], hf_ref[...], _NT, 6)
    rid = lax.broadcasted_iota(jnp.int32, (LANES, 1), 0)
    valid = rid < N_EXPERTS
    lg = jnp.where(valid, lt, -jnp.inf)
    mx = jnp.max(lg, axis=0, keepdims=True)
    ex = jnp.where(valid, jnp.exp(lg - mx), 0.0)
    scores = ex / jnp.sum(ex, axis=0, keepdims=True)
    sel = scores + rb_ref[...]
    s = [sel[e:e + 1, :] for e in range(N_EXPERTS)]
    p = [scores[e:e + 1, :] for e in range(N_EXPERTS)]
    gs = []
    for gi in range(N_GROUPS):
        mem = s[gi * EXPERTS_PER_GROUP:(gi + 1) * EXPERTS_PER_GROUP]
        best = None
        for a in range(EXPERTS_PER_GROUP):
            for b in range(a + 1, EXPERTS_PER_GROUP):
                pair = mem[a] + mem[b]
                best = pair if best is None else jnp.maximum(best, pair)
        gs.append(best)
    bg = jnp.where((gs[0] >= gs[1]) & (gs[0] >= gs[2]) & (gs[0] >= gs[3]), 0,
                   jnp.where((gs[1] >= gs[2]) & (gs[1] >= gs[3]), 1, jnp.where(gs[2] >= gs[3], 2, 3)))
    chosen = []
    for e in range(N_EXPERTS):
        gi = e // EXPERTS_PER_GROUP
        beats = None
        for j in range(gi * EXPERTS_PER_GROUP, (gi + 1) * EXPERTS_PER_GROUP):
            if j == e:
                continue
            cond = (s[j] >= s[e]) if j < e else (s[j] > s[e])
            cnt = jnp.where(cond, 1.0, 0.0)
            beats = cnt if beats is None else beats + cnt
        chosen.append((bg == gi) & (beats < 1.5))
    den = None
    for e in range(N_EXPERTS):
        t = jnp.where(chosen[e], p[e], 0.0)
        den = t if den is None else den + t
    out = jnp.zeros(lt.shape, F32)
    msk = jnp.zeros(lt.shape, F32)
    for e in range(N_EXPERTS):
        hit = (rid == e) & chosen[e]
        out = jnp.where(hit, p[e] / den, out)
        msk = jnp.where(hit, 1.0, msk)
    cw_ref[...] = out.T
    sel_ref[...] = msk.T


def _router_call(hf, wrt_pad, rb_pad):
    L, d = hf.shape
    tb = min(L, 256)
    row = lambda i: (i, 0)
    return pl.pallas_call(
        _router_kernel,
        out_shape=(jax.ShapeDtypeStruct((L, LANES), F32), jax.ShapeDtypeStruct((L, LANES), F32)),
        grid=(L // tb,),
        in_specs=[pl.BlockSpec((tb, d), row), _full(wrt_pad.shape), _full(rb_pad.shape)],
        out_specs=(pl.BlockSpec((tb, LANES), row), pl.BlockSpec((tb, LANES), row)),
        compiler_params=_cparams(("arbitrary",)),
        name="router",
    )(hf, wrt_pad, rb_pad)


MOE_BLK = 256


def _moe_ffn_kernel(tok_ref, slot_ref, be_ref, hf_hbm, wg_ref, wu_ref, wd_ref, y_hbm,
                    xbuf, ybuf, gsem, ssem, *, nb):
    b = pl.program_id(0)
    nrow = MOE_BLK

    def row_in(idx, sl, r):
        return pltpu.make_async_copy(hf_hbm.at[pl.ds(idx, 1)], xbuf.at[sl, pl.ds(r, 1)], gsem.at[sl])

    def row_out(idx, sl, r):
        return pltpu.make_async_copy(ybuf.at[sl, pl.ds(r, 1)], y_hbm.at[pl.ds(idx, 1)], ssem.at[sl])

    @pl.when(b == 0)
    def _():
        for r in range(nrow):
            row_in(tok_ref[r], 0, r).start()
        ybuf[1] = jnp.zeros(ybuf.shape[1:], F32)

    def step(s, o):
        for r in range(nrow):
            row_in(0, s, r).wait()

        @pl.when(b >= 1)
        def _():
            for r in range(nrow):
                row_out(0, s, r).wait()

        nxt = (b + 1) * nrow
        for r in range(nrow):
            row_in(tok_ref[nxt + r], o, r).start()
        cur = b * nrow
        for r in range(nrow):
            row_out(slot_ref[cur + r], o, r).start()

        x = xbuf[s].astype(BF)
        hg = jnp.dot(x, wg_ref[...], preferred_element_type=F32)
        hu = jnp.dot(x, wu_ref[...], preferred_element_type=F32)
        act = (hg * _sig(hg) * hu).astype(BF)
        ybuf[s] = jnp.dot(act, wd_ref[...], preferred_element_type=F32)

        @pl.when(b == nb)
        def _():
            for r in range(nrow):
                row_in(0, o, r).wait()
            for r in range(nrow):
                row_out(0, o, r).wait()

    par = lax.rem(b, 2)

    @pl.when(par == 0)
    def _():
        step(0, 1)

    @pl.when(par == 1)
    def _():
        step(1, 0)


def _moe_ffn_call(row_tok, row_slot, block_e, hf_all, wg, wu, wd, l):
    d = hf_all.shape[1]
    nb = block_e.shape[0] - 1
    wspec = pl.BlockSpec((None, None, d, d), lambda b, tok, slot, be: (l, be[b], 0, 0))
    return pl.pallas_call(
        functools.partial(_moe_ffn_kernel, nb=nb),
        out_shape=jax.ShapeDtypeStruct(((nb + 1) * MOE_BLK, d), F32),
        grid_spec=pltpu.PrefetchScalarGridSpec(
            num_scalar_prefetch=3,
            grid=(nb + 1,),
            in_specs=[pl.BlockSpec(memory_space=pl.ANY), wspec, wspec, wspec],
            out_specs=pl.BlockSpec(memory_space=pl.ANY),
            scratch_shapes=[pltpu.VMEM((2, MOE_BLK, d), F32), pltpu.VMEM((2, MOE_BLK, d), F32),
                            pltpu.SemaphoreType.DMA((2,)), pltpu.SemaphoreType.DMA((2,))]),
        compiler_params=_cparams(("arbitrary",)),
        name="moe_ffn",
    )(row_tok, row_slot, block_e, hf_all, wg, wu, wd)


def _moe_dispatch(sel, cw):
    t_all = sel.shape[0]
    blk = MOE_BLK
    nb = (2 * t_all) // blk + N_EXPERTS
    nr = nb * blk
    mi = (sel > 0.5).astype(jnp.int32)
    rank = jnp.cumsum(mi, axis=0) - mi
    counts = jnp.sum(mi, axis=0)
    padded = ((counts + blk - 1) // blk) * blk
    pend = jnp.cumsum(padded)
    dest = pend - padded + rank
    kk = jnp.cumsum(mi, axis=1) - mi
    first = (mi > 0) & (kk == 0)
    second = (mi > 0) & (kk == 1)
    d0 = jnp.sum(jnp.where(first, dest, 0), axis=1)
    d1 = jnp.sum(jnp.where(second, dest, 0), axis=1)
    w0 = jnp.sum(jnp.where(first, cw, 0.0), axis=1, keepdims=True)
    w1 = jnp.sum(jnp.where(second, cw, 0.0), axis=1, keepdims=True)
    tok2 = 2 * jnp.arange(t_all, dtype=jnp.int32)
    packed = jnp.full((nr,), -1, jnp.int32).at[jnp.concatenate([d0, d1])].set(
        jnp.concatenate([tok2, tok2 + 1]), unique_indices=True)
    is_pad = packed < 0
    row_tok = jnp.where(is_pad, 0, jnp.right_shift(packed, 1))
    blk_start = jnp.arange(nb, dtype=jnp.int32)[:, None] * blk
    block_e = jnp.clip(jnp.sum((blk_start >= pend[None, :]).astype(jnp.int32), axis=1), 0, N_EXPERTS - 1)
    real_before = jnp.repeat(jnp.cumsum(counts)[block_e], blk)
    pad_slot = 2 * t_all + jnp.arange(nr, dtype=jnp.int32) - real_before
    row_slot = jnp.where(is_pad, pad_slot, jnp.bitwise_and(packed, 1) * t_all + row_tok)
    row_tok = jnp.concatenate([row_tok, jnp.zeros((2 * blk,), jnp.int32)])
    row_slot = jnp.concatenate([nr + jnp.arange(blk, dtype=jnp.int32), row_slot])
    block_e = jnp.concatenate([block_e, block_e[-1:]])
    return row_tok, row_slot, block_e, w0, w1


def _combine_kernel(x_ref, y0_ref, y1_ref, w0_ref, w1_ref, gate_ref, lng_ref, lnb_ref, o_ref):
    y = y0_ref[...] * w0_ref[...] + y1_ref[...] * w1_ref[...]
    o_ref[...] = _layer_norm(ALPHA * x_ref[...] + gate_ref[...] * y, lng_ref[...], lnb_ref[...])


def _combine_call(x, y_slots, w0, w1, row_off, t_all, gate, lng, lnb, tm):
    L, d = x.shape
    off0 = row_off // tm
    off1 = (t_all + row_off) // tm
    vec = _full((1, d))
    return pl.pallas_call(
        _combine_kernel,
        out_shape=jax.ShapeDtypeStruct((L, d), F32),
        grid=(L // tm,),
        in_specs=[pl.BlockSpec((tm, d), lambda i: (i, 0)),
                  pl.BlockSpec((tm, d), lambda i: (off0 + i, 0)),
                  pl.BlockSpec((tm, d), lambda i: (off1 + i, 0)),
                  pl.BlockSpec((tm, 1), lambda i: (off0 + i, 0)),
                  pl.BlockSpec((tm, 1), lambda i: (off0 + i, 0)), vec, vec, vec],
        out_specs=pl.BlockSpec((tm, d), lambda i: (i, 0)),
        compiler_params=_cparams(("arbitrary",)),
        name="moe_combine",
    )(x, y_slots, y_slots, w0, w1, gate, lng, lnb)


def _moe_layer(segs, wrt_pad, rb_pad, lng, lnb, wg, wu, wd, l):
    routed = [_router_call(hf, wrt_pad, rb_pad) for (_, hf, _) in segs]
    cw = jnp.concatenate([r[0][:, :N_EXPERTS] for r in routed], axis=0)
    sel = jnp.concatenate([r[1][:, :N_EXPERTS] for r in routed], axis=0)
    hf_all = jnp.concatenate([hf for (_, hf, _) in segs], axis=0) if len(segs) > 1 else segs[0][1]
    t_all = hf_all.shape[0]
    row_tok, row_slot, block_e, w0, w1 = _moe_dispatch(sel, cw)
    y_slots = _moe_ffn_call(row_tok, row_slot, block_e, hf_all, wg, wu, wd, l)
    tm = min(256, min(x.shape[0] for (x, _, _) in segs))
    outs = []
    off = 0
    for (x, _, gate) in segs:
        assert off % tm == 0 and t_all % tm == 0 and x.shape[0] % tm == 0
        outs.append(_combine_call(x, y_slots, w0, w1, off, t_all, gate, lng, lnb, tm))
        off += x.shape[0]
    return outs


def _blockdiag2(m):
    z = jnp.zeros_like(m[0])
    return jnp.concatenate([jnp.concatenate([m[0], z], axis=1), jnp.concatenate([z, m[1]], axis=1)], axis=0)


def _mixer(x, mod_row, l, w_in_bf, prm, fp, hy, mp, s0, latent):
    L = x.shape[0]
    sh, sc, gate = mod_row[0], mod_row[1], mod_row[2]
    p_r, p_h, p_g = _inproj_call(x, sh, sc, w_in_bf, l)
    r, v, ka, lw, k, kb, lg, bv = _prepare_call(p_r, prm, latent)
    yf, yb, sfin = _scan_call(r, v, ka, lw, k, kb, s0)
    filt, asum = _hfilt_call(L, fp)
    z, x0 = _hconv3_call(p_h, hy["conv_w"], hy["conv_b"])
    ho = _long_conv_call(z, x0, filt, asum, hy["bias"])
    mpl = dict(mp)
    mpl["gate"], mpl["sh2"], mpl["sc2"] = gate, mod_row[3], mod_row[4]
    xn, hf = _merge_call(yf, yb, bv, lg, ho, p_g, x, mpl, l)
    return xn, hf, sfin


def kernel(x, c, ctx, c_ctx, w_mod, b_mod, w_in, rwkv_mu, rwkv_w0, rwkv_w2, rwkv_a0, rwkv_a2, rwkv_g2,
           rwkv_k_k, rwkv_k_a, rwkv_r_k, rwkv_lnx_g, rwkv_lnx_b, hy_conv_w, hy_conv_b, hy_f_w1, hy_f_b1,
           hy_f_w2, hy_f_b2, hy_f_w3, hy_f_b3, hy_f_wout, hy_freq, hy_bias, w_branch, w_out, ln_g, ln_b,
           w_router, router_bias, w_gate, w_up, w_down):
    b, n_lat, d = x.shape
    assert b == 1 and d == D_MODEL
    n_ctx = ctx.shape[1]
    depth = w_mod.shape[0]
    hw = RWKV_WIDTH
    xl = x[0]
    xc = ctx[0]

    cc = jnp.concatenate([c[:1], c_ctx[None, :], jnp.zeros((6, d), F32)], axis=0)
    mod = _mod_call(cc, w_mod, b_mod)

    head_of = np.arange(hw) // HEAD_DIM
    G = jnp.asarray((head_of[:, None] == head_of[None, :]).astype(np.float32), dtype=BF)
    bands = jnp.linspace(1e-4, FILTER_BANDS - 1, FILTER_BANDS, dtype=F32)[None, :]
    deltas = jnp.abs(jnp.linspace(HYENA_MIN_DECAY, HYENA_MAX_DECAY, HYENA_WIDTH, dtype=F32))[None, :]
    wr_pad = jnp.pad(w_router.T, ((0, LANES - N_EXPERTS), (0, 0)))
    rb_pad = jnp.pad(router_bias, (0, LANES - N_EXPERTS))[:, None]
    w_in_bf = w_in.astype(BF)
    w_branch_bf = w_branch.astype(BF)
    w_out_bf = w_out.astype(BF)
    g2_bf = rwkv_g2.astype(BF)
    wg_bf, wu_bf, wd_bf = w_gate.astype(BF), w_up.astype(BF), w_down.astype(BF)

    for l in range(depth):
        last = l == depth - 1
        ml = [mod[l, 0:1, j * d:(j + 1) * d] for j in range(6)]
        mc = [mod[l, 1:2, j * d:(j + 1) * d] for j in range(6)]
        prm = dict(mu=rwkv_mu[l][None, :],
                   w0=rwkv_w0[l].reshape(1, 2 * hw), w2=_blockdiag2(rwkv_w2[l]),
                   a0=rwkv_a0[l].reshape(1, 2 * hw), a2=_blockdiag2(rwkv_a2[l]),
                   k_k=rwkv_k_k[l][None, :], k_a=rwkv_k_a[l][None, :], r_k=rwkv_r_k[l][None, :], G=G)
        w1 = hy_f_w1[l]
        fp = dict(bands=bands, w1t=w1[0:1], w1c=w1[1:1 + FILTER_BANDS], w1s=w1[1 + FILTER_BANDS:],
                  b1=hy_f_b1[l][None, :], w2=hy_f_w2[l], b2=hy_f_b2[l][None, :], w3=hy_f_w3[l],
                  b3=hy_f_b3[l][None, :], wout=hy_f_wout[l], freq=hy_freq[l][None, :], deltas=deltas)
        hy = dict(conv_w=hy_conv_w[l], conv_b=hy_conv_b[l][None, :], bias=hy_bias[l][None, :])
        mp = dict(lnx_g=rwkv_lnx_g[l][None, :], lnx_b=rwkv_lnx_b[l][None, :], g2=g2_bf, G=G,
                  w_branch=w_branch_bf, w_out=w_out_bf, ln_g=ln_g[l, 0][None, :], ln_b=ln_b[l, 0][None, :])
        s0 = jnp.zeros((2, HEAD_DIM, hw), F32)
        xc_new, hf_c, s_ctx = _mixer(xc, mc, l, w_in_bf, prm, fp, hy, mp, s0, latent=False)
        xl, hf_l, _ = _mixer(xl, ml, l, w_in_bf, prm, fp, hy, mp, s_ctx, latent=True)
        lng, lnb = ln_g[l, 1][None, :], ln_b[l, 1][None, :]
        if last:
            (xl,) = _moe_layer([(xl, hf_l, ml[5])], wr_pad, rb_pad, lng, lnb, wg_bf, wu_bf, wd_bf, l)
        else:
            xc, xl = _moe_layer([(xc_new, hf_c, mc[5]), (xl, hf_l, ml[5])], wr_pad, rb_pad, lng, lnb,
                                wg_bf, wu_bf, wd_bf, l)
    return xl[None]
```

```python
import functools
import math

import numpy as np
import jax
import jax.numpy as jnp
from jax import lax
from jax.experimental import pallas as pl
from jax.experimental.pallas import tpu as pltpu

F32 = jnp.float32
BF = jnp.bfloat16

D_MODEL = 1024
DEPTH = 4
GRID_W = 64
RWKV_WIDTH = 512
HEAD_DIM = 64
RWKV_HEADS = 8
LORA_W = 64
LORA_A = 64
LORA_G = 128
DECAY_SCALE = 0.606531
GN_EPS = 6.4e-4
RWKV_COLS = 3 * RWKV_WIDTH + 2 * LORA_W + 2 * LORA_A + LORA_G
HYENA_WIDTH = 512
HYENA_COLS = 3 * HYENA_WIDTH
FILTER_BANDS = 16
FILTER_HIDDEN = 64
HYENA_MIN_DECAY = math.log(1e-2) / 1.5
HYENA_MAX_DECAY = math.log(1e-2) / 0.3
GATE_COLS = 2 * D_MODEL
PROJ_COLS = RWKV_COLS + HYENA_COLS + GATE_COLS
N_EXPERTS = 16
N_GROUPS = 4
EXPERTS_PER_GROUP = 4
ALPHA = (2 * DEPTH) ** 0.25
LN_EPS = 1e-5

SCAN_CHUNK = 64
LANES = 128
VMEM_LIMIT = 56 * 1024 * 1024

_NN = (((1,), (0,)), ((), ()))
_NT = (((1,), (1,)), ((), ()))
_TN = (((0,), (0,)), ((), ()))


def _sig(x):
    return 1.0 / (1.0 + jnp.exp(-x))


def _parts(a, n):
    out = []
    rem = a
    for i in range(n):
        hi = rem.astype(BF)
        out.append(hi)
        if i + 1 < n:
            rem = rem - hi.astype(F32)
    return out


def _mm(a, b, dn=_NN, passes=1):
    n = {1: 1, 3: 2, 6: 3}[passes]
    pa = _parts(a, n)
    pb = _parts(b, n)
    acc = None
    for i in range(n):
        for j in range(n - i):
            t = lax.dot_general(pa[i], pb[j], dn, preferred_element_type=F32)
            acc = t if acc is None else acc + t
    return acc


def _mm_rx(a, b_exact, n=3, dn=_NN):
    acc = None
    for p in _parts(a, n):
        t = lax.dot_general(p, b_exact, dn, preferred_element_type=F32)
        acc = t if acc is None else acc + t
    return acc


def _mm_lx(a_exact, b, n=3, dn=_NN):
    acc = None
    for p in _parts(b, n):
        t = lax.dot_general(a_exact, p, dn, preferred_element_type=F32)
        acc = t if acc is None else acc + t
    return acc


def _cparams(sem, vmem=VMEM_LIMIT):
    return pltpu.CompilerParams(dimension_semantics=sem, vmem_limit_bytes=vmem)


def _full(shape):
    nd = len(shape)
    return pl.BlockSpec(shape, lambda *_: (0,) * nd)


def _mod_kernel(c_ref, w_ref, b_ref, o_ref):
    cc = c_ref[...]
    s = cc * _sig(cc)
    o_ref[...] = jnp.dot(s.astype(BF), w_ref[...].astype(BF), preferred_element_type=F32) + b_ref[...]


def _mod_call(cc, w_mod, b_mod):
    depth, d, n6 = w_mod.shape
    tn = 1536
    return pl.pallas_call(
        _mod_kernel,
        out_shape=jax.ShapeDtypeStruct((depth, 8, n6), F32),
        grid=(depth, n6 // tn),
        in_specs=[
            pl.BlockSpec((8, d), lambda l, j: (0, 0)),
            pl.BlockSpec((None, d, tn), lambda l, j: (l, 0, j)),
            pl.BlockSpec((None, 1, tn), lambda l, j: (l, 0, j)),
        ],
        out_specs=pl.BlockSpec((None, 8, tn), lambda l, j: (l, 0, j)),
        compiler_params=_cparams(("arbitrary", "arbitrary")),
        name="mod",
    )(cc, w_mod, b_mod.reshape(depth, 1, n6))


def _inproj_kernel(x_ref, sh_ref, sc_ref, w_ref, pr_ref, ph_ref, pg_ref):
    xm = (x_ref[...] * (1.0 + sc_ref[...]) + sh_ref[...]).astype(BF)
    pr_ref[...] = jnp.dot(xm, w_ref[:, :RWKV_COLS], preferred_element_type=F32)
    ph_ref[...] = jnp.dot(xm, w_ref[:, RWKV_COLS:RWKV_COLS + HYENA_COLS], preferred_element_type=F32)
    pg_ref[...] = jnp.dot(xm, w_ref[:, RWKV_COLS + HYENA_COLS:], preferred_element_type=F32)


def _inproj_call(x, sh, sc, w_bf, l):
    L, d = x.shape
    tm = min(L, 256)
    row = lambda i: (i, 0)
    return pl.pallas_call(
        _inproj_kernel,
        out_shape=(jax.ShapeDtypeStruct((L, RWKV_COLS), F32),
                   jax.ShapeDtypeStruct((L, HYENA_COLS), F32),
                   jax.ShapeDtypeStruct((L, GATE_COLS), F32)),
        grid=(L // tm,),
        in_specs=[pl.BlockSpec((tm, d), row), _full((1, d)), _full((1, d)),
                  pl.BlockSpec((None, d, PROJ_COLS), lambda i: (l, 0, 0))],
        out_specs=(pl.BlockSpec((tm, RWKV_COLS), row), pl.BlockSpec((tm, HYENA_COLS), row),
                   pl.BlockSpec((tm, GATE_COLS), row)),
        compiler_params=_cparams(("arbitrary",)),
        name="inproj",
    )(x, sh, sc, w_bf)


def _prepare_kernel(*refs, latent, nblk):
    if latent:
        p_ref, up_ref, dn_ref = refs[:3]
        refs = refs[3:]
    else:
        p_ref = refs[0]
        refs = refs[1:]
    (mu_ref, w0_ref, w2_ref, a0_ref, a2_ref, kk_ref, ka_ref, rk_ref, g_ref,
     rvk_out, dir_out, lg_out, bv_out) = refs
    i = pl.program_id(0)
    p = p_ref[...]
    tb, w = p.shape
    row = lax.broadcasted_iota(jnp.int32, (tb, 1), 0)
    col = lax.broadcasted_iota(jnp.int32, (1, w), 1)
    if latent:
        gw = jnp.bitwise_and(row, GRID_W - 1)
        left = jnp.where(gw == 0, 0.0, pltpu.roll(p, 1, 0))
        right = jnp.where(gw == GRID_W - 1, 0.0, pltpu.roll(p, tb - 1, 0))
        upv = jnp.where(i == 0, 0.0, up_ref[...])
        dnv = jnp.where(i == nblk - 1, 0.0, dn_ref[...])
        if tb > GRID_W:
            up = jnp.concatenate([upv, p[:tb - GRID_W]], axis=0)
            down = jnp.concatenate([p[GRID_W:], dnv], axis=0)
        else:
            up, down = upv, dnv
        q = w // 4
        sh = jnp.where(col < q, left, jnp.where(col < 2 * q, right, jnp.where(col < 3 * q, up, down)))
    else:
        prev = jnp.where(row == 0, 0.0, pltpu.roll(p, 1, 0))
        nxt = jnp.where(row == tb - 1, 0.0, pltpu.roll(p, tb - 1, 0))
        sh = jnp.where(col < w // 2, prev, nxt)
    pm = p + (sh - p) * mu_ref[...]
    hw = RWKV_WIDTH
    r = pm[:, :hw]
    k = pm[:, hw:2 * hw]
    v = pm[:, 2 * hw:3 * hw]
    lw_in = jnp.tanh(pm[:, 3 * hw:3 * hw + 2 * LORA_W])
    la_in = pm[:, 3 * hw + 2 * LORA_W:3 * hw + 2 * LORA_W + 2 * LORA_A]
    lg = pm[:, 3 * hw + 2 * LORA_W + 2 * LORA_A:]
    lw = -DECAY_SCALE * _sig(w0_ref[...] + _mm(lw_in, w2_ref[...], passes=3))
    a = _sig(a0_ref[...] + _mm(la_in, a2_ref[...], passes=3))
    g = g_ref[...]
    kkr = k * kk_ref[...]
    nrm = jnp.sqrt(_mm_rx(kkr * kkr, g, 3))
    kk = kkr / jnp.maximum(nrm, 1e-12)
    k_a = ka_ref[...]
    k_f = k * (1.0 + (a[:, :hw] - 1.0) * k_a)
    k_b = k * (1.0 + (a[:, hw:] - 1.0) * k_a)
    bonus = _mm_rx(r * (k_f + k_b) * rk_ref[...], g, 3)
    rvk_out[:, :hw] = r
    rvk_out[:, hw:2 * hw] = v
    rvk_out[:, 2 * hw:] = -kk
    for d, (k_d, a_d) in enumerate(((k_f, a[:, :hw]), (k_b, a[:, hw:]))):
        dir_out[d, :, :hw] = lw[:, d * hw:(d + 1) * hw]
        dir_out[d, :, hw:2 * hw] = k_d
        dir_out[d, :, 2 * hw:] = kk * a_d
    lg_out[...] = lg
    bv_out[...] = bonus * v


def _prepare_call(p_r, prm, latent):
    L, w = p_r.shape
    hw = RWKV_WIDTH
    tb = 256 if latent else L
    nblk = L // tb
    row = lambda i: (i, 0)
    row3 = lambda i: (0, i, 0)
    in_specs = [pl.BlockSpec((tb, w), row)]
    args = [p_r]
    if latent:
        per = tb // GRID_W
        nrow = L // GRID_W
        in_specs += [
            pl.BlockSpec((GRID_W, w), lambda i: (jnp.maximum(i * per - 1, 0), 0)),
            pl.BlockSpec((GRID_W, w), lambda i: (jnp.minimum((i + 1) * per, nrow - 1), 0)),
        ]
        args += [p_r, p_r]
    names = ("mu", "w0", "w2", "a0", "a2", "k_k", "k_a", "r_k", "G")
    for nm in names:
        in_specs.append(_full(prm[nm].shape))
        args.append(prm[nm])
    sd = jax.ShapeDtypeStruct
    out_shape = (sd((L, 3 * hw), F32), sd((2, L, 3 * hw), F32), sd((L, LORA_G), F32), sd((L, hw), F32))
    out_specs = (pl.BlockSpec((tb, 3 * hw), row), pl.BlockSpec((2, tb, 3 * hw), row3),
                 pl.BlockSpec((tb, LORA_G), row), pl.BlockSpec((tb, hw), row))
    return pl.pallas_call(
        functools.partial(_prepare_kernel, latent=latent, nblk=nblk),
        out_shape=out_shape, grid=(nblk,), in_specs=in_specs, out_specs=out_specs,
        compiler_params=_cparams(("arbitrary",)),
        name="rwkv_prepare_lat" if latent else "rwkv_prepare_ctx",
    )(*args)


P_SCORE = 3
P_TINV = 3
P_APPLY = 1
P_STATE = 1


SCAN_SUB = 2


def _scan_chunk(fin, bin_, states, masks):
    C = SCAN_CHUNK
    hd = HEAD_DIM
    nh = RWKV_HEADS
    row, col, eye, lvl_masks = masks
    dirs = []
    for d, (r, v, ka, lw, k, kb) in enumerate((fin, bin_)):
        inc = (row >= col) if d == 0 else (row <= col)
        strict = (row > col) if d == 0 else (row < col)
        lc = _mm_lx(jnp.where(inc, 1.0, 0.0).astype(BF), lw, 3)
        e_neg = jnp.exp(-lc)
        e_tot = jnp.exp(jnp.sum(lw, axis=0, keepdims=True))
        kbn = kb * e_neg
        kkn = k * e_neg
        dirs.append(dict(inc=inc, strict=strict, v=v, aq=ka * jnp.exp(lc - lw), rq=r * jnp.exp(lc),
                         kbn=kbn, kkn=kkn, kbp=kbn * e_tot, kkp=kkn * e_tot, e_tot=e_tot, s=states[d]))

    units = [(d, h) for h in range(nh) for d in range(2)]
    hs = lambda arr, h: arr[:, h * hd:(h + 1) * hd]
    sc = {}
    for (d, h) in units:
        D = dirs[d]
        sc[d, h] = _mm(jnp.concatenate([hs(D["aq"], h), hs(D["rq"], h)], axis=0),
                       jnp.concatenate([hs(D["kbn"], h), hs(D["kkn"], h)], axis=0), _NT, P_SCORE)
    a_ab, t, x = {}, {}, {}
    for u in units:
        a_ab[u] = jnp.where(dirs[u[0]]["strict"], sc[u][:C, :C], 0.0)
        t[u] = eye + jnp.where(lvl_masks[0], a_ab[u], 0.0)
    for u in units:
        D = dirs[u[0]]
        lhs = jnp.concatenate([jnp.where(D["strict"], sc[u][:C, C:], 0.0),
                               jnp.where(D["inc"], sc[u][C:, C:], 0.0)], axis=0)
        x[u] = _mm(lhs, hs(D["v"], u[1]), _NN, P_APPLY)
    for m in lvl_masks[1:]:
        tmp = {u: _mm(jnp.where(m, a_ab[u], 0.0), t[u], _NN, 1) for u in units}
        for u in units:
            t[u] = t[u] + _mm(t[u], tmp[u], _NN, 1)
    res = {u: eye - t[u] + _mm(a_ab[u], t[u], _NN, P_TINV) for u in units}
    for u in units:
        t[u] = t[u] + _mm(t[u], res[u], _NN, 1)
    z, gm, uu, yy, sn = {}, {}, {}, {}, {}
    for u in units:
        z[u] = _mm(t[u], jnp.concatenate([hs(dirs[u[0]]["aq"], u[1]), x[u][:C]], axis=1), _NN, P_APPLY)
    for u in units:
        D = dirs[u[0]]
        gm[u] = _mm(jnp.concatenate([z[u][:, :hd], hs(D["rq"], u[1])], axis=0), hs(D["s"], u[1]), _NT, P_STATE)
        uu[u] = gm[u][:C] + z[u][:, hd:]
    for u in units:
        D = dirs[u[0]]
        b_rb = jnp.where(D["inc"], sc[u][C:, :C], 0.0)
        yy[u] = gm[u][C:] + _mm(b_rb, uu[u], _NN, P_APPLY) + x[u][C:]
        sn[u] = hs(D["s"], u[1]) * hs(D["e_tot"], u[1]) + _mm(
            jnp.concatenate([uu[u], hs(D["v"], u[1])], axis=0),
            jnp.concatenate([hs(D["kbp"], u[1]), hs(D["kkp"], u[1])], axis=0), _TN, P_STATE)
    cat = lambda d, src: jnp.concatenate([src[d, h] for h in range(nh)], axis=1)
    return cat(0, yy), cat(1, yy), [cat(0, sn), cat(1, sn)]


def _scan_kernel(sf_ref, sb_ref, df_ref, db_ref, s0_ref, yf_ref, yb_ref, sfin_ref, s_scr, *, nstep):
    c = pl.program_id(0)
    C = SCAN_CHUNK

    @pl.when(c == 0)
    def _():
        s_scr[...] = s0_ref[...]

    row = lax.broadcasted_iota(jnp.int32, (C, C), 0)
    col = lax.broadcasted_iota(jnp.int32, (C, C), 1)
    eye = jnp.where(row == col, 1.0, 0.0)
    lvl_masks = []
    for sh in range(int(math.log2(C))):
        same2 = jnp.right_shift(row, sh + 1) == jnp.right_shift(col, sh + 1)
        same1 = jnp.right_shift(row, sh) == jnp.right_shift(col, sh)
        lvl_masks.append(jnp.logical_and(same2, jnp.logical_not(same1)))
    masks = (row, col, eye, lvl_masks)

    hw = RWKV_WIDTH
    unpack = lambda sref, dref, rows: tuple(ref[rows, i * hw:(i + 1) * hw] for ref in (sref, dref) for i in range(3))
    states = [s_scr[0], s_scr[1]]
    for j in range(SCAN_SUB):
        fsl = slice(j * C, (j + 1) * C)
        bsl = slice((SCAN_SUB - 1 - j) * C, (SCAN_SUB - j) * C)
        yf, yb, states = _scan_chunk(unpack(sf_ref, df_ref, fsl), unpack(sb_ref, db_ref, bsl), states, masks)
        yf_ref[fsl, :] = yf
        yb_ref[bsl, :] = yb
    s_scr[0] = states[0]
    s_scr[1] = states[1]

    @pl.when(c == nstep - 1)
    def _():
        sfin_ref[...] = s_scr[...]


def _scan_call(rvk, dirp, s0):
    L = rvk.shape[0]
    hw = RWKV_WIDTH
    C = SCAN_CHUNK * SCAN_SUB
    nchunk = L // C
    assert L % C == 0
    sh_f = pl.BlockSpec((C, 3 * hw), lambda c: (c, 0))
    sh_b = pl.BlockSpec((C, 3 * hw), lambda c: (nchunk - 1 - c, 0))
    pd_f = pl.BlockSpec((None, C, 3 * hw), lambda c: (0, c, 0))
    pd_b = pl.BlockSpec((None, C, 3 * hw), lambda c: (1, nchunk - 1 - c, 0))
    y_f = pl.BlockSpec((C, hw), lambda c: (c, 0))
    y_b = pl.BlockSpec((C, hw), lambda c: (nchunk - 1 - c, 0))
    state = _full((2, HEAD_DIM, hw))
    return pl.pallas_call(
        functools.partial(_scan_kernel, nstep=nchunk),
        out_shape=(jax.ShapeDtypeStruct((L, hw), F32), jax.ShapeDtypeStruct((L, hw), F32),
                   jax.ShapeDtypeStruct((2, HEAD_DIM, hw), F32)),
        grid=(nchunk,),
        in_specs=[sh_f, sh_b, pd_f, pd_b, state],
        out_specs=(y_f, y_b, state),
        scratch_shapes=[pltpu.VMEM((2, HEAD_DIM, hw), F32)],
        compiler_params=_cparams(("arbitrary",)),
        name="delta_scan",
    )(rvk, rvk, dirp, dirp, s0)


def _hfilt_kernel(bands_ref, w1t_ref, w1c_ref, w1s_ref, b1_ref, w2_ref, b2_ref, w3_ref, b3_ref, wo_ref,
                  fr_ref, dl_ref, f_ref, asum_ref, *, n, rb):
    i = pl.program_id(0)
    pos = (i * rb + lax.broadcasted_iota(jnp.int32, (rb, 1), 0)).astype(F32)
    t = pos / float(max(n - 1, 1))
    ang = ((2.0 * math.pi / n) * pos) * bands_ref[...]
    fr = fr_ref[...]
    h = t * w1t_ref[...] + _mm(jnp.cos(ang), w1c_ref[...], passes=3) + _mm(-jnp.sin(ang), w1s_ref[...], passes=3)
    h = jnp.sin(fr * (h + b1_ref[...]))
    h = jnp.sin(fr * (_mm(h, w2_ref[...], passes=3) + b2_ref[...]))
    h = jnp.sin(fr * (_mm(h, w3_ref[...], passes=3) + b3_ref[...]))
    filt = _mm(h, wo_ref[...], passes=3)
    dist = jnp.abs(pos - float(n // 2)) * (2.0 / n)
    filt = filt * jnp.exp(-dist * dl_ref[...])
    f_ref[...] = filt

    @pl.when(i == 0)
    def _():
        asum_ref[...] = jnp.zeros_like(asum_ref)

    asum_ref[...] += jnp.sum(jnp.abs(filt), axis=0, keepdims=True)


def _hfilt_call(n, fp):
    rb = min(n, 512)
    names = ("bands", "w1t", "w1c", "w1s", "b1", "w2", "b2", "w3", "b3", "wout", "freq", "deltas")
    args = [fp[nm] for nm in names]
    return pl.pallas_call(
        functools.partial(_hfilt_kernel, n=n, rb=rb),
        out_shape=(jax.ShapeDtypeStruct((n, HYENA_WIDTH), F32), jax.ShapeDtypeStruct((1, HYENA_WIDTH), F32)),
        grid=(n // rb,),
        in_specs=[_full(a.shape) for a in args],
        out_specs=(pl.BlockSpec((rb, HYENA_WIDTH), lambda i: (i, 0)), _full((1, HYENA_WIDTH))),
        compiler_params=_cparams(("arbitrary",)),
        name="hyena_filter",
    )(*args)


def _hconv3_kernel(p_ref, pv_ref, nx_ref, cw_ref, cb_ref, z_ref, x0_ref, *, nblk):
    i = pl.program_id(0)
    p = p_ref[...]
    tb = p.shape[0]
    row = lax.broadcasted_iota(jnp.int32, (tb, 1), 0)
    prev_row = jnp.where(i == 0, 0.0, pv_ref[7:8, :])
    next_row = jnp.where(i == nblk - 1, 0.0, nx_ref[0:1, :])
    sp = jnp.where(row == 0, prev_row, pltpu.roll(p, 1, 0))
    sn = jnp.where(row == tb - 1, next_row, pltpu.roll(p, tb - 1, 0))
    u = sp * cw_ref[0:1, :] + p * cw_ref[1:2, :] + sn * cw_ref[2:3, :] + cb_ref[...]
    hw = HYENA_WIDTH
    z_ref[...] = u[:, 2 * hw:] * u[:, hw:2 * hw]
    x0_ref[...] = u[:, :hw]


def _hconv3_call(p_h, cw, cb):
    L, w = p_h.shape
    tb = min(L, 256)
    nblk = L // tb
    per = tb // 8
    row = lambda i: (i, 0)
    return pl.pallas_call(
        functools.partial(_hconv3_kernel, nblk=nblk),
        out_shape=(jax.ShapeDtypeStruct((L, HYENA_WIDTH), F32), jax.ShapeDtypeStruct((L, HYENA_WIDTH), F32)),
        grid=(nblk,),
        in_specs=[pl.BlockSpec((tb, w), row),
                  pl.BlockSpec((8, w), lambda i: (jnp.maximum(i * per - 1, 0), 0)),
                  pl.BlockSpec((8, w), lambda i: (jnp.minimum((i + 1) * per, L // 8 - 1), 0)),
                  _full(cw.shape), _full(cb.shape)],
        out_specs=(pl.BlockSpec((tb, HYENA_WIDTH), row), pl.BlockSpec((tb, HYENA_WIDTH), row)),
        compiler_params=_cparams(("arbitrary",)),
        name="hyena_conv3",
    )(p_h, p_h, p_h, cw, cb)


P_DFT = 1


DFT_NB = 8


def _dft1_kernel(m_ref, x_ref, o_ref):
    m = m_ref[...]
    n1 = o_ref.shape[1]
    for j in range(x_ref.shape[1]):
        r = _mm(m, x_ref[:, j, :], _NN, P_DFT)
        o_ref[0, :, j, :] = r[:n1]
        o_ref[1, :, j, :] = r[n1:]


def _dft1_call(m1, x3):
    rows, kdim = m1.shape
    _, n2, c = x3.shape
    n1 = rows // 2
    nbk = min(n2, DFT_NB)
    return pl.pallas_call(
        _dft1_kernel,
        out_shape=jax.ShapeDtypeStruct((2, n1, n2, c), F32),
        grid=(n2 // nbk,),
        in_specs=[_full(m1.shape), pl.BlockSpec((kdim, nbk, c), lambda j: (0, j, 0))],
        out_specs=pl.BlockSpec((2, n1, nbk, c), lambda j: (0, 0, j, 0)),
        compiler_params=_cparams(("arbitrary",)),
        name="dft_stage1",
    )(m1, x3)


def _dft2_matrix(fc_ref, fs_ref, tc_ref, ts_ref):
    fc, fs = fc_ref[...], fs_ref[...]
    tc, ts = tc_ref[...], ts_ref[...]
    fre = fc * tc - fs * ts
    fim = -(fc * ts + fs * tc)
    return jnp.concatenate([jnp.concatenate([fre, -fim], axis=1),
                            jnp.concatenate([fim, fre], axis=1)], axis=0)


def _dft2_conv_kernel(fc_ref, fs_ref, tc_ref, ts_ref, a_ref, af_ref, b_ref):
    big = _dft2_matrix(fc_ref, fs_ref, tc_ref, ts_ref)
    n2, c = a_ref.shape[1], a_ref.shape[2]
    rhs = jnp.concatenate([jnp.concatenate([a_ref[0], af_ref[0]], axis=1),
                           jnp.concatenate([a_ref[1], af_ref[1]], axis=1)], axis=0)
    xh = _mm(big, rhs, _NN, P_DFT)
    xr, hr = xh[:n2, :c], xh[:n2, c:]
    xi, hi = xh[n2:, :c], xh[n2:, c:]
    y = jnp.concatenate([xr * hr - xi * hi, xr * hi + xi * hr], axis=0)
    bb = _mm(big, y, _TN, P_DFT)
    b_ref[0] = bb[:n2]
    b_ref[1] = bb[n2:]


def _dft2_call(consts, a4, af4):
    _, n1, n2, c = a4.shape
    blk = pl.BlockSpec((2, None, n2, c), lambda q: (0, q, 0, 0))
    tw = pl.BlockSpec((None, 1, n2), lambda q: (q, 0, 0))
    return pl.pallas_call(
        _dft2_conv_kernel,
        out_shape=jax.ShapeDtypeStruct(a4.shape, F32),
        grid=(n1,),
        in_specs=[_full((n2, n2)), _full((n2, n2)), tw, tw, blk, blk],
        out_specs=blk,
        compiler_params=_cparams(("arbitrary",)),
        name="dft_stage2_conv",
    )(consts["fc"], consts["fs"], consts["twc"], consts["tws"], a4, af4)


def _dft3_kernel(m_ref, b_ref, z_ref, x0_ref, bias_ref, asum_ref, o_ref):
    m = m_ref[...]
    for j in range(z_ref.shape[1]):
        bj = jnp.concatenate([b_ref[0, :, j, :], b_ref[1, :, j, :]], axis=0)
        y = _mm(m, bj, _NN, P_DFT)
        o_ref[:, j, :] = (y / asum_ref[...] + z_ref[:, j, :] * bias_ref[...]) * x0_ref[:, j, :]


def _dft3_call(m3, b4, z3, x03, bias, asum):
    rows, _ = m3.shape
    _, n1, n2, c = b4.shape
    nbk = min(n2, DFT_NB)
    slab = pl.BlockSpec((rows, nbk, c), lambda j: (0, j, 0))
    return pl.pallas_call(
        _dft3_kernel,
        out_shape=jax.ShapeDtypeStruct((rows, n2, c), F32),
        grid=(n2 // nbk,),
        in_specs=[_full(m3.shape), pl.BlockSpec((2, n1, nbk, c), lambda j: (0, 0, j, 0)), slab, slab,
                  _full((1, c)), _full((1, c))],
        out_specs=slab,
        compiler_params=_cparams(("arbitrary",)),
        name="dft_stage3",
    )(m3, b4, z3, x03, bias, asum)


def _conv_direct_kernel(m1_ref, m3_ref, z_ref, f_ref, x0_ref, bias_ref, asum_ref, o_ref):
    m1 = m1_ref[...]
    z = z_ref[...]
    a = _mm(m1, z, _NN, P_DFT)
    h = _mm(m1, f_ref[...], _NN, P_DFT)
    nn = a.shape[0] // 2
    ar, ai, hr, hi = a[:nn], a[nn:], h[:nn], h[nn:]
    y = jnp.concatenate([ar * hr - ai * hi, ar * hi + ai * hr], axis=0)
    out = _mm(m3_ref[...], y, _NN, P_DFT)
    o_ref[...] = (out / asum_ref[...] + z * bias_ref[...]) * x0_ref[...]


def _conv_direct_call(m1, m3, z, filt, x0, bias, asum):
    args = (m1, m3, z, filt, x0, bias, asum)
    return pl.pallas_call(
        _conv_direct_kernel,
        out_shape=jax.ShapeDtypeStruct(z.shape, F32),
        grid=(1,),
        in_specs=[_full(a.shape) for a in args],
        out_specs=_full(z.shape),
        compiler_params=_cparams(("arbitrary",)),
        name="long_conv_direct",
    )(*args)


DIRECT_CONV_MAX = 256


@functools.lru_cache(maxsize=None)
def _dft_consts_np(n):
    big_n = 2 * n
    if n <= DIRECT_CONV_MAX:
        n1 = big_n
    else:
        n1 = 256 if n >= 8192 else 64
    n2 = big_n // n1
    k1 = np.arange(n1)[:, None].astype(np.float64)
    j1 = np.arange(n1 // 2)[None, :].astype(np.float64)
    ang1 = 2.0 * np.pi * k1 * j1 / n1
    m1 = np.concatenate([np.cos(ang1), -np.sin(ang1)], axis=0)
    o1 = (n1 // 4 + np.arange(n1 // 2))[:, None].astype(np.float64)
    q1 = np.arange(n1)[None, :].astype(np.float64)
    ang3 = 2.0 * np.pi * o1 * q1 / n1
    m3 = np.concatenate([np.cos(ang3), -np.sin(ang3)], axis=1) / big_n
    k2 = np.arange(n2)[:, None].astype(np.float64)
    j2 = np.arange(n2)[None, :].astype(np.float64)
    ang2 = 2.0 * np.pi * k2 * j2 / n2
    angt = 2.0 * np.pi * np.arange(n1)[:, None].astype(np.float64) * j2 / big_n
    f = lambda a: np.asarray(a, np.float32)
    return dict(n1=n1, n2=n2, m1=f(m1), m3=f(m3), fc=f(np.cos(ang2)), fs=f(np.sin(ang2)),
                twc=f(np.cos(angt))[:, None, :], tws=f(np.sin(angt))[:, None, :])


def _long_conv_call(z, x0, filt, asum, bias):
    n, c = z.shape
    cn = _dft_consts_np(n)
    n1, n2 = cn["n1"], cn["n2"]
    consts = {kk: jnp.asarray(vv) for kk, vv in cn.items() if kk not in ("n1", "n2")}
    if n2 == 1:
        return _conv_direct_call(consts["m1"], consts["m3"], z, filt, x0, bias, asum)
    z3 = z.reshape(n1 // 2, n2, c)
    a_f = _dft1_call(consts["m1"], filt.reshape(n1 // 2, n2, c))
    a_z = _dft1_call(consts["m1"], z3)
    b4 = _dft2_call(consts, a_z, a_f)
    out = _dft3_call(consts["m3"], b4, z3, x0.reshape(n1 // 2, n2, c), bias, asum)
    return out.reshape(n, c)


def _layer_norm(x, g, b):
    mu = jnp.mean(x, axis=-1, keepdims=True)
    xc = x - mu
    var = jnp.mean(xc * xc, axis=-1, keepdims=True)
    return xc * lax.rsqrt(var + LN_EPS) * g + b


def _merge_kernel(yf_ref, yb_ref, bv_ref, lg_ref, ho_ref, pg_ref, x_ref, gx_ref, bx_ref, g2_ref, g_ref,
                  wb_ref, wo_ref, gate_ref, lng_ref, lnb_ref, sh2_ref, sc2_ref, o_ref, hf_ref):
    g = g_ref[...]
    ys = yf_ref[...] + yb_ref[...]
    inv_hd = 1.0 / HEAD_DIM
    mu = _mm_rx(ys, g, 3) * inv_hd
    dd = ys - mu
    var = _mm_rx(dd * dd, g, 3) * inv_hd
    yn = dd * lax.rsqrt(var + GN_EPS) * gx_ref[...] + bx_ref[...]
    gate_r = jnp.dot(_sig(lg_ref[...]).astype(BF), g2_ref[...], preferred_element_type=F32)
    ro = (yn + bv_ref[...]) * gate_r
    br = jnp.dot(ro.astype(BF), wb_ref[0], preferred_element_type=F32)
    bh = jnp.dot(ho_ref[...].astype(BF), wb_ref[1], preferred_element_type=F32)
    sg = _sig(pg_ref[...])
    m = sg[:, :D_MODEL] * br + sg[:, D_MODEL:] * bh
    mix = jnp.dot(m.astype(BF), wo_ref[...], preferred_element_type=F32)
    xn = _layer_norm(ALPHA * x_ref[...] + gate_ref[...] * mix, lng_ref[...], lnb_ref[...])
    o_ref[...] = xn
    hf_ref[...] = xn * (1.0 + sc2_ref[...]) + sh2_ref[...]


def _merge_call(yf, yb, bv, lg, ho, pg, x, mp, l):
    L, d = x.shape
    hw = RWKV_WIDTH
    tb = min(L, 256)
    row = lambda i: (i, 0)
    names = ("lnx_g", "lnx_b", "g2", "G", "w_branch", "w_out", "gate", "ln_g", "ln_b", "sh2", "sc2")
    pargs = [mp[nm] for nm in names]

    def pspec(nm, a):
        if nm in ("g2", "w_branch", "w_out"):
            nd = a.ndim - 1
            return pl.BlockSpec((None,) + a.shape[1:], lambda i: (l,) + (0,) * nd)
        return _full(a.shape)

    return pl.pallas_call(
        _merge_kernel,
        out_shape=(jax.ShapeDtypeStruct((L, d), F32), jax.ShapeDtypeStruct((L, d), F32)),
        grid=(L // tb,),
        in_specs=[pl.BlockSpec((tb, hw), row), pl.BlockSpec((tb, hw), row), pl.BlockSpec((tb, hw), row),
                  pl.BlockSpec((tb, LORA_G), row), pl.BlockSpec((tb, hw), row),
                  pl.BlockSpec((tb, GATE_COLS), row), pl.BlockSpec((tb, d), row)]
                 + [pspec(nm, a) for nm, a in zip(names, pargs)],
        out_specs=(pl.BlockSpec((tb, d), row), pl.BlockSpec((tb, d), row)),
        compiler_params=_cparams(("arbitrary",)),
        name="merge_postnorm",
    )(yf, yb, bv, lg, ho, pg, x, *pargs)


def _router_kernel(hf_ref, wrt_ref, rb_ref, cw_ref, sel_ref):
    lt = _mm(wrt_ref[...], hf_ref[...], _NT, 6)
    rid = lax.broadcasted_iota(jnp.int32, (LANES, 1), 0)
    valid = rid < N_EXPERTS
    lg = jnp.where(valid, lt, -jnp.inf)
    mx = jnp.max(lg, axis=0, keepdims=True)
    ex = jnp.where(valid, jnp.exp(lg - mx), 0.0)
    scores = ex / jnp.sum(ex, axis=0, keepdims=True)
    sel = scores + rb_ref[...]
    s = [sel[e:e + 1, :] for e in range(N_EXPERTS)]
    p = [scores[e:e + 1, :] for e in range(N_EXPERTS)]
    gs = []
    for gi in range(N_GROUPS):
        mem = s[gi * EXPERTS_PER_GROUP:(gi + 1) * EXPERTS_PER_GROUP]
        best = None
        for a in range(EXPERTS_PER_GROUP):
            for b in range(a + 1, EXPERTS_PER_GROUP):
                pair = mem[a] + mem[b]
                best = pair if best is None else jnp.maximum(best, pair)
        gs.append(best)
    bg = jnp.where((gs[0] >= gs[1]) & (gs[0] >= gs[2]) & (gs[0] >= gs[3]), 0,
                   jnp.where((gs[1] >= gs[2]) & (gs[1] >= gs[3]), 1, jnp.where(gs[2] >= gs[3], 2, 3)))
    chosen = []
    for e in range(N_EXPERTS):
        gi = e // EXPERTS_PER_GROUP
        beats = None
        for j in range(gi * EXPERTS_PER_GROUP, (gi + 1) * EXPERTS_PER_GROUP):
            if j == e:
                continue
            cond = (s[j] >= s[e]) if j < e else (s[j] > s[e])
            cnt = jnp.where(cond, 1.0, 0.0)
            beats = cnt if beats is None else beats + cnt
        chosen.append((bg == gi) & (beats < 1.5))
    den = None
    for e in range(N_EXPERTS):
        t = jnp.where(chosen[e], p[e], 0.0)
        den = t if den is None else den + t
    out = jnp.zeros(lt.shape, F32)
    msk = jnp.zeros(lt.shape, F32)
    for e in range(N_EXPERTS):
        hit = (rid == e) & chosen[e]
        out = jnp.where(hit, p[e] / den, out)
        msk = jnp.where(hit, 1.0, msk)
    cw_ref[...] = out.T
    sel_ref[...] = msk.T


def _router_call(hf, wrt_pad, rb_pad):
    L, d = hf.shape
    tb = min(L, 256)
    row = lambda i: (i, 0)
    return pl.pallas_call(
        _router_kernel,
        out_shape=(jax.ShapeDtypeStruct((L, LANES), F32), jax.ShapeDtypeStruct((L, LANES), F32)),
        grid=(L // tb,),
        in_specs=[pl.BlockSpec((tb, d), row), _full(wrt_pad.shape), _full(rb_pad.shape)],
        out_specs=(pl.BlockSpec((tb, LANES), row), pl.BlockSpec((tb, LANES), row)),
        compiler_params=_cparams(("arbitrary",)),
        name="router",
    )(hf, wrt_pad, rb_pad)


MOE_BLK = 256


MOE_RING = 3


def _moe_ffn_kernel(tok_ref, slot_ref, be_ref, hf_hbm, wg_ref, wu_ref, wd_ref, y_hbm,
                    xbuf, ybuf, wbf, gsem, ssem, *, nb):
    b = pl.program_id(0)
    nrow = MOE_BLK

    def row_in(idx, sl, r):
        return pltpu.make_async_copy(hf_hbm.at[pl.ds(idx, 1)], xbuf.at[sl, pl.ds(r, 1)], gsem.at[sl])

    def row_out(idx, sl, r):
        return pltpu.make_async_copy(ybuf.at[sl, pl.ds(r, 1)], y_hbm.at[pl.ds(idx, 1)], ssem.at[sl])

    @pl.when(b == 0)
    def _():
        for blk in range(MOE_RING - 1):
            for r in range(nrow):
                row_in(tok_ref[blk * nrow + r], blk, r).start()
        ybuf[MOE_RING - 1] = jnp.zeros(ybuf.shape[1:], F32)

    @pl.when((b == 0) | (be_ref[b] != be_ref[jnp.maximum(b - 1, 0)]))
    def _():
        wbf[0] = wg_ref[...].astype(BF)
        wbf[1] = wu_ref[...].astype(BF)
        wbf[2] = wd_ref[...].astype(BF)

    def step(s):
        s1, s2 = (s + 1) % MOE_RING, (s + 2) % MOE_RING
        for r in range(nrow):
            row_in(0, s, r).wait()

        @pl.when(b >= 2)
        def _():
            for r in range(nrow):
                row_out(0, s, r).wait()

        nxt = (b + 2) * nrow
        for r in range(nrow):
            row_in(tok_ref[nxt + r], s2, r).start()
        cur = b * nrow
        for r in range(nrow):
            row_out(slot_ref[cur + r], s2, r).start()

        x = xbuf[s].astype(BF)
        hg = jnp.dot(x, wbf[0], preferred_element_type=F32)
        hu = jnp.dot(x, wbf[1], preferred_element_type=F32)
        act = (hg * _sig(hg) * hu).astype(BF)
        ybuf[s] = jnp.dot(act, wbf[2], preferred_element_type=F32)

        @pl.when(b == nb)
        def _():
            for sl in (s1, s2):
                for r in range(nrow):
                    row_in(0, sl, r).wait()
                for r in range(nrow):
                    row_out(0, sl, r).wait()

    ring = lax.rem(b, MOE_RING)
    for s in range(MOE_RING):
        pl.when(ring == s)(functools.partial(step, s))


def _moe_ffn_call(row_tok, row_slot, block_e, hf_all, wg, wu, wd, l):
    d = hf_all.shape[1]
    nb = block_e.shape[0] - 1
    assert nb >= MOE_RING
    wspec = pl.BlockSpec((None, None, d, d), lambda b, tok, slot, be: (l, be[b], 0, 0))
    return pl.pallas_call(
        functools.partial(_moe_ffn_kernel, nb=nb),
        out_shape=jax.ShapeDtypeStruct(((nb + 1) * MOE_BLK, d), F32),
        grid_spec=pltpu.PrefetchScalarGridSpec(
            num_scalar_prefetch=3,
            grid=(nb + 1,),
            in_specs=[pl.BlockSpec(memory_space=pl.ANY), wspec, wspec, wspec],
            out_specs=pl.BlockSpec(memory_space=pl.ANY),
            scratch_shapes=[pltpu.VMEM((MOE_RING, MOE_BLK, d), F32), pltpu.VMEM((MOE_RING, MOE_BLK, d), F32),
                            pltpu.VMEM((3, d, d), BF),
                            pltpu.SemaphoreType.DMA((MOE_RING,)), pltpu.SemaphoreType.DMA((MOE_RING,))]),
        compiler_params=_cparams(("arbitrary",)),
        name="moe_ffn",
    )(row_tok, row_slot, block_e, hf_all, wg, wu, wd)


def _moe_dispatch(sel, cw):
    t_all = sel.shape[0]
    blk = MOE_BLK
    nb = (2 * t_all) // blk + N_EXPERTS
    nr = nb * blk
    mi = (sel > 0.5).astype(jnp.int32)
    rank = jnp.cumsum(mi, axis=0) - mi
    counts = jnp.sum(mi, axis=0)
    padded = ((counts + blk - 1) // blk) * blk
    pend = jnp.cumsum(padded)
    dest = pend - padded + rank
    kk = jnp.cumsum(mi, axis=1) - mi
    first = (mi > 0) & (kk == 0)
    second = (mi > 0) & (kk == 1)
    d0 = jnp.sum(jnp.where(first, dest, 0), axis=1)
    d1 = jnp.sum(jnp.where(second, dest, 0), axis=1)
    w0 = jnp.sum(jnp.where(first, cw, 0.0), axis=1, keepdims=True)
    w1 = jnp.sum(jnp.where(second, cw, 0.0), axis=1, keepdims=True)
    tok2 = 2 * jnp.arange(t_all, dtype=jnp.int32)
    packed = jnp.full((nr,), -1, jnp.int32).at[jnp.concatenate([d0, d1])].set(
        jnp.concatenate([tok2, tok2 + 1]), unique_indices=True)
    is_pad = packed < 0
    row_tok = jnp.where(is_pad, 0, jnp.right_shift(packed, 1))
    blk_start = jnp.arange(nb, dtype=jnp.int32)[:, None] * blk
    block_e = jnp.clip(jnp.sum((blk_start >= pend[None, :]).astype(jnp.int32), axis=1), 0, N_EXPERTS - 1)
    real_before = jnp.repeat(jnp.cumsum(counts)[block_e], blk)
    pad_slot = 2 * t_all + jnp.arange(nr, dtype=jnp.int32) - real_before
    row_slot = jnp.where(is_pad, pad_slot, jnp.bitwise_and(packed, 1) * t_all + row_tok)
    row_tok = jnp.concatenate([row_tok, jnp.zeros((MOE_RING * blk,), jnp.int32)])
    row_slot = jnp.concatenate([nr + jnp.arange(blk, dtype=jnp.int32), row_slot])
    block_e = jnp.concatenate([block_e, block_e[-1:]])
    return row_tok, row_slot, block_e, w0, w1


def _combine_kernel(x_ref, y0_ref, y1_ref, w0_ref, w1_ref, gate_ref, lng_ref, lnb_ref, o_ref):
    y = y0_ref[...] * w0_ref[...] + y1_ref[...] * w1_ref[...]
    o_ref[...] = _layer_norm(ALPHA * x_ref[...] + gate_ref[...] * y, lng_ref[...], lnb_ref[...])


def _combine_call(x, y_slots, w0, w1, row_off, t_all, gate, lng, lnb, tm):
    L, d = x.shape
    off0 = row_off // tm
    off1 = (t_all + row_off) // tm
    vec = _full((1, d))
    return pl.pallas_call(
        _combine_kernel,
        out_shape=jax.ShapeDtypeStruct((L, d), F32),
        grid=(L // tm,),
        in_specs=[pl.BlockSpec((tm, d), lambda i: (i, 0)),
                  pl.BlockSpec((tm, d), lambda i: (off0 + i, 0)),
                  pl.BlockSpec((tm, d), lambda i: (off1 + i, 0)),
                  pl.BlockSpec((tm, 1), lambda i: (off0 + i, 0)),
                  pl.BlockSpec((tm, 1), lambda i: (off0 + i, 0)), vec, vec, vec],
        out_specs=pl.BlockSpec((tm, d), lambda i: (i, 0)),
        compiler_params=_cparams(("arbitrary",)),
        name="moe_combine",
    )(x, y_slots, y_slots, w0, w1, gate, lng, lnb)


def _moe_layer(segs, wrt_pad, rb_pad, lng, lnb, wg, wu, wd, l):
    routed = [_router_call(hf, wrt_pad, rb_pad) for (_, hf, _) in segs]
    cw = jnp.concatenate([r[0][:, :N_EXPERTS] for r in routed], axis=0)
    sel = jnp.concatenate([r[1][:, :N_EXPERTS] for r in routed], axis=0)
    hf_all = jnp.concatenate([hf for (_, hf, _) in segs], axis=0) if len(segs) > 1 else segs[0][1]
    t_all = hf_all.shape[0]
    row_tok, row_slot, block_e, w0, w1 = _moe_dispatch(sel, cw)
    y_slots = _moe_ffn_call(row_tok, row_slot, block_e, hf_all, wg, wu, wd, l)
    tm = min(256, min(x.shape[0] for (x, _, _) in segs))
    outs = []
    off = 0
    for (x, _, gate) in segs:
        assert off % tm == 0 and t_all % tm == 0 and x.shape[0] % tm == 0
        outs.append(_combine_call(x, y_slots, w0, w1, off, t_all, gate, lng, lnb, tm))
        off += x.shape[0]
    return outs


def _blockdiag2(m):
    z = jnp.zeros_like(m[0])
    return jnp.concatenate([jnp.concatenate([m[0], z], axis=1), jnp.concatenate([z, m[1]], axis=1)], axis=0)


def _mixer(x, mod_row, l, w_in_bf, prm, fp, hy, mp, s0, latent):
    L = x.shape[0]
    sh, sc, gate = mod_row[0], mod_row[1], mod_row[2]
    p_r, p_h, p_g = _inproj_call(x, sh, sc, w_in_bf, l)
    rvk, dirp, lg, bv = _prepare_call(p_r, prm, latent)
    yf, yb, sfin = _scan_call(rvk, dirp, s0)
    filt, asum = _hfilt_call(L, fp)
    z, x0 = _hconv3_call(p_h, hy["conv_w"], hy["conv_b"])
    ho = _long_conv_call(z, x0, filt, asum, hy["bias"])
    mpl = dict(mp)
    mpl["gate"], mpl["sh2"], mpl["sc2"] = gate, mod_row[3], mod_row[4]
    xn, hf = _merge_call(yf, yb, bv, lg, ho, p_g, x, mpl, l)
    return xn, hf, sfin


def kernel(x, c, ctx, c_ctx, w_mod, b_mod, w_in, rwkv_mu, rwkv_w0, rwkv_w2, rwkv_a0, rwkv_a2, rwkv_g2,
           rwkv_k_k, rwkv_k_a, rwkv_r_k, rwkv_lnx_g, rwkv_lnx_b, hy_conv_w, hy_conv_b, hy_f_w1, hy_f_b1,
           hy_f_w2, hy_f_b2, hy_f_w3, hy_f_b3, hy_f_wout, hy_freq, hy_bias, w_branch, w_out, ln_g, ln_b,
           w_router, router_bias, w_gate, w_up, w_down):
    b, n_lat, d = x.shape
    assert b == 1 and d == D_MODEL
    n_ctx = ctx.shape[1]
    depth = w_mod.shape[0]
    hw = RWKV_WIDTH
    xl = x[0]
    xc = ctx[0]

    cc = jnp.concatenate([c[:1], c_ctx[None, :], jnp.zeros((6, d), F32)], axis=0)
    mod = _mod_call(cc, w_mod, b_mod)

    head_of = np.arange(hw) // HEAD_DIM
    G = jnp.asarray((head_of[:, None] == head_of[None, :]).astype(np.float32), dtype=BF)
    bands = jnp.linspace(1e-4, FILTER_BANDS - 1, FILTER_BANDS, dtype=F32)[None, :]
    deltas = jnp.abs(jnp.linspace(HYENA_MIN_DECAY, HYENA_MAX_DECAY, HYENA_WIDTH, dtype=F32))[None, :]
    wr_pad = jnp.pad(w_router.T, ((0, LANES - N_EXPERTS), (0, 0)))
    rb_pad = jnp.pad(router_bias, (0, LANES - N_EXPERTS))[:, None]
    w_in_bf = w_in.astype(BF)
    w_branch_bf = w_branch.astype(BF)
    w_out_bf = w_out.astype(BF)
    g2_bf = rwkv_g2.astype(BF)

    for l in range(depth):
        last = l == depth - 1
        ml = [mod[l, 0:1, j * d:(j + 1) * d] for j in range(6)]
        mc = [mod[l, 1:2, j * d:(j + 1) * d] for j in range(6)]
        prm = dict(mu=rwkv_mu[l][None, :],
                   w0=rwkv_w0[l].reshape(1, 2 * hw), w2=_blockdiag2(rwkv_w2[l]),
                   a0=rwkv_a0[l].reshape(1, 2 * hw), a2=_blockdiag2(rwkv_a2[l]),
                   k_k=rwkv_k_k[l][None, :], k_a=rwkv_k_a[l][None, :], r_k=rwkv_r_k[l][None, :], G=G)
        w1 = hy_f_w1[l]
        fp = dict(bands=bands, w1t=w1[0:1], w1c=w1[1:1 + FILTER_BANDS], w1s=w1[1 + FILTER_BANDS:],
                  b1=hy_f_b1[l][None, :], w2=hy_f_w2[l], b2=hy_f_b2[l][None, :], w3=hy_f_w3[l],
                  b3=hy_f_b3[l][None, :], wout=hy_f_wout[l], freq=hy_freq[l][None, :], deltas=deltas)
        hy = dict(conv_w=hy_conv_w[l], conv_b=hy_conv_b[l][None, :], bias=hy_bias[l][None, :])
        mp = dict(lnx_g=rwkv_lnx_g[l][None, :], lnx_b=rwkv_lnx_b[l][None, :], g2=g2_bf, G=G,
                  w_branch=w_branch_bf, w_out=w_out_bf, ln_g=ln_g[l, 0][None, :], ln_b=ln_b[l, 0][None, :])
        s0 = jnp.zeros((2, HEAD_DIM, hw), F32)
        xc_new, hf_c, s_ctx = _mixer(xc, mc, l, w_in_bf, prm, fp, hy, mp, s0, latent=False)
        xl, hf_l, _ = _mixer(xl, ml, l, w_in_bf, prm, fp, hy, mp, s_ctx, latent=True)
        lng, lnb = ln_g[l, 1][None, :], ln_b[l, 1][None, :]
        if last:
            (xl,) = _moe_layer([(xl, hf_l, ml[5])], wr_pad, rb_pad, lng, lnb, w_gate, w_up, w_down, l)
        else:
            xc, xl = _moe_layer([(xc_new, hf_c, mc[5]), (xl, hf_l, ml[5])], wr_pad, rb_pad, lng, lnb,
                                w_gate, w_up, w_down, l)
    return xl[None]
```

```python
import functools
import math

import numpy as np
import jax
import jax.numpy as jnp
from jax import lax
from jax.experimental import pallas as pl
from jax.experimental.pallas import tpu as pltpu

F32 = jnp.float32
BF = jnp.bfloat16

D_MODEL = 1024
DEPTH = 4
GRID_W = 64
RWKV_WIDTH = 512
HEAD_DIM = 64
RWKV_HEADS = 8
LORA_W = 64
LORA_A = 64
LORA_G = 128
DECAY_SCALE = 0.606531
GN_EPS = 6.4e-4
RWKV_COLS = 3 * RWKV_WIDTH + 2 * LORA_W + 2 * LORA_A + LORA_G
HYENA_WIDTH = 512
HYENA_COLS = 3 * HYENA_WIDTH
FILTER_BANDS = 16
FILTER_HIDDEN = 64
HYENA_MIN_DECAY = math.log(1e-2) / 1.5
HYENA_MAX_DECAY = math.log(1e-2) / 0.3
GATE_COLS = 2 * D_MODEL
PROJ_COLS = RWKV_COLS + HYENA_COLS + GATE_COLS
N_EXPERTS = 16
N_GROUPS = 4
EXPERTS_PER_GROUP = 4
ALPHA = (2 * DEPTH) ** 0.25
LN_EPS = 1e-5

SCAN_CHUNK = 64
LANES = 128
VMEM_LIMIT = 56 * 1024 * 1024

_NN = (((1,), (0,)), ((), ()))
_NT = (((1,), (1,)), ((), ()))
_TN = (((0,), (0,)), ((), ()))


def _sig(x):
    return 1.0 / (1.0 + jnp.exp(-x))


def _parts(a, n):
    out = []
    rem = a
    for i in range(n):
        hi = rem.astype(BF)
        out.append(hi)
        if i + 1 < n:
            rem = rem - hi.astype(F32)
    return out


def _mm(a, b, dn=_NN, passes=1):
    n = {1: 1, 3: 2, 6: 3}[passes]
    pa = _parts(a, n)
    pb = _parts(b, n)
    acc = None
    for i in range(n):
        for j in range(n - i):
            t = lax.dot_general(pa[i], pb[j], dn, preferred_element_type=F32)
            acc = t if acc is None else acc + t
    return acc


def _mm_rx(a, b_exact, n=3, dn=_NN):
    acc = None
    for p in _parts(a, n):
        t = lax.dot_general(p, b_exact, dn, preferred_element_type=F32)
        acc = t if acc is None else acc + t
    return acc


def _mm_lx(a_exact, b, n=3, dn=_NN):
    acc = None
    for p in _parts(b, n):
        t = lax.dot_general(a_exact, p, dn, preferred_element_type=F32)
        acc = t if acc is None else acc + t
    return acc


def _cparams(sem, vmem=VMEM_LIMIT):
    return pltpu.CompilerParams(dimension_semantics=sem, vmem_limit_bytes=vmem)


def _full(shape):
    nd = len(shape)
    return pl.BlockSpec(shape, lambda *_: (0,) * nd)


def _mod_kernel(c_ref, w_ref, b_ref, o_ref):
    cc = c_ref[...]
    s = cc * _sig(cc)
    o_ref[...] = jnp.dot(s.astype(BF), w_ref[...].astype(BF), preferred_element_type=F32) + b_ref[...]


def _mod_call(cc, w_mod, b_mod):
    depth, d, n6 = w_mod.shape
    tn = 1536
    return pl.pallas_call(
        _mod_kernel,
        out_shape=jax.ShapeDtypeStruct((depth, 8, n6), F32),
        grid=(depth, n6 // tn),
        in_specs=[
            pl.BlockSpec((8, d), lambda l, j: (0, 0)),
            pl.BlockSpec((None, d, tn), lambda l, j: (l, 0, j)),
            pl.BlockSpec((None, 1, tn), lambda l, j: (l, 0, j)),
        ],
        out_specs=pl.BlockSpec((None, 8, tn), lambda l, j: (l, 0, j)),
        compiler_params=_cparams(("arbitrary", "arbitrary")),
        name="mod",
    )(cc, w_mod, b_mod.reshape(depth, 1, n6))


def _inproj_kernel(x_ref, sh_ref, sc_ref, w_ref, pr_ref, ph_ref, pg_ref):
    xm = (x_ref[...] * (1.0 + sc_ref[...]) + sh_ref[...]).astype(BF)
    pr_ref[...] = jnp.dot(xm, w_ref[:, :RWKV_COLS], preferred_element_type=F32)
    ph_ref[...] = jnp.dot(xm, w_ref[:, RWKV_COLS:RWKV_COLS + HYENA_COLS], preferred_element_type=F32)
    pg_ref[...] = jnp.dot(xm, w_ref[:, RWKV_COLS + HYENA_COLS:], preferred_element_type=F32)


def _inproj_call(x, sh, sc, w_bf, l):
    L, d = x.shape
    tm = min(L, 256)
    row = lambda i: (i, 0)
    return pl.pallas_call(
        _inproj_kernel,
        out_shape=(jax.ShapeDtypeStruct((L, RWKV_COLS), F32),
                   jax.ShapeDtypeStruct((L, HYENA_COLS), F32),
                   jax.ShapeDtypeStruct((L, GATE_COLS), F32)),
        grid=(L // tm,),
        in_specs=[pl.BlockSpec((tm, d), row), _full((1, d)), _full((1, d)),
                  pl.BlockSpec((None, d, PROJ_COLS), lambda i: (l, 0, 0))],
        out_specs=(pl.BlockSpec((tm, RWKV_COLS), row), pl.BlockSpec((tm, HYENA_COLS), row),
                   pl.BlockSpec((tm, GATE_COLS), row)),
        compiler_params=_cparams(("arbitrary",)),
        name="inproj",
    )(x, sh, sc, w_bf)


def _prepare_kernel(*refs, latent, nblk):
    if latent:
        p_ref, up_ref, dn_ref = refs[:3]
        refs = refs[3:]
    else:
        p_ref = refs[0]
        refs = refs[1:]
    (mu_ref, w0_ref, w2_ref, a0_ref, a2_ref, kk_ref, ka_ref, rk_ref, g_ref,
     rvk_out, dir_out, lg_out, bv_out) = refs
    i = pl.program_id(0)
    p = p_ref[...]
    tb, w = p.shape
    row = lax.broadcasted_iota(jnp.int32, (tb, 1), 0)
    col = lax.broadcasted_iota(jnp.int32, (1, w), 1)
    if latent:
        gw = jnp.bitwise_and(row, GRID_W - 1)
        left = jnp.where(gw == 0, 0.0, pltpu.roll(p, 1, 0))
        right = jnp.where(gw == GRID_W - 1, 0.0, pltpu.roll(p, tb - 1, 0))
        upv = jnp.where(i == 0, 0.0, up_ref[...])
        dnv = jnp.where(i == nblk - 1, 0.0, dn_ref[...])
        if tb > GRID_W:
            up = jnp.concatenate([upv, p[:tb - GRID_W]], axis=0)
            down = jnp.concatenate([p[GRID_W:], dnv], axis=0)
        else:
            up, down = upv, dnv
        q = w // 4
        sh = jnp.where(col < q, left, jnp.where(col < 2 * q, right, jnp.where(col < 3 * q, up, down)))
    else:
        prev = jnp.where(row == 0, 0.0, pltpu.roll(p, 1, 0))
        nxt = jnp.where(row == tb - 1, 0.0, pltpu.roll(p, tb - 1, 0))
        sh = jnp.where(col < w // 2, prev, nxt)
    pm = p + (sh - p) * mu_ref[...]
    hw = RWKV_WIDTH
    r = pm[:, :hw]
    k = pm[:, hw:2 * hw]
    v = pm[:, 2 * hw:3 * hw]
    lw_in = jnp.tanh(pm[:, 3 * hw:3 * hw + 2 * LORA_W])
    la_in = pm[:, 3 * hw + 2 * LORA_W:3 * hw + 2 * LORA_W + 2 * LORA_A]
    lg = pm[:, 3 * hw + 2 * LORA_W + 2 * LORA_A:]
    lw = -DECAY_SCALE * _sig(w0_ref[...] + _mm(lw_in, w2_ref[...], passes=3))
    a = _sig(a0_ref[...] + _mm(la_in, a2_ref[...], passes=3))
    g = g_ref[...]
    kkr = k * kk_ref[...]
    nrm = jnp.sqrt(_mm_rx(kkr * kkr, g, 3))
    kk = kkr / jnp.maximum(nrm, 1e-12)
    k_a = ka_ref[...]
    k_f = k * (1.0 + (a[:, :hw] - 1.0) * k_a)
    k_b = k * (1.0 + (a[:, hw:] - 1.0) * k_a)
    bonus = _mm_rx(r * (k_f + k_b) * rk_ref[...], g, 3)
    rvk_out[:, :hw] = r
    rvk_out[:, hw:2 * hw] = v
    rvk_out[:, 2 * hw:] = -kk
    for d, (k_d, a_d) in enumerate(((k_f, a[:, :hw]), (k_b, a[:, hw:]))):
        dir_out[d, :, :hw] = lw[:, d * hw:(d + 1) * hw]
        dir_out[d, :, hw:2 * hw] = k_d
        dir_out[d, :, 2 * hw:] = kk * a_d
    lg_out[...] = lg
    bv_out[...] = bonus * v


def _prepare_call(p_r, prm, latent):
    L, w = p_r.shape
    hw = RWKV_WIDTH
    tb = 256 if latent else L
    nblk = L // tb
    row = lambda i: (i, 0)
    row3 = lambda i: (0, i, 0)
    in_specs = [pl.BlockSpec((tb, w), row)]
    args = [p_r]
    if latent:
        per = tb // GRID_W
        nrow = L // GRID_W
        in_specs += [
            pl.BlockSpec((GRID_W, w), lambda i: (jnp.maximum(i * per - 1, 0), 0)),
            pl.BlockSpec((GRID_W, w), lambda i: (jnp.minimum((i + 1) * per, nrow - 1), 0)),
        ]
        args += [p_r, p_r]
    names = ("mu", "w0", "w2", "a0", "a2", "k_k", "k_a", "r_k", "G")
    for nm in names:
        in_specs.append(_full(prm[nm].shape))
        args.append(prm[nm])
    sd = jax.ShapeDtypeStruct
    out_shape = (sd((L, 3 * hw), F32), sd((2, L, 3 * hw), F32), sd((L, LORA_G), F32), sd((L, hw), F32))
    out_specs = (pl.BlockSpec((tb, 3 * hw), row), pl.BlockSpec((2, tb, 3 * hw), row3),
                 pl.BlockSpec((tb, LORA_G), row), pl.BlockSpec((tb, hw), row))
    return pl.pallas_call(
        functools.partial(_prepare_kernel, latent=latent, nblk=nblk),
        out_shape=out_shape, grid=(nblk,), in_specs=in_specs, out_specs=out_specs,
        compiler_params=_cparams(("arbitrary",)),
        name="rwkv_prepare_lat" if latent else "rwkv_prepare_ctx",
    )(*args)


P_SCORE = 1
P_TINV = 1
P_APPLY = 1
P_STATE = 1


SCAN_SUB = 2


def _scan_chunk(fin, bin_, states, masks):
    C = SCAN_CHUNK
    hd = HEAD_DIM
    nh = RWKV_HEADS
    row, col, eye, lvl_masks = masks
    dirs = []
    for d, (r, v, ka, lw, k, kb) in enumerate((fin, bin_)):
        inc = (row >= col) if d == 0 else (row <= col)
        strict = (row > col) if d == 0 else (row < col)
        lc = _mm_lx(jnp.where(inc, 1.0, 0.0).astype(BF), lw, 3)
        e_neg = jnp.exp(-lc)
        e_tot = jnp.exp(jnp.sum(lw, axis=0, keepdims=True))
        kbn = kb * e_neg
        kkn = k * e_neg
        dirs.append(dict(inc=inc, strict=strict, v=v, aq=ka * jnp.exp(lc - lw), rq=r * jnp.exp(lc),
                         kbn=kbn, kkn=kkn, kbp=kbn * e_tot, kkp=kkn * e_tot, e_tot=e_tot, s=states[d]))

    units = [(d, h) for h in range(nh) for d in range(2)]
    hs = lambda arr, h: arr[:, h * hd:(h + 1) * hd]
    sc = {}
    for (d, h) in units:
        D = dirs[d]
        sc[d, h] = _mm(jnp.concatenate([hs(D["aq"], h), hs(D["rq"], h)], axis=0),
                       jnp.concatenate([hs(D["kbn"], h), hs(D["kkn"], h)], axis=0), _NT, P_SCORE)
    a_ab, t, x = {}, {}, {}
    for u in units:
        a_ab[u] = jnp.where(dirs[u[0]]["strict"], sc[u][:C, :C], 0.0)
        t[u] = eye + jnp.where(lvl_masks[0], a_ab[u], 0.0)
    for u in units:
        D = dirs[u[0]]
        lhs = jnp.concatenate([jnp.where(D["strict"], sc[u][:C, C:], 0.0),
                               jnp.where(D["inc"], sc[u][C:, C:], 0.0)], axis=0)
        x[u] = _mm(lhs, hs(D["v"], u[1]), _NN, P_APPLY)
    for m in lvl_masks[1:]:
        tmp = {u: _mm(jnp.where(m, a_ab[u], 0.0), t[u], _NN, P_TINV) for u in units}
        for u in units:
            t[u] = t[u] + _mm(t[u], tmp[u], _NN, P_TINV)
    z, gm, uu, yy, sn = {}, {}, {}, {}, {}
    for u in units:
        z[u] = _mm(t[u], jnp.concatenate([hs(dirs[u[0]]["aq"], u[1]), x[u][:C]], axis=1), _NN, P_APPLY)
    for u in units:
        D = dirs[u[0]]
        gm[u] = _mm(jnp.concatenate([z[u][:, :hd], hs(D["rq"], u[1])], axis=0), hs(D["s"], u[1]), _NT, P_STATE)
        uu[u] = gm[u][:C] + z[u][:, hd:]
    for u in units:
        D = dirs[u[0]]
        b_rb = jnp.where(D["inc"], sc[u][C:, :C], 0.0)
        yy[u] = gm[u][C:] + _mm(b_rb, uu[u], _NN, P_APPLY) + x[u][C:]
        sn[u] = hs(D["s"], u[1]) * hs(D["e_tot"], u[1]) + _mm(
            jnp.concatenate([uu[u], hs(D["v"], u[1])], axis=0),
            jnp.concatenate([hs(D["kbp"], u[1]), hs(D["kkp"], u[1])], axis=0), _TN, P_STATE)
    cat = lambda d, src: jnp.concatenate([src[d, h] for h in range(nh)], axis=1)
    return cat(0, yy), cat(1, yy), [cat(0, sn), cat(1, sn)]


def _scan_kernel(sf_ref, sb_ref, df_ref, db_ref, s0_ref, yf_ref, yb_ref, sfin_ref, s_scr, *, nstep):
    c = pl.program_id(0)
    C = SCAN_CHUNK

    @pl.when(c == 0)
    def _():
        s_scr[...] = s0_ref[...]

    row = lax.broadcasted_iota(jnp.int32, (C, C), 0)
    col = lax.broadcasted_iota(jnp.int32, (C, C), 1)
    eye = jnp.where(row == col, 1.0, 0.0)
    lvl_masks = []
    for sh in range(int(math.log2(C))):
        same2 = jnp.right_shift(row, sh + 1) == jnp.right_shift(col, sh + 1)
        same1 = jnp.right_shift(row, sh) == jnp.right_shift(col, sh)
        lvl_masks.append(jnp.logical_and(same2, jnp.logical_not(same1)))
    masks = (row, col, eye, lvl_masks)

    hw = RWKV_WIDTH
    unpack = lambda sref, dref, rows: tuple(ref[rows, i * hw:(i + 1) * hw] for ref in (sref, dref) for i in range(3))
    states = [s_scr[0], s_scr[1]]
    for j in range(SCAN_SUB):
        fsl = slice(j * C, (j + 1) * C)
        bsl = slice((SCAN_SUB - 1 - j) * C, (SCAN_SUB - j) * C)
        yf, yb, states = _scan_chunk(unpack(sf_ref, df_ref, fsl), unpack(sb_ref, db_ref, bsl), states, masks)
        yf_ref[fsl, :] = yf
        yb_ref[bsl, :] = yb
    s_scr[0] = states[0]
    s_scr[1] = states[1]

    @pl.when(c == nstep - 1)
    def _():
        sfin_ref[...] = s_scr[...]


def _scan_call(rvk, dirp, s0):
    L = rvk.shape[0]
    hw = RWKV_WIDTH
    C = SCAN_CHUNK * SCAN_SUB
    nchunk = L // C
    assert L % C == 0
    sh_f = pl.BlockSpec((C, 3 * hw), lambda c: (c, 0))
    sh_b = pl.BlockSpec((C, 3 * hw), lambda c: (nchunk - 1 - c, 0))
    pd_f = pl.BlockSpec((None, C, 3 * hw), lambda c: (0, c, 0))
    pd_b = pl.BlockSpec((None, C, 3 * hw), lambda c: (1, nchunk - 1 - c, 0))
    y_f = pl.BlockSpec((C, hw), lambda c: (c, 0))
    y_b = pl.BlockSpec((C, hw), lambda c: (nchunk - 1 - c, 0))
    state = _full((2, HEAD_DIM, hw))
    return pl.pallas_call(
        functools.partial(_scan_kernel, nstep=nchunk),
        out_shape=(jax.ShapeDtypeStruct((L, hw), F32), jax.ShapeDtypeStruct((L, hw), F32),
                   jax.ShapeDtypeStruct((2, HEAD_DIM, hw), F32)),
        grid=(nchunk,),
        in_specs=[sh_f, sh_b, pd_f, pd_b, state],
        out_specs=(y_f, y_b, state),
        scratch_shapes=[pltpu.VMEM((2, HEAD_DIM, hw), F32)],
        compiler_params=_cparams(("arbitrary",)),
        name="delta_scan",
    )(rvk, rvk, dirp, dirp, s0)


def _hfilt_kernel(bands_ref, w1t_ref, w1c_ref, w1s_ref, b1_ref, w2_ref, b2_ref, w3_ref, b3_ref, wo_ref,
                  fr_ref, dl_ref, f_ref, asum_ref, *, n, rb):
    i = pl.program_id(0)
    posr = (i * rb + lax.broadcasted_iota(jnp.int32, (1, rb), 1)).astype(F32)
    tr = posr / float(max(n - 1, 1))
    ang = ((2.0 * math.pi / n) * posr) * bands_ref[...]
    fr = fr_ref[...]
    h = (w1t_ref[...] * tr + _mm(w1c_ref[...], jnp.cos(ang), passes=3)
         + _mm(w1s_ref[...], -jnp.sin(ang), passes=3))
    h = jnp.sin(fr * (h + b1_ref[...]))
    h = jnp.sin(fr * (_mm(w2_ref[...], h, passes=3) + b2_ref[...]))
    h = jnp.sin(fr * (_mm(w3_ref[...], h, passes=3) + b3_ref[...]))
    filt = _mm(h, wo_ref[...], _TN, 3)
    pos = (i * rb + lax.broadcasted_iota(jnp.int32, (rb, 1), 0)).astype(F32)
    dist = jnp.abs(pos - float(n // 2)) * (2.0 / n)
    filt = filt * jnp.exp(-dist * dl_ref[...])
    f_ref[...] = filt

    @pl.when(i == 0)
    def _():
        asum_ref[...] = jnp.zeros_like(asum_ref)

    asum_ref[...] += jnp.sum(jnp.abs(filt), axis=0, keepdims=True)


def _hfilt_call(n, fp):
    rb = min(n, 512)
    names = ("bands", "w1t", "w1c", "w1s", "b1", "w2", "b2", "w3", "b3", "wout", "freq", "deltas")
    args = [fp[nm] for nm in names]
    return pl.pallas_call(
        functools.partial(_hfilt_kernel, n=n, rb=rb),
        out_shape=(jax.ShapeDtypeStruct((n, HYENA_WIDTH), F32), jax.ShapeDtypeStruct((1, HYENA_WIDTH), F32)),
        grid=(n // rb,),
        in_specs=[_full(a.shape) for a in args],
        out_specs=(pl.BlockSpec((rb, HYENA_WIDTH), lambda i: (i, 0)), _full((1, HYENA_WIDTH))),
        compiler_params=_cparams(("arbitrary",)),
        name="hyena_filter",
    )(*args)


def _hconv3_kernel(p_ref, pv_ref, nx_ref, cw_ref, cb_ref, z_ref, x0_ref, *, nblk):
    i = pl.program_id(0)
    p = p_ref[...]
    tb = p.shape[0]
    row = lax.broadcasted_iota(jnp.int32, (tb, 1), 0)
    prev_row = jnp.where(i == 0, 0.0, pv_ref[7:8, :])
    next_row = jnp.where(i == nblk - 1, 0.0, nx_ref[0:1, :])
    sp = jnp.where(row == 0, prev_row, pltpu.roll(p, 1, 0))
    sn = jnp.where(row == tb - 1, next_row, pltpu.roll(p, tb - 1, 0))
    u = sp * cw_ref[0:1, :] + p * cw_ref[1:2, :] + sn * cw_ref[2:3, :] + cb_ref[...]
    hw = HYENA_WIDTH
    z_ref[...] = u[:, 2 * hw:] * u[:, hw:2 * hw]
    x0_ref[...] = u[:, :hw]


def _hconv3_call(p_h, cw, cb):
    L, w = p_h.shape
    tb = min(L, 256)
    nblk = L // tb
    per = tb // 8
    row = lambda i: (i, 0)
    return pl.pallas_call(
        functools.partial(_hconv3_kernel, nblk=nblk),
        out_shape=(jax.ShapeDtypeStruct((L, HYENA_WIDTH), F32), jax.ShapeDtypeStruct((L, HYENA_WIDTH), F32)),
        grid=(nblk,),
        in_specs=[pl.BlockSpec((tb, w), row),
                  pl.BlockSpec((8, w), lambda i: (jnp.maximum(i * per - 1, 0), 0)),
                  pl.BlockSpec((8, w), lambda i: (jnp.minimum((i + 1) * per, L // 8 - 1), 0)),
                  _full(cw.shape), _full(cb.shape)],
        out_specs=(pl.BlockSpec((tb, HYENA_WIDTH), row), pl.BlockSpec((tb, HYENA_WIDTH), row)),
        compiler_params=_cparams(("arbitrary",)),
        name="hyena_conv3",
    )(p_h, p_h, p_h, cw, cb)


P_DFT = 1


DFT_NB = 8


def _dft1_kernel(m_ref, x_ref, o_ref):
    m = m_ref[...]
    n1 = o_ref.shape[2]
    for j in range(x_ref.shape[1]):
        r = _mm(m, x_ref[:, j, :], _NN, P_DFT)
        o_ref[0, j] = r[:n1]
        o_ref[1, j] = r[n1:]


def _dft1_call(m1, x3):
    rows, kdim = m1.shape
    _, n2, c = x3.shape
    n1 = rows // 2
    nbk = min(n2, DFT_NB)
    return pl.pallas_call(
        _dft1_kernel,
        out_shape=jax.ShapeDtypeStruct((2, n2, n1, c), F32),
        grid=(n2 // nbk,),
        in_specs=[_full(m1.shape), pl.BlockSpec((kdim, nbk, c), lambda j: (0, j, 0))],
        out_specs=pl.BlockSpec((2, nbk, n1, c), lambda j: (0, j, 0, 0)),
        compiler_params=_cparams(("arbitrary",)),
        name="dft_stage1",
    )(m1, x3)


def _dft2_matrix(fc, fs, tc, ts):
    fre = fc * tc - fs * ts
    fim = -(fc * ts + fs * tc)
    return jnp.concatenate([jnp.concatenate([fre, -fim], axis=1),
                            jnp.concatenate([fim, fre], axis=1)], axis=0)


def _dft2_conv_kernel(fc_ref, fs_ref, tc_ref, ts_ref, a_ref, af_ref, b_ref):
    fc, fs = fc_ref[...], fs_ref[...]
    n2, c = a_ref.shape[1], a_ref.shape[3]
    for kk in range(a_ref.shape[2]):
        big = _dft2_matrix(fc, fs, tc_ref[kk], ts_ref[kk])
        rhs = jnp.concatenate([jnp.concatenate([a_ref[0, :, kk, :], af_ref[0, :, kk, :]], axis=1),
                               jnp.concatenate([a_ref[1, :, kk, :], af_ref[1, :, kk, :]], axis=1)], axis=0)
        xh = _mm(big, rhs, _NN, P_DFT)
        xr, hr = xh[:n2, :c], xh[:n2, c:]
        xi, hi = xh[n2:, :c], xh[n2:, c:]
        y = jnp.concatenate([xr * hr - xi * hi, xr * hi + xi * hr], axis=0)
        bb = _mm(big, y, _TN, P_DFT)
        b_ref[0, kk] = bb[:n2]
        b_ref[1, kk] = bb[n2:]


def _dft2_call(consts, a4, af4):
    _, n2, n1, c = a4.shape
    kb = min(n1, DFT_NB)
    blk = pl.BlockSpec((2, n2, kb, c), lambda q: (0, 0, q, 0))
    tw = pl.BlockSpec((kb, 1, n2), lambda q: (q, 0, 0))
    return pl.pallas_call(
        _dft2_conv_kernel,
        out_shape=jax.ShapeDtypeStruct((2, n1, n2, c), F32),
        grid=(n1 // kb,),
        in_specs=[_full((n2, n2)), _full((n2, n2)), tw, tw, blk, blk],
        out_specs=pl.BlockSpec((2, kb, n2, c), lambda q: (0, q, 0, 0)),
        compiler_params=_cparams(("arbitrary",)),
        name="dft_stage2_conv",
    )(consts["fc"], consts["fs"], consts["twc"], consts["tws"], a4, af4)


def _dft3_kernel(m_ref, b_ref, z_ref, x0_ref, bias_ref, asum_ref, o_ref):
    m = m_ref[...]
    for j in range(z_ref.shape[1]):
        bj = jnp.concatenate([b_ref[0, :, j, :], b_ref[1, :, j, :]], axis=0)
        y = _mm(m, bj, _NN, P_DFT)
        o_ref[:, j, :] = (y / asum_ref[...] + z_ref[:, j, :] * bias_ref[...]) * x0_ref[:, j, :]


def _dft3_call(m3, b4, z3, x03, bias, asum):
    rows, _ = m3.shape
    _, n1, n2, c = b4.shape
    nbk = min(n2, DFT_NB)
    slab = pl.BlockSpec((rows, nbk, c), lambda j: (0, j, 0))
    return pl.pallas_call(
        _dft3_kernel,
        out_shape=jax.ShapeDtypeStruct((rows, n2, c), F32),
        grid=(n2 // nbk,),
        in_specs=[_full(m3.shape), pl.BlockSpec((2, n1, nbk, c), lambda j: (0, 0, j, 0)), slab, slab,
                  _full((1, c)), _full((1, c))],
        out_specs=slab,
        compiler_params=_cparams(("arbitrary",)),
        name="dft_stage3",
    )(m3, b4, z3, x03, bias, asum)


def _conv_direct_kernel(m1_ref, m3_ref, z_ref, f_ref, x0_ref, bias_ref, asum_ref, o_ref):
    m1 = m1_ref[...]
    z = z_ref[...]
    a = _mm(m1, z, _NN, P_DFT)
    h = _mm(m1, f_ref[...], _NN, P_DFT)
    nn = a.shape[0] // 2
    ar, ai, hr, hi = a[:nn], a[nn:], h[:nn], h[nn:]
    y = jnp.concatenate([ar * hr - ai * hi, ar * hi + ai * hr], axis=0)
    out = _mm(m3_ref[...], y, _NN, P_DFT)
    o_ref[...] = (out / asum_ref[...] + z * bias_ref[...]) * x0_ref[...]


def _conv_direct_call(m1, m3, z, filt, x0, bias, asum):
    args = (m1, m3, z, filt, x0, bias, asum)
    return pl.pallas_call(
        _conv_direct_kernel,
        out_shape=jax.ShapeDtypeStruct(z.shape, F32),
        grid=(1,),
        in_specs=[_full(a.shape) for a in args],
        out_specs=_full(z.shape),
        compiler_params=_cparams(("arbitrary",)),
        name="long_conv_direct",
    )(*args)


DIRECT_CONV_MAX = 256


@functools.lru_cache(maxsize=None)
def _dft_consts_np(n):
    big_n = 2 * n
    if n <= DIRECT_CONV_MAX:
        n1 = big_n
    else:
        n1 = 256 if n >= 8192 else 64
    n2 = big_n // n1
    k1 = np.arange(n1)[:, None].astype(np.float64)
    j1 = np.arange(n1 // 2)[None, :].astype(np.float64)
    ang1 = 2.0 * np.pi * k1 * j1 / n1
    m1 = np.concatenate([np.cos(ang1), -np.sin(ang1)], axis=0)
    o1 = (n1 // 4 + np.arange(n1 // 2))[:, None].astype(np.float64)
    q1 = np.arange(n1)[None, :].astype(np.float64)
    ang3 = 2.0 * np.pi * o1 * q1 / n1
    m3 = np.concatenate([np.cos(ang3), -np.sin(ang3)], axis=1) / big_n
    k2 = np.arange(n2)[:, None].astype(np.float64)
    j2 = np.arange(n2)[None, :].astype(np.float64)
    ang2 = 2.0 * np.pi * k2 * j2 / n2
    angt = 2.0 * np.pi * np.arange(n1)[:, None].astype(np.float64) * j2 / big_n
    f = lambda a: np.asarray(a, np.float32)
    return dict(n1=n1, n2=n2, m1=f(m1), m3=f(m3), fc=f(np.cos(ang2)), fs=f(np.sin(ang2)),
                twc=f(np.cos(angt))[:, None, :], tws=f(np.sin(angt))[:, None, :])


def _long_conv_call(z, x0, filt, asum, bias):
    n, c = z.shape
    cn = _dft_consts_np(n)
    n1, n2 = cn["n1"], cn["n2"]
    consts = {kk: jnp.asarray(vv) for kk, vv in cn.items() if kk not in ("n1", "n2")}
    if n2 == 1:
        return _conv_direct_call(consts["m1"], consts["m3"], z, filt, x0, bias, asum)
    z3 = z.reshape(n1 // 2, n2, c)
    a_f = _dft1_call(consts["m1"], filt.reshape(n1 // 2, n2, c))
    a_z = _dft1_call(consts["m1"], z3)
    b4 = _dft2_call(consts, a_z, a_f)
    out = _dft3_call(consts["m3"], b4, z3, x0.reshape(n1 // 2, n2, c), bias, asum)
    return out.reshape(n, c)


def _layer_norm(x, g, b):
    mu = jnp.mean(x, axis=-1, keepdims=True)
    xc = x - mu
    var = jnp.mean(xc * xc, axis=-1, keepdims=True)
    return xc * lax.rsqrt(var + LN_EPS) * g + b


def _merge_kernel(yf_ref, yb_ref, bv_ref, lg_ref, ho_ref, pg_ref, x_ref, gx_ref, bx_ref, g2_ref, g_ref,
                  wb_ref, wo_ref, gate_ref, lng_ref, lnb_ref, sh2_ref, sc2_ref, o_ref, hf_ref):
    g = g_ref[...]
    ys = yf_ref[...] + yb_ref[...]
    inv_hd = 1.0 / HEAD_DIM
    mu = _mm_rx(ys, g, 3) * inv_hd
    dd = ys - mu
    var = _mm_rx(dd * dd, g, 3) * inv_hd
    yn = dd * lax.rsqrt(var + GN_EPS) * gx_ref[...] + bx_ref[...]
    gate_r = jnp.dot(_sig(lg_ref[...]).astype(BF), g2_ref[...], preferred_element_type=F32)
    ro = (yn + bv_ref[...]) * gate_r
    br = jnp.dot(ro.astype(BF), wb_ref[0], preferred_element_type=F32)
    bh = jnp.dot(ho_ref[...].astype(BF), wb_ref[1], preferred_element_type=F32)
    sg = _sig(pg_ref[...])
    m = sg[:, :D_MODEL] * br + sg[:, D_MODEL:] * bh
    mix = jnp.dot(m.astype(BF), wo_ref[...], preferred_element_type=F32)
    xn = _layer_norm(ALPHA * x_ref[...] + gate_ref[...] * mix, lng_ref[...], lnb_ref[...])
    o_ref[...] = xn
    hf_ref[...] = xn * (1.0 + sc2_ref[...]) + sh2_ref[...]


def _merge_call(yf, yb, bv, lg, ho, pg, x, mp, l):
    L, d = x.shape
    hw = RWKV_WIDTH
    tb = min(L, 256)
    row = lambda i: (i, 0)
    names = ("lnx_g", "lnx_b", "g2", "G", "w_branch", "w_out", "gate", "ln_g", "ln_b", "sh2", "sc2")
    pargs = [mp[nm] for nm in names]

    def pspec(nm, a):
        if nm in ("g2", "w_branch", "w_out"):
            nd = a.ndim - 1
            return pl.BlockSpec((None,) + a.shape[1:], lambda i: (l,) + (0,) * nd)
        return _full(a.shape)

    return pl.pallas_call(
        _merge_kernel,
        out_shape=(jax.ShapeDtypeStruct((L, d), F32), jax.ShapeDtypeStruct((L, d), F32)),
        grid=(L // tb,),
        in_specs=[pl.BlockSpec((tb, hw), row), pl.BlockSpec((tb, hw), row), pl.BlockSpec((tb, hw), row),
                  pl.BlockSpec((tb, LORA_G), row), pl.BlockSpec((tb, hw), row),
                  pl.BlockSpec((tb, GATE_COLS), row), pl.BlockSpec((tb, d), row)]
                 + [pspec(nm, a) for nm, a in zip(names, pargs)],
        out_specs=(pl.BlockSpec((tb, d), row), pl.BlockSpec((tb, d), row)),
        compiler_params=_cparams(("arbitrary",)),
        name="merge_postnorm",
    )(yf, yb, bv, lg, ho, pg, x, *pargs)


def _router_kernel(hf_ref, wrt_ref, rb_ref, cw_ref, sel_ref):
    lt = _mm(wrt_ref[...], hf_ref[...], _NT, 6)
    rid = lax.broadcasted_iota(jnp.int32, (LANES, 1), 0)
    valid = rid < N_EXPERTS
    lg = jnp.where(valid, lt, -jnp.inf)
    mx = jnp.max(lg, axis=0, keepdims=True)
    ex = jnp.where(valid, jnp.exp(lg - mx), 0.0)
    scores = ex / jnp.sum(ex, axis=0, keepdims=True)
    sel = scores + rb_ref[...]
    s = [sel[e:e + 1, :] for e in range(N_EXPERTS)]
    p = [scores[e:e + 1, :] for e in range(N_EXPERTS)]
    gs = []
    for gi in range(N_GROUPS):
        mem = s[gi * EXPERTS_PER_GROUP:(gi + 1) * EXPERTS_PER_GROUP]
        best = None
        for a in range(EXPERTS_PER_GROUP):
            for b in range(a + 1, EXPERTS_PER_GROUP):
                pair = mem[a] + mem[b]
                best = pair if best is None else jnp.maximum(best, pair)
        gs.append(best)
    bg = jnp.where((gs[0] >= gs[1]) & (gs[0] >= gs[2]) & (gs[0] >= gs[3]), 0,
                   jnp.where((gs[1] >= gs[2]) & (gs[1] >= gs[3]), 1, jnp.where(gs[2] >= gs[3], 2, 3)))
    chosen = []
    for e in range(N_EXPERTS):
        gi = e // EXPERTS_PER_GROUP
        beats = None
        for j in range(gi * EXPERTS_PER_GROUP, (gi + 1) * EXPERTS_PER_GROUP):
            if j == e:
                continue
            cond = (s[j] >= s[e]) if j < e else (s[j] > s[e])
            cnt = jnp.where(cond, 1.0, 0.0)
            beats = cnt if beats is None else beats + cnt
        chosen.append((bg == gi) & (beats < 1.5))
    den = None
    for e in range(N_EXPERTS):
        t = jnp.where(chosen[e], p[e], 0.0)
        den = t if den is None else den + t
    out = jnp.zeros(lt.shape, F32)
    msk = jnp.zeros(lt.shape, F32)
    for e in range(N_EXPERTS):
        hit = (rid == e) & chosen[e]
        out = jnp.where(hit, p[e] / den, out)
        msk = jnp.where(hit, 1.0, msk)
    cw_ref[...] = out.T
    sel_ref[...] = msk.T


def _router_call(hf, wrt_pad, rb_pad):
    L, d = hf.shape
    tb = min(L, 256)
    row = lambda i: (i, 0)
    return pl.pallas_call(
        _router_kernel,
        out_shape=(jax.ShapeDtypeStruct((L, LANES), F32), jax.ShapeDtypeStruct((L, LANES), F32)),
        grid=(L // tb,),
        in_specs=[pl.BlockSpec((tb, d), row), _full(wrt_pad.shape), _full(rb_pad.shape)],
        out_specs=(pl.BlockSpec((tb, LANES), row), pl.BlockSpec((tb, LANES), row)),
        compiler_params=_cparams(("arbitrary",)),
        name="router",
    )(hf, wrt_pad, rb_pad)


MOE_BLK = 256


MOE_RING = 3


def _moe_ffn_kernel(tok_ref, slot_ref, be_ref, hf_hbm, wg_ref, wu_ref, wd_ref, y_hbm,
                    xbuf, ybuf, wbf, gsem, ssem, *, nb):
    b = pl.program_id(0)
    nrow = MOE_BLK

    def row_in(idx, sl, r):
        return pltpu.make_async_copy(hf_hbm.at[pl.ds(idx, 1)], xbuf.at[sl, pl.ds(r, 1)], gsem.at[sl])

    def row_out(idx, sl, r):
        return pltpu.make_async_copy(ybuf.at[sl, pl.ds(r, 1)], y_hbm.at[pl.ds(idx, 1)], ssem.at[sl])

    @pl.when(b == 0)
    def _():
        for blk in range(MOE_RING - 1):
            for r in range(nrow):
                row_in(tok_ref[blk * nrow + r], blk, r).start()
        ybuf[MOE_RING - 1] = jnp.zeros(ybuf.shape[1:], F32)

    @pl.when((b == 0) | (be_ref[b] != be_ref[jnp.maximum(b - 1, 0)]))
    def _():
        wbf[0] = wg_ref[...].astype(BF)
        wbf[1] = wu_ref[...].astype(BF)
        wbf[2] = wd_ref[...].astype(BF)

    def step(s):
        s1, s2 = (s + 1) % MOE_RING, (s + 2) % MOE_RING
        for r in range(nrow):
            row_in(0, s, r).wait()

        @pl.when(b >= 2)
        def _():
            for r in range(nrow):
                row_out(0, s, r).wait()

        nxt = (b + 2) * nrow
        for r in range(nrow):
            row_in(tok_ref[nxt + r], s2, r).start(priority=r % 2)
        cur = b * nrow
        for r in range(nrow):
            row_out(slot_ref[cur + r], s2, r).start(priority=r % 2)

        x = xbuf[s].astype(BF)
        hg = jnp.dot(x, wbf[0], preferred_element_type=F32)
        hu = jnp.dot(x, wbf[1], preferred_element_type=F32)
        act = (hg * _sig(hg) * hu).astype(BF)
        ybuf[s] = jnp.dot(act, wbf[2], preferred_element_type=F32)

        @pl.when(b == nb)
        def _():
            for sl in (s1, s2):
                for r in range(nrow):
                    row_in(0, sl, r).wait()
                for r in range(nrow):
                    row_out(0, sl, r).wait()

    ring = lax.rem(b, MOE_RING)
    for s in range(MOE_RING):
        pl.when(ring == s)(functools.partial(step, s))


def _moe_ffn_call(row_tok, row_slot, block_e, hf_all, wg, wu, wd, l):
    d = hf_all.shape[1]
    nb = block_e.shape[0] - 1
    assert nb >= MOE_RING
    wspec = pl.BlockSpec((None, None, d, d), lambda b, tok, slot, be: (l, be[b], 0, 0))
    return pl.pallas_call(
        functools.partial(_moe_ffn_kernel, nb=nb),
        out_shape=jax.ShapeDtypeStruct(((nb + 1) * MOE_BLK, d), F32),
        grid_spec=pltpu.PrefetchScalarGridSpec(
            num_scalar_prefetch=3,
            grid=(nb + 1,),
            in_specs=[pl.BlockSpec(memory_space=pl.ANY), wspec, wspec, wspec],
            out_specs=pl.BlockSpec(memory_space=pl.ANY),
            scratch_shapes=[pltpu.VMEM((MOE_RING, MOE_BLK, d), F32), pltpu.VMEM((MOE_RING, MOE_BLK, d), F32),
                            pltpu.VMEM((3, d, d), BF),
                            pltpu.SemaphoreType.DMA((MOE_RING,)), pltpu.SemaphoreType.DMA((MOE_RING,))]),
        compiler_params=_cparams(("arbitrary",)),
        name="moe_ffn",
    )(row_tok, row_slot, block_e, hf_all, wg, wu, wd)


def _moe_dispatch(sel, cw):
    t_all = sel.shape[0]
    blk = MOE_BLK
    nb = (2 * t_all) // blk + N_EXPERTS
    nr = nb * blk
    mi = (sel > 0.5).astype(jnp.int32)
    rank = jnp.cumsum(mi, axis=0) - mi
    counts = jnp.sum(mi, axis=0)
    padded = ((counts + blk - 1) // blk) * blk
    pend = jnp.cumsum(padded)
    dest = pend - padded + rank
    kk = jnp.cumsum(mi, axis=1) - mi
    first = (mi > 0) & (kk == 0)
    second = (mi > 0) & (kk == 1)
    d0 = jnp.sum(jnp.where(first, dest, 0), axis=1)
    d1 = jnp.sum(jnp.where(second, dest, 0), axis=1)
    w0 = jnp.sum(jnp.where(first, cw, 0.0), axis=1, keepdims=True)
    w1 = jnp.sum(jnp.where(second, cw, 0.0), axis=1, keepdims=True)
    tok2 = 2 * jnp.arange(t_all, dtype=jnp.int32)
    packed = jnp.full((nr,), -1, jnp.int32).at[jnp.concatenate([d0, d1])].set(
        jnp.concatenate([tok2, tok2 + 1]), unique_indices=True)
    is_pad = packed < 0
    row_tok = jnp.where(is_pad, 0, jnp.right_shift(packed, 1))
    blk_start = jnp.arange(nb, dtype=jnp.int32)[:, None] * blk
    block_e = jnp.clip(jnp.sum((blk_start >= pend[None, :]).astype(jnp.int32), axis=1), 0, N_EXPERTS - 1)
    real_before = jnp.repeat(jnp.cumsum(counts)[block_e], blk)
    pad_slot = 2 * t_all + jnp.arange(nr, dtype=jnp.int32) - real_before
    row_slot = jnp.where(is_pad, pad_slot, jnp.bitwise_and(packed, 1) * t_all + row_tok)
    row_tok = jnp.concatenate([row_tok, jnp.zeros((MOE_RING * blk,), jnp.int32)])
    row_slot = jnp.concatenate([nr + jnp.arange(blk, dtype=jnp.int32), row_slot])
    block_e = jnp.concatenate([block_e, block_e[-1:]])
    return row_tok, row_slot, block_e, w0, w1


def _combine_kernel(x_ref, y0_ref, y1_ref, w0_ref, w1_ref, gate_ref, lng_ref, lnb_ref, o_ref):
    y = y0_ref[...] * w0_ref[...] + y1_ref[...] * w1_ref[...]
    o_ref[...] = _layer_norm(ALPHA * x_ref[...] + gate_ref[...] * y, lng_ref[...], lnb_ref[...])


def _combine_call(x, y_slots, w0, w1, row_off, t_all, gate, lng, lnb, tm):
    L, d = x.shape
    off0 = row_off // tm
    off1 = (t_all + row_off) // tm
    vec = _full((1, d))
    return pl.pallas_call(
        _combine_kernel,
        out_shape=jax.ShapeDtypeStruct((L, d), F32),
        grid=(L // tm,),
        in_specs=[pl.BlockSpec((tm, d), lambda i: (i, 0)),
                  pl.BlockSpec((tm, d), lambda i: (off0 + i, 0)),
                  pl.BlockSpec((tm, d), lambda i: (off1 + i, 0)),
                  pl.BlockSpec((tm, 1), lambda i: (off0 + i, 0)),
                  pl.BlockSpec((tm, 1), lambda i: (off0 + i, 0)), vec, vec, vec],
        out_specs=pl.BlockSpec((tm, d), lambda i: (i, 0)),
        compiler_params=_cparams(("arbitrary",)),
        name="moe_combine",
    )(x, y_slots, y_slots, w0, w1, gate, lng, lnb)


def _moe_layer(segs, wrt_pad, rb_pad, lng, lnb, wg, wu, wd, l):
    routed = [_router_call(hf, wrt_pad, rb_pad) for (_, hf, _) in segs]
    cw = jnp.concatenate([r[0][:, :N_EXPERTS] for r in routed], axis=0)
    sel = jnp.concatenate([r[1][:, :N_EXPERTS] for r in routed], axis=0)
    hf_all = jnp.concatenate([hf for (_, hf, _) in segs], axis=0) if len(segs) > 1 else segs[0][1]
    t_all = hf_all.shape[0]
    row_tok, row_slot, block_e, w0, w1 = _moe_dispatch(sel, cw)
    y_slots = _moe_ffn_call(row_tok, row_slot, block_e, hf_all, wg, wu, wd, l)
    tm = min(256, min(x.shape[0] for (x, _, _) in segs))
    outs = []
    off = 0
    for (x, _, gate) in segs:
        assert off % tm == 0 and t_all % tm == 0 and x.shape[0] % tm == 0
        outs.append(_combine_call(x, y_slots, w0, w1, off, t_all, gate, lng, lnb, tm))
        off += x.shape[0]
    return outs


def _blockdiag2(m):
    z = jnp.zeros_like(m[0])
    return jnp.concatenate([jnp.concatenate([m[0], z], axis=1), jnp.concatenate([z, m[1]], axis=1)], axis=0)


def _mixer(x, mod_row, l, w_in_bf, prm, fp, hy, mp, s0, latent):
    L = x.shape[0]
    sh, sc, gate = mod_row[0], mod_row[1], mod_row[2]
    p_r, p_h, p_g = _inproj_call(x, sh, sc, w_in_bf, l)
    rvk, dirp, lg, bv = _prepare_call(p_r, prm, latent)
    yf, yb, sfin = _scan_call(rvk, dirp, s0)
    filt, asum = _hfilt_call(L, fp)
    z, x0 = _hconv3_call(p_h, hy["conv_w"], hy["conv_b"])
    ho = _long_conv_call(z, x0, filt, asum, hy["bias"])
    mpl = dict(mp)
    mpl["gate"], mpl["sh2"], mpl["sc2"] = gate, mod_row[3], mod_row[4]
    xn, hf = _merge_call(yf, yb, bv, lg, ho, p_g, x, mpl, l)
    return xn, hf, sfin


def kernel(x, c, ctx, c_ctx, w_mod, b_mod, w_in, rwkv_mu, rwkv_w0, rwkv_w2, rwkv_a0, rwkv_a2, rwkv_g2,
           rwkv_k_k, rwkv_k_a, rwkv_r_k, rwkv_lnx_g, rwkv_lnx_b, hy_conv_w, hy_conv_b, hy_f_w1, hy_f_b1,
           hy_f_w2, hy_f_b2, hy_f_w3, hy_f_b3, hy_f_wout, hy_freq, hy_bias, w_branch, w_out, ln_g, ln_b,
           w_router, router_bias, w_gate, w_up, w_down):
    b, n_lat, d = x.shape
    assert b == 1 and d == D_MODEL
    n_ctx = ctx.shape[1]
    depth = w_mod.shape[0]
    hw = RWKV_WIDTH
    xl = x[0]
    xc = ctx[0]

    cc = jnp.concatenate([c[:1], c_ctx[None, :], jnp.zeros((6, d), F32)], axis=0)
    mod = _mod_call(cc, w_mod, b_mod)

    head_of = np.arange(hw) // HEAD_DIM
    G = jnp.asarray((head_of[:, None] == head_of[None, :]).astype(np.float32), dtype=BF)
    bands = jnp.linspace(1e-4, FILTER_BANDS - 1, FILTER_BANDS, dtype=F32)[None, :]
    deltas = jnp.abs(jnp.linspace(HYENA_MIN_DECAY, HYENA_MAX_DECAY, HYENA_WIDTH, dtype=F32))[None, :]
    wr_pad = jnp.pad(w_router.T, ((0, LANES - N_EXPERTS), (0, 0)))
    rb_pad = jnp.pad(router_bias, (0, LANES - N_EXPERTS))[:, None]
    w_in_bf = w_in.astype(BF)
    w_branch_bf = w_branch.astype(BF)
    w_out_bf = w_out.astype(BF)
    g2_bf = rwkv_g2.astype(BF)

    for l in range(depth):
        last = l == depth - 1
        ml = [mod[l, 0:1, j * d:(j + 1) * d] for j in range(6)]
        mc = [mod[l, 1:2, j * d:(j + 1) * d] for j in range(6)]
        prm = dict(mu=rwkv_mu[l][None, :],
                   w0=rwkv_w0[l].reshape(1, 2 * hw), w2=_blockdiag2(rwkv_w2[l]),
                   a0=rwkv_a0[l].reshape(1, 2 * hw), a2=_blockdiag2(rwkv_a2[l]),
                   k_k=rwkv_k_k[l][None, :], k_a=rwkv_k_a[l][None, :], r_k=rwkv_r_k[l][None, :], G=G)
        w1 = hy_f_w1[l]
        fp = dict(bands=bands.T, w1t=w1[0:1].T, w1c=w1[1:1 + FILTER_BANDS].T, w1s=w1[1 + FILTER_BANDS:].T,
                  b1=hy_f_b1[l][:, None], w2=hy_f_w2[l].T, b2=hy_f_b2[l][:, None], w3=hy_f_w3[l].T,
                  b3=hy_f_b3[l][:, None], wout=hy_f_wout[l], freq=hy_freq[l][:, None], deltas=deltas)
        hy = dict(conv_w=hy_conv_w[l], conv_b=hy_conv_b[l][None, :], bias=hy_bias[l][None, :])
        mp = dict(lnx_g=rwkv_lnx_g[l][None, :], lnx_b=rwkv_lnx_b[l][None, :], g2=g2_bf, G=G,
                  w_branch=w_branch_bf, w_out=w_out_bf, ln_g=ln_g[l, 0][None, :], ln_b=ln_b[l, 0][None, :])
        s0 = jnp.zeros((2, HEAD_DIM, hw), F32)
        xc_new, hf_c, s_ctx = _mixer(xc, mc, l, w_in_bf, prm, fp, hy, mp, s0, latent=False)
        xl, hf_l, _ = _mixer(xl, ml, l, w_in_bf, prm, fp, hy, mp, s_ctx, latent=True)
        lng, lnb = ln_g[l, 1][None, :], ln_b[l, 1][None, :]
        if last:
            (xl,) = _moe_layer([(xl, hf_l, ml[5])], wr_pad, rb_pad, lng, lnb, w_gate, w_up, w_down, l)
        else:
            xc, xl = _moe_layer([(xc_new, hf_c, mc[5]), (xl, hf_l, ml[5])], wr_pad, rb_pad, lng, lnb,
                                w_gate, w_up, w_down, l)
    return xl[None]
```

```python
import functools
import math

import numpy as np
import jax
import jax.numpy as jnp
from jax import lax
from jax.experimental import pallas as pl
from jax.experimental.pallas import tpu as pltpu

F32 = jnp.float32
BF = jnp.bfloat16

D_MODEL = 1024
DEPTH = 4
GRID_W = 64
RWKV_WIDTH = 512
HEAD_DIM = 64
RWKV_HEADS = 8
LORA_W = 64
LORA_A = 64
LORA_G = 128
DECAY_SCALE = 0.606531
GN_EPS = 6.4e-4
RWKV_COLS = 3 * RWKV_WIDTH + 2 * LORA_W + 2 * LORA_A + LORA_G
HYENA_WIDTH = 512
HYENA_COLS = 3 * HYENA_WIDTH
FILTER_BANDS = 16
FILTER_HIDDEN = 64
HYENA_MIN_DECAY = math.log(1e-2) / 1.5
HYENA_MAX_DECAY = math.log(1e-2) / 0.3
GATE_COLS = 2 * D_MODEL
PROJ_COLS = RWKV_COLS + HYENA_COLS + GATE_COLS
N_EXPERTS = 16
N_GROUPS = 4
EXPERTS_PER_GROUP = 4
ALPHA = (2 * DEPTH) ** 0.25
LN_EPS = 1e-5

SCAN_CHUNK = 64
LANES = 128
VMEM_LIMIT = 56 * 1024 * 1024

_NN = (((1,), (0,)), ((), ()))
_NT = (((1,), (1,)), ((), ()))
_TN = (((0,), (0,)), ((), ()))


def _sig(x):
    return 1.0 / (1.0 + jnp.exp(-x))


def _parts(a, n):
    out = []
    rem = a
    for i in range(n):
        hi = rem.astype(BF)
        out.append(hi)
        if i + 1 < n:
            rem = rem - hi.astype(F32)
    return out


def _mm(a, b, dn=_NN, passes=1):
    n = {1: 1, 3: 2, 6: 3}[passes]
    pa = _parts(a, n)
    pb = _parts(b, n)
    acc = None
    for i in range(n):
        for j in range(n - i):
            t = lax.dot_general(pa[i], pb[j], dn, preferred_element_type=F32)
            acc = t if acc is None else acc + t
    return acc


def _mm_rx(a, b_exact, n=3, dn=_NN):
    acc = None
    for p in _parts(a, n):
        t = lax.dot_general(p, b_exact, dn, preferred_element_type=F32)
        acc = t if acc is None else acc + t
    return acc


def _mm_lx(a_exact, b, n=3, dn=_NN):
    acc = None
    for p in _parts(b, n):
        t = lax.dot_general(a_exact, p, dn, preferred_element_type=F32)
        acc = t if acc is None else acc + t
    return acc


def _cparams(sem, vmem=VMEM_LIMIT):
    return pltpu.CompilerParams(dimension_semantics=sem, vmem_limit_bytes=vmem)


def _full(shape):
    nd = len(shape)
    return pl.BlockSpec(shape, lambda *_: (0,) * nd)


def _mod_kernel(c_ref, w_ref, b_ref, o_ref):
    cc = c_ref[...]
    s = cc * _sig(cc)
    o_ref[...] = jnp.dot(s.astype(BF), w_ref[...].astype(BF), preferred_element_type=F32) + b_ref[...]


def _mod_call(cc, w_mod, b_mod):
    depth, d, n6 = w_mod.shape
    tn = 1536
    return pl.pallas_call(
        _mod_kernel,
        out_shape=jax.ShapeDtypeStruct((depth, 8, n6), F32),
        grid=(depth, n6 // tn),
        in_specs=[
            pl.BlockSpec((8, d), lambda l, j: (0, 0)),
            pl.BlockSpec((None, d, tn), lambda l, j: (l, 0, j)),
            pl.BlockSpec((None, 1, tn), lambda l, j: (l, 0, j)),
        ],
        out_specs=pl.BlockSpec((None, 8, tn), lambda l, j: (l, 0, j)),
        compiler_params=_cparams(("arbitrary", "arbitrary")),
        name="mod",
    )(cc, w_mod, b_mod.reshape(depth, 1, n6))


def _inproj_kernel(x_ref, sh_ref, sc_ref, w_ref, pr_ref, ph_ref, pg_ref):
    xm = (x_ref[...] * (1.0 + sc_ref[...]) + sh_ref[...]).astype(BF)
    pr_ref[...] = jnp.dot(xm, w_ref[:, :RWKV_COLS], preferred_element_type=F32)
    ph_ref[...] = jnp.dot(xm, w_ref[:, RWKV_COLS:RWKV_COLS + HYENA_COLS], preferred_element_type=F32)
    pg_ref[...] = jnp.dot(xm, w_ref[:, RWKV_COLS + HYENA_COLS:], preferred_element_type=F32)


def _inproj_call(x, sh, sc, w_bf, l):
    L, d = x.shape
    tm = min(L, 256)
    row = lambda i: (i, 0)
    return pl.pallas_call(
        _inproj_kernel,
        out_shape=(jax.ShapeDtypeStruct((L, RWKV_COLS), F32),
                   jax.ShapeDtypeStruct((L, HYENA_COLS), F32),
                   jax.ShapeDtypeStruct((L, GATE_COLS), F32)),
        grid=(L // tm,),
        in_specs=[pl.BlockSpec((tm, d), row), _full((1, d)), _full((1, d)),
                  pl.BlockSpec((None, d, PROJ_COLS), lambda i: (l, 0, 0))],
        out_specs=(pl.BlockSpec((tm, RWKV_COLS), row), pl.BlockSpec((tm, HYENA_COLS), row),
                   pl.BlockSpec((tm, GATE_COLS), row)),
        compiler_params=_cparams(("arbitrary",)),
        name="inproj",
    )(x, sh, sc, w_bf)


def _prepare_kernel(*refs, latent, nblk):
    if latent:
        p_ref, up_ref, dn_ref = refs[:3]
        refs = refs[3:]
    else:
        p_ref = refs[0]
        refs = refs[1:]
    (mu_ref, w0_ref, w2_ref, a0_ref, a2_ref, kk_ref, ka_ref, rk_ref, g_ref,
     rvk_out, dir_out, lg_out, bv_out) = refs
    i = pl.program_id(0)
    p = p_ref[...]
    tb, w = p.shape
    row = lax.broadcasted_iota(jnp.int32, (tb, 1), 0)
    col = lax.broadcasted_iota(jnp.int32, (1, w), 1)
    if latent:
        gw = jnp.bitwise_and(row, GRID_W - 1)
        left = jnp.where(gw == 0, 0.0, pltpu.roll(p, 1, 0))
        right = jnp.where(gw == GRID_W - 1, 0.0, pltpu.roll(p, tb - 1, 0))
        upv = jnp.where(i == 0, 0.0, up_ref[...])
        dnv = jnp.where(i == nblk - 1, 0.0, dn_ref[...])
        if tb > GRID_W:
            up = jnp.concatenate([upv, p[:tb - GRID_W]], axis=0)
            down = jnp.concatenate([p[GRID_W:], dnv], axis=0)
        else:
            up, down = upv, dnv
        q = w // 4
        sh = jnp.where(col < q, left, jnp.where(col < 2 * q, right, jnp.where(col < 3 * q, up, down)))
    else:
        prev = jnp.where(row == 0, 0.0, pltpu.roll(p, 1, 0))
        nxt = jnp.where(row == tb - 1, 0.0, pltpu.roll(p, tb - 1, 0))
        sh = jnp.where(col < w // 2, prev, nxt)
    pm = p + (sh - p) * mu_ref[...]
    hw = RWKV_WIDTH
    r = pm[:, :hw]
    k = pm[:, hw:2 * hw]
    v = pm[:, 2 * hw:3 * hw]
    lw_in = jnp.tanh(pm[:, 3 * hw:3 * hw + 2 * LORA_W])
    la_in = pm[:, 3 * hw + 2 * LORA_W:3 * hw + 2 * LORA_W + 2 * LORA_A]
    lg = pm[:, 3 * hw + 2 * LORA_W + 2 * LORA_A:]
    lw = -DECAY_SCALE * _sig(w0_ref[...] + _mm(lw_in, w2_ref[...], passes=3))
    a = _sig(a0_ref[...] + _mm(la_in, a2_ref[...], passes=3))
    g = g_ref[...]
    kkr = k * kk_ref[...]
    nrm = jnp.sqrt(_mm_rx(kkr * kkr, g, 3))
    kk = kkr / jnp.maximum(nrm, 1e-12)
    k_a = ka_ref[...]
    k_f = k * (1.0 + (a[:, :hw] - 1.0) * k_a)
    k_b = k * (1.0 + (a[:, hw:] - 1.0) * k_a)
    bonus = _mm_rx(r * (k_f + k_b) * rk_ref[...], g, 3)
    rvk_out[:, :hw] = r
    rvk_out[:, hw:2 * hw] = v
    rvk_out[:, 2 * hw:] = -kk
    for d, (k_d, a_d) in enumerate(((k_f, a[:, :hw]), (k_b, a[:, hw:]))):
        dir_out[d, :, :hw] = lw[:, d * hw:(d + 1) * hw]
        dir_out[d, :, hw:2 * hw] = k_d
        dir_out[d, :, 2 * hw:] = kk * a_d
    lg_out[...] = lg
    bv_out[...] = bonus * v


def _prepare_call(p_r, prm, latent):
    L, w = p_r.shape
    hw = RWKV_WIDTH
    tb = 256 if latent else L
    nblk = L // tb
    row = lambda i: (i, 0)
    row3 = lambda i: (0, i, 0)
    in_specs = [pl.BlockSpec((tb, w), row)]
    args = [p_r]
    if latent:
        per = tb // GRID_W
        nrow = L // GRID_W
        in_specs += [
            pl.BlockSpec((GRID_W, w), lambda i: (jnp.maximum(i * per - 1, 0), 0)),
            pl.BlockSpec((GRID_W, w), lambda i: (jnp.minimum((i + 1) * per, nrow - 1), 0)),
        ]
        args += [p_r, p_r]
    names = ("mu", "w0", "w2", "a0", "a2", "k_k", "k_a", "r_k", "G")
    for nm in names:
        in_specs.append(_full(prm[nm].shape))
        args.append(prm[nm])
    sd = jax.ShapeDtypeStruct
    out_shape = (sd((L, 3 * hw), F32), sd((2, L, 3 * hw), F32), sd((L, LORA_G), F32), sd((L, hw), F32))
    out_specs = (pl.BlockSpec((tb, 3 * hw), row), pl.BlockSpec((2, tb, 3 * hw), row3),
                 pl.BlockSpec((tb, LORA_G), row), pl.BlockSpec((tb, hw), row))
    return pl.pallas_call(
        functools.partial(_prepare_kernel, latent=latent, nblk=nblk),
        out_shape=out_shape, grid=(nblk,), in_specs=in_specs, out_specs=out_specs,
        compiler_params=_cparams(("arbitrary",)),
        name="rwkv_prepare_lat" if latent else "rwkv_prepare_ctx",
    )(*args)


P_SCORE = 1
P_TINV = 1
P_APPLY = 1
P_STATE = 1


SCAN_SUB = 2


def _scan_chunk(fin, bin_, states, masks):
    C = SCAN_CHUNK
    hd = HEAD_DIM
    nh = RWKV_HEADS
    row, col, eye, lvl_masks = masks
    dirs = []
    for d, (r, v, ka, lw, k, kb) in enumerate((fin, bin_)):
        inc = (row >= col) if d == 0 else (row <= col)
        strict = (row > col) if d == 0 else (row < col)
        lc = _mm_lx(jnp.where(inc, 1.0, 0.0).astype(BF), lw, 3)
        e_neg = jnp.exp(-lc)
        e_tot = jnp.exp(jnp.sum(lw, axis=0, keepdims=True))
        kbn = kb * e_neg
        kkn = k * e_neg
        dirs.append(dict(inc=inc, strict=strict, v=v, aq=ka * jnp.exp(lc - lw), rq=r * jnp.exp(lc),
                         kbn=kbn, kkn=kkn, kbp=kbn * e_tot, kkp=kkn * e_tot, e_tot=e_tot, s=states[d]))

    units = [(d, h) for h in range(nh) for d in range(2)]
    hs = lambda arr, h: arr[:, h * hd:(h + 1) * hd]
    sc = {}
    for (d, h) in units:
        D = dirs[d]
        sc[d, h] = _mm(jnp.concatenate([hs(D["aq"], h), hs(D["rq"], h)], axis=0),
                       jnp.concatenate([hs(D["kbn"], h), hs(D["kkn"], h)], axis=0), _NT, P_SCORE)
    a_ab, t, x = {}, {}, {}
    for u in units:
        a_ab[u] = jnp.where(dirs[u[0]]["strict"], sc[u][:C, :C], 0.0)
        t[u] = eye + jnp.where(lvl_masks[0], a_ab[u], 0.0)
    for u in units:
        D = dirs[u[0]]
        lhs = jnp.concatenate([jnp.where(D["strict"], sc[u][:C, C:], 0.0),
                               jnp.where(D["inc"], sc[u][C:, C:], 0.0)], axis=0)
        x[u] = _mm(lhs, hs(D["v"], u[1]), _NN, P_APPLY)
    for m in lvl_masks[1:]:
        tmp = {u: _mm(jnp.where(m, a_ab[u], 0.0), t[u], _NN, P_TINV) for u in units}
        for u in units:
            t[u] = t[u] + _mm(t[u], tmp[u], _NN, P_TINV)
    z, gm, uu, yy, sn = {}, {}, {}, {}, {}
    for u in units:
        z[u] = _mm(t[u], jnp.concatenate([hs(dirs[u[0]]["aq"], u[1]), x[u][:C]], axis=1), _NN, P_APPLY)
    for u in units:
        D = dirs[u[0]]
        gm[u] = _mm(jnp.concatenate([z[u][:, :hd], hs(D["rq"], u[1])], axis=0), hs(D["s"], u[1]), _NT, P_STATE)
        uu[u] = gm[u][:C] + z[u][:, hd:]
    for u in units:
        D = dirs[u[0]]
        b_rb = jnp.where(D["inc"], sc[u][C:, :C], 0.0)
        yy[u] = gm[u][C:] + _mm(b_rb, uu[u], _NN, P_APPLY) + x[u][C:]
        sn[u] = hs(D["s"], u[1]) * hs(D["e_tot"], u[1]) + _mm(
            jnp.concatenate([uu[u], hs(D["v"], u[1])], axis=0),
            jnp.concatenate([hs(D["kbp"], u[1]), hs(D["kkp"], u[1])], axis=0), _TN, P_STATE)
    cat = lambda d, src: jnp.concatenate([src[d, h] for h in range(nh)], axis=1)
    return cat(0, yy), cat(1, yy), [cat(0, sn), cat(1, sn)]


def _scan_kernel(sf_ref, sb_ref, df_ref, db_ref, s0_ref, yf_ref, yb_ref, sfin_ref, s_scr, *, nstep):
    c = pl.program_id(0)
    C = SCAN_CHUNK

    @pl.when(c == 0)
    def _():
        s_scr[...] = s0_ref[...]

    row = lax.broadcasted_iota(jnp.int32, (C, C), 0)
    col = lax.broadcasted_iota(jnp.int32, (C, C), 1)
    eye = jnp.where(row == col, 1.0, 0.0)
    lvl_masks = []
    for sh in range(int(math.log2(C))):
        same2 = jnp.right_shift(row, sh + 1) == jnp.right_shift(col, sh + 1)
        same1 = jnp.right_shift(row, sh) == jnp.right_shift(col, sh)
        lvl_masks.append(jnp.logical_and(same2, jnp.logical_not(same1)))
    masks = (row, col, eye, lvl_masks)

    hw = RWKV_WIDTH
    unpack = lambda sref, dref, rows: tuple(ref[rows, i * hw:(i + 1) * hw] for ref in (sref, dref) for i in range(3))
    states = [s_scr[0], s_scr[1]]
    for j in range(SCAN_SUB):
        fsl = slice(j * C, (j + 1) * C)
        bsl = slice((SCAN_SUB - 1 - j) * C, (SCAN_SUB - j) * C)
        yf, yb, states = _scan_chunk(unpack(sf_ref, df_ref, fsl), unpack(sb_ref, db_ref, bsl), states, masks)
        yf_ref[fsl, :] = yf
        yb_ref[bsl, :] = yb
    s_scr[0] = states[0]
    s_scr[1] = states[1]

    @pl.when(c == nstep - 1)
    def _():
        sfin_ref[...] = s_scr[...]


def _scan_call(rvk, dirp, s0):
    L = rvk.shape[0]
    hw = RWKV_WIDTH
    C = SCAN_CHUNK * SCAN_SUB
    nchunk = L // C
    assert L % C == 0
    sh_f = pl.BlockSpec((C, 3 * hw), lambda c: (c, 0))
    sh_b = pl.BlockSpec((C, 3 * hw), lambda c: (nchunk - 1 - c, 0))
    pd_f = pl.BlockSpec((None, C, 3 * hw), lambda c: (0, c, 0))
    pd_b = pl.BlockSpec((None, C, 3 * hw), lambda c: (1, nchunk - 1 - c, 0))
    y_f = pl.BlockSpec((C, hw), lambda c: (c, 0))
    y_b = pl.BlockSpec((C, hw), lambda c: (nchunk - 1 - c, 0))
    state = _full((2, HEAD_DIM, hw))
    return pl.pallas_call(
        functools.partial(_scan_kernel, nstep=nchunk),
        out_shape=(jax.ShapeDtypeStruct((L, hw), F32), jax.ShapeDtypeStruct((L, hw), F32),
                   jax.ShapeDtypeStruct((2, HEAD_DIM, hw), F32)),
        grid=(nchunk,),
        in_specs=[sh_f, sh_b, pd_f, pd_b, state],
        out_specs=(y_f, y_b, state),
        scratch_shapes=[pltpu.VMEM((2, HEAD_DIM, hw), F32)],
        compiler_params=_cparams(("arbitrary",)),
        name="delta_scan",
    )(rvk, rvk, dirp, dirp, s0)


def _hfilt_kernel(bands_ref, w1t_ref, w1c_ref, w1s_ref, b1_ref, w2_ref, b2_ref, w3_ref, b3_ref, wo_ref,
                  fr_ref, dl_ref, f_ref, asum_ref, *, n, rb):
    i = pl.program_id(0)
    posr = (i * rb + lax.broadcasted_iota(jnp.int32, (1, rb), 1)).astype(F32)
    tr = posr / float(max(n - 1, 1))
    ang = ((2.0 * math.pi / n) * posr) * bands_ref[...]
    fr = fr_ref[...]
    h = (w1t_ref[...] * tr + _mm(w1c_ref[...], jnp.cos(ang), passes=3)
         + _mm(w1s_ref[...], -jnp.sin(ang), passes=3))
    h = jnp.sin(fr * (h + b1_ref[...]))
    h = jnp.sin(fr * (_mm(w2_ref[...], h, passes=3) + b2_ref[...]))
    h = jnp.sin(fr * (_mm(w3_ref[...], h, passes=3) + b3_ref[...]))
    filt = _mm(h, wo_ref[...], _TN, 3)
    pos = (i * rb + lax.broadcasted_iota(jnp.int32, (rb, 1), 0)).astype(F32)
    dist = jnp.abs(pos - float(n // 2)) * (2.0 / n)
    filt = filt * jnp.exp(-dist * dl_ref[...])
    f_ref[...] = filt

    @pl.when(i == 0)
    def _():
        asum_ref[...] = jnp.zeros_like(asum_ref)

    asum_ref[...] += jnp.sum(jnp.abs(filt), axis=0, keepdims=True)


def _hfilt_call(n, fp):
    rb = min(n, 512)
    names = ("bands", "w1t", "w1c", "w1s", "b1", "w2", "b2", "w3", "b3", "wout", "freq", "deltas")
    args = [fp[nm] for nm in names]
    return pl.pallas_call(
        functools.partial(_hfilt_kernel, n=n, rb=rb),
        out_shape=(jax.ShapeDtypeStruct((n, HYENA_WIDTH), F32), jax.ShapeDtypeStruct((1, HYENA_WIDTH), F32)),
        grid=(n // rb,),
        in_specs=[_full(a.shape) for a in args],
        out_specs=(pl.BlockSpec((rb, HYENA_WIDTH), lambda i: (i, 0)), _full((1, HYENA_WIDTH))),
        compiler_params=_cparams(("arbitrary",)),
        name="hyena_filter",
    )(*args)


def _hconv3_kernel(p_ref, pv_ref, nx_ref, cw_ref, cb_ref, z_ref, x0_ref, *, nblk):
    i = pl.program_id(0)
    p = p_ref[...]
    tb = p.shape[0]
    row = lax.broadcasted_iota(jnp.int32, (tb, 1), 0)
    prev_row = jnp.where(i == 0, 0.0, pv_ref[7:8, :])
    next_row = jnp.where(i == nblk - 1, 0.0, nx_ref[0:1, :])
    sp = jnp.where(row == 0, prev_row, pltpu.roll(p, 1, 0))
    sn = jnp.where(row == tb - 1, next_row, pltpu.roll(p, tb - 1, 0))
    u = sp * cw_ref[0:1, :] + p * cw_ref[1:2, :] + sn * cw_ref[2:3, :] + cb_ref[...]
    hw = HYENA_WIDTH
    z_ref[...] = u[:, 2 * hw:] * u[:, hw:2 * hw]
    x0_ref[...] = u[:, :hw]


def _hconv3_call(p_h, cw, cb):
    L, w = p_h.shape
    tb = min(L, 256)
    nblk = L // tb
    per = tb // 8
    row = lambda i: (i, 0)
    return pl.pallas_call(
        functools.partial(_hconv3_kernel, nblk=nblk),
        out_shape=(jax.ShapeDtypeStruct((L, HYENA_WIDTH), F32), jax.ShapeDtypeStruct((L, HYENA_WIDTH), F32)),
        grid=(nblk,),
        in_specs=[pl.BlockSpec((tb, w), row),
                  pl.BlockSpec((8, w), lambda i: (jnp.maximum(i * per - 1, 0), 0)),
                  pl.BlockSpec((8, w), lambda i: (jnp.minimum((i + 1) * per, L // 8 - 1), 0)),
                  _full(cw.shape), _full(cb.shape)],
        out_specs=(pl.BlockSpec((tb, HYENA_WIDTH), row), pl.BlockSpec((tb, HYENA_WIDTH), row)),
        compiler_params=_cparams(("arbitrary",)),
        name="hyena_conv3",
    )(p_h, p_h, p_h, cw, cb)


P_DFT = 1


DFT_NB = 8


def _dft1_kernel(m_ref, x_ref, o_ref):
    m = m_ref[...]
    n1 = o_ref.shape[2]
    for j in range(x_ref.shape[1]):
        r = _mm(m, x_ref[:, j, :], _NN, P_DFT)
        o_ref[0, j] = r[:n1]
        o_ref[1, j] = r[n1:]


def _dft1_call(m1, x3):
    rows, kdim = m1.shape
    _, n2, c = x3.shape
    n1 = rows // 2
    nbk = min(n2, DFT_NB)
    return pl.pallas_call(
        _dft1_kernel,
        out_shape=jax.ShapeDtypeStruct((2, n2, n1, c), F32),
        grid=(n2 // nbk,),
        in_specs=[_full(m1.shape), pl.BlockSpec((kdim, nbk, c), lambda j: (0, j, 0))],
        out_specs=pl.BlockSpec((2, nbk, n1, c), lambda j: (0, j, 0, 0)),
        compiler_params=_cparams(("arbitrary",)),
        name="dft_stage1",
    )(m1, x3)


def _dft2_matrix(fc, fs, tc, ts):
    fre = fc * tc - fs * ts
    fim = -(fc * ts + fs * tc)
    return jnp.concatenate([jnp.concatenate([fre, -fim], axis=1),
                            jnp.concatenate([fim, fre], axis=1)], axis=0)


def _dft2_conv_kernel(fc_ref, fs_ref, tc_ref, ts_ref, a_ref, af_ref, b_ref):
    fc, fs = fc_ref[...], fs_ref[...]
    n2, c = a_ref.shape[1], a_ref.shape[3]
    for kk in range(a_ref.shape[2]):
        big = _dft2_matrix(fc, fs, tc_ref[kk], ts_ref[kk])
        rhs = jnp.concatenate([jnp.concatenate([a_ref[0, :, kk, :], af_ref[0, :, kk, :]], axis=1),
                               jnp.concatenate([a_ref[1, :, kk, :], af_ref[1, :, kk, :]], axis=1)], axis=0)
        xh = _mm(big, rhs, _NN, P_DFT)
        xr, hr = xh[:n2, :c], xh[:n2, c:]
        xi, hi = xh[n2:, :c], xh[n2:, c:]
        y = jnp.concatenate([xr * hr - xi * hi, xr * hi + xi * hr], axis=0)
        bb = _mm(big, y, _TN, P_DFT)
        b_ref[0, kk] = bb[:n2]
        b_ref[1, kk] = bb[n2:]


def _dft2_call(consts, a4, af4):
    _, n2, n1, c = a4.shape
    kb = min(n1, DFT_NB)
    blk = pl.BlockSpec((2, n2, kb, c), lambda q: (0, 0, q, 0))
    tw = pl.BlockSpec((kb, 1, n2), lambda q: (q, 0, 0))
    return pl.pallas_call(
        _dft2_conv_kernel,
        out_shape=jax.ShapeDtypeStruct((2, n1, n2, c), F32),
        grid=(n1 // kb,),
        in_specs=[_full((n2, n2)), _full((n2, n2)), tw, tw, blk, blk],
        out_specs=pl.BlockSpec((2, kb, n2, c), lambda q: (0, q, 0, 0)),
        compiler_params=_cparams(("arbitrary",)),
        name="dft_stage2_conv",
    )(consts["fc"], consts["fs"], consts["twc"], consts["tws"], a4, af4)


def _dft3_kernel(m_ref, b_ref, z_ref, x0_ref, bias_ref, asum_ref, o_ref):
    m = m_ref[...]
    for j in range(z_ref.shape[1]):
        bj = jnp.concatenate([b_ref[0, :, j, :], b_ref[1, :, j, :]], axis=0)
        y = _mm(m, bj, _NN, P_DFT)
        o_ref[:, j, :] = (y / asum_ref[...] + z_ref[:, j, :] * bias_ref[...]) * x0_ref[:, j, :]


def _dft3_call(m3, b4, z3, x03, bias, asum):
    rows, _ = m3.shape
    _, n1, n2, c = b4.shape
    nbk = min(n2, DFT_NB)
    slab = pl.BlockSpec((rows, nbk, c), lambda j: (0, j, 0))
    return pl.pallas_call(
        _dft3_kernel,
        out_shape=jax.ShapeDtypeStruct((rows, n2, c), F32),
        grid=(n2 // nbk,),
        in_specs=[_full(m3.shape), pl.BlockSpec((2, n1, nbk, c), lambda j: (0, 0, j, 0)), slab, slab,
                  _full((1, c)), _full((1, c))],
        out_specs=slab,
        compiler_params=_cparams(("arbitrary",)),
        name="dft_stage3",
    )(m3, b4, z3, x03, bias, asum)


def _conv_direct_kernel(m1_ref, m3_ref, z_ref, f_ref, x0_ref, bias_ref, asum_ref, o_ref):
    m1 = m1_ref[...]
    z = z_ref[...]
    a = _mm(m1, z, _NN, P_DFT)
    h = _mm(m1, f_ref[...], _NN, P_DFT)
    nn = a.shape[0] // 2
    ar, ai, hr, hi = a[:nn], a[nn:], h[:nn], h[nn:]
    y = jnp.concatenate([ar * hr - ai * hi, ar * hi + ai * hr], axis=0)
    out = _mm(m3_ref[...], y, _NN, P_DFT)
    o_ref[...] = (out / asum_ref[...] + z * bias_ref[...]) * x0_ref[...]


def _conv_direct_call(m1, m3, z, filt, x0, bias, asum):
    args = (m1, m3, z, filt, x0, bias, asum)
    return pl.pallas_call(
        _conv_direct_kernel,
        out_shape=jax.ShapeDtypeStruct(z.shape, F32),
        grid=(1,),
        in_specs=[_full(a.shape) for a in args],
        out_specs=_full(z.shape),
        compiler_params=_cparams(("arbitrary",)),
        name="long_conv_direct",
    )(*args)


DIRECT_CONV_MAX = 256


@functools.lru_cache(maxsize=None)
def _dft_consts_np(n):
    big_n = 2 * n
    if n <= DIRECT_CONV_MAX:
        n1 = big_n
    else:
        n1 = 256 if n >= 8192 else 64
    n2 = big_n // n1
    k1 = np.arange(n1)[:, None].astype(np.float64)
    j1 = np.arange(n1 // 2)[None, :].astype(np.float64)
    ang1 = 2.0 * np.pi * k1 * j1 / n1
    m1 = np.concatenate([np.cos(ang1), -np.sin(ang1)], axis=0)
    o1 = (n1 // 4 + np.arange(n1 // 2))[:, None].astype(np.float64)
    q1 = np.arange(n1)[None, :].astype(np.float64)
    ang3 = 2.0 * np.pi * o1 * q1 / n1
    m3 = np.concatenate([np.cos(ang3), -np.sin(ang3)], axis=1) / big_n
    k2 = np.arange(n2)[:, None].astype(np.float64)
    j2 = np.arange(n2)[None, :].astype(np.float64)
    ang2 = 2.0 * np.pi * k2 * j2 / n2
    angt = 2.0 * np.pi * np.arange(n1)[:, None].astype(np.float64) * j2 / big_n
    f = lambda a: np.asarray(a, np.float32)
    return dict(n1=n1, n2=n2, m1=f(m1), m3=f(m3), fc=f(np.cos(ang2)), fs=f(np.sin(ang2)),
                twc=f(np.cos(angt))[:, None, :], tws=f(np.sin(angt))[:, None, :])


def _long_conv_call(z, x0, filt, asum, bias):
    n, c = z.shape
    cn = _dft_consts_np(n)
    n1, n2 = cn["n1"], cn["n2"]
    consts = {kk: jnp.asarray(vv) for kk, vv in cn.items() if kk not in ("n1", "n2")}
    if n2 == 1:
        return _conv_direct_call(consts["m1"], consts["m3"], z, filt, x0, bias, asum)
    z3 = z.reshape(n1 // 2, n2, c)
    a_f = _dft1_call(consts["m1"], filt.reshape(n1 // 2, n2, c))
    a_z = _dft1_call(consts["m1"], z3)
    b4 = _dft2_call(consts, a_z, a_f)
    out = _dft3_call(consts["m3"], b4, z3, x0.reshape(n1 // 2, n2, c), bias, asum)
    return out.reshape(n, c)


def _layer_norm(x, g, b):
    mu = jnp.mean(x, axis=-1, keepdims=True)
    xc = x - mu
    var = jnp.mean(xc * xc, axis=-1, keepdims=True)
    return xc * lax.rsqrt(var + LN_EPS) * g + b


def _merge_kernel(yf_ref, yb_ref, bv_ref, lg_ref, ho_ref, pg_ref, x_ref, gx_ref, bx_ref, g2_ref, g_ref,
                  wb_ref, wo_ref, gate_ref, lng_ref, lnb_ref, sh2_ref, sc2_ref, o_ref, hf_ref):
    g = g_ref[...]
    ys = yf_ref[...] + yb_ref[...]
    inv_hd = 1.0 / HEAD_DIM
    mu = _mm_rx(ys, g, 3) * inv_hd
    dd = ys - mu
    var = _mm_rx(dd * dd, g, 3) * inv_hd
    yn = dd * lax.rsqrt(var + GN_EPS) * gx_ref[...] + bx_ref[...]
    gate_r = jnp.dot(_sig(lg_ref[...]).astype(BF), g2_ref[...], preferred_element_type=F32)
    ro = (yn + bv_ref[...]) * gate_r
    br = jnp.dot(ro.astype(BF), wb_ref[0], preferred_element_type=F32)
    bh = jnp.dot(ho_ref[...].astype(BF), wb_ref[1], preferred_element_type=F32)
    sg = _sig(pg_ref[...])
    m = sg[:, :D_MODEL] * br + sg[:, D_MODEL:] * bh
    mix = jnp.dot(m.astype(BF), wo_ref[...], preferred_element_type=F32)
    xn = _layer_norm(ALPHA * x_ref[...] + gate_ref[...] * mix, lng_ref[...], lnb_ref[...])
    o_ref[...] = xn
    hf_ref[...] = xn * (1.0 + sc2_ref[...]) + sh2_ref[...]


def _merge_call(yf, yb, bv, lg, ho, pg, x, mp, l):
    L, d = x.shape
    hw = RWKV_WIDTH
    tb = min(L, 256)
    row = lambda i: (i, 0)
    names = ("lnx_g", "lnx_b", "g2", "G", "w_branch", "w_out", "gate", "ln_g", "ln_b", "sh2", "sc2")
    pargs = [mp[nm] for nm in names]

    def pspec(nm, a):
        if nm in ("g2", "w_branch", "w_out"):
            nd = a.ndim - 1
            return pl.BlockSpec((None,) + a.shape[1:], lambda i: (l,) + (0,) * nd)
        return _full(a.shape)

    return pl.pallas_call(
        _merge_kernel,
        out_shape=(jax.ShapeDtypeStruct((L, d), F32), jax.ShapeDtypeStruct((L, d), F32)),
        grid=(L // tb,),
        in_specs=[pl.BlockSpec((tb, hw), row), pl.BlockSpec((tb, hw), row), pl.BlockSpec((tb, hw), row),
                  pl.BlockSpec((tb, LORA_G), row), pl.BlockSpec((tb, hw), row),
                  pl.BlockSpec((tb, GATE_COLS), row), pl.BlockSpec((tb, d), row)]
                 + [pspec(nm, a) for nm, a in zip(names, pargs)],
        out_specs=(pl.BlockSpec((tb, d), row), pl.BlockSpec((tb, d), row)),
        compiler_params=_cparams(("arbitrary",)),
        name="merge_postnorm",
    )(yf, yb, bv, lg, ho, pg, x, *pargs)


def _router_kernel(hf_ref, wrt_ref, rb_ref, cw_ref, sel_ref):
    lt = _mm(wrt_ref[...], hf_ref[...], _NT, 6)
    rid = lax.broadcasted_iota(jnp.int32, (LANES, 1), 0)
    valid = rid < N_EXPERTS
    lg = jnp.where(valid, lt, -jnp.inf)
    mx = jnp.max(lg, axis=0, keepdims=True)
    ex = jnp.where(valid, jnp.exp(lg - mx), 0.0)
    scores = ex / jnp.sum(ex, axis=0, keepdims=True)
    sel = scores + rb_ref[...]
    s = [sel[e:e + 1, :] for e in range(N_EXPERTS)]
    p = [scores[e:e + 1, :] for e in range(N_EXPERTS)]
    gs = []
    for gi in range(N_GROUPS):
        mem = s[gi * EXPERTS_PER_GROUP:(gi + 1) * EXPERTS_PER_GROUP]
        best = None
        for a in range(EXPERTS_PER_GROUP):
            for b in range(a + 1, EXPERTS_PER_GROUP):
                pair = mem[a] + mem[b]
                best = pair if best is None else jnp.maximum(best, pair)
        gs.append(best)
    bg = jnp.where((gs[0] >= gs[1]) & (gs[0] >= gs[2]) & (gs[0] >= gs[3]), 0,
                   jnp.where((gs[1] >= gs[2]) & (gs[1] >= gs[3]), 1, jnp.where(gs[2] >= gs[3], 2, 3)))
    chosen = []
    for e in range(N_EXPERTS):
        gi = e // EXPERTS_PER_GROUP
        beats = None
        for j in range(gi * EXPERTS_PER_GROUP, (gi + 1) * EXPERTS_PER_GROUP):
            if j == e:
                continue
            cond = (s[j] >= s[e]) if j < e else (s[j] > s[e])
            cnt = jnp.where(cond, 1.0, 0.0)
            beats = cnt if beats is None else beats + cnt
        chosen.append((bg == gi) & (beats < 1.5))
    den = None
    for e in range(N_EXPERTS):
        t = jnp.where(chosen[e], p[e], 0.0)
        den = t if den is None else den + t
    out = jnp.zeros(lt.shape, F32)
    msk = jnp.zeros(lt.shape, F32)
    for e in range(N_EXPERTS):
        hit = (rid == e) & chosen[e]
        out = jnp.where(hit, p[e] / den, out)
        msk = jnp.where(hit, 1.0, msk)
    cw_ref[...] = out.T
    sel_ref[...] = msk.T


def _router_call(hf, wrt_pad, rb_pad):
    L, d = hf.shape
    tb = min(L, 256)
    row = lambda i: (i, 0)
    return pl.pallas_call(
        _router_kernel,
        out_shape=(jax.ShapeDtypeStruct((L, LANES), F32), jax.ShapeDtypeStruct((L, LANES), F32)),
        grid=(L // tb,),
        in_specs=[pl.BlockSpec((tb, d), row), _full(wrt_pad.shape), _full(rb_pad.shape)],
        out_specs=(pl.BlockSpec((tb, LANES), row), pl.BlockSpec((tb, LANES), row)),
        compiler_params=_cparams(("arbitrary",)),
        name="router",
    )(hf, wrt_pad, rb_pad)


MOE_BLK = 256


MOE_RING = 3
COMBINE_TM = 256


def _moe_ffn_kernel(tok_ref, be_ref, hf_hbm, wg_ref, wu_ref, wd_ref, y_ref, xbuf, wbf, gsem, *, nb):
    b = pl.program_id(0)
    nrow = MOE_BLK

    def row_in(idx, sl, r):
        return pltpu.make_async_copy(hf_hbm.at[pl.ds(idx, 1)], xbuf.at[sl, pl.ds(r, 1)], gsem.at[sl])

    @pl.when(b == 0)
    def _():
        for blk in range(MOE_RING - 1):
            for r in range(nrow):
                row_in(tok_ref[blk * nrow + r], blk, r).start(priority=r % 2)

    @pl.when((b == 0) | (be_ref[b] != be_ref[jnp.maximum(b - 1, 0)]))
    def _():
        wbf[0] = wg_ref[...].astype(BF)
        wbf[1] = wu_ref[...].astype(BF)
        wbf[2] = wd_ref[...].astype(BF)

    def step(s):
        s1, s2 = (s + 1) % MOE_RING, (s + 2) % MOE_RING
        for r in range(nrow):
            row_in(0, s, r).wait()
        nxt = (b + 2) * nrow
        for r in range(nrow):
            row_in(tok_ref[nxt + r], s2, r).start(priority=r % 2)
        x = xbuf[s].astype(BF)
        hg = jnp.dot(x, wbf[0], preferred_element_type=F32)
        hu = jnp.dot(x, wbf[1], preferred_element_type=F32)
        act = (hg * _sig(hg) * hu).astype(BF)
        y_ref[...] = jnp.dot(act, wbf[2], preferred_element_type=F32)

        @pl.when(b == nb - 1)
        def _():
            for sl in (s1, s2):
                for r in range(nrow):
                    row_in(0, sl, r).wait()

    ring = lax.rem(b, MOE_RING)
    for s in range(MOE_RING):
        pl.when(ring == s)(functools.partial(step, s))


def _moe_ffn_call(row_tok, block_e, hf_all, wg, wu, wd, l):
    d = hf_all.shape[1]
    nb = block_e.shape[0]
    assert nb >= MOE_RING
    wspec = pl.BlockSpec((None, None, d, d), lambda b, tok, be: (l, be[b], 0, 0))
    return pl.pallas_call(
        functools.partial(_moe_ffn_kernel, nb=nb),
        out_shape=jax.ShapeDtypeStruct((nb * MOE_BLK, d), F32),
        grid_spec=pltpu.PrefetchScalarGridSpec(
            num_scalar_prefetch=2,
            grid=(nb,),
            in_specs=[pl.BlockSpec(memory_space=pl.ANY), wspec, wspec, wspec],
            out_specs=pl.BlockSpec((MOE_BLK, d), lambda b, tok, be: (b, 0)),
            scratch_shapes=[pltpu.VMEM((MOE_RING, MOE_BLK, d), F32), pltpu.VMEM((3, d, d), BF),
                            pltpu.SemaphoreType.DMA((MOE_RING,))]),
        compiler_params=_cparams(("arbitrary",)),
        name="moe_ffn",
    )(row_tok, block_e, hf_all, wg, wu, wd)


def _moe_dispatch(sel, cw):
    t_all = sel.shape[0]
    blk = MOE_BLK
    nb = (2 * t_all) // blk + N_EXPERTS
    nr = nb * blk
    mi = (sel > 0.5).astype(jnp.int32)
    rank = jnp.cumsum(mi, axis=0) - mi
    counts = jnp.sum(mi, axis=0)
    padded = ((counts + blk - 1) // blk) * blk
    pend = jnp.cumsum(padded)
    dest = pend - padded + rank
    kk = jnp.cumsum(mi, axis=1) - mi
    first = (mi > 0) & (kk == 0)
    second = (mi > 0) & (kk == 1)
    d0 = jnp.sum(jnp.where(first, dest, 0), axis=1)
    d1 = jnp.sum(jnp.where(second, dest, 0), axis=1)
    w0 = jnp.sum(jnp.where(first, cw, 0.0), axis=1, keepdims=True)
    w1 = jnp.sum(jnp.where(second, cw, 0.0), axis=1, keepdims=True)
    tok = jnp.arange(t_all, dtype=jnp.int32)
    row_tok = jnp.zeros((nr + (MOE_RING - 1) * blk,), jnp.int32).at[jnp.concatenate([d0, d1])].set(
        jnp.concatenate([tok, tok]), unique_indices=True)
    blk_start = jnp.arange(nb, dtype=jnp.int32)[:, None] * blk
    block_e = jnp.clip(jnp.sum((blk_start >= pend[None, :]).astype(jnp.int32), axis=1), 0, N_EXPERTS - 1)
    spare = jnp.zeros((COMBINE_TM,), jnp.int32)
    return row_tok, block_e, w0, w1, jnp.concatenate([d0, spare]), jnp.concatenate([d1, spare])


def _combine_kernel(d0_ref, d1_ref, x_ref, w0_ref, w1_ref, gate_ref, lng_ref, lnb_ref, yb_hbm, o_ref,
                    ybuf, sem, *, off, nstep):
    i = pl.program_id(0)
    tm = x_ref.shape[0]

    def row(k, idx, sl, r):
        return pltpu.make_async_copy(yb_hbm.at[pl.ds(idx, 1)], ybuf.at[sl, k, pl.ds(r, 1)], sem.at[sl])

    def fetch(tile, sl):
        base = off + tile * tm
        for r in range(tm):
            row(0, d0_ref[base + r], sl, r).start(priority=0)
            row(1, d1_ref[base + r], sl, r).start(priority=1)

    @pl.when(i == 0)
    def _():
        fetch(0, 0)

    def step(s):
        for r in range(tm):
            row(0, 0, s, r).wait()
            row(1, 0, s, r).wait()
        fetch(i + 1, 1 - s)
        y = ybuf[s, 0] * w0_ref[...] + ybuf[s, 1] * w1_ref[...]
        o_ref[...] = _layer_norm(ALPHA * x_ref[...] + gate_ref[...] * y, lng_ref[...], lnb_ref[...])

        @pl.when(i == nstep - 1)
        def _():
            for r in range(tm):
                row(0, 0, 1 - s, r).wait()
                row(1, 0, 1 - s, r).wait()

    par = lax.rem(i, 2)
    for s in range(2):
        pl.when(par == s)(functools.partial(step, s))


def _combine_call(x, yb, d0, d1, w0, w1, row_off, gate, lng, lnb, tm):
    L, d = x.shape
    nstep = L // tm
    offb = row_off // tm
    vec = pl.BlockSpec((1, d), lambda i, a, b: (0, 0))
    col = pl.BlockSpec((tm, 1), lambda i, a, b: (offb + i, 0))
    tile = pl.BlockSpec((tm, d), lambda i, a, b: (i, 0))
    return pl.pallas_call(
        functools.partial(_combine_kernel, off=row_off, nstep=nstep),
        out_shape=jax.ShapeDtypeStruct((L, d), F32),
        grid_spec=pltpu.PrefetchScalarGridSpec(
            num_scalar_prefetch=2,
            grid=(nstep,),
            in_specs=[tile, col, col, vec, vec, vec, pl.BlockSpec(memory_space=pl.ANY)],
            out_specs=tile,
            scratch_shapes=[pltpu.VMEM((2, 2, tm, d), F32), pltpu.SemaphoreType.DMA((2,))]),
        compiler_params=_cparams(("arbitrary",)),
        name="moe_combine",
    )(d0, d1, x, w0, w1, gate, lng, lnb, yb)


def _moe_layer(segs, wrt_pad, rb_pad, lng, lnb, wg, wu, wd, l):
    routed = [_router_call(hf, wrt_pad, rb_pad) for (_, hf, _) in segs]
    cw = jnp.concatenate([r[0][:, :N_EXPERTS] for r in routed], axis=0)
    sel = jnp.concatenate([r[1][:, :N_EXPERTS] for r in routed], axis=0)
    hf_all = jnp.concatenate([hf for (_, hf, _) in segs], axis=0) if len(segs) > 1 else segs[0][1]
    row_tok, block_e, w0, w1, d0, d1 = _moe_dispatch(sel, cw)
    yb = _moe_ffn_call(row_tok, block_e, hf_all, wg, wu, wd, l)
    tm = min(COMBINE_TM, min(x.shape[0] for (x, _, _) in segs))
    outs = []
    off = 0
    for (x, _, gate) in segs:
        assert off % tm == 0 and x.shape[0] % tm == 0
        outs.append(_combine_call(x, yb, d0, d1, w0, w1, off, gate, lng, lnb, tm))
        off += x.shape[0]
    return outs


def _blockdiag2(m):
    z = jnp.zeros_like(m[0])
    return jnp.concatenate([jnp.concatenate([m[0], z], axis=1), jnp.concatenate([z, m[1]], axis=1)], axis=0)


def _mixer(x, mod_row, l, w_in_bf, prm, fp, hy, mp, s0, latent):
    L = x.shape[0]
    sh, sc, gate = mod_row[0], mod_row[1], mod_row[2]
    p_r, p_h, p_g = _inproj_call(x, sh, sc, w_in_bf, l)
    rvk, dirp, lg, bv = _prepare_call(p_r, prm, latent)
    yf, yb, sfin = _scan_call(rvk, dirp, s0)
    filt, asum = _hfilt_call(L, fp)
    z, x0 = _hconv3_call(p_h, hy["conv_w"], hy["conv_b"])
    ho = _long_conv_call(z, x0, filt, asum, hy["bias"])
    mpl = dict(mp)
    mpl["gate"], mpl["sh2"], mpl["sc2"] = gate, mod_row[3], mod_row[4]
    xn, hf = _merge_call(yf, yb, bv, lg, ho, p_g, x, mpl, l)
    return xn, hf, sfin


def kernel(x, c, ctx, c_ctx, w_mod, b_mod, w_in, rwkv_mu, rwkv_w0, rwkv_w2, rwkv_a0, rwkv_a2, rwkv_g2,
           rwkv_k_k, rwkv_k_a, rwkv_r_k, rwkv_lnx_g, rwkv_lnx_b, hy_conv_w, hy_conv_b, hy_f_w1, hy_f_b1,
           hy_f_w2, hy_f_b2, hy_f_w3, hy_f_b3, hy_f_wout, hy_freq, hy_bias, w_branch, w_out, ln_g, ln_b,
           w_router, router_bias, w_gate, w_up, w_down):
    b, n_lat, d = x.shape
    assert b == 1 and d == D_MODEL
    n_ctx = ctx.shape[1]
    depth = w_mod.shape[0]
    hw = RWKV_WIDTH
    xl = x[0]
    xc = ctx[0]

    cc = jnp.concatenate([c[:1], c_ctx[None, :], jnp.zeros((6, d), F32)], axis=0)
    mod = _mod_call(cc, w_mod, b_mod)

    head_of = np.arange(hw) // HEAD_DIM
    G = jnp.asarray((head_of[:, None] == head_of[None, :]).astype(np.float32), dtype=BF)
    bands = jnp.linspace(1e-4, FILTER_BANDS - 1, FILTER_BANDS, dtype=F32)[None, :]
    deltas = jnp.abs(jnp.linspace(HYENA_MIN_DECAY, HYENA_MAX_DECAY, HYENA_WIDTH, dtype=F32))[None, :]
    wr_pad = jnp.pad(w_router.T, ((0, LANES - N_EXPERTS), (0, 0)))
    rb_pad = jnp.pad(router_bias, (0, LANES - N_EXPERTS))[:, None]
    w_in_bf = w_in.astype(BF)
    w_branch_bf = w_branch.astype(BF)
    w_out_bf = w_out.astype(BF)
    g2_bf = rwkv_g2.astype(BF)

    for l in range(depth):
        last = l == depth - 1
        ml = [mod[l, 0:1, j * d:(j + 1) * d] for j in range(6)]
        mc = [mod[l, 1:2, j * d:(j + 1) * d] for j in range(6)]
        prm = dict(mu=rwkv_mu[l][None, :],
                   w0=rwkv_w0[l].reshape(1, 2 * hw), w2=_blockdiag2(rwkv_w2[l]),
                   a0=rwkv_a0[l].reshape(1, 2 * hw), a2=_blockdiag2(rwkv_a2[l]),
                   k_k=rwkv_k_k[l][None, :], k_a=rwkv_k_a[l][None, :], r_k=rwkv_r_k[l][None, :], G=G)
        w1 = hy_f_w1[l]
        fp = dict(bands=bands.T, w1t=w1[0:1].T, w1c=w1[1:1 + FILTER_BANDS].T, w1s=w1[1 + FILTER_BANDS:].T,
                  b1=hy_f_b1[l][:, None], w2=hy_f_w2[l].T, b2=hy_f_b2[l][:, None], w3=hy_f_w3[l].T,
                  b3=hy_f_b3[l][:, None], wout=hy_f_wout[l], freq=hy_freq[l][:, None], deltas=deltas)
        hy = dict(conv_w=hy_conv_w[l], conv_b=hy_conv_b[l][None, :], bias=hy_bias[l][None, :])
        mp = dict(lnx_g=rwkv_lnx_g[l][None, :], lnx_b=rwkv_lnx_b[l][None, :], g2=g2_bf, G=G,
                  w_branch=w_branch_bf, w_out=w_out_bf, ln_g=ln_g[l, 0][None, :], ln_b=ln_b[l, 0][None, :])
        s0 = jnp.zeros((2, HEAD_DIM, hw), F32)
        xc_new, hf_c, s_ctx = _mixer(xc, mc, l, w_in_bf, prm, fp, hy, mp, s0, latent=False)
        xl, hf_l, _ = _mixer(xl, ml, l, w_in_bf, prm, fp, hy, mp, s_ctx, latent=True)
        lng, lnb = ln_g[l, 1][None, :], ln_b[l, 1][None, :]
        if last:
            (xl,) = _moe_layer([(xl, hf_l, ml[5])], wr_pad, rb_pad, lng, lnb, w_gate, w_up, w_down, l)
        else:
            xc, xl = _moe_layer([(xc_new, hf_c, mc[5]), (xl, hf_l, ml[5])], wr_pad, rb_pad, lng, lnb,
                                w_gate, w_up, w_down, l)
    return xl[None]
```

```python
import functools
import math

import numpy as np
import jax
import jax.numpy as jnp
from jax import lax
from jax.experimental import pallas as pl
from jax.experimental.pallas import tpu as pltpu

F32 = jnp.float32
BF = jnp.bfloat16

D_MODEL = 1024
DEPTH = 4
GRID_W = 64
RWKV_WIDTH = 512
HEAD_DIM = 64
RWKV_HEADS = 8
LORA_W = 64
LORA_A = 64
LORA_G = 128
DECAY_SCALE = 0.606531
GN_EPS = 6.4e-4
RWKV_COLS = 3 * RWKV_WIDTH + 2 * LORA_W + 2 * LORA_A + LORA_G
HYENA_WIDTH = 512
HYENA_COLS = 3 * HYENA_WIDTH
FILTER_BANDS = 16
FILTER_HIDDEN = 64
HYENA_MIN_DECAY = math.log(1e-2) / 1.5
HYENA_MAX_DECAY = math.log(1e-2) / 0.3
GATE_COLS = 2 * D_MODEL
PROJ_COLS = RWKV_COLS + HYENA_COLS + GATE_COLS
N_EXPERTS = 16
N_GROUPS = 4
EXPERTS_PER_GROUP = 4
ALPHA = (2 * DEPTH) ** 0.25
LN_EPS = 1e-5

SCAN_CHUNK = 64
LANES = 128
VMEM_LIMIT = 56 * 1024 * 1024

_NN = (((1,), (0,)), ((), ()))
_NT = (((1,), (1,)), ((), ()))
_TN = (((0,), (0,)), ((), ()))


def _sig(x):
    return 1.0 / (1.0 + jnp.exp(-x))


def _parts(a, n):
    out = []
    rem = a
    for i in range(n):
        hi = rem.astype(BF)
        out.append(hi)
        if i + 1 < n:
            rem = rem - hi.astype(F32)
    return out


def _mm(a, b, dn=_NN, passes=1):
    n = {1: 1, 3: 2, 6: 3}[passes]
    pa = _parts(a, n)
    pb = _parts(b, n)
    acc = None
    for i in range(n):
        for j in range(n - i):
            t = lax.dot_general(pa[i], pb[j], dn, preferred_element_type=F32)
            acc = t if acc is None else acc + t
    return acc


def _mm_rx(a, b_exact, n=3, dn=_NN):
    acc = None
    for p in _parts(a, n):
        t = lax.dot_general(p, b_exact, dn, preferred_element_type=F32)
        acc = t if acc is None else acc + t
    return acc


def _mm_lx(a_exact, b, n=3, dn=_NN):
    acc = None
    for p in _parts(b, n):
        t = lax.dot_general(a_exact, p, dn, preferred_element_type=F32)
        acc = t if acc is None else acc + t
    return acc


def _cparams(sem, vmem=VMEM_LIMIT):
    return pltpu.CompilerParams(dimension_semantics=sem, vmem_limit_bytes=vmem)


def _full(shape):
    nd = len(shape)
    return pl.BlockSpec(shape, lambda *_: (0,) * nd)


def _mod_kernel(c_ref, w_ref, b_ref, o_ref):
    cc = c_ref[...]
    s = cc * _sig(cc)
    o_ref[...] = jnp.dot(s.astype(BF), w_ref[...].astype(BF), preferred_element_type=F32) + b_ref[...]


def _mod_call(cc, w_mod, b_mod):
    depth, d, n6 = w_mod.shape
    tn = 1536
    return pl.pallas_call(
        _mod_kernel,
        out_shape=jax.ShapeDtypeStruct((depth, 8, n6), F32),
        grid=(depth, n6 // tn),
        in_specs=[
            pl.BlockSpec((8, d), lambda l, j: (0, 0)),
            pl.BlockSpec((None, d, tn), lambda l, j: (l, 0, j)),
            pl.BlockSpec((None, 1, tn), lambda l, j: (l, 0, j)),
        ],
        out_specs=pl.BlockSpec((None, 8, tn), lambda l, j: (l, 0, j)),
        compiler_params=_cparams(("arbitrary", "arbitrary")),
        name="mod",
    )(cc, w_mod, b_mod.reshape(depth, 1, n6))


def _inproj_kernel(x_ref, sh_ref, sc_ref, w_ref, pr_ref, ph_ref, pg_ref):
    xm = (x_ref[...] * (1.0 + sc_ref[...]) + sh_ref[...]).astype(BF)
    pr_ref[...] = jnp.dot(xm, w_ref[:, :RWKV_COLS], preferred_element_type=F32)
    ph_ref[...] = jnp.dot(xm, w_ref[:, RWKV_COLS:RWKV_COLS + HYENA_COLS], preferred_element_type=F32)
    pg_ref[...] = jnp.dot(xm, w_ref[:, RWKV_COLS + HYENA_COLS:], preferred_element_type=F32)


def _inproj_call(x, sh, sc, w_bf, l):
    L, d = x.shape
    tm = min(L, 256)
    row = lambda i: (i, 0)
    return pl.pallas_call(
        _inproj_kernel,
        out_shape=(jax.ShapeDtypeStruct((L, RWKV_COLS), F32),
                   jax.ShapeDtypeStruct((L, HYENA_COLS), F32),
                   jax.ShapeDtypeStruct((L, GATE_COLS), F32)),
        grid=(L // tm,),
        in_specs=[pl.BlockSpec((tm, d), row), _full((1, d)), _full((1, d)),
                  pl.BlockSpec((None, d, PROJ_COLS), lambda i: (l, 0, 0))],
        out_specs=(pl.BlockSpec((tm, RWKV_COLS), row), pl.BlockSpec((tm, HYENA_COLS), row),
                   pl.BlockSpec((tm, GATE_COLS), row)),
        compiler_params=_cparams(("arbitrary",)),
        name="inproj",
    )(x, sh, sc, w_bf)


def _prepare_kernel(*refs, latent, nblk):
    if latent:
        p_ref, up_ref, dn_ref = refs[:3]
        refs = refs[3:]
    else:
        p_ref = refs[0]
        refs = refs[1:]
    (mu_ref, w0_ref, w2_ref, a0_ref, a2_ref, kk_ref, ka_ref, rk_ref, g_ref,
     rvk_out, dir_out, lg_out, bv_out) = refs
    i = pl.program_id(0)
    p = p_ref[...]
    tb, w = p.shape
    row = lax.broadcasted_iota(jnp.int32, (tb, 1), 0)
    col = lax.broadcasted_iota(jnp.int32, (1, w), 1)
    if latent:
        gw = jnp.bitwise_and(row, GRID_W - 1)
        left = jnp.where(gw == 0, 0.0, pltpu.roll(p, 1, 0))
        right = jnp.where(gw == GRID_W - 1, 0.0, pltpu.roll(p, tb - 1, 0))
        upv = jnp.where(i == 0, 0.0, up_ref[...])
        dnv = jnp.where(i == nblk - 1, 0.0, dn_ref[...])
        if tb > GRID_W:
            up = jnp.concatenate([upv, p[:tb - GRID_W]], axis=0)
            down = jnp.concatenate([p[GRID_W:], dnv], axis=0)
        else:
            up, down = upv, dnv
        q = w // 4
        sh = jnp.where(col < q, left, jnp.where(col < 2 * q, right, jnp.where(col < 3 * q, up, down)))
    else:
        prev = jnp.where(row == 0, 0.0, pltpu.roll(p, 1, 0))
        nxt = jnp.where(row == tb - 1, 0.0, pltpu.roll(p, tb - 1, 0))
        sh = jnp.where(col < w // 2, prev, nxt)
    pm = p + (sh - p) * mu_ref[...]
    hw = RWKV_WIDTH
    r = pm[:, :hw]
    k = pm[:, hw:2 * hw]
    v = pm[:, 2 * hw:3 * hw]
    lw_in = jnp.tanh(pm[:, 3 * hw:3 * hw + 2 * LORA_W])
    la_in = pm[:, 3 * hw + 2 * LORA_W:3 * hw + 2 * LORA_W + 2 * LORA_A]
    lg = pm[:, 3 * hw + 2 * LORA_W + 2 * LORA_A:]
    lw = -DECAY_SCALE * _sig(w0_ref[...] + _mm(lw_in, w2_ref[...], passes=3))
    a = _sig(a0_ref[...] + _mm(la_in, a2_ref[...], passes=3))
    g = g_ref[...]
    kkr = k * kk_ref[...]
    nrm = jnp.sqrt(_mm_rx(kkr * kkr, g, 2))
    kk = kkr / jnp.maximum(nrm, 1e-12)
    k_a = ka_ref[...]
    k_f = k * (1.0 + (a[:, :hw] - 1.0) * k_a)
    k_b = k * (1.0 + (a[:, hw:] - 1.0) * k_a)
    bonus = _mm_rx(r * (k_f + k_b) * rk_ref[...], g, 2)
    rvk_out[:, :hw] = r
    rvk_out[:, hw:2 * hw] = v
    rvk_out[:, 2 * hw:] = -kk
    for d, (k_d, a_d) in enumerate(((k_f, a[:, :hw]), (k_b, a[:, hw:]))):
        dir_out[d, :, :hw] = lw[:, d * hw:(d + 1) * hw]
        dir_out[d, :, hw:2 * hw] = k_d
        dir_out[d, :, 2 * hw:] = kk * a_d
    lg_out[...] = lg
    bv_out[...] = bonus * v


def _prepare_call(p_r, prm, latent):
    L, w = p_r.shape
    hw = RWKV_WIDTH
    tb = 256 if latent else L
    nblk = L // tb
    row = lambda i: (i, 0)
    row3 = lambda i: (0, i, 0)
    in_specs = [pl.BlockSpec((tb, w), row)]
    args = [p_r]
    if latent:
        per = tb // GRID_W
        nrow = L // GRID_W
        in_specs += [
            pl.BlockSpec((GRID_W, w), lambda i: (jnp.maximum(i * per - 1, 0), 0)),
            pl.BlockSpec((GRID_W, w), lambda i: (jnp.minimum((i + 1) * per, nrow - 1), 0)),
        ]
        args += [p_r, p_r]
    names = ("mu", "w0", "w2", "a0", "a2", "k_k", "k_a", "r_k", "G")
    for nm in names:
        in_specs.append(_full(prm[nm].shape))
        args.append(prm[nm])
    sd = jax.ShapeDtypeStruct
    out_shape = (sd((L, 3 * hw), F32), sd((2, L, 3 * hw), F32), sd((L, LORA_G), F32), sd((L, hw), F32))
    out_specs = (pl.BlockSpec((tb, 3 * hw), row), pl.BlockSpec((2, tb, 3 * hw), row3),
                 pl.BlockSpec((tb, LORA_G), row), pl.BlockSpec((tb, hw), row))
    return pl.pallas_call(
        functools.partial(_prepare_kernel, latent=latent, nblk=nblk),
        out_shape=out_shape, grid=(nblk,), in_specs=in_specs, out_specs=out_specs,
        compiler_params=_cparams(("arbitrary",)),
        name="rwkv_prepare_lat" if latent else "rwkv_prepare_ctx",
    )(*args)


P_SCORE = 1
P_TINV = 1
P_APPLY = 1
P_STATE = 1


SCAN_SUB = 2


def _scan_chunk(fin, bin_, states, masks):
    C = SCAN_CHUNK
    hd = HEAD_DIM
    nh = RWKV_HEADS
    row, col, eye, lvl_masks = masks
    dirs = []
    for d, (r, v, ka, lw, k, kb) in enumerate((fin, bin_)):
        inc = (row >= col) if d == 0 else (row <= col)
        strict = (row > col) if d == 0 else (row < col)
        lc = _mm_lx(jnp.where(inc, 1.0, 0.0).astype(BF), lw, 3)
        e_neg = jnp.exp(-lc)
        e_tot = jnp.exp(jnp.sum(lw, axis=0, keepdims=True))
        kbn = kb * e_neg
        kkn = k * e_neg
        dirs.append(dict(inc=inc, strict=strict, v=v, aq=ka * jnp.exp(lc - lw), rq=r * jnp.exp(lc),
                         kbn=kbn, kkn=kkn, kbp=kbn * e_tot, kkp=kkn * e_tot, e_tot=e_tot, s=states[d]))

    units = [(d, h) for h in range(nh) for d in range(2)]
    hs = lambda arr, h: arr[:, h * hd:(h + 1) * hd]
    sc = {}
    for (d, h) in units:
        D = dirs[d]
        sc[d, h] = _mm(jnp.concatenate([hs(D["aq"], h), hs(D["rq"], h)], axis=0),
                       jnp.concatenate([hs(D["kbn"], h), hs(D["kkn"], h)], axis=0), _NT, P_SCORE)
    a_ab, t, x = {}, {}, {}
    for u in units:
        a_ab[u] = jnp.where(dirs[u[0]]["strict"], sc[u][:C, :C], 0.0)
        t[u] = eye + jnp.where(lvl_masks[0], a_ab[u], 0.0)
    for u in units:
        D = dirs[u[0]]
        lhs = jnp.concatenate([jnp.where(D["strict"], sc[u][:C, C:], 0.0),
                               jnp.where(D["inc"], sc[u][C:, C:], 0.0)], axis=0)
        x[u] = _mm(lhs, hs(D["v"], u[1]), _NN, P_APPLY)
    for m in lvl_masks[1:]:
        tmp = {u: _mm(jnp.where(m, a_ab[u], 0.0), t[u], _NN, P_TINV) for u in units}
        for u in units:
            t[u] = t[u] + _mm(t[u], tmp[u], _NN, P_TINV)
    z, gm, uu, yy, sn = {}, {}, {}, {}, {}
    for u in units:
        z[u] = _mm(t[u], jnp.concatenate([hs(dirs[u[0]]["aq"], u[1]), x[u][:C]], axis=1), _NN, P_APPLY)
    for u in units:
        D = dirs[u[0]]
        gm[u] = _mm(jnp.concatenate([z[u][:, :hd], hs(D["rq"], u[1])], axis=0), hs(D["s"], u[1]), _NT, P_STATE)
        uu[u] = gm[u][:C] + z[u][:, hd:]
    for u in units:
        D = dirs[u[0]]
        b_rb = jnp.where(D["inc"], sc[u][C:, :C], 0.0)
        yy[u] = gm[u][C:] + _mm(b_rb, uu[u], _NN, P_APPLY) + x[u][C:]
        sn[u] = hs(D["s"], u[1]) * hs(D["e_tot"], u[1]) + _mm(
            jnp.concatenate([uu[u], hs(D["v"], u[1])], axis=0),
            jnp.concatenate([hs(D["kbp"], u[1]), hs(D["kkp"], u[1])], axis=0), _TN, P_STATE)
    cat = lambda d, src: jnp.concatenate([src[d, h] for h in range(nh)], axis=1)
    return cat(0, yy), cat(1, yy), [cat(0, sn), cat(1, sn)]


def _scan_kernel(sf_ref, sb_ref, df_ref, db_ref, s0_ref, yf_ref, yb_ref, sfin_ref, s_scr, *, nstep):
    c = pl.program_id(0)
    C = SCAN_CHUNK

    @pl.when(c == 0)
    def _():
        s_scr[...] = s0_ref[...]

    row = lax.broadcasted_iota(jnp.int32, (C, C), 0)
    col = lax.broadcasted_iota(jnp.int32, (C, C), 1)
    eye = jnp.where(row == col, 1.0, 0.0)
    lvl_masks = []
    for sh in range(int(math.log2(C))):
        same2 = jnp.right_shift(row, sh + 1) == jnp.right_shift(col, sh + 1)
        same1 = jnp.right_shift(row, sh) == jnp.right_shift(col, sh)
        lvl_masks.append(jnp.logical_and(same2, jnp.logical_not(same1)))
    masks = (row, col, eye, lvl_masks)

    hw = RWKV_WIDTH
    unpack = lambda sref, dref, rows: tuple(ref[rows, i * hw:(i + 1) * hw] for ref in (sref, dref) for i in range(3))
    states = [s_scr[0], s_scr[1]]
    for j in range(SCAN_SUB):
        fsl = slice(j * C, (j + 1) * C)
        bsl = slice((SCAN_SUB - 1 - j) * C, (SCAN_SUB - j) * C)
        yf, yb, states = _scan_chunk(unpack(sf_ref, df_ref, fsl), unpack(sb_ref, db_ref, bsl), states, masks)
        yf_ref[fsl, :] = yf
        yb_ref[bsl, :] = yb
    s_scr[0] = states[0]
    s_scr[1] = states[1]

    @pl.when(c == nstep - 1)
    def _():
        sfin_ref[...] = s_scr[...]


def _scan_call(rvk, dirp, s0):
    L = rvk.shape[0]
    hw = RWKV_WIDTH
    C = SCAN_CHUNK * SCAN_SUB
    nchunk = L // C
    assert L % C == 0
    sh_f = pl.BlockSpec((C, 3 * hw), lambda c: (c, 0))
    sh_b = pl.BlockSpec((C, 3 * hw), lambda c: (nchunk - 1 - c, 0))
    pd_f = pl.BlockSpec((None, C, 3 * hw), lambda c: (0, c, 0))
    pd_b = pl.BlockSpec((None, C, 3 * hw), lambda c: (1, nchunk - 1 - c, 0))
    y_f = pl.BlockSpec((C, hw), lambda c: (c, 0))
    y_b = pl.BlockSpec((C, hw), lambda c: (nchunk - 1 - c, 0))
    state = _full((2, HEAD_DIM, hw))
    return pl.pallas_call(
        functools.partial(_scan_kernel, nstep=nchunk),
        out_shape=(jax.ShapeDtypeStruct((L, hw), F32), jax.ShapeDtypeStruct((L, hw), F32),
                   jax.ShapeDtypeStruct((2, HEAD_DIM, hw), F32)),
        grid=(nchunk,),
        in_specs=[sh_f, sh_b, pd_f, pd_b, state],
        out_specs=(y_f, y_b, state),
        scratch_shapes=[pltpu.VMEM((2, HEAD_DIM, hw), F32)],
        compiler_params=_cparams(("arbitrary",)),
        name="delta_scan",
    )(rvk, rvk, dirp, dirp, s0)


def _hfilt_kernel(bands_ref, w1t_ref, w1c_ref, w1s_ref, b1_ref, w2_ref, b2_ref, w3_ref, b3_ref, wo_ref,
                  fr_ref, dl_ref, f_ref, asum_ref, *, n, rb):
    i = pl.program_id(0)
    posr = (i * rb + lax.broadcasted_iota(jnp.int32, (1, rb), 1)).astype(F32)
    tr = posr / float(max(n - 1, 1))
    ang = ((2.0 * math.pi / n) * posr) * bands_ref[...]
    fr = fr_ref[...]
    h = (w1t_ref[...] * tr + _mm(w1c_ref[...], jnp.cos(ang), passes=3)
         + _mm(w1s_ref[...], -jnp.sin(ang), passes=3))
    h = jnp.sin(fr * (h + b1_ref[...]))
    h = jnp.sin(fr * (_mm(w2_ref[...], h, passes=3) + b2_ref[...]))
    h = jnp.sin(fr * (_mm(w3_ref[...], h, passes=3) + b3_ref[...]))
    filt = _mm(h, wo_ref[...], _TN, 3)
    pos = (i * rb + lax.broadcasted_iota(jnp.int32, (rb, 1), 0)).astype(F32)
    dist = jnp.abs(pos - float(n // 2)) * (2.0 / n)
    filt = filt * jnp.exp(-dist * dl_ref[...])
    f_ref[...] = filt

    @pl.when(i == 0)
    def _():
        asum_ref[...] = jnp.zeros_like(asum_ref)

    asum_ref[...] += jnp.sum(jnp.abs(filt), axis=0, keepdims=True)


def _hfilt_call(n, fp):
    rb = min(n, 512)
    names = ("bands", "w1t", "w1c", "w1s", "b1", "w2", "b2", "w3", "b3", "wout", "freq", "deltas")
    args = [fp[nm] for nm in names]
    return pl.pallas_call(
        functools.partial(_hfilt_kernel, n=n, rb=rb),
        out_shape=(jax.ShapeDtypeStruct((n, HYENA_WIDTH), F32), jax.ShapeDtypeStruct((1, HYENA_WIDTH), F32)),
        grid=(n // rb,),
        in_specs=[_full(a.shape) for a in args],
        out_specs=(pl.BlockSpec((rb, HYENA_WIDTH), lambda i: (i, 0)), _full((1, HYENA_WIDTH))),
        compiler_params=_cparams(("arbitrary",)),
        name="hyena_filter",
    )(*args)


def _hconv3_kernel(p_ref, pv_ref, nx_ref, cw_ref, cb_ref, z_ref, x0_ref, *, nblk):
    i = pl.program_id(0)
    p = p_ref[...]
    tb = p.shape[0]
    row = lax.broadcasted_iota(jnp.int32, (tb, 1), 0)
    prev_row = jnp.where(i == 0, 0.0, pv_ref[7:8, :])
    next_row = jnp.where(i == nblk - 1, 0.0, nx_ref[0:1, :])
    sp = jnp.where(row == 0, prev_row, pltpu.roll(p, 1, 0))
    sn = jnp.where(row == tb - 1, next_row, pltpu.roll(p, tb - 1, 0))
    u = sp * cw_ref[0:1, :] + p * cw_ref[1:2, :] + sn * cw_ref[2:3, :] + cb_ref[...]
    hw = HYENA_WIDTH
    z_ref[...] = u[:, 2 * hw:] * u[:, hw:2 * hw]
    x0_ref[...] = u[:, :hw]


def _hconv3_call(p_h, cw, cb):
    L, w = p_h.shape
    tb = min(L, 256)
    nblk = L // tb
    per = tb // 8
    row = lambda i: (i, 0)
    return pl.pallas_call(
        functools.partial(_hconv3_kernel, nblk=nblk),
        out_shape=(jax.ShapeDtypeStruct((L, HYENA_WIDTH), F32), jax.ShapeDtypeStruct((L, HYENA_WIDTH), F32)),
        grid=(nblk,),
        in_specs=[pl.BlockSpec((tb, w), row),
                  pl.BlockSpec((8, w), lambda i: (jnp.maximum(i * per - 1, 0), 0)),
                  pl.BlockSpec((8, w), lambda i: (jnp.minimum((i + 1) * per, L // 8 - 1), 0)),
                  _full(cw.shape), _full(cb.shape)],
        out_specs=(pl.BlockSpec((tb, HYENA_WIDTH), row), pl.BlockSpec((tb, HYENA_WIDTH), row)),
        compiler_params=_cparams(("arbitrary",)),
        name="hyena_conv3",
    )(p_h, p_h, p_h, cw, cb)


P_DFT = 1


DFT_NB = 8


def _dft1_kernel(m_ref, x_ref, o_ref):
    m = m_ref[...]
    n1 = o_ref.shape[2]
    for j in range(x_ref.shape[1]):
        r = _mm(m, x_ref[:, j, :], _NN, P_DFT)
        o_ref[0, j] = r[:n1]
        o_ref[1, j] = r[n1:]


def _dft1_call(m1, x3):
    rows, kdim = m1.shape
    _, n2, c = x3.shape
    n1 = rows // 2
    nbk = min(n2, DFT_NB)
    return pl.pallas_call(
        _dft1_kernel,
        out_shape=jax.ShapeDtypeStruct((2, n2, n1, c), F32),
        grid=(n2 // nbk,),
        in_specs=[_full(m1.shape), pl.BlockSpec((kdim, nbk, c), lambda j: (0, j, 0))],
        out_specs=pl.BlockSpec((2, nbk, n1, c), lambda j: (0, j, 0, 0)),
        compiler_params=_cparams(("arbitrary",)),
        name="dft_stage1",
    )(m1, x3)


def _dft2_matrix(fc, fs, tc, ts):
    fre = fc * tc - fs * ts
    fim = -(fc * ts + fs * tc)
    return jnp.concatenate([jnp.concatenate([fre, -fim], axis=1),
                            jnp.concatenate([fim, fre], axis=1)], axis=0)


def _dft2_conv_kernel(fc_ref, fs_ref, tc_ref, ts_ref, a_ref, af_ref, b_ref):
    fc, fs = fc_ref[...], fs_ref[...]
    n2, c = a_ref.shape[1], a_ref.shape[3]
    for kk in range(a_ref.shape[2]):
        big = _dft2_matrix(fc, fs, tc_ref[kk], ts_ref[kk])
        rhs = jnp.concatenate([jnp.concatenate([a_ref[0, :, kk, :], af_ref[0, :, kk, :]], axis=1),
                               jnp.concatenate([a_ref[1, :, kk, :], af_ref[1, :, kk, :]], axis=1)], axis=0)
        xh = _mm(big, rhs, _NN, P_DFT)
        xr, hr = xh[:n2, :c], xh[:n2, c:]
        xi, hi = xh[n2:, :c], xh[n2:, c:]
        y = jnp.concatenate([xr * hr - xi * hi, xr * hi + xi * hr], axis=0)
        bb = _mm(big, y, _TN, P_DFT)
        b_ref[0, kk] = bb[:n2]
        b_ref[1, kk] = bb[n2:]


def _dft2_call(consts, a4, af4):
    _, n2, n1, c = a4.shape
    kb = min(n1, DFT_NB)
    blk = pl.BlockSpec((2, n2, kb, c), lambda q: (0, 0, q, 0))
    tw = pl.BlockSpec((kb, 1, n2), lambda q: (q, 0, 0))
    return pl.pallas_call(
        _dft2_conv_kernel,
        out_shape=jax.ShapeDtypeStruct((2, n1, n2, c), F32),
        grid=(n1 // kb,),
        in_specs=[_full((n2, n2)), _full((n2, n2)), tw, tw, blk, blk],
        out_specs=pl.BlockSpec((2, kb, n2, c), lambda q: (0, q, 0, 0)),
        compiler_params=_cparams(("arbitrary",)),
        name="dft_stage2_conv",
    )(consts["fc"], consts["fs"], consts["twc"], consts["tws"], a4, af4)


def _dft3_kernel(m_ref, b_ref, z_ref, x0_ref, bias_ref, asum_ref, o_ref):
    m = m_ref[...]
    for j in range(z_ref.shape[1]):
        bj = jnp.concatenate([b_ref[0, :, j, :], b_ref[1, :, j, :]], axis=0)
        y = _mm(m, bj, _NN, P_DFT)
        o_ref[:, j, :] = (y / asum_ref[...] + z_ref[:, j, :] * bias_ref[...]) * x0_ref[:, j, :]


def _dft3_call(m3, b4, z3, x03, bias, asum):
    rows, _ = m3.shape
    _, n1, n2, c = b4.shape
    nbk = min(n2, DFT_NB)
    slab = pl.BlockSpec((rows, nbk, c), lambda j: (0, j, 0))
    return pl.pallas_call(
        _dft3_kernel,
        out_shape=jax.ShapeDtypeStruct((rows, n2, c), F32),
        grid=(n2 // nbk,),
        in_specs=[_full(m3.shape), pl.BlockSpec((2, n1, nbk, c), lambda j: (0, 0, j, 0)), slab, slab,
                  _full((1, c)), _full((1, c))],
        out_specs=slab,
        compiler_params=_cparams(("arbitrary",)),
        name="dft_stage3",
    )(m3, b4, z3, x03, bias, asum)


def _conv_direct_kernel(m1_ref, m3_ref, z_ref, f_ref, x0_ref, bias_ref, asum_ref, o_ref):
    m1 = m1_ref[...]
    z = z_ref[...]
    a = _mm(m1, z, _NN, P_DFT)
    h = _mm(m1, f_ref[...], _NN, P_DFT)
    nn = a.shape[0] // 2
    ar, ai, hr, hi = a[:nn], a[nn:], h[:nn], h[nn:]
    y = jnp.concatenate([ar * hr - ai * hi, ar * hi + ai * hr], axis=0)
    out = _mm(m3_ref[...], y, _NN, P_DFT)
    o_ref[...] = (out / asum_ref[...] + z * bias_ref[...]) * x0_ref[...]


def _conv_direct_call(m1, m3, z, filt, x0, bias, asum):
    args = (m1, m3, z, filt, x0, bias, asum)
    return pl.pallas_call(
        _conv_direct_kernel,
        out_shape=jax.ShapeDtypeStruct(z.shape, F32),
        grid=(1,),
        in_specs=[_full(a.shape) for a in args],
        out_specs=_full(z.shape),
        compiler_params=_cparams(("arbitrary",)),
        name="long_conv_direct",
    )(*args)


DIRECT_CONV_MAX = 256


@functools.lru_cache(maxsize=None)
def _dft_consts_np(n):
    big_n = 2 * n
    if n <= DIRECT_CONV_MAX:
        n1 = big_n
    else:
        n1 = 256 if n >= 8192 else 64
    n2 = big_n // n1
    k1 = np.arange(n1)[:, None].astype(np.float64)
    j1 = np.arange(n1 // 2)[None, :].astype(np.float64)
    ang1 = 2.0 * np.pi * k1 * j1 / n1
    m1 = np.concatenate([np.cos(ang1), -np.sin(ang1)], axis=0)
    o1 = (n1 // 4 + np.arange(n1 // 2))[:, None].astype(np.float64)
    q1 = np.arange(n1)[None, :].astype(np.float64)
    ang3 = 2.0 * np.pi * o1 * q1 / n1
    m3 = np.concatenate([np.cos(ang3), -np.sin(ang3)], axis=1) / big_n
    k2 = np.arange(n2)[:, None].astype(np.float64)
    j2 = np.arange(n2)[None, :].astype(np.float64)
    ang2 = 2.0 * np.pi * k2 * j2 / n2
    angt = 2.0 * np.pi * np.arange(n1)[:, None].astype(np.float64) * j2 / big_n
    f = lambda a: np.asarray(a, np.float32)
    return dict(n1=n1, n2=n2, m1=f(m1), m3=f(m3), fc=f(np.cos(ang2)), fs=f(np.sin(ang2)),
                twc=f(np.cos(angt))[:, None, :], tws=f(np.sin(angt))[:, None, :])


def _long_conv_call(z, x0, filt, asum, bias):
    n, c = z.shape
    cn = _dft_consts_np(n)
    n1, n2 = cn["n1"], cn["n2"]
    consts = {kk: jnp.asarray(vv) for kk, vv in cn.items() if kk not in ("n1", "n2")}
    if n2 == 1:
        return _conv_direct_call(consts["m1"], consts["m3"], z, filt, x0, bias, asum)
    z3 = z.reshape(n1 // 2, n2, c)
    a_f = _dft1_call(consts["m1"], filt.reshape(n1 // 2, n2, c))
    a_z = _dft1_call(consts["m1"], z3)
    b4 = _dft2_call(consts, a_z, a_f)
    out = _dft3_call(consts["m3"], b4, z3, x0.reshape(n1 // 2, n2, c), bias, asum)
    return out.reshape(n, c)


def _layer_norm(x, g, b):
    mu = jnp.mean(x, axis=-1, keepdims=True)
    xc = x - mu
    var = jnp.mean(xc * xc, axis=-1, keepdims=True)
    return xc * lax.rsqrt(var + LN_EPS) * g + b


def _merge_kernel(yf_ref, yb_ref, bv_ref, lg_ref, ho_ref, pg_ref, x_ref, gx_ref, bx_ref, g2_ref, g_ref,
                  wb_ref, wo_ref, gate_ref, lng_ref, lnb_ref, sh2_ref, sc2_ref, o_ref, hf_ref):
    g = g_ref[...]
    ys = yf_ref[...] + yb_ref[...]
    inv_hd = 1.0 / HEAD_DIM
    mu = _mm_rx(ys, g, 2) * inv_hd
    dd = ys - mu
    var = _mm_rx(dd * dd, g, 2) * inv_hd
    yn = dd * lax.rsqrt(var + GN_EPS) * gx_ref[...] + bx_ref[...]
    gate_r = jnp.dot(_sig(lg_ref[...]).astype(BF), g2_ref[...], preferred_element_type=F32)
    ro = (yn + bv_ref[...]) * gate_r
    br = jnp.dot(ro.astype(BF), wb_ref[0], preferred_element_type=F32)
    bh = jnp.dot(ho_ref[...].astype(BF), wb_ref[1], preferred_element_type=F32)
    sg = _sig(pg_ref[...])
    m = sg[:, :D_MODEL] * br + sg[:, D_MODEL:] * bh
    mix = jnp.dot(m.astype(BF), wo_ref[...], preferred_element_type=F32)
    xn = _layer_norm(ALPHA * x_ref[...] + gate_ref[...] * mix, lng_ref[...], lnb_ref[...])
    o_ref[...] = xn
    hf_ref[...] = xn * (1.0 + sc2_ref[...]) + sh2_ref[...]


def _merge_call(yf, yb, bv, lg, ho, pg, x, mp, l):
    L, d = x.shape
    hw = RWKV_WIDTH
    tb = min(L, 256)
    row = lambda i: (i, 0)
    names = ("lnx_g", "lnx_b", "g2", "G", "w_branch", "w_out", "gate", "ln_g", "ln_b", "sh2", "sc2")
    pargs = [mp[nm] for nm in names]

    def pspec(nm, a):
        if nm in ("g2", "w_branch", "w_out"):
            nd = a.ndim - 1
            return pl.BlockSpec((None,) + a.shape[1:], lambda i: (l,) + (0,) * nd)
        return _full(a.shape)

    return pl.pallas_call(
        _merge_kernel,
        out_shape=(jax.ShapeDtypeStruct((L, d), F32), jax.ShapeDtypeStruct((L, d), F32)),
        grid=(L // tb,),
        in_specs=[pl.BlockSpec((tb, hw), row), pl.BlockSpec((tb, hw), row), pl.BlockSpec((tb, hw), row),
                  pl.BlockSpec((tb, LORA_G), row), pl.BlockSpec((tb, hw), row),
                  pl.BlockSpec((tb, GATE_COLS), row), pl.BlockSpec((tb, d), row)]
                 + [pspec(nm, a) for nm, a in zip(names, pargs)],
        out_specs=(pl.BlockSpec((tb, d), row), pl.BlockSpec((tb, d), row)),
        compiler_params=_cparams(("arbitrary",)),
        name="merge_postnorm",
    )(yf, yb, bv, lg, ho, pg, x, *pargs)


def _router_kernel(hf_ref, wrt_ref, rb_ref, cw_ref, sel_ref):
    lt = _mm(wrt_ref[...], hf_ref[...], _NT, 6)
    rid = lax.broadcasted_iota(jnp.int32, (LANES, 1), 0)
    valid = rid < N_EXPERTS
    lg = jnp.where(valid, lt, -jnp.inf)
    mx = jnp.max(lg, axis=0, keepdims=True)
    ex = jnp.where(valid, jnp.exp(lg - mx), 0.0)
    scores = ex / jnp.sum(ex, axis=0, keepdims=True)
    sel = scores + rb_ref[...]
    s = [sel[e:e + 1, :] for e in range(N_EXPERTS)]
    p = [scores[e:e + 1, :] for e in range(N_EXPERTS)]
    gs = []
    for gi in range(N_GROUPS):
        mem = s[gi * EXPERTS_PER_GROUP:(gi + 1) * EXPERTS_PER_GROUP]
        best = None
        for a in range(EXPERTS_PER_GROUP):
            for b in range(a + 1, EXPERTS_PER_GROUP):
                pair = mem[a] + mem[b]
                best = pair if best is None else jnp.maximum(best, pair)
        gs.append(best)
    bg = jnp.where((gs[0] >= gs[1]) & (gs[0] >= gs[2]) & (gs[0] >= gs[3]), 0,
                   jnp.where((gs[1] >= gs[2]) & (gs[1] >= gs[3]), 1, jnp.where(gs[2] >= gs[3], 2, 3)))
    chosen = []
    for e in range(N_EXPERTS):
        gi = e // EXPERTS_PER_GROUP
        beats = None
        for j in range(gi * EXPERTS_PER_GROUP, (gi + 1) * EXPERTS_PER_GROUP):
            if j == e:
                continue
            cond = (s[j] >= s[e]) if j < e else (s[j] > s[e])
            cnt = jnp.where(cond, 1.0, 0.0)
            beats = cnt if beats is None else beats + cnt
        chosen.append((bg == gi) & (beats < 1.5))
    den = None
    for e in range(N_EXPERTS):
        t = jnp.where(chosen[e], p[e], 0.0)
        den = t if den is None else den + t
    out = jnp.zeros(lt.shape, F32)
    msk = jnp.zeros(lt.shape, F32)
    for e in range(N_EXPERTS):
        hit = (rid == e) & chosen[e]
        out = jnp.where(hit, p[e] / den, out)
        msk = jnp.where(hit, 1.0, msk)
    cw_ref[...] = out.T
    sel_ref[...] = msk.T


def _router_call(hf, wrt_pad, rb_pad):
    L, d = hf.shape
    tb = min(L, 256)
    row = lambda i: (i, 0)
    return pl.pallas_call(
        _router_kernel,
        out_shape=(jax.ShapeDtypeStruct((L, LANES), F32), jax.ShapeDtypeStruct((L, LANES), F32)),
        grid=(L // tb,),
        in_specs=[pl.BlockSpec((tb, d), row), _full(wrt_pad.shape), _full(rb_pad.shape)],
        out_specs=(pl.BlockSpec((tb, LANES), row), pl.BlockSpec((tb, LANES), row)),
        compiler_params=_cparams(("arbitrary",)),
        name="router",
    )(hf, wrt_pad, rb_pad)


MOE_BLK = 512


MOE_RING = 3
COMBINE_TM = 256


def _moe_ffn_kernel(tok_ref, be_ref, hf_hbm, wg_ref, wu_ref, wd_ref, y_ref, xbuf, wbf, gsem, *, nb):
    b = pl.program_id(0)
    nrow = MOE_BLK

    def row_in(idx, sl, r):
        return pltpu.make_async_copy(hf_hbm.at[pl.ds(idx, 1)], xbuf.at[sl, pl.ds(r, 1)], gsem.at[sl])

    @pl.when(b == 0)
    def _():
        for blk in range(MOE_RING - 1):
            for r in range(nrow):
                row_in(tok_ref[blk * nrow + r], blk, r).start(priority=r % 2)

    @pl.when((b == 0) | (be_ref[b] != be_ref[jnp.maximum(b - 1, 0)]))
    def _():
        wbf[0] = wg_ref[...].astype(BF)
        wbf[1] = wu_ref[...].astype(BF)
        wbf[2] = wd_ref[...].astype(BF)

    def step(s):
        s1, s2 = (s + 1) % MOE_RING, (s + 2) % MOE_RING
        for r in range(nrow):
            row_in(0, s, r).wait()
        nxt = (b + 2) * nrow
        for r in range(nrow):
            row_in(tok_ref[nxt + r], s2, r).start(priority=r % 2)
        x = xbuf[s].astype(BF)
        hg = jnp.dot(x, wbf[0], preferred_element_type=F32)
        hu = jnp.dot(x, wbf[1], preferred_element_type=F32)
        act = (hg * _sig(hg) * hu).astype(BF)
        y_ref[...] = jnp.dot(act, wbf[2], preferred_element_type=F32)

        @pl.when(b == nb - 1)
        def _():
            for sl in (s1, s2):
                for r in range(nrow):
                    row_in(0, sl, r).wait()

    ring = lax.rem(b, MOE_RING)
    for s in range(MOE_RING):
        pl.when(ring == s)(functools.partial(step, s))


def _moe_ffn_call(row_tok, block_e, hf_all, wg, wu, wd, l):
    d = hf_all.shape[1]
    nb = block_e.shape[0]
    assert nb >= MOE_RING
    wspec = pl.BlockSpec((None, None, d, d), lambda b, tok, be: (l, be[b], 0, 0))
    return pl.pallas_call(
        functools.partial(_moe_ffn_kernel, nb=nb),
        out_shape=jax.ShapeDtypeStruct((nb * MOE_BLK, d), F32),
        grid_spec=pltpu.PrefetchScalarGridSpec(
            num_scalar_prefetch=2,
            grid=(nb,),
            in_specs=[pl.BlockSpec(memory_space=pl.ANY), wspec, wspec, wspec],
            out_specs=pl.BlockSpec((MOE_BLK, d), lambda b, tok, be: (b, 0)),
            scratch_shapes=[pltpu.VMEM((MOE_RING, MOE_BLK, d), F32), pltpu.VMEM((3, d, d), BF),
                            pltpu.SemaphoreType.DMA((MOE_RING,))]),
        compiler_params=_cparams(("arbitrary",)),
        name="moe_ffn",
    )(row_tok, block_e, hf_all, wg, wu, wd)


def _moe_dispatch(sel, cw):
    t_all = sel.shape[0]
    blk = MOE_BLK
    nb = (2 * t_all) // blk + N_EXPERTS
    nr = nb * blk
    mi = (sel > 0.5).astype(jnp.int32)
    rank = jnp.cumsum(mi, axis=0) - mi
    counts = jnp.sum(mi, axis=0)
    padded = ((counts + blk - 1) // blk) * blk
    pend = jnp.cumsum(padded)
    dest = pend - padded + rank
    kk = jnp.cumsum(mi, axis=1) - mi
    first = (mi > 0) & (kk == 0)
    second = (mi > 0) & (kk == 1)
    d0 = jnp.sum(jnp.where(first, dest, 0), axis=1)
    d1 = jnp.sum(jnp.where(second, dest, 0), axis=1)
    w0 = jnp.sum(jnp.where(first, cw, 0.0), axis=1, keepdims=True)
    w1 = jnp.sum(jnp.where(second, cw, 0.0), axis=1, keepdims=True)
    tok = jnp.arange(t_all, dtype=jnp.int32)
    row_tok = jnp.zeros((nr + (MOE_RING - 1) * blk,), jnp.int32).at[jnp.concatenate([d0, d1])].set(
        jnp.concatenate([tok, tok]), unique_indices=True)
    blk_start = jnp.arange(nb, dtype=jnp.int32)[:, None] * blk
    block_e = jnp.clip(jnp.sum((blk_start >= pend[None, :]).astype(jnp.int32), axis=1), 0, N_EXPERTS - 1)
    spare = jnp.zeros((COMBINE_TM,), jnp.int32)
    return row_tok, block_e, w0, w1, jnp.concatenate([d0, spare]), jnp.concatenate([d1, spare])


def _combine_kernel(d0_ref, d1_ref, x_ref, w0_ref, w1_ref, gate_ref, lng_ref, lnb_ref, yb_hbm, o_ref,
                    ybuf, sem, *, off, nstep):
    i = pl.program_id(0)
    tm = x_ref.shape[0]

    def row(k, idx, sl, r):
        return pltpu.make_async_copy(yb_hbm.at[pl.ds(idx, 1)], ybuf.at[sl, k, pl.ds(r, 1)], sem.at[sl])

    def fetch(tile, sl):
        base = off + tile * tm
        for r in range(tm):
            row(0, d0_ref[base + r], sl, r).start(priority=0)
            row(1, d1_ref[base + r], sl, r).start(priority=1)

    @pl.when(i == 0)
    def _():
        fetch(0, 0)

    def step(s):
        for r in range(tm):
            row(0, 0, s, r).wait()
            row(1, 0, s, r).wait()
        fetch(i + 1, 1 - s)
        y = ybuf[s, 0] * w0_ref[...] + ybuf[s, 1] * w1_ref[...]
        o_ref[...] = _layer_norm(ALPHA * x_ref[...] + gate_ref[...] * y, lng_ref[...], lnb_ref[...])

        @pl.when(i == nstep - 1)
        def _():
            for r in range(tm):
                row(0, 0, 1 - s, r).wait()
                row(1, 0, 1 - s, r).wait()

    par = lax.rem(i, 2)
    for s in range(2):
        pl.when(par == s)(functools.partial(step, s))


def _combine_call(x, yb, d0, d1, w0, w1, row_off, gate, lng, lnb, tm):
    L, d = x.shape
    nstep = L // tm
    offb = row_off // tm
    vec = pl.BlockSpec((1, d), lambda i, a, b: (0, 0))
    col = pl.BlockSpec((tm, 1), lambda i, a, b: (offb + i, 0))
    tile = pl.BlockSpec((tm, d), lambda i, a, b: (i, 0))
    return pl.pallas_call(
        functools.partial(_combine_kernel, off=row_off, nstep=nstep),
        out_shape=jax.ShapeDtypeStruct((L, d), F32),
        grid_spec=pltpu.PrefetchScalarGridSpec(
            num_scalar_prefetch=2,
            grid=(nstep,),
            in_specs=[tile, col, col, vec, vec, vec, pl.BlockSpec(memory_space=pl.ANY)],
            out_specs=tile,
            scratch_shapes=[pltpu.VMEM((2, 2, tm, d), F32), pltpu.SemaphoreType.DMA((2,))]),
        compiler_params=_cparams(("arbitrary",)),
        name="moe_combine",
    )(d0, d1, x, w0, w1, gate, lng, lnb, yb)


def _moe_layer(segs, wrt_pad, rb_pad, lng, lnb, wg, wu, wd, l):
    routed = [_router_call(hf, wrt_pad, rb_pad) for (_, hf, _) in segs]
    cw = jnp.concatenate([r[0][:, :N_EXPERTS] for r in routed], axis=0)
    sel = jnp.concatenate([r[1][:, :N_EXPERTS] for r in routed], axis=0)
    hf_all = jnp.concatenate([hf for (_, hf, _) in segs], axis=0) if len(segs) > 1 else segs[0][1]
    row_tok, block_e, w0, w1, d0, d1 = _moe_dispatch(sel, cw)
    yb = _moe_ffn_call(row_tok, block_e, hf_all, wg, wu, wd, l)
    tm = min(COMBINE_TM, min(x.shape[0] for (x, _, _) in segs))
    outs = []
    off = 0
    for (x, _, gate) in segs:
        assert off % tm == 0 and x.shape[0] % tm == 0
        outs.append(_combine_call(x, yb, d0, d1, w0, w1, off, gate, lng, lnb, tm))
        off += x.shape[0]
    return outs


def _blockdiag2(m):
    z = jnp.zeros_like(m[0])
    return jnp.concatenate([jnp.concatenate([m[0], z], axis=1), jnp.concatenate([z, m[1]], axis=1)], axis=0)


def _mixer(x, mod_row, l, w_in_bf, prm, fp, hy, mp, s0, latent):
    L = x.shape[0]
    sh, sc, gate = mod_row[0], mod_row[1], mod_row[2]
    p_r, p_h, p_g = _inproj_call(x, sh, sc, w_in_bf, l)
    rvk, dirp, lg, bv = _prepare_call(p_r, prm, latent)
    yf, yb, sfin = _scan_call(rvk, dirp, s0)
    filt, asum = _hfilt_call(L, fp)
    z, x0 = _hconv3_call(p_h, hy["conv_w"], hy["conv_b"])
    ho = _long_conv_call(z, x0, filt, asum, hy["bias"])
    mpl = dict(mp)
    mpl["gate"], mpl["sh2"], mpl["sc2"] = gate, mod_row[3], mod_row[4]
    xn, hf = _merge_call(yf, yb, bv, lg, ho, p_g, x, mpl, l)
    return xn, hf, sfin


def kernel(x, c, ctx, c_ctx, w_mod, b_mod, w_in, rwkv_mu, rwkv_w0, rwkv_w2, rwkv_a0, rwkv_a2, rwkv_g2,
           rwkv_k_k, rwkv_k_a, rwkv_r_k, rwkv_lnx_g, rwkv_lnx_b, hy_conv_w, hy_conv_b, hy_f_w1, hy_f_b1,
           hy_f_w2, hy_f_b2, hy_f_w3, hy_f_b3, hy_f_wout, hy_freq, hy_bias, w_branch, w_out, ln_g, ln_b,
           w_router, router_bias, w_gate, w_up, w_down):
    b, n_lat, d = x.shape
    assert b == 1 and d == D_MODEL
    n_ctx = ctx.shape[1]
    depth = w_mod.shape[0]
    hw = RWKV_WIDTH
    xl = x[0]
    xc = ctx[0]

    cc = jnp.concatenate([c[:1], c_ctx[None, :], jnp.zeros((6, d), F32)], axis=0)
    mod = _mod_call(cc, w_mod, b_mod)

    head_of = np.arange(hw) // HEAD_DIM
    G = jnp.asarray((head_of[:, None] == head_of[None, :]).astype(np.float32), dtype=BF)
    bands = jnp.linspace(1e-4, FILTER_BANDS - 1, FILTER_BANDS, dtype=F32)[None, :]
    deltas = jnp.abs(jnp.linspace(HYENA_MIN_DECAY, HYENA_MAX_DECAY, HYENA_WIDTH, dtype=F32))[None, :]
    wr_pad = jnp.pad(w_router.T, ((0, LANES - N_EXPERTS), (0, 0)))
    rb_pad = jnp.pad(router_bias, (0, LANES - N_EXPERTS))[:, None]
    w_in_bf = w_in.astype(BF)
    w_branch_bf = w_branch.astype(BF)
    w_out_bf = w_out.astype(BF)
    g2_bf = rwkv_g2.astype(BF)

    for l in range(depth):
        last = l == depth - 1
        ml = [mod[l, 0:1, j * d:(j + 1) * d] for j in range(6)]
        mc = [mod[l, 1:2, j * d:(j + 1) * d] for j in range(6)]
        prm = dict(mu=rwkv_mu[l][None, :],
                   w0=rwkv_w0[l].reshape(1, 2 * hw), w2=_blockdiag2(rwkv_w2[l]),
                   a0=rwkv_a0[l].reshape(1, 2 * hw), a2=_blockdiag2(rwkv_a2[l]),
                   k_k=rwkv_k_k[l][None, :], k_a=rwkv_k_a[l][None, :], r_k=rwkv_r_k[l][None, :], G=G)
        w1 = hy_f_w1[l]
        fp = dict(bands=bands.T, w1t=w1[0:1].T, w1c=w1[1:1 + FILTER_BANDS].T, w1s=w1[1 + FILTER_BANDS:].T,
                  b1=hy_f_b1[l][:, None], w2=hy_f_w2[l].T, b2=hy_f_b2[l][:, None], w3=hy_f_w3[l].T,
                  b3=hy_f_b3[l][:, None], wout=hy_f_wout[l], freq=hy_freq[l][:, None], deltas=deltas)
        hy = dict(conv_w=hy_conv_w[l], conv_b=hy_conv_b[l][None, :], bias=hy_bias[l][None, :])
        mp = dict(lnx_g=rwkv_lnx_g[l][None, :], lnx_b=rwkv_lnx_b[l][None, :], g2=g2_bf, G=G,
                  w_branch=w_branch_bf, w_out=w_out_bf, ln_g=ln_g[l, 0][None, :], ln_b=ln_b[l, 0][None, :])
        s0 = jnp.zeros((2, HEAD_DIM, hw), F32)
        xc_new, hf_c, s_ctx = _mixer(xc, mc, l, w_in_bf, prm, fp, hy, mp, s0, latent=False)
        xl, hf_l, _ = _mixer(xl, ml, l, w_in_bf, prm, fp, hy, mp, s_ctx, latent=True)
        lng, lnb = ln_g[l, 1][None, :], ln_b[l, 1][None, :]
        if last:
            (xl,) = _moe_layer([(xl, hf_l, ml[5])], wr_pad, rb_pad, lng, lnb, w_gate, w_up, w_down, l)
        else:
            xc, xl = _moe_layer([(xc_new, hf_c, mc[5]), (xl, hf_l, ml[5])], wr_pad, rb_pad, lng, lnb,
                                w_gate, w_up, w_down, l)
    return xl[None]
```

```python
import functools
import math

import numpy as np
import jax
import jax.numpy as jnp
from jax import lax
from jax.experimental import pallas as pl
from jax.experimental.pallas import tpu as pltpu

F32 = jnp.float32
BF = jnp.bfloat16

D_MODEL = 1024
DEPTH = 4
GRID_W = 64
RWKV_WIDTH = 512
HEAD_DIM = 64
RWKV_HEADS = 8
LORA_W = 64
LORA_A = 64
LORA_G = 128
DECAY_SCALE = 0.606531
GN_EPS = 6.4e-4
RWKV_COLS = 3 * RWKV_WIDTH + 2 * LORA_W + 2 * LORA_A + LORA_G
HYENA_WIDTH = 512
HYENA_COLS = 3 * HYENA_WIDTH
FILTER_BANDS = 16
FILTER_HIDDEN = 64
HYENA_MIN_DECAY = math.log(1e-2) / 1.5
HYENA_MAX_DECAY = math.log(1e-2) / 0.3
GATE_COLS = 2 * D_MODEL
PROJ_COLS = RWKV_COLS + HYENA_COLS + GATE_COLS
N_EXPERTS = 16
N_GROUPS = 4
EXPERTS_PER_GROUP = 4
ALPHA = (2 * DEPTH) ** 0.25
LN_EPS = 1e-5

SCAN_CHUNK = 64
LANES = 128
VMEM_LIMIT = 56 * 1024 * 1024

_NN = (((1,), (0,)), ((), ()))
_NT = (((1,), (1,)), ((), ()))
_TN = (((0,), (0,)), ((), ()))


def _sig(x):
    return 1.0 / (1.0 + jnp.exp(-x))


def _parts(a, n):
    out = []
    rem = a
    for i in range(n):
        hi = rem.astype(BF)
        out.append(hi)
        if i + 1 < n:
            rem = rem - hi.astype(F32)
    return out


def _mm(a, b, dn=_NN, passes=1):
    n = {1: 1, 3: 2, 6: 3}[passes]
    pa = _parts(a, n)
    pb = _parts(b, n)
    acc = None
    for i in range(n):
        for j in range(n - i):
            t = lax.dot_general(pa[i], pb[j], dn, preferred_element_type=F32)
            acc = t if acc is None else acc + t
    return acc


def _mm_rx(a, b_exact, n=3, dn=_NN):
    acc = None
    for p in _parts(a, n):
        t = lax.dot_general(p, b_exact, dn, preferred_element_type=F32)
        acc = t if acc is None else acc + t
    return acc


def _mm_lx(a_exact, b, n=3, dn=_NN):
    acc = None
    for p in _parts(b, n):
        t = lax.dot_general(a_exact, p, dn, preferred_element_type=F32)
        acc = t if acc is None else acc + t
    return acc


def _cparams(sem, vmem=VMEM_LIMIT):
    return pltpu.CompilerParams(dimension_semantics=sem, vmem_limit_bytes=vmem)


def _full(shape):
    nd = len(shape)
    return pl.BlockSpec(shape, lambda *_: (0,) * nd)


def _mod_kernel(c_ref, w_ref, b_ref, o_ref):
    cc = c_ref[...]
    s = cc * _sig(cc)
    o_ref[...] = jnp.dot(s.astype(BF), w_ref[...].astype(BF), preferred_element_type=F32) + b_ref[...]


def _mod_call(cc, w_mod, b_mod):
    depth, d, n6 = w_mod.shape
    tn = 1536
    return pl.pallas_call(
        _mod_kernel,
        out_shape=jax.ShapeDtypeStruct((depth, 8, n6), F32),
        grid=(depth, n6 // tn),
        in_specs=[
            pl.BlockSpec((8, d), lambda l, j: (0, 0)),
            pl.BlockSpec((None, d, tn), lambda l, j: (l, 0, j)),
            pl.BlockSpec((None, 1, tn), lambda l, j: (l, 0, j)),
        ],
        out_specs=pl.BlockSpec((None, 8, tn), lambda l, j: (l, 0, j)),
        compiler_params=_cparams(("arbitrary", "arbitrary")),
        name="mod",
    )(cc, w_mod, b_mod.reshape(depth, 1, n6))


def _inproj_kernel(x_ref, sh_ref, sc_ref, w_ref, pr_ref, ph_ref, pg_ref):
    xm = (x_ref[...] * (1.0 + sc_ref[...]) + sh_ref[...]).astype(BF)
    pr_ref[...] = jnp.dot(xm, w_ref[:, :RWKV_COLS], preferred_element_type=F32)
    ph_ref[...] = jnp.dot(xm, w_ref[:, RWKV_COLS:RWKV_COLS + HYENA_COLS], preferred_element_type=F32)
    pg_ref[...] = jnp.dot(xm, w_ref[:, RWKV_COLS + HYENA_COLS:], preferred_element_type=F32)


def _inproj_call(x, sh, sc, w_bf, l):
    L, d = x.shape
    tm = min(L, 256)
    row = lambda i: (i, 0)
    return pl.pallas_call(
        _inproj_kernel,
        out_shape=(jax.ShapeDtypeStruct((L, RWKV_COLS), F32),
                   jax.ShapeDtypeStruct((L, HYENA_COLS), F32),
                   jax.ShapeDtypeStruct((L, GATE_COLS), F32)),
        grid=(L // tm,),
        in_specs=[pl.BlockSpec((tm, d), row), _full((1, d)), _full((1, d)),
                  pl.BlockSpec((None, d, PROJ_COLS), lambda i: (l, 0, 0))],
        out_specs=(pl.BlockSpec((tm, RWKV_COLS), row), pl.BlockSpec((tm, HYENA_COLS), row),
                   pl.BlockSpec((tm, GATE_COLS), row)),
        compiler_params=_cparams(("arbitrary",)),
        name="inproj",
    )(x, sh, sc, w_bf)


def _prepare_kernel(*refs, latent, nblk):
    if latent:
        p_ref, up_ref, dn_ref = refs[:3]
        refs = refs[3:]
    else:
        p_ref = refs[0]
        refs = refs[1:]
    (mu_ref, w0_ref, w2_ref, a0_ref, a2_ref, kk_ref, ka_ref, rk_ref, g_ref,
     rvk_out, dir_out, lg_out, bv_out) = refs
    i = pl.program_id(0)
    p = p_ref[...]
    tb, w = p.shape
    row = lax.broadcasted_iota(jnp.int32, (tb, 1), 0)
    col = lax.broadcasted_iota(jnp.int32, (1, w), 1)
    if latent:
        gw = jnp.bitwise_and(row, GRID_W - 1)
        left = jnp.where(gw == 0, 0.0, pltpu.roll(p, 1, 0))
        right = jnp.where(gw == GRID_W - 1, 0.0, pltpu.roll(p, tb - 1, 0))
        upv = jnp.where(i == 0, 0.0, up_ref[...])
        dnv = jnp.where(i == nblk - 1, 0.0, dn_ref[...])
        if tb > GRID_W:
            up = jnp.concatenate([upv, p[:tb - GRID_W]], axis=0)
            down = jnp.concatenate([p[GRID_W:], dnv], axis=0)
        else:
            up, down = upv, dnv
        q = w // 4
        sh = jnp.where(col < q, left, jnp.where(col < 2 * q, right, jnp.where(col < 3 * q, up, down)))
    else:
        prev = jnp.where(row == 0, 0.0, pltpu.roll(p, 1, 0))
        nxt = jnp.where(row == tb - 1, 0.0, pltpu.roll(p, tb - 1, 0))
        sh = jnp.where(col < w // 2, prev, nxt)
    pm = p + (sh - p) * mu_ref[...]
    hw = RWKV_WIDTH
    r = pm[:, :hw]
    k = pm[:, hw:2 * hw]
    v = pm[:, 2 * hw:3 * hw]
    lw_in = jnp.tanh(pm[:, 3 * hw:3 * hw + 2 * LORA_W])
    la_in = pm[:, 3 * hw + 2 * LORA_W:3 * hw + 2 * LORA_W + 2 * LORA_A]
    lg = pm[:, 3 * hw + 2 * LORA_W + 2 * LORA_A:]
    lw = -DECAY_SCALE * _sig(w0_ref[...] + _mm(lw_in, w2_ref[...], passes=3))
    a = _sig(a0_ref[...] + _mm(la_in, a2_ref[...], passes=3))
    g = g_ref[...]
    kkr = k * kk_ref[...]
    nrm = jnp.sqrt(_mm_rx(kkr * kkr, g, 3))
    kk = kkr / jnp.maximum(nrm, 1e-12)
    k_a = ka_ref[...]
    k_f = k * (1.0 + (a[:, :hw] - 1.0) * k_a)
    k_b = k * (1.0 + (a[:, hw:] - 1.0) * k_a)
    bonus = _mm_rx(r * (k_f + k_b) * rk_ref[...], g, 3)
    rvk_out[:, :hw] = r
    rvk_out[:, hw:2 * hw] = v
    rvk_out[:, 2 * hw:] = -kk
    for d, (k_d, a_d) in enumerate(((k_f, a[:, :hw]), (k_b, a[:, hw:]))):
        dir_out[d, :, :hw] = lw[:, d * hw:(d + 1) * hw]
        dir_out[d, :, hw:2 * hw] = k_d
        dir_out[d, :, 2 * hw:] = kk * a_d
    lg_out[...] = lg
    bv_out[...] = bonus * v


def _prepare_call(p_r, prm, latent):
    L, w = p_r.shape
    hw = RWKV_WIDTH
    tb = 256 if latent else L
    nblk = L // tb
    row = lambda i: (i, 0)
    row3 = lambda i: (0, i, 0)
    in_specs = [pl.BlockSpec((tb, w), row)]
    args = [p_r]
    if latent:
        per = tb // GRID_W
        nrow = L // GRID_W
        in_specs += [
            pl.BlockSpec((GRID_W, w), lambda i: (jnp.maximum(i * per - 1, 0), 0)),
            pl.BlockSpec((GRID_W, w), lambda i: (jnp.minimum((i + 1) * per, nrow - 1), 0)),
        ]
        args += [p_r, p_r]
    names = ("mu", "w0", "w2", "a0", "a2", "k_k", "k_a", "r_k", "G")
    for nm in names:
        in_specs.append(_full(prm[nm].shape))
        args.append(prm[nm])
    sd = jax.ShapeDtypeStruct
    out_shape = (sd((L, 3 * hw), F32), sd((2, L, 3 * hw), F32), sd((L, LORA_G), F32), sd((L, hw), F32))
    out_specs = (pl.BlockSpec((tb, 3 * hw), row), pl.BlockSpec((2, tb, 3 * hw), row3),
                 pl.BlockSpec((tb, LORA_G), row), pl.BlockSpec((tb, hw), row))
    return pl.pallas_call(
        functools.partial(_prepare_kernel, latent=latent, nblk=nblk),
        out_shape=out_shape, grid=(nblk,), in_specs=in_specs, out_specs=out_specs,
        compiler_params=_cparams(("arbitrary",)),
        name="rwkv_prepare_lat" if latent else "rwkv_prepare_ctx",
    )(*args)


P_SCORE = 1
P_TINV = 1
P_APPLY = 1
P_STATE = 1


SCAN_SUB = 4


def _scan_chunk(fin, bin_, states, masks):
    C = SCAN_CHUNK
    hd = HEAD_DIM
    nh = RWKV_HEADS
    row, col, eye, lvl_masks = masks
    dirs = []
    for d, (r, v, ka, lw, k, kb) in enumerate((fin, bin_)):
        inc = (row >= col) if d == 0 else (row <= col)
        strict = (row > col) if d == 0 else (row < col)
        lc = _mm_lx(jnp.where(inc, 1.0, 0.0).astype(BF), lw, 3)
        e_neg = jnp.exp(-lc)
        e_tot = jnp.exp(jnp.sum(lw, axis=0, keepdims=True))
        kbn = kb * e_neg
        kkn = k * e_neg
        dirs.append(dict(inc=inc, strict=strict, v=v, aq=ka * jnp.exp(lc - lw), rq=r * jnp.exp(lc),
                         kbn=kbn, kkn=kkn, kbp=kbn * e_tot, kkp=kkn * e_tot, e_tot=e_tot, s=states[d]))

    units = [(d, h) for h in range(nh) for d in range(2)]
    hs = lambda arr, h: arr[:, h * hd:(h + 1) * hd]
    sc = {}
    for (d, h) in units:
        D = dirs[d]
        sc[d, h] = _mm(jnp.concatenate([hs(D["aq"], h), hs(D["rq"], h)], axis=0),
                       jnp.concatenate([hs(D["kbn"], h), hs(D["kkn"], h)], axis=0), _NT, P_SCORE)
    a_ab, t, x = {}, {}, {}
    for u in units:
        a_ab[u] = jnp.where(dirs[u[0]]["strict"], sc[u][:C, :C], 0.0)
        t[u] = eye + jnp.where(lvl_masks[0], a_ab[u], 0.0)
    for u in units:
        D = dirs[u[0]]
        lhs = jnp.concatenate([jnp.where(D["strict"], sc[u][:C, C:], 0.0),
                               jnp.where(D["inc"], sc[u][C:, C:], 0.0)], axis=0)
        x[u] = _mm(lhs, hs(D["v"], u[1]), _NN, P_APPLY)
    for m in lvl_masks[1:]:
        tmp = {u: _mm(jnp.where(m, a_ab[u], 0.0), t[u], _NN, P_TINV) for u in units}
        for u in units:
            t[u] = t[u] + _mm(t[u], tmp[u], _NN, P_TINV)
    z, gm, uu, yy, sn = {}, {}, {}, {}, {}
    for u in units:
        z[u] = _mm(t[u], jnp.concatenate([hs(dirs[u[0]]["aq"], u[1]), x[u][:C]], axis=1), _NN, P_APPLY)
    for u in units:
        D = dirs[u[0]]
        gm[u] = _mm(jnp.concatenate([z[u][:, :hd], hs(D["rq"], u[1])], axis=0), hs(D["s"], u[1]), _NT, P_STATE)
        uu[u] = gm[u][:C] + z[u][:, hd:]
    for u in units:
        D = dirs[u[0]]
        b_rb = jnp.where(D["inc"], sc[u][C:, :C], 0.0)
        yy[u] = gm[u][C:] + _mm(b_rb, uu[u], _NN, P_APPLY) + x[u][C:]
        sn[u] = hs(D["s"], u[1]) * hs(D["e_tot"], u[1]) + _mm(
            jnp.concatenate([uu[u], hs(D["v"], u[1])], axis=0),
            jnp.concatenate([hs(D["kbp"], u[1]), hs(D["kkp"], u[1])], axis=0), _TN, P_STATE)
    cat = lambda d, src: jnp.concatenate([src[d, h] for h in range(nh)], axis=1)
    return cat(0, yy), cat(1, yy), [cat(0, sn), cat(1, sn)]


def _scan_kernel(sf_ref, sb_ref, df_ref, db_ref, s0_ref, yf_ref, yb_ref, sfin_ref, s_scr, *, nstep):
    c = pl.program_id(0)
    C = SCAN_CHUNK

    @pl.when(c == 0)
    def _():
        s_scr[...] = s0_ref[...]

    row = lax.broadcasted_iota(jnp.int32, (C, C), 0)
    col = lax.broadcasted_iota(jnp.int32, (C, C), 1)
    eye = jnp.where(row == col, 1.0, 0.0)
    lvl_masks = []
    for sh in range(int(math.log2(C))):
        same2 = jnp.right_shift(row, sh + 1) == jnp.right_shift(col, sh + 1)
        same1 = jnp.right_shift(row, sh) == jnp.right_shift(col, sh)
        lvl_masks.append(jnp.logical_and(same2, jnp.logical_not(same1)))
    masks = (row, col, eye, lvl_masks)

    hw = RWKV_WIDTH
    unpack = lambda sref, dref, rows: tuple(ref[rows, i * hw:(i + 1) * hw] for ref in (sref, dref) for i in range(3))
    states = [s_scr[0], s_scr[1]]
    for j in range(SCAN_SUB):
        fsl = slice(j * C, (j + 1) * C)
        bsl = slice((SCAN_SUB - 1 - j) * C, (SCAN_SUB - j) * C)
        yf, yb, states = _scan_chunk(unpack(sf_ref, df_ref, fsl), unpack(sb_ref, db_ref, bsl), states, masks)
        yf_ref[fsl, :] = yf
        yb_ref[bsl, :] = yb
    s_scr[0] = states[0]
    s_scr[1] = states[1]

    @pl.when(c == nstep - 1)
    def _():
        sfin_ref[...] = s_scr[...]


def _scan_call(rvk, dirp, s0):
    L = rvk.shape[0]
    hw = RWKV_WIDTH
    C = SCAN_CHUNK * SCAN_SUB
    nchunk = L // C
    assert L % C == 0
    sh_f = pl.BlockSpec((C, 3 * hw), lambda c: (c, 0))
    sh_b = pl.BlockSpec((C, 3 * hw), lambda c: (nchunk - 1 - c, 0))
    pd_f = pl.BlockSpec((None, C, 3 * hw), lambda c: (0, c, 0))
    pd_b = pl.BlockSpec((None, C, 3 * hw), lambda c: (1, nchunk - 1 - c, 0))
    y_f = pl.BlockSpec((C, hw), lambda c: (c, 0))
    y_b = pl.BlockSpec((C, hw), lambda c: (nchunk - 1 - c, 0))
    state = _full((2, HEAD_DIM, hw))
    return pl.pallas_call(
        functools.partial(_scan_kernel, nstep=nchunk),
        out_shape=(jax.ShapeDtypeStruct((L, hw), F32), jax.ShapeDtypeStruct((L, hw), F32),
                   jax.ShapeDtypeStruct((2, HEAD_DIM, hw), F32)),
        grid=(nchunk,),
        in_specs=[sh_f, sh_b, pd_f, pd_b, state],
        out_specs=(y_f, y_b, state),
        scratch_shapes=[pltpu.VMEM((2, HEAD_DIM, hw), F32)],
        compiler_params=_cparams(("arbitrary",)),
        name="delta_scan",
    )(rvk, rvk, dirp, dirp, s0)


def _hfilt_kernel(bands_ref, w1t_ref, w1c_ref, w1s_ref, b1_ref, w2_ref, b2_ref, w3_ref, b3_ref, wo_ref,
                  fr_ref, dl_ref, f_ref, asum_ref, *, n, rb):
    i = pl.program_id(0)
    posr = (i * rb + lax.broadcasted_iota(jnp.int32, (1, rb), 1)).astype(F32)
    tr = posr / float(max(n - 1, 1))
    ang = ((2.0 * math.pi / n) * posr) * bands_ref[...]
    fr = fr_ref[...]
    h = (w1t_ref[...] * tr + _mm(w1c_ref[...], jnp.cos(ang), passes=3)
         + _mm(w1s_ref[...], -jnp.sin(ang), passes=3))
    h = jnp.sin(fr * (h + b1_ref[...]))
    h = jnp.sin(fr * (_mm(w2_ref[...], h, passes=3) + b2_ref[...]))
    h = jnp.sin(fr * (_mm(w3_ref[...], h, passes=3) + b3_ref[...]))
    filt = _mm(h, wo_ref[...], _TN, 3)
    pos = (i * rb + lax.broadcasted_iota(jnp.int32, (rb, 1), 0)).astype(F32)
    dist = jnp.abs(pos - float(n // 2)) * (2.0 / n)
    filt = filt * jnp.exp(-dist * dl_ref[...])
    f_ref[...] = filt

    @pl.when(i == 0)
    def _():
        asum_ref[...] = jnp.zeros_like(asum_ref)

    asum_ref[...] += jnp.sum(jnp.abs(filt), axis=0, keepdims=True)


def _hfilt_call(n, fp):
    rb = min(n, 512)
    names = ("bands", "w1t", "w1c", "w1s", "b1", "w2", "b2", "w3", "b3", "wout", "freq", "deltas")
    args = [fp[nm] for nm in names]
    return pl.pallas_call(
        functools.partial(_hfilt_kernel, n=n, rb=rb),
        out_shape=(jax.ShapeDtypeStruct((n, HYENA_WIDTH), F32), jax.ShapeDtypeStruct((1, HYENA_WIDTH), F32)),
        grid=(n // rb,),
        in_specs=[_full(a.shape) for a in args],
        out_specs=(pl.BlockSpec((rb, HYENA_WIDTH), lambda i: (i, 0)), _full((1, HYENA_WIDTH))),
        compiler_params=_cparams(("arbitrary",)),
        name="hyena_filter",
    )(*args)


def _hconv3_kernel(p_ref, pv_ref, nx_ref, cw_ref, cb_ref, z_ref, x0_ref, *, nblk):
    i = pl.program_id(0)
    p = p_ref[...]
    tb = p.shape[0]
    row = lax.broadcasted_iota(jnp.int32, (tb, 1), 0)
    prev_row = jnp.where(i == 0, 0.0, pv_ref[7:8, :])
    next_row = jnp.where(i == nblk - 1, 0.0, nx_ref[0:1, :])
    sp = jnp.where(row == 0, prev_row, pltpu.roll(p, 1, 0))
    sn = jnp.where(row == tb - 1, next_row, pltpu.roll(p, tb - 1, 0))
    u = sp * cw_ref[0:1, :] + p * cw_ref[1:2, :] + sn * cw_ref[2:3, :] + cb_ref[...]
    hw = HYENA_WIDTH
    z_ref[...] = u[:, 2 * hw:] * u[:, hw:2 * hw]
    x0_ref[...] = u[:, :hw]


def _hconv3_call(p_h, cw, cb):
    L, w = p_h.shape
    tb = min(L, 256)
    nblk = L // tb
    per = tb // 8
    row = lambda i: (i, 0)
    return pl.pallas_call(
        functools.partial(_hconv3_kernel, nblk=nblk),
        out_shape=(jax.ShapeDtypeStruct((L, HYENA_WIDTH), F32), jax.ShapeDtypeStruct((L, HYENA_WIDTH), F32)),
        grid=(nblk,),
        in_specs=[pl.BlockSpec((tb, w), row),
                  pl.BlockSpec((8, w), lambda i: (jnp.maximum(i * per - 1, 0), 0)),
                  pl.BlockSpec((8, w), lambda i: (jnp.minimum((i + 1) * per, L // 8 - 1), 0)),
                  _full(cw.shape), _full(cb.shape)],
        out_specs=(pl.BlockSpec((tb, HYENA_WIDTH), row), pl.BlockSpec((tb, HYENA_WIDTH), row)),
        compiler_params=_cparams(("arbitrary",)),
        name="hyena_conv3",
    )(p_h, p_h, p_h, cw, cb)


P_DFT = 1


DFT_NB = 8


def _dft1_kernel(m_ref, x_ref, o_ref):
    m = m_ref[...]
    n1 = o_ref.shape[2]
    for j in range(x_ref.shape[1]):
        r = _mm(m, x_ref[:, j, :], _NN, P_DFT)
        o_ref[0, j] = r[:n1]
        o_ref[1, j] = r[n1:]


def _dft1_call(m1, x3):
    rows, kdim = m1.shape
    _, n2, c = x3.shape
    n1 = rows // 2
    nbk = min(n2, DFT_NB)
    return pl.pallas_call(
        _dft1_kernel,
        out_shape=jax.ShapeDtypeStruct((2, n2, n1, c), F32),
        grid=(n2 // nbk,),
        in_specs=[_full(m1.shape), pl.BlockSpec((kdim, nbk, c), lambda j: (0, j, 0))],
        out_specs=pl.BlockSpec((2, nbk, n1, c), lambda j: (0, j, 0, 0)),
        compiler_params=_cparams(("arbitrary",)),
        name="dft_stage1",
    )(m1, x3)


def _dft2_matrix(fc, fs, tc, ts):
    fre = fc * tc - fs * ts
    fim = -(fc * ts + fs * tc)
    return jnp.concatenate([jnp.concatenate([fre, -fim], axis=1),
                            jnp.concatenate([fim, fre], axis=1)], axis=0)


def _dft2_conv_kernel(fc_ref, fs_ref, tc_ref, ts_ref, a_ref, af_ref, b_ref):
    fc, fs = fc_ref[...], fs_ref[...]
    n2, c = a_ref.shape[1], a_ref.shape[3]
    for kk in range(a_ref.shape[2]):
        big = _dft2_matrix(fc, fs, tc_ref[kk], ts_ref[kk])
        rhs = jnp.concatenate([jnp.concatenate([a_ref[0, :, kk, :], af_ref[0, :, kk, :]], axis=1),
                               jnp.concatenate([a_ref[1, :, kk, :], af_ref[1, :, kk, :]], axis=1)], axis=0)
        xh = _mm(big, rhs, _NN, P_DFT)
        xr, hr = xh[:n2, :c], xh[:n2, c:]
        xi, hi = xh[n2:, :c], xh[n2:, c:]
        y = jnp.concatenate([xr * hr - xi * hi, xr * hi + xi * hr], axis=0)
        bb = _mm(big, y, _TN, P_DFT)
        b_ref[0, kk] = bb[:n2]
        b_ref[1, kk] = bb[n2:]


def _dft2_call(consts, a4, af4):
    _, n2, n1, c = a4.shape
    kb = min(n1, DFT_NB)
    blk = pl.BlockSpec((2, n2, kb, c), lambda q: (0, 0, q, 0))
    tw = pl.BlockSpec((kb, 1, n2), lambda q: (q, 0, 0))
    return pl.pallas_call(
        _dft2_conv_kernel,
        out_shape=jax.ShapeDtypeStruct((2, n1, n2, c), F32),
        grid=(n1 // kb,),
        in_specs=[_full((n2, n2)), _full((n2, n2)), tw, tw, blk, blk],
        out_specs=pl.BlockSpec((2, kb, n2, c), lambda q: (0, q, 0, 0)),
        compiler_params=_cparams(("arbitrary",)),
        name="dft_stage2_conv",
    )(consts["fc"], consts["fs"], consts["twc"], consts["tws"], a4, af4)


def _dft3_kernel(m_ref, b_ref, z_ref, x0_ref, bias_ref, asum_ref, o_ref):
    m = m_ref[...]
    for j in range(z_ref.shape[1]):
        bj = jnp.concatenate([b_ref[0, :, j, :], b_ref[1, :, j, :]], axis=0)
        y = _mm(m, bj, _NN, P_DFT)
        o_ref[:, j, :] = (y / asum_ref[...] + z_ref[:, j, :] * bias_ref[...]) * x0_ref[:, j, :]


def _dft3_call(m3, b4, z3, x03, bias, asum):
    rows, _ = m3.shape
    _, n1, n2, c = b4.shape
    nbk = min(n2, DFT_NB)
    slab = pl.BlockSpec((rows, nbk, c), lambda j: (0, j, 0))
    return pl.pallas_call(
        _dft3_kernel,
        out_shape=jax.ShapeDtypeStruct((rows, n2, c), F32),
        grid=(n2 // nbk,),
        in_specs=[_full(m3.shape), pl.BlockSpec((2, n1, nbk, c), lambda j: (0, 0, j, 0)), slab, slab,
                  _full((1, c)), _full((1, c))],
        out_specs=slab,
        compiler_params=_cparams(("arbitrary",)),
        name="dft_stage3",
    )(m3, b4, z3, x03, bias, asum)


def _conv_direct_kernel(m1_ref, m3_ref, z_ref, f_ref, x0_ref, bias_ref, asum_ref, o_ref):
    m1 = m1_ref[...]
    z = z_ref[...]
    a = _mm(m1, z, _NN, P_DFT)
    h = _mm(m1, f_ref[...], _NN, P_DFT)
    nn = a.shape[0] // 2
    ar, ai, hr, hi = a[:nn], a[nn:], h[:nn], h[nn:]
    y = jnp.concatenate([ar * hr - ai * hi, ar * hi + ai * hr], axis=0)
    out = _mm(m3_ref[...], y, _NN, P_DFT)
    o_ref[...] = (out / asum_ref[...] + z * bias_ref[...]) * x0_ref[...]


def _conv_direct_call(m1, m3, z, filt, x0, bias, asum):
    args = (m1, m3, z, filt, x0, bias, asum)
    return pl.pallas_call(
        _conv_direct_kernel,
        out_shape=jax.ShapeDtypeStruct(z.shape, F32),
        grid=(1,),
        in_specs=[_full(a.shape) for a in args],
        out_specs=_full(z.shape),
        compiler_params=_cparams(("arbitrary",)),
        name="long_conv_direct",
    )(*args)


DIRECT_CONV_MAX = 256


@functools.lru_cache(maxsize=None)
def _dft_consts_np(n):
    big_n = 2 * n
    if n <= DIRECT_CONV_MAX:
        n1 = big_n
    else:
        n1 = 256 if n >= 8192 else 64
    n2 = big_n // n1
    k1 = np.arange(n1)[:, None].astype(np.float64)
    j1 = np.arange(n1 // 2)[None, :].astype(np.float64)
    ang1 = 2.0 * np.pi * k1 * j1 / n1
    m1 = np.concatenate([np.cos(ang1), -np.sin(ang1)], axis=0)
    o1 = (n1 // 4 + np.arange(n1 // 2))[:, None].astype(np.float64)
    q1 = np.arange(n1)[None, :].astype(np.float64)
    ang3 = 2.0 * np.pi * o1 * q1 / n1
    m3 = np.concatenate([np.cos(ang3), -np.sin(ang3)], axis=1) / big_n
    k2 = np.arange(n2)[:, None].astype(np.float64)
    j2 = np.arange(n2)[None, :].astype(np.float64)
    ang2 = 2.0 * np.pi * k2 * j2 / n2
    angt = 2.0 * np.pi * np.arange(n1)[:, None].astype(np.float64) * j2 / big_n
    f = lambda a: np.asarray(a, np.float32)
    return dict(n1=n1, n2=n2, m1=f(m1), m3=f(m3), fc=f(np.cos(ang2)), fs=f(np.sin(ang2)),
                twc=f(np.cos(angt))[:, None, :], tws=f(np.sin(angt))[:, None, :])


def _long_conv_call(z, x0, filt, asum, bias):
    n, c = z.shape
    cn = _dft_consts_np(n)
    n1, n2 = cn["n1"], cn["n2"]
    consts = {kk: jnp.asarray(vv) for kk, vv in cn.items() if kk not in ("n1", "n2")}
    if n2 == 1:
        return _conv_direct_call(consts["m1"], consts["m3"], z, filt, x0, bias, asum)
    z3 = z.reshape(n1 // 2, n2, c)
    a_f = _dft1_call(consts["m1"], filt.reshape(n1 // 2, n2, c))
    a_z = _dft1_call(consts["m1"], z3)
    b4 = _dft2_call(consts, a_z, a_f)
    out = _dft3_call(consts["m3"], b4, z3, x0.reshape(n1 // 2, n2, c), bias, asum)
    return out.reshape(n, c)


def _layer_norm(x, g, b):
    mu = jnp.mean(x, axis=-1, keepdims=True)
    xc = x - mu
    var = jnp.mean(xc * xc, axis=-1, keepdims=True)
    return xc * lax.rsqrt(var + LN_EPS) * g + b


def _merge_kernel(yf_ref, yb_ref, bv_ref, lg_ref, ho_ref, pg_ref, x_ref, gx_ref, bx_ref, g2_ref, g_ref,
                  wb_ref, wo_ref, gate_ref, lng_ref, lnb_ref, sh2_ref, sc2_ref, o_ref, hf_ref):
    g = g_ref[...]
    ys = yf_ref[...] + yb_ref[...]
    inv_hd = 1.0 / HEAD_DIM
    mu = _mm_rx(ys, g, 3) * inv_hd
    dd = ys - mu
    var = _mm_rx(dd * dd, g, 3) * inv_hd
    yn = dd * lax.rsqrt(var + GN_EPS) * gx_ref[...] + bx_ref[...]
    gate_r = jnp.dot(_sig(lg_ref[...]).astype(BF), g2_ref[...], preferred_element_type=F32)
    ro = (yn + bv_ref[...]) * gate_r
    br = jnp.dot(ro.astype(BF), wb_ref[0], preferred_element_type=F32)
    bh = jnp.dot(ho_ref[...].astype(BF), wb_ref[1], preferred_element_type=F32)
    sg = _sig(pg_ref[...])
    m = sg[:, :D_MODEL] * br + sg[:, D_MODEL:] * bh
    mix = jnp.dot(m.astype(BF), wo_ref[...], preferred_element_type=F32)
    xn = _layer_norm(ALPHA * x_ref[...] + gate_ref[...] * mix, lng_ref[...], lnb_ref[...])
    o_ref[...] = xn
    hf_ref[...] = xn * (1.0 + sc2_ref[...]) + sh2_ref[...]


def _merge_call(yf, yb, bv, lg, ho, pg, x, mp, l):
    L, d = x.shape
    hw = RWKV_WIDTH
    tb = min(L, 256)
    row = lambda i: (i, 0)
    names = ("lnx_g", "lnx_b", "g2", "G", "w_branch", "w_out", "gate", "ln_g", "ln_b", "sh2", "sc2")
    pargs = [mp[nm] for nm in names]

    def pspec(nm, a):
        if nm in ("g2", "w_branch", "w_out"):
            nd = a.ndim - 1
            return pl.BlockSpec((None,) + a.shape[1:], lambda i: (l,) + (0,) * nd)
        return _full(a.shape)

    return pl.pallas_call(
        _merge_kernel,
        out_shape=(jax.ShapeDtypeStruct((L, d), F32), jax.ShapeDtypeStruct((L, d), F32)),
        grid=(L // tb,),
        in_specs=[pl.BlockSpec((tb, hw), row), pl.BlockSpec((tb, hw), row), pl.BlockSpec((tb, hw), row),
                  pl.BlockSpec((tb, LORA_G), row), pl.BlockSpec((tb, hw), row),
                  pl.BlockSpec((tb, GATE_COLS), row), pl.BlockSpec((tb, d), row)]
                 + [pspec(nm, a) for nm, a in zip(names, pargs)],
        out_specs=(pl.BlockSpec((tb, d), row), pl.BlockSpec((tb, d), row)),
        compiler_params=_cparams(("arbitrary",)),
        name="merge_postnorm",
    )(yf, yb, bv, lg, ho, pg, x, *pargs)


def _router_kernel(hf_ref, wrt_ref, rb_ref, cw_ref, sel_ref):
    lt = _mm(wrt_ref[...], hf_ref[...], _NT, 6)
    rid = lax.broadcasted_iota(jnp.int32, (LANES, 1), 0)
    valid = rid < N_EXPERTS
    lg = jnp.where(valid, lt, -jnp.inf)
    mx = jnp.max(lg, axis=0, keepdims=True)
    ex = jnp.where(valid, jnp.exp(lg - mx), 0.0)
    scores = ex / jnp.sum(ex, axis=0, keepdims=True)
    sel = scores + rb_ref[...]
    s = [sel[e:e + 1, :] for e in range(N_EXPERTS)]
    p = [scores[e:e + 1, :] for e in range(N_EXPERTS)]
    gs = []
    for gi in range(N_GROUPS):
        mem = s[gi * EXPERTS_PER_GROUP:(gi + 1) * EXPERTS_PER_GROUP]
        best = None
        for a in range(EXPERTS_PER_GROUP):
            for b in range(a + 1, EXPERTS_PER_GROUP):
                pair = mem[a] + mem[b]
                best = pair if best is None else jnp.maximum(best, pair)
        gs.append(best)
    bg = jnp.where((gs[0] >= gs[1]) & (gs[0] >= gs[2]) & (gs[0] >= gs[3]), 0,
                   jnp.where((gs[1] >= gs[2]) & (gs[1] >= gs[3]), 1, jnp.where(gs[2] >= gs[3], 2, 3)))
    chosen = []
    for e in range(N_EXPERTS):
        gi = e // EXPERTS_PER_GROUP
        beats = None
        for j in range(gi * EXPERTS_PER_GROUP, (gi + 1) * EXPERTS_PER_GROUP):
            if j == e:
                continue
            cond = (s[j] >= s[e]) if j < e else (s[j] > s[e])
            cnt = jnp.where(cond, 1.0, 0.0)
            beats = cnt if beats is None else beats + cnt
        chosen.append((bg == gi) & (beats < 1.5))
    den = None
    for e in range(N_EXPERTS):
        t = jnp.where(chosen[e], p[e], 0.0)
        den = t if den is None else den + t
    out = jnp.zeros(lt.shape, F32)
    msk = jnp.zeros(lt.shape, F32)
    for e in range(N_EXPERTS):
        hit = (rid == e) & chosen[e]
        out = jnp.where(hit, p[e] / den, out)
        msk = jnp.where(hit, 1.0, msk)
    cw_ref[...] = out.T
    sel_ref[...] = msk.T


def _router_call(hf, wrt_pad, rb_pad):
    L, d = hf.shape
    tb = min(L, 256)
    row = lambda i: (i, 0)
    return pl.pallas_call(
        _router_kernel,
        out_shape=(jax.ShapeDtypeStruct((L, LANES), F32), jax.ShapeDtypeStruct((L, LANES), F32)),
        grid=(L // tb,),
        in_specs=[pl.BlockSpec((tb, d), row), _full(wrt_pad.shape), _full(rb_pad.shape)],
        out_specs=(pl.BlockSpec((tb, LANES), row), pl.BlockSpec((tb, LANES), row)),
        compiler_params=_cparams(("arbitrary",)),
        name="router",
    )(hf, wrt_pad, rb_pad)


MOE_BLK = 256


MOE_RING = 3


def _moe_ffn_kernel(tok_ref, slot_ref, be_ref, hf_hbm, wg_ref, wu_ref, wd_ref, y_hbm,
                    xbuf, ybuf, wbf, gsem, ssem, *, nb):
    b = pl.program_id(0)
    nrow = MOE_BLK

    def row_in(idx, sl, r):
        return pltpu.make_async_copy(hf_hbm.at[pl.ds(idx, 1)], xbuf.at[sl, pl.ds(r, 1)], gsem.at[sl])

    def row_out(idx, sl, r):
        return pltpu.make_async_copy(ybuf.at[sl, pl.ds(r, 1)], y_hbm.at[pl.ds(idx, 1)], ssem.at[sl])

    @pl.when(b == 0)
    def _():
        for blk in range(MOE_RING - 1):
            for r in range(nrow):
                row_in(tok_ref[blk * nrow + r], blk, r).start()
        ybuf[MOE_RING - 1] = jnp.zeros(ybuf.shape[1:], F32)

    @pl.when((b == 0) | (be_ref[b] != be_ref[jnp.maximum(b - 1, 0)]))
    def _():
        wbf[0] = wg_ref[...].astype(BF)
        wbf[1] = wu_ref[...].astype(BF)
        wbf[2] = wd_ref[...].astype(BF)

    def step(s):
        s1, s2 = (s + 1) % MOE_RING, (s + 2) % MOE_RING
        for r in range(nrow):
            row_in(0, s, r).wait()

        @pl.when(b >= 2)
        def _():
            for r in range(nrow):
                row_out(0, s, r).wait()

        nxt = (b + 2) * nrow
        for r in range(nrow):
            row_in(tok_ref[nxt + r], s2, r).start(priority=r % 2)
        cur = b * nrow
        for r in range(nrow):
            row_out(slot_ref[cur + r], s2, r).start(priority=r % 2)

        x = xbuf[s].astype(BF)
        hg = jnp.dot(x, wbf[0], preferred_element_type=F32)
        hu = jnp.dot(x, wbf[1], preferred_element_type=F32)
        act = (hg * _sig(hg) * hu).astype(BF)
        ybuf[s] = jnp.dot(act, wbf[2], preferred_element_type=F32)

        @pl.when(b == nb)
        def _():
            for sl in (s1, s2):
                for r in range(nrow):
                    row_in(0, sl, r).wait()
                for r in range(nrow):
                    row_out(0, sl, r).wait()

    ring = lax.rem(b, MOE_RING)
    for s in range(MOE_RING):
        pl.when(ring == s)(functools.partial(step, s))


def _moe_ffn_call(row_tok, row_slot, block_e, hf_all, wg, wu, wd, l):
    d = hf_all.shape[1]
    nb = block_e.shape[0] - 1
    assert nb >= MOE_RING
    wspec = pl.BlockSpec((None, None, d, d), lambda b, tok, slot, be: (l, be[b], 0, 0))
    return pl.pallas_call(
        functools.partial(_moe_ffn_kernel, nb=nb),
        out_shape=jax.ShapeDtypeStruct(((nb + 1) * MOE_BLK, d), F32),
        grid_spec=pltpu.PrefetchScalarGridSpec(
            num_scalar_prefetch=3,
            grid=(nb + 1,),
            in_specs=[pl.BlockSpec(memory_space=pl.ANY), wspec, wspec, wspec],
            out_specs=pl.BlockSpec(memory_space=pl.ANY),
            scratch_shapes=[pltpu.VMEM((MOE_RING, MOE_BLK, d), F32), pltpu.VMEM((MOE_RING, MOE_BLK, d), F32),
                            pltpu.VMEM((3, d, d), BF),
                            pltpu.SemaphoreType.DMA((MOE_RING,)), pltpu.SemaphoreType.DMA((MOE_RING,))]),
        compiler_params=_cparams(("arbitrary",)),
        name="moe_ffn",
    )(row_tok, row_slot, block_e, hf_all, wg, wu, wd)


def _moe_dispatch(sel, cw):
    t_all = sel.shape[0]
    blk = MOE_BLK
    nb = (2 * t_all) // blk + N_EXPERTS
    nr = nb * blk
    mi = (sel > 0.5).astype(jnp.int32)
    rank = jnp.cumsum(mi, axis=0) - mi
    counts = jnp.sum(mi, axis=0)
    padded = ((counts + blk - 1) // blk) * blk
    pend = jnp.cumsum(padded)
    dest = pend - padded + rank
    kk = jnp.cumsum(mi, axis=1) - mi
    first = (mi > 0) & (kk == 0)
    second = (mi > 0) & (kk == 1)
    d0 = jnp.sum(jnp.where(first, dest, 0), axis=1)
    d1 = jnp.sum(jnp.where(second, dest, 0), axis=1)
    w0 = jnp.sum(jnp.where(first, cw, 0.0), axis=1, keepdims=True)
    w1 = jnp.sum(jnp.where(second, cw, 0.0), axis=1, keepdims=True)
    tok2 = 2 * jnp.arange(t_all, dtype=jnp.int32)
    packed = jnp.full((nr,), -1, jnp.int32).at[jnp.concatenate([d0, d1])].set(
        jnp.concatenate([tok2, tok2 + 1]), unique_indices=True)
    is_pad = packed < 0
    row_tok = jnp.where(is_pad, 0, jnp.right_shift(packed, 1))
    blk_start = jnp.arange(nb, dtype=jnp.int32)[:, None] * blk
    block_e = jnp.clip(jnp.sum((blk_start >= pend[None, :]).astype(jnp.int32), axis=1), 0, N_EXPERTS - 1)
    real_before = jnp.repeat(jnp.cumsum(counts)[block_e], blk)
    pad_slot = 2 * t_all + jnp.arange(nr, dtype=jnp.int32) - real_before
    row_slot = jnp.where(is_pad, pad_slot, jnp.bitwise_and(packed, 1) * t_all + row_tok)
    row_tok = jnp.concatenate([row_tok, jnp.zeros((MOE_RING * blk,), jnp.int32)])
    row_slot = jnp.concatenate([nr + jnp.arange(blk, dtype=jnp.int32), row_slot])
    block_e = jnp.concatenate([block_e, block_e[-1:]])
    return row_tok, row_slot, block_e, w0, w1


def _combine_kernel(x_ref, y0_ref, y1_ref, w0_ref, w1_ref, gate_ref, lng_ref, lnb_ref, o_ref):
    y = y0_ref[...] * w0_ref[...] + y1_ref[...] * w1_ref[...]
    o_ref[...] = _layer_norm(ALPHA * x_ref[...] + gate_ref[...] * y, lng_ref[...], lnb_ref[...])


def _combine_call(x, y_slots, w0, w1, row_off, t_all, gate, lng, lnb, tm):
    L, d = x.shape
    off0 = row_off // tm
    off1 = (t_all + row_off) // tm
    vec = _full((1, d))
    return pl.pallas_call(
        _combine_kernel,
        out_shape=jax.ShapeDtypeStruct((L, d), F32),
        grid=(L // tm,),
        in_specs=[pl.BlockSpec((tm, d), lambda i: (i, 0)),
                  pl.BlockSpec((tm, d), lambda i: (off0 + i, 0)),
                  pl.BlockSpec((tm, d), lambda i: (off1 + i, 0)),
                  pl.BlockSpec((tm, 1), lambda i: (off0 + i, 0)),
                  pl.BlockSpec((tm, 1), lambda i: (off0 + i, 0)), vec, vec, vec],
        out_specs=pl.BlockSpec((tm, d), lambda i: (i, 0)),
        compiler_params=_cparams(("arbitrary",)),
        name="moe_combine",
    )(x, y_slots, y_slots, w0, w1, gate, lng, lnb)


def _moe_layer(segs, wrt_pad, rb_pad, lng, lnb, wg, wu, wd, l):
    routed = [_router_call(hf, wrt_pad, rb_pad) for (_, hf, _) in segs]
    cw = jnp.concatenate([r[0][:, :N_EXPERTS] for r in routed], axis=0)
    sel = jnp.concatenate([r[1][:, :N_EXPERTS] for r in routed], axis=0)
    hf_all = jnp.concatenate([hf for (_, hf, _) in segs], axis=0) if len(segs) > 1 else segs[0][1]
    t_all = hf_all.shape[0]
    row_tok, row_slot, block_e, w0, w1 = _moe_dispatch(sel, cw)
    y_slots = _moe_ffn_call(row_tok, row_slot, block_e, hf_all, wg, wu, wd, l)
    tm = min(256, min(x.shape[0] for (x, _, _) in segs))
    outs = []
    off = 0
    for (x, _, gate) in segs:
        assert off % tm == 0 and t_all % tm == 0 and x.shape[0] % tm == 0
        outs.append(_combine_call(x, y_slots, w0, w1, off, t_all, gate, lng, lnb, tm))
        off += x.shape[0]
    return outs


def _blockdiag2(m):
    z = jnp.zeros_like(m[0])
    return jnp.concatenate([jnp.concatenate([m[0], z], axis=1), jnp.concatenate([z, m[1]], axis=1)], axis=0)


def _mixer(x, mod_row, l, w_in_bf, prm, fp, hy, mp, s0, latent):
    L = x.shape[0]
    sh, sc, gate = mod_row[0], mod_row[1], mod_row[2]
    p_r, p_h, p_g = _inproj_call(x, sh, sc, w_in_bf, l)
    rvk, dirp, lg, bv = _prepare_call(p_r, prm, latent)
    yf, yb, sfin = _scan_call(rvk, dirp, s0)
    filt, asum = _hfilt_call(L, fp)
    z, x0 = _hconv3_call(p_h, hy["conv_w"], hy["conv_b"])
    ho = _long_conv_call(z, x0, filt, asum, hy["bias"])
    mpl = dict(mp)
    mpl["gate"], mpl["sh2"], mpl["sc2"] = gate, mod_row[3], mod_row[4]
    xn, hf = _merge_call(yf, yb, bv, lg, ho, p_g, x, mpl, l)
    return xn, hf, sfin


def kernel(x, c, ctx, c_ctx, w_mod, b_mod, w_in, rwkv_mu, rwkv_w0, rwkv_w2, rwkv_a0, rwkv_a2, rwkv_g2,
           rwkv_k_k, rwkv_k_a, rwkv_r_k, rwkv_lnx_g, rwkv_lnx_b, hy_conv_w, hy_conv_b, hy_f_w1, hy_f_b1,
           hy_f_w2, hy_f_b2, hy_f_w3, hy_f_b3, hy_f_wout, hy_freq, hy_bias, w_branch, w_out, ln_g, ln_b,
           w_router, router_bias, w_gate, w_up, w_down):
    b, n_lat, d = x.shape
    assert b == 1 and d == D_MODEL
    n_ctx = ctx.shape[1]
    depth = w_mod.shape[0]
    hw = RWKV_WIDTH
    xl = x[0]
    xc = ctx[0]

    cc = jnp.concatenate([c[:1], c_ctx[None, :], jnp.zeros((6, d), F32)], axis=0)
    mod = _mod_call(cc, w_mod, b_mod)

    head_of = np.arange(hw) // HEAD_DIM
    G = jnp.asarray((head_of[:, None] == head_of[None, :]).astype(np.float32), dtype=BF)
    bands = jnp.linspace(1e-4, FILTER_BANDS - 1, FILTER_BANDS, dtype=F32)[None, :]
    deltas = jnp.abs(jnp.linspace(HYENA_MIN_DECAY, HYENA_MAX_DECAY, HYENA_WIDTH, dtype=F32))[None, :]
    wr_pad = jnp.pad(w_router.T, ((0, LANES - N_EXPERTS), (0, 0)))
    rb_pad = jnp.pad(router_bias, (0, LANES - N_EXPERTS))[:, None]
    w_in_bf = w_in.astype(BF)
    w_branch_bf = w_branch.astype(BF)
    w_out_bf = w_out.astype(BF)
    g2_bf = rwkv_g2.astype(BF)

    for l in range(depth):
        last = l == depth - 1
        ml = [mod[l, 0:1, j * d:(j + 1) * d] for j in range(6)]
        mc = [mod[l, 1:2, j * d:(j + 1) * d] for j in range(6)]
        prm = dict(mu=rwkv_mu[l][None, :],
                   w0=rwkv_w0[l].reshape(1, 2 * hw), w2=_blockdiag2(rwkv_w2[l]),
                   a0=rwkv_a0[l].reshape(1, 2 * hw), a2=_blockdiag2(rwkv_a2[l]),
                   k_k=rwkv_k_k[l][None, :], k_a=rwkv_k_a[l][None, :], r_k=rwkv_r_k[l][None, :], G=G)
        w1 = hy_f_w1[l]
        fp = dict(bands=bands.T, w1t=w1[0:1].T, w1c=w1[1:1 + FILTER_BANDS].T, w1s=w1[1 + FILTER_BANDS:].T,
                  b1=hy_f_b1[l][:, None], w2=hy_f_w2[l].T, b2=hy_f_b2[l][:, None], w3=hy_f_w3[l].T,
                  b3=hy_f_b3[l][:, None], wout=hy_f_wout[l], freq=hy_freq[l][:, None], deltas=deltas)
        hy = dict(conv_w=hy_conv_w[l], conv_b=hy_conv_b[l][None, :], bias=hy_bias[l][None, :])
        mp = dict(lnx_g=rwkv_lnx_g[l][None, :], lnx_b=rwkv_lnx_b[l][None, :], g2=g2_bf, G=G,
                  w_branch=w_branch_bf, w_out=w_out_bf, ln_g=ln_g[l, 0][None, :], ln_b=ln_b[l, 0][None, :])
        s0 = jnp.zeros((2, HEAD_DIM, hw), F32)
        xc_new, hf_c, s_ctx = _mixer(xc, mc, l, w_in_bf, prm, fp, hy, mp, s0, latent=False)
        xl, hf_l, _ = _mixer(xl, ml, l, w_in_bf, prm, fp, hy, mp, s_ctx, latent=True)
        lng, lnb = ln_g[l, 1][None, :], ln_b[l, 1][None, :]
        if last:
            (xl,) = _moe_layer([(xl, hf_l, ml[5])], wr_pad, rb_pad, lng, lnb, w_gate, w_up, w_down, l)
        else:
            xc, xl = _moe_layer([(xc_new, hf_c, mc[5]), (xl, hf_l, ml[5])], wr_pad, rb_pad, lng, lnb,
                                w_gate, w_up, w_down, l)
    return xl[None]
```

```python
import functools
import math

import numpy as np
import jax
import jax.numpy as jnp
from jax import lax
from jax.experimental import pallas as pl
from jax.experimental.pallas import tpu as pltpu

F32 = jnp.float32
BF = jnp.bfloat16

D_MODEL = 1024
DEPTH = 4
GRID_W = 64
RWKV_WIDTH = 512
HEAD_DIM = 64
RWKV_HEADS = 8
LORA_W = 64
LORA_A = 64
LORA_G = 128
DECAY_SCALE = 0.606531
GN_EPS = 6.4e-4
RWKV_COLS = 3 * RWKV_WIDTH + 2 * LORA_W + 2 * LORA_A + LORA_G
HYENA_WIDTH = 512
HYENA_COLS = 3 * HYENA_WIDTH
FILTER_BANDS = 16
FILTER_HIDDEN = 64
HYENA_MIN_DECAY = math.log(1e-2) / 1.5
HYENA_MAX_DECAY = math.log(1e-2) / 0.3
GATE_COLS = 2 * D_MODEL
PROJ_COLS = RWKV_COLS + HYENA_COLS + GATE_COLS
N_EXPERTS = 16
N_GROUPS = 4
EXPERTS_PER_GROUP = 4
ALPHA = (2 * DEPTH) ** 0.25
LN_EPS = 1e-5

SCAN_CHUNK = 64
LANES = 128
VMEM_LIMIT = 56 * 1024 * 1024

_NN = (((1,), (0,)), ((), ()))
_NT = (((1,), (1,)), ((), ()))
_TN = (((0,), (0,)), ((), ()))


def _sig(x):
    return 1.0 / (1.0 + jnp.exp(-x))


def _parts(a, n):
    out = []
    rem = a
    for i in range(n):
        hi = rem.astype(BF)
        out.append(hi)
        if i + 1 < n:
            rem = rem - hi.astype(F32)
    return out


def _mm(a, b, dn=_NN, passes=1):
    n = {1: 1, 3: 2, 6: 3}[passes]
    pa = _parts(a, n)
    pb = _parts(b, n)
    acc = None
    for i in range(n):
        for j in range(n - i):
            t = lax.dot_general(pa[i], pb[j], dn, preferred_element_type=F32)
            acc = t if acc is None else acc + t
    return acc


def _mm_rx(a, b_exact, n=3, dn=_NN):
    acc = None
    for p in _parts(a, n):
        t = lax.dot_general(p, b_exact, dn, preferred_element_type=F32)
        acc = t if acc is None else acc + t
    return acc


def _mm_lx(a_exact, b, n=3, dn=_NN):
    acc = None
    for p in _parts(b, n):
        t = lax.dot_general(a_exact, p, dn, preferred_element_type=F32)
        acc = t if acc is None else acc + t
    return acc


def _cparams(sem, vmem=VMEM_LIMIT):
    return pltpu.CompilerParams(dimension_semantics=sem, vmem_limit_bytes=vmem)


def _full(shape):
    nd = len(shape)
    return pl.BlockSpec(shape, lambda *_: (0,) * nd)


def _mod_kernel(c_ref, w_ref, b_ref, o_ref):
    cc = c_ref[...]
    s = cc * _sig(cc)
    o_ref[...] = jnp.dot(s.astype(BF), w_ref[...].astype(BF), preferred_element_type=F32) + b_ref[...]


def _mod_call(cc, w_mod, b_mod):
    depth, d, n6 = w_mod.shape
    tn = 1536
    return pl.pallas_call(
        _mod_kernel,
        out_shape=jax.ShapeDtypeStruct((depth, 8, n6), F32),
        grid=(depth, n6 // tn),
        in_specs=[
            pl.BlockSpec((8, d), lambda l, j: (0, 0)),
            pl.BlockSpec((None, d, tn), lambda l, j: (l, 0, j)),
            pl.BlockSpec((None, 1, tn), lambda l, j: (l, 0, j)),
        ],
        out_specs=pl.BlockSpec((None, 8, tn), lambda l, j: (l, 0, j)),
        compiler_params=_cparams(("arbitrary", "arbitrary")),
        name="mod",
    )(cc, w_mod, b_mod.reshape(depth, 1, n6))


def _inproj_kernel(x_ref, sh_ref, sc_ref, w_ref, pr_ref, ph_ref, pg_ref):
    xm = (x_ref[...] * (1.0 + sc_ref[...]) + sh_ref[...]).astype(BF)
    pr_ref[...] = jnp.dot(xm, w_ref[:, :RWKV_COLS], preferred_element_type=F32)
    ph_ref[...] = jnp.dot(xm, w_ref[:, RWKV_COLS:RWKV_COLS + HYENA_COLS], preferred_element_type=F32)
    pg_ref[...] = jnp.dot(xm, w_ref[:, RWKV_COLS + HYENA_COLS:], preferred_element_type=F32)


def _inproj_call(x, sh, sc, w_bf, l):
    L, d = x.shape
    tm = min(L, 256)
    row = lambda i: (i, 0)
    return pl.pallas_call(
        _inproj_kernel,
        out_shape=(jax.ShapeDtypeStruct((L, RWKV_COLS), F32),
                   jax.ShapeDtypeStruct((L, HYENA_COLS), F32),
                   jax.ShapeDtypeStruct((L, GATE_COLS), F32)),
        grid=(L // tm,),
        in_specs=[pl.BlockSpec((tm, d), row), _full((1, d)), _full((1, d)),
                  pl.BlockSpec((None, d, PROJ_COLS), lambda i: (l, 0, 0))],
        out_specs=(pl.BlockSpec((tm, RWKV_COLS), row), pl.BlockSpec((tm, HYENA_COLS), row),
                   pl.BlockSpec((tm, GATE_COLS), row)),
        compiler_params=_cparams(("arbitrary",)),
        name="inproj",
    )(x, sh, sc, w_bf)


def _prepare_kernel(*refs, latent, nblk):
    if latent:
        p_ref, up_ref, dn_ref = refs[:3]
        refs = refs[3:]
    else:
        p_ref = refs[0]
        refs = refs[1:]
    (mu_ref, w0_ref, w2_ref, a0_ref, a2_ref, kk_ref, ka_ref, rk_ref, g_ref,
     rvk_out, dir_out, lg_out, bv_out) = refs
    i = pl.program_id(0)
    p = p_ref[...]
    tb, w = p.shape
    row = lax.broadcasted_iota(jnp.int32, (tb, 1), 0)
    col = lax.broadcasted_iota(jnp.int32, (1, w), 1)
    if latent:
        gw = jnp.bitwise_and(row, GRID_W - 1)
        first, last = i == 0, i == nblk - 1

        def src(kind, lo, hi):
            ps = p[:, lo:hi]
            if kind == 0:
                return jnp.where(gw == 0, 0.0, pltpu.roll(ps, 1, 0))
            if kind == 1:
                return jnp.where(gw == GRID_W - 1, 0.0, pltpu.roll(ps, tb - 1, 0))
            if kind == 2:
                halo = jnp.where(first, 0.0, up_ref[:, lo:hi])
                return jnp.concatenate([halo, ps[:tb - GRID_W]], axis=0) if tb > GRID_W else halo
            halo = jnp.where(last, 0.0, dn_ref[:, lo:hi])
            return jnp.concatenate([ps[GRID_W:], halo], axis=0) if tb > GRID_W else halo

        bounds = [k * (w // 4) for k in range(1, 4)]
    else:
        def src(kind, lo, hi):
            ps = p[:, lo:hi]
            if kind == 0:
                return jnp.where(row == 0, 0.0, pltpu.roll(ps, 1, 0))
            return jnp.where(row == tb - 1, 0.0, pltpu.roll(ps, tb - 1, 0))

        bounds = [w // 2]
    cuts = sorted({0, w} | {(b // LANES) * LANES for b in bounds} | {-(-b // LANES) * LANES for b in bounds})
    segs = []
    for lo, hi in zip(cuts[:-1], cuts[1:]):
        k_lo = sum(lo >= b for b in bounds)
        k_hi = sum(hi - 1 >= b for b in bounds)
        if k_lo == k_hi:
            segs.append(src(k_lo, lo, hi))
        else:
            segs.append(jnp.where(col[:, lo:hi] < bounds[k_lo], src(k_lo, lo, hi), src(k_hi, lo, hi)))
    sh = jnp.concatenate(segs, axis=1)
    pm = p + (sh - p) * mu_ref[...]
    hw = RWKV_WIDTH
    r = pm[:, :hw]
    k = pm[:, hw:2 * hw]
    v = pm[:, 2 * hw:3 * hw]
    lw_in = jnp.tanh(pm[:, 3 * hw:3 * hw + 2 * LORA_W])
    la_in = pm[:, 3 * hw + 2 * LORA_W:3 * hw + 2 * LORA_W + 2 * LORA_A]
    lg = pm[:, 3 * hw + 2 * LORA_W + 2 * LORA_A:]
    lw = -DECAY_SCALE * _sig(w0_ref[...] + _mm(lw_in, w2_ref[...], passes=3))
    a = _sig(a0_ref[...] + _mm(la_in, a2_ref[...], passes=3))
    g = g_ref[...]
    kkr = k * kk_ref[...]
    nrm = jnp.sqrt(_mm_rx(kkr * kkr, g, 3))
    kk = kkr / jnp.maximum(nrm, 1e-12)
    k_a = ka_ref[...]
    k_f = k * (1.0 + (a[:, :hw] - 1.0) * k_a)
    k_b = k * (1.0 + (a[:, hw:] - 1.0) * k_a)
    bonus = _mm_rx(r * (k_f + k_b) * rk_ref[...], g, 3)
    rvk_out[:, :hw] = r
    rvk_out[:, hw:2 * hw] = v
    rvk_out[:, 2 * hw:] = -kk
    for d, (k_d, a_d) in enumerate(((k_f, a[:, :hw]), (k_b, a[:, hw:]))):
        dir_out[d, :, :hw] = lw[:, d * hw:(d + 1) * hw]
        dir_out[d, :, hw:2 * hw] = k_d
        dir_out[d, :, 2 * hw:] = kk * a_d
    lg_out[...] = lg
    bv_out[...] = bonus * v


def _prepare_call(p_r, prm, latent):
    L, w = p_r.shape
    hw = RWKV_WIDTH
    tb = 256 if latent else L
    nblk = L // tb
    row = lambda i: (i, 0)
    row3 = lambda i: (0, i, 0)
    in_specs = [pl.BlockSpec((tb, w), row)]
    args = [p_r]
    if latent:
        per = tb // GRID_W
        nrow = L // GRID_W
        in_specs += [
            pl.BlockSpec((GRID_W, w), lambda i: (jnp.maximum(i * per - 1, 0), 0)),
            pl.BlockSpec((GRID_W, w), lambda i: (jnp.minimum((i + 1) * per, nrow - 1), 0)),
        ]
        args += [p_r, p_r]
    names = ("mu", "w0", "w2", "a0", "a2", "k_k", "k_a", "r_k", "G")
    for nm in names:
        in_specs.append(_full(prm[nm].shape))
        args.append(prm[nm])
    sd = jax.ShapeDtypeStruct
    out_shape = (sd((L, 3 * hw), F32), sd((2, L, 3 * hw), F32), sd((L, LORA_G), F32), sd((L, hw), F32))
    out_specs = (pl.BlockSpec((tb, 3 * hw), row), pl.BlockSpec((2, tb, 3 * hw), row3),
                 pl.BlockSpec((tb, LORA_G), row), pl.BlockSpec((tb, hw), row))
    return pl.pallas_call(
        functools.partial(_prepare_kernel, latent=latent, nblk=nblk),
        out_shape=out_shape, grid=(nblk,), in_specs=in_specs, out_specs=out_specs,
        compiler_params=_cparams(("arbitrary",)),
        name="rwkv_prepare_lat" if latent else "rwkv_prepare_ctx",
    )(*args)


P_SCORE = 1
P_TINV = 1
P_APPLY = 1
P_STATE = 1


SCAN_SUB = 2


def _scan_chunk(fin, bin_, states, masks):
    C = SCAN_CHUNK
    hd = HEAD_DIM
    nh = RWKV_HEADS
    row, col, eye, lvl_masks = masks
    dirs = []
    for d, (r, v, ka, lw, k, kb) in enumerate((fin, bin_)):
        inc = (row >= col) if d == 0 else (row <= col)
        strict = (row > col) if d == 0 else (row < col)
        lc = _mm_lx(jnp.where(inc, 1.0, 0.0).astype(BF), lw, 3)
        e_neg = jnp.exp(-lc)
        e_tot = jnp.exp(jnp.sum(lw, axis=0, keepdims=True))
        kbn = kb * e_neg
        kkn = k * e_neg
        dirs.append(dict(inc=inc, strict=strict, v=v, aq=ka * jnp.exp(lc - lw), rq=r * jnp.exp(lc),
                         kbn=kbn, kkn=kkn, kbp=kbn * e_tot, kkp=kkn * e_tot, e_tot=e_tot, s=states[d]))

    units = [(d, h) for h in range(nh) for d in range(2)]
    hs = lambda arr, h: arr[:, h * hd:(h + 1) * hd]
    sc = {}
    for (d, h) in units:
        D = dirs[d]
        sc[d, h] = _mm(jnp.concatenate([hs(D["aq"], h), hs(D["rq"], h)], axis=0),
                       jnp.concatenate([hs(D["kbn"], h), hs(D["kkn"], h)], axis=0), _NT, P_SCORE)
    a_ab, t, x = {}, {}, {}
    for u in units:
        a_ab[u] = jnp.where(dirs[u[0]]["strict"], sc[u][:C, :C], 0.0)
        t[u] = eye + jnp.where(lvl_masks[0], a_ab[u], 0.0)
    for u in units:
        D = dirs[u[0]]
        lhs = jnp.concatenate([jnp.where(D["strict"], sc[u][:C, C:], 0.0),
                               jnp.where(D["inc"], sc[u][C:, C:], 0.0)], axis=0)
        x[u] = _mm(lhs, hs(D["v"], u[1]), _NN, P_APPLY)
    for m in lvl_masks[1:]:
        tmp = {u: _mm(jnp.where(m, a_ab[u], 0.0), t[u], _NN, P_TINV) for u in units}
        for u in units:
            t[u] = t[u] + _mm(t[u], tmp[u], _NN, P_TINV)
    z, gm, uu, yy, sn = {}, {}, {}, {}, {}
    for u in units:
        z[u] = _mm(t[u], jnp.concatenate([hs(dirs[u[0]]["aq"], u[1]), x[u][:C]], axis=1), _NN, P_APPLY)
    for u in units:
        D = dirs[u[0]]
        gm[u] = _mm(jnp.concatenate([z[u][:, :hd], hs(D["rq"], u[1])], axis=0), hs(D["s"], u[1]), _NT, P_STATE)
        uu[u] = gm[u][:C] + z[u][:, hd:]
    for u in units:
        D = dirs[u[0]]
        b_rb = jnp.where(D["inc"], sc[u][C:, :C], 0.0)
        yy[u] = gm[u][C:] + _mm(b_rb, uu[u], _NN, P_APPLY) + x[u][C:]
        sn[u] = hs(D["s"], u[1]) * hs(D["e_tot"], u[1]) + _mm(
            jnp.concatenate([uu[u], hs(D["v"], u[1])], axis=0),
            jnp.concatenate([hs(D["kbp"], u[1]), hs(D["kkp"], u[1])], axis=0), _TN, P_STATE)
    cat = lambda d, src: jnp.concatenate([src[d, h] for h in range(nh)], axis=1)
    return cat(0, yy), cat(1, yy), [cat(0, sn), cat(1, sn)]


def _scan_kernel(sf_ref, sb_ref, df_ref, db_ref, s0_ref, yf_ref, yb_ref, sfin_ref, s_scr, *, nstep):
    c = pl.program_id(0)
    C = SCAN_CHUNK

    @pl.when(c == 0)
    def _():
        s_scr[...] = s0_ref[...]

    row = lax.broadcasted_iota(jnp.int32, (C, C), 0)
    col = lax.broadcasted_iota(jnp.int32, (C, C), 1)
    eye = jnp.where(row == col, 1.0, 0.0)
    lvl_masks = []
    for sh in range(int(math.log2(C))):
        same2 = jnp.right_shift(row, sh + 1) == jnp.right_shift(col, sh + 1)
        same1 = jnp.right_shift(row, sh) == jnp.right_shift(col, sh)
        lvl_masks.append(jnp.logical_and(same2, jnp.logical_not(same1)))
    masks = (row, col, eye, lvl_masks)

    hw = RWKV_WIDTH
    unpack = lambda sref, dref, rows: tuple(ref[rows, i * hw:(i + 1) * hw] for ref in (sref, dref) for i in range(3))
    states = [s_scr[0], s_scr[1]]
    for j in range(SCAN_SUB):
        fsl = slice(j * C, (j + 1) * C)
        bsl = slice((SCAN_SUB - 1 - j) * C, (SCAN_SUB - j) * C)
        yf, yb, states = _scan_chunk(unpack(sf_ref, df_ref, fsl), unpack(sb_ref, db_ref, bsl), states, masks)
        yf_ref[fsl, :] = yf
        yb_ref[bsl, :] = yb
    s_scr[0] = states[0]
    s_scr[1] = states[1]

    @pl.when(c == nstep - 1)
    def _():
        sfin_ref[...] = s_scr[...]


def _scan_call(rvk, dirp, s0):
    L = rvk.shape[0]
    hw = RWKV_WIDTH
    C = SCAN_CHUNK * SCAN_SUB
    nchunk = L // C
    assert L % C == 0
    sh_f = pl.BlockSpec((C, 3 * hw), lambda c: (c, 0))
    sh_b = pl.BlockSpec((C, 3 * hw), lambda c: (nchunk - 1 - c, 0))
    pd_f = pl.BlockSpec((None, C, 3 * hw), lambda c: (0, c, 0))
    pd_b = pl.BlockSpec((None, C, 3 * hw), lambda c: (1, nchunk - 1 - c, 0))
    y_f = pl.BlockSpec((C, hw), lambda c: (c, 0))
    y_b = pl.BlockSpec((C, hw), lambda c: (nchunk - 1 - c, 0))
    state = _full((2, HEAD_DIM, hw))
    return pl.pallas_call(
        functools.partial(_scan_kernel, nstep=nchunk),
        out_shape=(jax.ShapeDtypeStruct((L, hw), F32), jax.ShapeDtypeStruct((L, hw), F32),
                   jax.ShapeDtypeStruct((2, HEAD_DIM, hw), F32)),
        grid=(nchunk,),
        in_specs=[sh_f, sh_b, pd_f, pd_b, state],
        out_specs=(y_f, y_b, state),
        scratch_shapes=[pltpu.VMEM((2, HEAD_DIM, hw), F32)],
        compiler_params=_cparams(("arbitrary",)),
        name="delta_scan",
    )(rvk, rvk, dirp, dirp, s0)


def _hfilt_kernel(bands_ref, w1t_ref, w1c_ref, w1s_ref, b1_ref, w2_ref, b2_ref, w3_ref, b3_ref, wo_ref,
                  fr_ref, dl_ref, f_ref, asum_ref, *, n, rb):
    i = pl.program_id(0)
    posr = (i * rb + lax.broadcasted_iota(jnp.int32, (1, rb), 1)).astype(F32)
    tr = posr / float(max(n - 1, 1))
    ang = ((2.0 * math.pi / n) * posr) * bands_ref[...]
    fr = fr_ref[...]
    h = (w1t_ref[...] * tr + _mm(w1c_ref[...], jnp.cos(ang), passes=3)
         + _mm(w1s_ref[...], -jnp.sin(ang), passes=3))
    h = jnp.sin(fr * (h + b1_ref[...]))
    h = jnp.sin(fr * (_mm(w2_ref[...], h, passes=3) + b2_ref[...]))
    h = jnp.sin(fr * (_mm(w3_ref[...], h, passes=3) + b3_ref[...]))
    filt = _mm(h, wo_ref[...], _TN, 3)
    pos = (i * rb + lax.broadcasted_iota(jnp.int32, (rb, 1), 0)).astype(F32)
    dist = jnp.abs(pos - float(n // 2)) * (2.0 / n)
    filt = filt * jnp.exp(-dist * dl_ref[...])
    f_ref[...] = filt

    @pl.when(i == 0)
    def _():
        asum_ref[...] = jnp.zeros_like(asum_ref)

    asum_ref[...] += jnp.sum(jnp.abs(filt), axis=0, keepdims=True)


def _hfilt_call(n, fp):
    rb = min(n, 512)
    names = ("bands", "w1t", "w1c", "w1s", "b1", "w2", "b2", "w3", "b3", "wout", "freq", "deltas")
    args = [fp[nm] for nm in names]
    return pl.pallas_call(
        functools.partial(_hfilt_kernel, n=n, rb=rb),
        out_shape=(jax.ShapeDtypeStruct((n, HYENA_WIDTH), F32), jax.ShapeDtypeStruct((1, HYENA_WIDTH), F32)),
        grid=(n // rb,),
        in_specs=[_full(a.shape) for a in args],
        out_specs=(pl.BlockSpec((rb, HYENA_WIDTH), lambda i: (i, 0)), _full((1, HYENA_WIDTH))),
        compiler_params=_cparams(("arbitrary",)),
        name="hyena_filter",
    )(*args)


def _hconv3_kernel(p_ref, pv_ref, nx_ref, cw_ref, cb_ref, z_ref, x0_ref, *, nblk):
    i = pl.program_id(0)
    p = p_ref[...]
    tb = p.shape[0]
    row = lax.broadcasted_iota(jnp.int32, (tb, 1), 0)
    prev_row = jnp.where(i == 0, 0.0, pv_ref[7:8, :])
    next_row = jnp.where(i == nblk - 1, 0.0, nx_ref[0:1, :])
    sp = jnp.where(row == 0, prev_row, pltpu.roll(p, 1, 0))
    sn = jnp.where(row == tb - 1, next_row, pltpu.roll(p, tb - 1, 0))
    u = sp * cw_ref[0:1, :] + p * cw_ref[1:2, :] + sn * cw_ref[2:3, :] + cb_ref[...]
    hw = HYENA_WIDTH
    z_ref[...] = u[:, 2 * hw:] * u[:, hw:2 * hw]
    x0_ref[...] = u[:, :hw]


def _hconv3_call(p_h, cw, cb):
    L, w = p_h.shape
    tb = min(L, 256)
    nblk = L // tb
    per = tb // 8
    row = lambda i: (i, 0)
    return pl.pallas_call(
        functools.partial(_hconv3_kernel, nblk=nblk),
        out_shape=(jax.ShapeDtypeStruct((L, HYENA_WIDTH), F32), jax.ShapeDtypeStruct((L, HYENA_WIDTH), F32)),
        grid=(nblk,),
        in_specs=[pl.BlockSpec((tb, w), row),
                  pl.BlockSpec((8, w), lambda i: (jnp.maximum(i * per - 1, 0), 0)),
                  pl.BlockSpec((8, w), lambda i: (jnp.minimum((i + 1) * per, L // 8 - 1), 0)),
                  _full(cw.shape), _full(cb.shape)],
        out_specs=(pl.BlockSpec((tb, HYENA_WIDTH), row), pl.BlockSpec((tb, HYENA_WIDTH), row)),
        compiler_params=_cparams(("arbitrary",)),
        name="hyena_conv3",
    )(p_h, p_h, p_h, cw, cb)


P_DFT = 1


DFT_NB = 8


def _dft1_kernel(m_ref, x_ref, o_ref):
    m = m_ref[...]
    n1 = o_ref.shape[2]
    for j in range(x_ref.shape[1]):
        r = _mm(m, x_ref[:, j, :], _NN, P_DFT)
        o_ref[0, j] = r[:n1]
        o_ref[1, j] = r[n1:]


def _dft1_call(m1, x3):
    rows, kdim = m1.shape
    _, n2, c = x3.shape
    n1 = rows // 2
    nbk = min(n2, DFT_NB)
    return pl.pallas_call(
        _dft1_kernel,
        out_shape=jax.ShapeDtypeStruct((2, n2, n1, c), F32),
        grid=(n2 // nbk,),
        in_specs=[_full(m1.shape), pl.BlockSpec((kdim, nbk, c), lambda j: (0, j, 0))],
        out_specs=pl.BlockSpec((2, nbk, n1, c), lambda j: (0, j, 0, 0)),
        compiler_params=_cparams(("arbitrary",)),
        name="dft_stage1",
    )(m1, x3)


def _dft2_matrix(fc, fs, tc, ts):
    fre = fc * tc - fs * ts
    fim = -(fc * ts + fs * tc)
    return jnp.concatenate([jnp.concatenate([fre, -fim], axis=1),
                            jnp.concatenate([fim, fre], axis=1)], axis=0)


def _dft2_conv_kernel(fc_ref, fs_ref, tc_ref, ts_ref, a_ref, af_ref, b_ref):
    fc, fs = fc_ref[...], fs_ref[...]
    n2, c = a_ref.shape[1], a_ref.shape[3]
    for kk in range(a_ref.shape[2]):
        big = _dft2_matrix(fc, fs, tc_ref[kk], ts_ref[kk])
        rhs = jnp.concatenate([jnp.concatenate([a_ref[0, :, kk, :], af_ref[0, :, kk, :]], axis=1),
                               jnp.concatenate([a_ref[1, :, kk, :], af_ref[1, :, kk, :]], axis=1)], axis=0)
        xh = _mm(big, rhs, _NN, P_DFT)
        xr, hr = xh[:n2, :c], xh[:n2, c:]
        xi, hi = xh[n2:, :c], xh[n2:, c:]
        y = jnp.concatenate([xr * hr - xi * hi, xr * hi + xi * hr], axis=0)
        bb = _mm(big, y, _TN, P_DFT)
        b_ref[0, kk] = bb[:n2]
        b_ref[1, kk] = bb[n2:]


def _dft2_call(consts, a4, af4):
    _, n2, n1, c = a4.shape
    kb = min(n1, DFT_NB)
    blk = pl.BlockSpec((2, n2, kb, c), lambda q: (0, 0, q, 0))
    tw = pl.BlockSpec((kb, 1, n2), lambda q: (q, 0, 0))
    return pl.pallas_call(
        _dft2_conv_kernel,
        out_shape=jax.ShapeDtypeStruct((2, n1, n2, c), F32),
        grid=(n1 // kb,),
        in_specs=[_full((n2, n2)), _full((n2, n2)), tw, tw, blk, blk],
        out_specs=pl.BlockSpec((2, kb, n2, c), lambda q: (0, q, 0, 0)),
        compiler_params=_cparams(("arbitrary",)),
        name="dft_stage2_conv",
    )(consts["fc"], consts["fs"], consts["twc"], consts["tws"], a4, af4)


def _dft3_kernel(m_ref, b_ref, z_ref, x0_ref, bias_ref, asum_ref, o_ref):
    m = m_ref[...]
    for j in range(z_ref.shape[1]):
        bj = jnp.concatenate([b_ref[0, :, j, :], b_ref[1, :, j, :]], axis=0)
        y = _mm(m, bj, _NN, P_DFT)
        o_ref[:, j, :] = (y / asum_ref[...] + z_ref[:, j, :] * bias_ref[...]) * x0_ref[:, j, :]


def _dft3_call(m3, b4, z3, x03, bias, asum):
    rows, _ = m3.shape
    _, n1, n2, c = b4.shape
    nbk = min(n2, DFT_NB)
    slab = pl.BlockSpec((rows, nbk, c), lambda j: (0, j, 0))
    return pl.pallas_call(
        _dft3_kernel,
        out_shape=jax.ShapeDtypeStruct((rows, n2, c), F32),
        grid=(n2 // nbk,),
        in_specs=[_full(m3.shape), pl.BlockSpec((2, n1, nbk, c), lambda j: (0, 0, j, 0)), slab, slab,
                  _full((1, c)), _full((1, c))],
        out_specs=slab,
        compiler_params=_cparams(("arbitrary",)),
        name="dft_stage3",
    )(m3, b4, z3, x03, bias, asum)


def _conv_direct_kernel(m1_ref, m3_ref, z_ref, f_ref, x0_ref, bias_ref, asum_ref, o_ref):
    m1 = m1_ref[...]
    z = z_ref[...]
    a = _mm(m1, z, _NN, P_DFT)
    h = _mm(m1, f_ref[...], _NN, P_DFT)
    nn = a.shape[0] // 2
    ar, ai, hr, hi = a[:nn], a[nn:], h[:nn], h[nn:]
    y = jnp.concatenate([ar * hr - ai * hi, ar * hi + ai * hr], axis=0)
    out = _mm(m3_ref[...], y, _NN, P_DFT)
    o_ref[...] = (out / asum_ref[...] + z * bias_ref[...]) * x0_ref[...]


def _conv_direct_call(m1, m3, z, filt, x0, bias, asum):
    args = (m1, m3, z, filt, x0, bias, asum)
    return pl.pallas_call(
        _conv_direct_kernel,
        out_shape=jax.ShapeDtypeStruct(z.shape, F32),
        grid=(1,),
        in_specs=[_full(a.shape) for a in args],
        out_specs=_full(z.shape),
        compiler_params=_cparams(("arbitrary",)),
        name="long_conv_direct",
    )(*args)


DIRECT_CONV_MAX = 256


@functools.lru_cache(maxsize=None)
def _dft_consts_np(n):
    big_n = 2 * n
    if n <= DIRECT_CONV_MAX:
        n1 = big_n
    else:
        n1 = 256 if n >= 8192 else 64
    n2 = big_n // n1
    k1 = np.arange(n1)[:, None].astype(np.float64)
    j1 = np.arange(n1 // 2)[None, :].astype(np.float64)
    ang1 = 2.0 * np.pi * k1 * j1 / n1
    m1 = np.concatenate([np.cos(ang1), -np.sin(ang1)], axis=0)
    o1 = (n1 // 4 + np.arange(n1 // 2))[:, None].astype(np.float64)
    q1 = np.arange(n1)[None, :].astype(np.float64)
    ang3 = 2.0 * np.pi * o1 * q1 / n1
    m3 = np.concatenate([np.cos(ang3), -np.sin(ang3)], axis=1) / big_n
    k2 = np.arange(n2)[:, None].astype(np.float64)
    j2 = np.arange(n2)[None, :].astype(np.float64)
    ang2 = 2.0 * np.pi * k2 * j2 / n2
    angt = 2.0 * np.pi * np.arange(n1)[:, None].astype(np.float64) * j2 / big_n
    f = lambda a: np.asarray(a, np.float32)
    return dict(n1=n1, n2=n2, m1=f(m1), m3=f(m3), fc=f(np.cos(ang2)), fs=f(np.sin(ang2)),
                twc=f(np.cos(angt))[:, None, :], tws=f(np.sin(angt))[:, None, :])


def _long_conv_call(z, x0, filt, asum, bias):
    n, c = z.shape
    cn = _dft_consts_np(n)
    n1, n2 = cn["n1"], cn["n2"]
    consts = {kk: jnp.asarray(vv) for kk, vv in cn.items() if kk not in ("n1", "n2")}
    if n2 == 1:
        return _conv_direct_call(consts["m1"], consts["m3"], z, filt, x0, bias, asum)
    z3 = z.reshape(n1 // 2, n2, c)
    a_f = _dft1_call(consts["m1"], filt.reshape(n1 // 2, n2, c))
    a_z = _dft1_call(consts["m1"], z3)
    b4 = _dft2_call(consts, a_z, a_f)
    out = _dft3_call(consts["m3"], b4, z3, x0.reshape(n1 // 2, n2, c), bias, asum)
    return out.reshape(n, c)


def _layer_norm(x, g, b):
    mu = jnp.mean(x, axis=-1, keepdims=True)
    xc = x - mu
    var = jnp.mean(xc * xc, axis=-1, keepdims=True)
    return xc * lax.rsqrt(var + LN_EPS) * g + b


def _merge_kernel(yf_ref, yb_ref, bv_ref, lg_ref, ho_ref, pg_ref, x_ref, gx_ref, bx_ref, g2_ref, g_ref,
                  wb_ref, wo_ref, gate_ref, lng_ref, lnb_ref, sh2_ref, sc2_ref, o_ref, hf_ref):
    g = g_ref[...]
    ys = yf_ref[...] + yb_ref[...]
    inv_hd = 1.0 / HEAD_DIM
    mu = _mm_rx(ys, g, 3) * inv_hd
    dd = ys - mu
    var = _mm_rx(dd * dd, g, 3) * inv_hd
    yn = dd * lax.rsqrt(var + GN_EPS) * gx_ref[...] + bx_ref[...]
    gate_r = jnp.dot(_sig(lg_ref[...]).astype(BF), g2_ref[...], preferred_element_type=F32)
    ro = (yn + bv_ref[...]) * gate_r
    br = jnp.dot(ro.astype(BF), wb_ref[0], preferred_element_type=F32)
    bh = jnp.dot(ho_ref[...].astype(BF), wb_ref[1], preferred_element_type=F32)
    sg = _sig(pg_ref[...])
    m = sg[:, :D_MODEL] * br + sg[:, D_MODEL:] * bh
    mix = jnp.dot(m.astype(BF), wo_ref[...], preferred_element_type=F32)
    xn = _layer_norm(ALPHA * x_ref[...] + gate_ref[...] * mix, lng_ref[...], lnb_ref[...])
    o_ref[...] = xn
    hf_ref[...] = xn * (1.0 + sc2_ref[...]) + sh2_ref[...]


def _merge_call(yf, yb, bv, lg, ho, pg, x, mp, l):
    L, d = x.shape
    hw = RWKV_WIDTH
    tb = min(L, 256)
    row = lambda i: (i, 0)
    names = ("lnx_g", "lnx_b", "g2", "G", "w_branch", "w_out", "gate", "ln_g", "ln_b", "sh2", "sc2")
    pargs = [mp[nm] for nm in names]

    def pspec(nm, a):
        if nm in ("g2", "w_branch", "w_out"):
            nd = a.ndim - 1
            return pl.BlockSpec((None,) + a.shape[1:], lambda i: (l,) + (0,) * nd)
        return _full(a.shape)

    return pl.pallas_call(
        _merge_kernel,
        out_shape=(jax.ShapeDtypeStruct((L, d), F32), jax.ShapeDtypeStruct((L, d), F32)),
        grid=(L // tb,),
        in_specs=[pl.BlockSpec((tb, hw), row), pl.BlockSpec((tb, hw), row), pl.BlockSpec((tb, hw), row),
                  pl.BlockSpec((tb, LORA_G), row), pl.BlockSpec((tb, hw), row),
                  pl.BlockSpec((tb, GATE_COLS), row), pl.BlockSpec((tb, d), row)]
                 + [pspec(nm, a) for nm, a in zip(names, pargs)],
        out_specs=(pl.BlockSpec((tb, d), row), pl.BlockSpec((tb, d), row)),
        compiler_params=_cparams(("arbitrary",)),
        name="merge_postnorm",
    )(yf, yb, bv, lg, ho, pg, x, *pargs)


def _router_kernel(hf_ref, wrt_ref, rb_ref, cw_ref, sel_ref):
    lt = _mm(wrt_ref[...], hf_ref[...], _NT, 6)
    rid = lax.broadcasted_iota(jnp.int32, (LANES, 1), 0)
    valid = rid < N_EXPERTS
    lg = jnp.where(valid, lt, -jnp.inf)
    mx = jnp.max(lg, axis=0, keepdims=True)
    ex = jnp.where(valid, jnp.exp(lg - mx), 0.0)
    scores = ex / jnp.sum(ex, axis=0, keepdims=True)
    sel = scores + rb_ref[...]
    s = [sel[e:e + 1, :] for e in range(N_EXPERTS)]
    p = [scores[e:e + 1, :] for e in range(N_EXPERTS)]
    gs = []
    for gi in range(N_GROUPS):
        mem = s[gi * EXPERTS_PER_GROUP:(gi + 1) * EXPERTS_PER_GROUP]
        best = None
        for a in range(EXPERTS_PER_GROUP):
            for b in range(a + 1, EXPERTS_PER_GROUP):
                pair = mem[a] + mem[b]
                best = pair if best is None else jnp.maximum(best, pair)
        gs.append(best)
    bg = jnp.where((gs[0] >= gs[1]) & (gs[0] >= gs[2]) & (gs[0] >= gs[3]), 0,
                   jnp.where((gs[1] >= gs[2]) & (gs[1] >= gs[3]), 1, jnp.where(gs[2] >= gs[3], 2, 3)))
    chosen = []
    for e in range(N_EXPERTS):
        gi = e // EXPERTS_PER_GROUP
        beats = None
        for j in range(gi * EXPERTS_PER_GROUP, (gi + 1) * EXPERTS_PER_GROUP):
            if j == e:
                continue
            cond = (s[j] >= s[e]) if j < e else (s[j] > s[e])
            cnt = jnp.where(cond, 1.0, 0.0)
            beats = cnt if beats is None else beats + cnt
        chosen.append((bg == gi) & (beats < 1.5))
    den = None
    for e in range(N_EXPERTS):
        t = jnp.where(chosen[e], p[e], 0.0)
        den = t if den is None else den + t
    out = jnp.zeros(lt.shape, F32)
    msk = jnp.zeros(lt.shape, F32)
    for e in range(N_EXPERTS):
        hit = (rid == e) & chosen[e]
        out = jnp.where(hit, p[e] / den, out)
        msk = jnp.where(hit, 1.0, msk)
    cw_ref[...] = out.T
    sel_ref[...] = msk.T


def _router_call(hf, wrt_pad, rb_pad):
    L, d = hf.shape
    tb = min(L, 256)
    row = lambda i: (i, 0)
    return pl.pallas_call(
        _router_kernel,
        out_shape=(jax.ShapeDtypeStruct((L, LANES), F32), jax.ShapeDtypeStruct((L, LANES), F32)),
        grid=(L // tb,),
        in_specs=[pl.BlockSpec((tb, d), row), _full(wrt_pad.shape), _full(rb_pad.shape)],
        out_specs=(pl.BlockSpec((tb, LANES), row), pl.BlockSpec((tb, LANES), row)),
        compiler_params=_cparams(("arbitrary",)),
        name="router",
    )(hf, wrt_pad, rb_pad)


MOE_BLK = 256


MOE_RING = 3


def _moe_ffn_kernel(tok_ref, slot_ref, be_ref, hf_hbm, wg_ref, wu_ref, wd_ref, y_hbm,
                    xbuf, ybuf, wbf, gsem, ssem, *, nb):
    b = pl.program_id(0)
    nrow = MOE_BLK

    def row_in(idx, sl, r):
        return pltpu.make_async_copy(hf_hbm.at[pl.ds(idx, 1)], xbuf.at[sl, pl.ds(r, 1)], gsem.at[sl])

    def row_out(idx, sl, r):
        return pltpu.make_async_copy(ybuf.at[sl, pl.ds(r, 1)], y_hbm.at[pl.ds(idx, 1)], ssem.at[sl])

    @pl.when(b == 0)
    def _():
        for blk in range(MOE_RING - 1):
            for r in range(nrow):
                row_in(tok_ref[blk * nrow + r], blk, r).start()
        ybuf[MOE_RING - 1] = jnp.zeros(ybuf.shape[1:], F32)

    @pl.when((b == 0) | (be_ref[b] != be_ref[jnp.maximum(b - 1, 0)]))
    def _():
        wbf[0] = wg_ref[...].astype(BF)
        wbf[1] = wu_ref[...].astype(BF)
        wbf[2] = wd_ref[...].astype(BF)

    def step(s):
        s1, s2 = (s + 1) % MOE_RING, (s + 2) % MOE_RING
        for r in range(nrow):
            row_in(0, s, r).wait()

        @pl.when(b >= 2)
        def _():
            for r in range(nrow):
                row_out(0, s, r).wait()

        nxt = (b + 2) * nrow
        for r in range(nrow):
            row_in(tok_ref[nxt + r], s2, r).start(priority=r % 2)
        cur = b * nrow
        for r in range(nrow):
            row_out(slot_ref[cur + r], s2, r).start(priority=r % 2)

        x = xbuf[s].astype(BF)
        hg = jnp.dot(x, wbf[0], preferred_element_type=F32)
        hu = jnp.dot(x, wbf[1], preferred_element_type=F32)
        act = (hg * _sig(hg) * hu).astype(BF)
        ybuf[s] = jnp.dot(act, wbf[2], preferred_element_type=F32)

        @pl.when(b == nb)
        def _():
            for sl in (s1, s2):
                for r in range(nrow):
                    row_in(0, sl, r).wait()
                for r in range(nrow):
                    row_out(0, sl, r).wait()

    ring = lax.rem(b, MOE_RING)
    for s in range(MOE_RING):
        pl.when(ring == s)(functools.partial(step, s))


def _moe_ffn_call(row_tok, row_slot, block_e, hf_all, wg, wu, wd, l):
    d = hf_all.shape[1]
    nb = block_e.shape[0] - 1
    assert nb >= MOE_RING
    wspec = pl.BlockSpec((None, None, d, d), lambda b, tok, slot, be: (l, be[b], 0, 0))
    return pl.pallas_call(
        functools.partial(_moe_ffn_kernel, nb=nb),
        out_shape=jax.ShapeDtypeStruct(((nb + 1) * MOE_BLK, d), F32),
        grid_spec=pltpu.PrefetchScalarGridSpec(
            num_scalar_prefetch=3,
            grid=(nb + 1,),
            in_specs=[pl.BlockSpec(memory_space=pl.ANY), wspec, wspec, wspec],
            out_specs=pl.BlockSpec(memory_space=pl.ANY),
            scratch_shapes=[pltpu.VMEM((MOE_RING, MOE_BLK, d), F32), pltpu.VMEM((MOE_RING, MOE_BLK, d), F32),
                            pltpu.VMEM((3, d, d), BF),
                            pltpu.SemaphoreType.DMA((MOE_RING,)), pltpu.SemaphoreType.DMA((MOE_RING,))]),
        compiler_params=_cparams(("arbitrary",)),
        name="moe_ffn",
    )(row_tok, row_slot, block_e, hf_all, wg, wu, wd)


def _moe_dispatch(sel, cw):
    t_all = sel.shape[0]
    blk = MOE_BLK
    nb = (2 * t_all) // blk + N_EXPERTS
    nr = nb * blk
    mi = (sel > 0.5).astype(jnp.int32)
    rank = jnp.cumsum(mi, axis=0) - mi
    counts = jnp.sum(mi, axis=0)
    padded = ((counts + blk - 1) // blk) * blk
    pend = jnp.cumsum(padded)
    dest = pend - padded + rank
    kk = jnp.cumsum(mi, axis=1) - mi
    first = (mi > 0) & (kk == 0)
    second = (mi > 0) & (kk == 1)
    d0 = jnp.sum(jnp.where(first, dest, 0), axis=1)
    d1 = jnp.sum(jnp.where(second, dest, 0), axis=1)
    w0 = jnp.sum(jnp.where(first, cw, 0.0), axis=1, keepdims=True)
    w1 = jnp.sum(jnp.where(second, cw, 0.0), axis=1, keepdims=True)
    tok2 = 2 * jnp.arange(t_all, dtype=jnp.int32)
    packed = jnp.full((nr,), -1, jnp.int32).at[jnp.concatenate([d0, d1])].set(
        jnp.concatenate([tok2, tok2 + 1]), unique_indices=True)
    is_pad = packed < 0
    row_tok = jnp.where(is_pad, 0, jnp.right_shift(packed, 1))
    blk_start = jnp.arange(nb, dtype=jnp.int32)[:, None] * blk
    block_e = jnp.clip(jnp.sum((blk_start >= pend[None, :]).astype(jnp.int32), axis=1), 0, N_EXPERTS - 1)
    real_before = jnp.repeat(jnp.cumsum(counts)[block_e], blk)
    pad_slot = 2 * t_all + jnp.arange(nr, dtype=jnp.int32) - real_before
    row_slot = jnp.where(is_pad, pad_slot, jnp.bitwise_and(packed, 1) * t_all + row_tok)
    row_tok = jnp.concatenate([row_tok, jnp.zeros((MOE_RING * blk,), jnp.int32)])
    row_slot = jnp.concatenate([nr + jnp.arange(blk, dtype=jnp.int32), row_slot])
    block_e = jnp.concatenate([block_e, block_e[-1:]])
    return row_tok, row_slot, block_e, w0, w1


def _combine_kernel(x_ref, y0_ref, y1_ref, w0_ref, w1_ref, gate_ref, lng_ref, lnb_ref, o_ref):
    y = y0_ref[...] * w0_ref[...] + y1_ref[...] * w1_ref[...]
    o_ref[...] = _layer_norm(ALPHA * x_ref[...] + gate_ref[...] * y, lng_ref[...], lnb_ref[...])


def _combine_call(x, y_slots, w0, w1, row_off, t_all, gate, lng, lnb, tm):
    L, d = x.shape
    off0 = row_off // tm
    off1 = (t_all + row_off) // tm
    vec = _full((1, d))
    return pl.pallas_call(
        _combine_kernel,
        out_shape=jax.ShapeDtypeStruct((L, d), F32),
        grid=(L // tm,),
        in_specs=[pl.BlockSpec((tm, d), lambda i: (i, 0)),
                  pl.BlockSpec((tm, d), lambda i: (off0 + i, 0)),
                  pl.BlockSpec((tm, d), lambda i: (off1 + i, 0)),
                  pl.BlockSpec((tm, 1), lambda i: (off0 + i, 0)),
                  pl.BlockSpec((tm, 1), lambda i: (off0 + i, 0)), vec, vec, vec],
        out_specs=pl.BlockSpec((tm, d), lambda i: (i, 0)),
        compiler_params=_cparams(("arbitrary",)),
        name="moe_combine",
    )(x, y_slots, y_slots, w0, w1, gate, lng, lnb)


def _moe_layer(segs, wrt_pad, rb_pad, lng, lnb, wg, wu, wd, l):
    routed = [_router_call(hf, wrt_pad, rb_pad) for (_, hf, _) in segs]
    cw = jnp.concatenate([r[0][:, :N_EXPERTS] for r in routed], axis=0)
    sel = jnp.concatenate([r[1][:, :N_EXPERTS] for r in routed], axis=0)
    hf_all = jnp.concatenate([hf for (_, hf, _) in segs], axis=0) if len(segs) > 1 else segs[0][1]
    t_all = hf_all.shape[0]
    row_tok, row_slot, block_e, w0, w1 = _moe_dispatch(sel, cw)
    y_slots = _moe_ffn_call(row_tok, row_slot, block_e, hf_all, wg, wu, wd, l)
    tm = min(256, min(x.shape[0] for (x, _, _) in segs))
    outs = []
    off = 0
    for (x, _, gate) in segs:
        assert off % tm == 0 and t_all % tm == 0 and x.shape[0] % tm == 0
        outs.append(_combine_call(x, y_slots, w0, w1, off, t_all, gate, lng, lnb, tm))
        off += x.shape[0]
    return outs


def _blockdiag2(m):
    z = jnp.zeros_like(m[0])
    return jnp.concatenate([jnp.concatenate([m[0], z], axis=1), jnp.concatenate([z, m[1]], axis=1)], axis=0)


def _mixer(x, mod_row, l, w_in_bf, prm, fp, hy, mp, s0, latent):
    L = x.shape[0]
    sh, sc, gate = mod_row[0], mod_row[1], mod_row[2]
    p_r, p_h, p_g = _inproj_call(x, sh, sc, w_in_bf, l)
    rvk, dirp, lg, bv = _prepare_call(p_r, prm, latent)
    yf, yb, sfin = _scan_call(rvk, dirp, s0)
    filt, asum = _hfilt_call(L, fp)
    z, x0 = _hconv3_call(p_h, hy["conv_w"], hy["conv_b"])
    ho = _long_conv_call(z, x0, filt, asum, hy["bias"])
    mpl = dict(mp)
    mpl["gate"], mpl["sh2"], mpl["sc2"] = gate, mod_row[3], mod_row[4]
    xn, hf = _merge_call(yf, yb, bv, lg, ho, p_g, x, mpl, l)
    return xn, hf, sfin


def kernel(x, c, ctx, c_ctx, w_mod, b_mod, w_in, rwkv_mu, rwkv_w0, rwkv_w2, rwkv_a0, rwkv_a2, rwkv_g2,
           rwkv_k_k, rwkv_k_a, rwkv_r_k, rwkv_lnx_g, rwkv_lnx_b, hy_conv_w, hy_conv_b, hy_f_w1, hy_f_b1,
           hy_f_w2, hy_f_b2, hy_f_w3, hy_f_b3, hy_f_wout, hy_freq, hy_bias, w_branch, w_out, ln_g, ln_b,
           w_router, router_bias, w_gate, w_up, w_down):
    b, n_lat, d = x.shape
    assert b == 1 and d == D_MODEL
    n_ctx = ctx.shape[1]
    depth = w_mod.shape[0]
    hw = RWKV_WIDTH
    xl = x[0]
    xc = ctx[0]

    cc = jnp.concatenate([c[:1], c_ctx[None, :], jnp.zeros((6, d), F32)], axis=0)
    mod = _mod_call(cc, w_mod, b_mod)

    head_of = np.arange(hw) // HEAD_DIM
    G = jnp.asarray((head_of[:, None] == head_of[None, :]).astype(np.float32), dtype=BF)
    bands = jnp.linspace(1e-4, FILTER_BANDS - 1, FILTER_BANDS, dtype=F32)[None, :]
    deltas = jnp.abs(jnp.linspace(HYENA_MIN_DECAY, HYENA_MAX_DECAY, HYENA_WIDTH, dtype=F32))[None, :]
    wr_pad = jnp.pad(w_router.T, ((0, LANES - N_EXPERTS), (0, 0)))
    rb_pad = jnp.pad(router_bias, (0, LANES - N_EXPERTS))[:, None]
    w_in_bf = w_in.astype(BF)
    w_branch_bf = w_branch.astype(BF)
    w_out_bf = w_out.astype(BF)
    g2_bf = rwkv_g2.astype(BF)

    for l in range(depth):
        last = l == depth - 1
        ml = [mod[l, 0:1, j * d:(j + 1) * d] for j in range(6)]
        mc = [mod[l, 1:2, j * d:(j + 1) * d] for j in range(6)]
        prm = dict(mu=rwkv_mu[l][None, :],
                   w0=rwkv_w0[l].reshape(1, 2 * hw), w2=_blockdiag2(rwkv_w2[l]),
                   a0=rwkv_a0[l].reshape(1, 2 * hw), a2=_blockdiag2(rwkv_a2[l]),
                   k_k=rwkv_k_k[l][None, :], k_a=rwkv_k_a[l][None, :], r_k=rwkv_r_k[l][None, :], G=G)
        w1 = hy_f_w1[l]
        fp = dict(bands=bands.T, w1t=w1[0:1].T, w1c=w1[1:1 + FILTER_BANDS].T, w1s=w1[1 + FILTER_BANDS:].T,
                  b1=hy_f_b1[l][:, None], w2=hy_f_w2[l].T, b2=hy_f_b2[l][:, None], w3=hy_f_w3[l].T,
                  b3=hy_f_b3[l][:, None], wout=hy_f_wout[l], freq=hy_freq[l][:, None], deltas=deltas)
        hy = dict(conv_w=hy_conv_w[l], conv_b=hy_conv_b[l][None, :], bias=hy_bias[l][None, :])
        mp = dict(lnx_g=rwkv_lnx_g[l][None, :], lnx_b=rwkv_lnx_b[l][None, :], g2=g2_bf, G=G,
                  w_branch=w_branch_bf, w_out=w_out_bf, ln_g=ln_g[l, 0][None, :], ln_b=ln_b[l, 0][None, :])
        s0 = jnp.zeros((2, HEAD_DIM, hw), F32)
        xc_new, hf_c, s_ctx = _mixer(xc, mc, l, w_in_bf, prm, fp, hy, mp, s0, latent=False)
        xl, hf_l, _ = _mixer(xl, ml, l, w_in_bf, prm, fp, hy, mp, s_ctx, latent=True)
        lng, lnb = ln_g[l, 1][None, :], ln_b[l, 1][None, :]
        if last:
            (xl,) = _moe_layer([(xl, hf_l, ml[5])], wr_pad, rb_pad, lng, lnb, w_gate, w_up, w_down, l)
        else:
            xc, xl = _moe_layer([(xc_new, hf_c, mc[5]), (xl, hf_l, ml[5])], wr_pad, rb_pad, lng, lnb,
                                w_gate, w_up, w_down, l)
    return xl[None]
```
